```python
import math
import jax
import jax.numpy as jnp
from jax import lax
import numpy as np

D_MODEL = 1024
BATCH = 8
SEQ = 2048
DEPTH = 2

GRID_W = 64
CTX_LEN = 256

N_EVEN = (DEPTH + 1) // 2
N_ODD = DEPTH // 2
MIX_W = 1024
NORM_EPS = 1e-6
ROPE_BASE = 10000.0

A_HEADS = 8
A_HEAD = 64
A_DIM = A_HEADS * A_HEAD
LORA_W = 64
LORA_A = 64
LORA_G = 128
W_DECAY_SCALE = 0.606531
RWKV_GN_EPS = 64e-5
POOL_WINDOWS = (2, 4, 8, 16)
POOL_GROUP = 128
B_DIM = POOL_GROUP * 4
C_HEADS = 4
C_HEAD = 64
C_VDIM = 2 * C_HEAD
C_DIM = C_HEADS * C_VDIM
DIFF_EPS = 1e-5
Q_BLOCK = 128
D_HEADS = 4
D_KDIM = 64
D_VDIM = 128
D_DIM = D_HEADS * D_VDIM
RET_CHUNK = 128

A_K_OFF = 0
A_V_OFF = A_DIM
W_LORA_OFF = 2 * A_DIM
A_LORA_OFF = W_LORA_OFF + LORA_W
EVEN_CTX_COLS = A_LORA_OFF + LORA_A
A_R_OFF = EVEN_CTX_COLS
G_LORA_OFF = A_R_OFF + A_DIM
POOL_OFF = G_LORA_OFF + LORA_G
EVEN_IN_COLS = POOL_OFF + B_DIM
C_K_OFF = 0
C_V_OFF = C_HEADS * 2 * C_HEAD
D_K_OFF = C_V_OFF + C_DIM
D_V_OFF = D_K_OFF + D_HEADS * D_KDIM
ODD_CTX_COLS = D_V_OFF + D_DIM
C_Q_OFF = ODD_CTX_COLS
D_Q_OFF = C_Q_OFF + C_HEADS * 2 * C_HEAD
D_G_OFF = D_Q_OFF + D_HEADS * D_KDIM
ODD_IN_COLS = D_G_OFF + D_DIM

MOE_GROUPS = 4
MOE_PER_GROUP = 8
MOE_EXPERTS = MOE_GROUPS * MOE_PER_GROUP
MOE_TOP_K = 2
D_EXPERT = 512
EXPERT_BLOCK = 256

kernel_name = 'hybrid_rwkv7_pool_diffattn_retnet_hmoe_dit'


def _rmsnorm(x, g, eps=NORM_EPS):
    xf = x.astype(jnp.float32)
    y = xf * lax.rsqrt(jnp.mean(xf * xf, axis=-1, keepdims=True) + eps)
    return (y * g.astype(jnp.float32)).astype(x.dtype)


def _group_norm_heads(y, g, b, n_heads, eps):
    shp = y.shape
    yh = y.reshape(shp[:-1] + (n_heads, shp[-1] // n_heads)).astype(jnp.float32)
    mu = jnp.mean(yh, axis=-1, keepdims=True)
    var = jnp.mean(jnp.square(yh - mu), axis=-1, keepdims=True)
    yh = (yh - mu) * lax.rsqrt(var + eps)
    return yh.reshape(shp) * g + b


def _flip(u):
    return None if u is None else jnp.flip(u, axis=1)


def _centred_shift(u, kern):
    up = jnp.pad(u, ((0, 0), (1, 1), (0, 0)))
    return kern[0] * up[:, :-2] + kern[1] * up[:, 1:-1] + kern[2] * up[:, 2:]


def _rope_1d(x, pos):
    half = x.shape[-1] // 2
    inv = ROPE_BASE ** (-jnp.arange(half, dtype=jnp.float32) / half)
    ang = pos.astype(jnp.float32)[:, None] * inv[None, :]
    cos = jnp.cos(ang)[None, :, None, :]
    sin = jnp.sin(ang)[None, :, None, :]
    xf = x.astype(jnp.float32)
    x1, x2 = xf[..., :half], xf[..., half:]
    return jnp.concatenate([x1 * cos - x2 * sin, x1 * sin + x2 * cos], axis=-1).astype(x.dtype)


def _axial_rope(x, row, col):
    n = x.shape[-1] // 2
    return jnp.concatenate([_rope_1d(x[..., :n], row), _rope_1d(x[..., n:], col)], axis=-1)


def _rwkv7_scan(state0, w, k, v, kk, b, r=None):
    seq = (w, k, v, kk, b) if r is None else (w, k, v, kk, b, r)

    def step(S, inp):
        w_t, k_t, v_t, kk_t, b_t = inp[:5]
        sa = jnp.einsum('bhvk,bhk->bhv', S, -kk_t)
        S = S * w_t[:, :, None, :] + sa[..., None] * b_t[:, :, None, :] + v_t[..., None] * k_t[:, :, None, :]
        if r is None:
            return S, None
        return S, jnp.einsum('bhvk,bhk->bhv', S, inp[5])

    S, ys = lax.scan(step, state0, tuple(jnp.swapaxes(a, 0, 1) for a in seq))
    return S, (None if r is None else jnp.swapaxes(ys, 0, 1))


def _rwkv_prep(p, shift_k, w0, w_up, a0, a_up, k_k, k_a, with_r):
    Bn, T = p.shape[:2]
    f32 = jnp.float32

    def heads(u):
        return u.reshape(Bn, T, A_HEADS, A_HEAD).astype(f32)

    k = _centred_shift(p[..., A_K_OFF:A_K_OFF + A_DIM], shift_k[:, 0:A_DIM]).astype(f32)
    v = _centred_shift(p[..., A_V_OFF:A_V_OFF + A_DIM], shift_k[:, A_DIM:2 * A_DIM])
    wd = jnp.tanh(p[..., W_LORA_OFF:W_LORA_OFF + LORA_W])
    ad = p[..., A_LORA_OFF:A_LORA_OFF + LORA_A]
    kk = heads(k * k_k)
    kk = kk * lax.rsqrt(jnp.maximum(jnp.sum(kk * kk, axis=-1, keepdims=True), 1e-12))
    dirs = []
    for d in range(2):
        w = jnp.exp(-W_DECAY_SCALE * jax.nn.sigmoid((w0[d] + wd @ w_up[d]).astype(f32)))
        a = jax.nn.sigmoid((a0[d] + ad @ a_up[d]).astype(f32))
        kt = k * (1.0 + (a - 1.0) * k_a)
        dirs.append((heads(w), heads(kt), heads(a) * kk))
    r = None
    if with_r:
        r = heads(_centred_shift(p[..., A_R_OFF:A_R_OFF + A_DIM], shift_k[:, 2 * A_DIM:3 * A_DIM]))
    return heads(v), kk, dirs, r


def _pool_mixer(u, pool_w, pool_scale):
    Bn, T, _ = u.shape
    uf = u.astype(jnp.float32)
    cs = jnp.concatenate([jnp.zeros((Bn, 1, B_DIM), jnp.float32), jnp.cumsum(uf, axis=1)], axis=1)
    t = jnp.arange(T)
    diffs = []
    for gi, win in enumerate(POOL_WINDOWS):
        lo = jnp.clip(t - win // 2, 0, T)
        hi = jnp.clip(t - win // 2 + win, 0, T)
        sl = slice(gi * POOL_GROUP, (gi + 1) * POOL_GROUP)
        mean = (cs[:, hi, sl] - cs[:, lo, sl]) / (hi - lo).astype(jnp.float32)[None, :, None]
        diffs.append(mean - uf[:, :, sl])
    d = jnp.stack(diffs, axis=2)
    y = jnp.einsum('btgc,gcd->btgd', d, pool_w.astype(jnp.float32)).reshape(Bn, T, B_DIM)
    return (y * pool_scale).astype(u.dtype)


def _even_mixer(h_ctx, h_lat, need_ctx, w_in, w_out, shift_k, w0, w_up, a0, a_up, g_up,
                k_k, k_a, r_k, ln_g, ln_b, pool_w, pool_scale):
    p_lat = h_lat @ w_in
    p_ctx = h_ctx @ (w_in if need_ctx else w_in[:, :EVEN_CTX_COLS])
    v_c, kk_c, (fc, bc), r_c = _rwkv_prep(p_ctx, shift_k, w0, w_up, a0, a_up, k_k, k_a, need_ctx)
    v_l, kk_l, (fl, bl), r_l = _rwkv_prep(p_lat, shift_k, w0, w_up, a0, a_up, k_k, k_a, True)
    zero = jnp.zeros((h_lat.shape[0], A_HEADS, A_HEAD, A_HEAD), jnp.float32)
    s_f, yf_c = _rwkv7_scan(zero, fc[0], fc[1], v_c, kk_c, fc[2], r_c)
    s_b, yb_c = _rwkv7_scan(zero, _flip(bc[0]), _flip(bc[1]), _flip(v_c), _flip(kk_c), _flip(bc[2]), _flip(r_c))
    _, yf_l = _rwkv7_scan(s_f, fl[0], fl[1], v_l, kk_l, fl[2], r_l)
    _, yb_l = _rwkv7_scan(s_b, _flip(bl[0]), _flip(bl[1]), _flip(v_l), _flip(kk_l), _flip(bl[2]), _flip(r_l))
    r_kh = r_k.reshape(A_HEADS, A_HEAD)

    def merge(p, yf, yb, r, v, kt_f, kt_b):
        Bn, T = p.shape[:2]
        y = _group_norm_heads((yf + _flip(yb)).reshape(Bn, T, A_DIM), ln_g, ln_b, A_HEADS, RWKV_GN_EPS)
        bonus = jnp.sum(r * (kt_f + kt_b) * r_kh, axis=-1, keepdims=True) * v
        g = jax.nn.sigmoid(p[..., G_LORA_OFF:G_LORA_OFF + LORA_G]) @ g_up
        y_a = ((y + bonus.reshape(Bn, T, A_DIM)) * g).astype(p.dtype)
        y_b = _pool_mixer(p[..., POOL_OFF:POOL_OFF + B_DIM], pool_w, pool_scale)
        return jnp.concatenate([y_a, y_b], axis=-1) @ w_out

    y_lat = merge(p_lat, yf_l, yb_l, r_l, v_l, fl[1], bl[1])
    y_ctx = merge(p_ctx, yf_c, yb_c, r_c, v_c, fc[1], bc[1]) if need_ctx else None
    return y_ctx, y_lat


def _diff_attend(q1, q2, k1, k2, v, lam):
    sc = C_HEAD ** -0.5
    a1 = jax.nn.softmax(jnp.einsum('bqhd,bkhd->bhqk', q1, k1).astype(jnp.float32) * sc, axis=-1)
    a2 = jax.nn.softmax(jnp.einsum('bqhd,bkhd->bhqk', q2, k2).astype(jnp.float32) * sc, axis=-1)
    return jnp.einsum('bhqk,bkhd->bqhd', (a1 - lam * a2).astype(v.dtype), v)


def _blocked_diff_attend(q1, q2, k1, k2, v, lam):
    Bn, T, H, d = q1.shape
    nb = T // Q_BLOCK

    def blk(q):
        return jnp.swapaxes(q.reshape(Bn, nb, Q_BLOCK, H, d), 0, 1)

    out = lax.map(lambda qs: _diff_attend(qs[0], qs[1], k1, k2, v, lam), (blk(q1), blk(q2)))
    return jnp.swapaxes(out, 0, 1).reshape(Bn, T, H, v.shape[-1])


def _retention_log_gammas():
    lg = jnp.log(1.0 - 2.0 ** (-5.0 - jnp.arange(D_HEADS, dtype=jnp.float32)))
    return lg, lg[::-1]


def _retention_scan(state0, q, k, v, log_gamma):
    Bn, T, H, _ = k.shape
    dv = v.shape[-1]
    nc = T // RET_CHUNK
    pos = jnp.arange(RET_CHUNK, dtype=jnp.float32)
    k_dec = jnp.exp(log_gamma[:, None] * (RET_CHUNK - 1 - pos)[None, :])
    chunk_dec = jnp.exp(log_gamma * RET_CHUNK)
    if q is not None:
        diff = pos[:, None] - pos[None, :]
        inner_dec = jnp.where(diff[None] >= 0, jnp.exp(log_gamma[:, None, None] * jnp.maximum(diff, 0.0)[None]), 0.0)
        q_dec = jnp.swapaxes(jnp.exp(log_gamma[:, None] * (pos + 1.0)[None, :]), 0, 1)[None, :, :, None]

    def chunks(u):
        return jnp.swapaxes(u.reshape(Bn, nc, RET_CHUNK, H, u.shape[-1]), 0, 1)

    def step(R, inp):
        kc, vc = inp[0], inp[1]
        R_new = R * chunk_dec[None, :, None, None] + jnp.einsum('bshk,hs,bshv->bhkv', kc, k_dec, vc)
        if q is None:
            return R_new, None
        qc = inp[2]
        inner = jnp.einsum('bihk,bshk->bhis', qc, kc) * inner_dec[None]
        o = jnp.einsum('bhis,bshv->bihv', inner, vc) + jnp.einsum('bihk,bhkv->bihv', qc, R) * q_dec
        return R_new, o

    xs = (chunks(k), chunks(v)) + (() if q is None else (chunks(q),))
    R, os = lax.scan(step, state0, xs)
    return R, (None if q is None else jnp.swapaxes(os, 0, 1).reshape(Bn, T, H, dv))


def _hsplit(p, off, n_heads, dh):
    return p[..., off:off + n_heads * dh].reshape(p.shape[:2] + (n_heads, dh))


def _odd_mixer(h_ctx, h_lat, need_ctx, lam_init, row, col, w_in, w_out, diff_lambda, diff_subln, ret_norm):
    f32 = jnp.float32
    p_lat = h_lat @ w_in
    p_ctx = h_ctx @ (w_in if need_ctx else w_in[:, :ODD_CTX_COLS])

    def rope(u):
        return _axial_rope(u, row, col)

    ck = _hsplit(p_ctx, C_K_OFF, C_HEADS, 2 * C_HEAD)
    cv = _hsplit(p_ctx, C_V_OFF, C_HEADS, C_VDIM)
    lk = _hsplit(p_lat, C_K_OFF, C_HEADS, 2 * C_HEAD)
    lv = _hsplit(p_lat, C_V_OFF, C_HEADS, C_VDIM)
    lq = _hsplit(p_lat, C_Q_OFF, C_HEADS, 2 * C_HEAD)
    k1 = jnp.concatenate([ck[..., :C_HEAD], rope(lk[..., :C_HEAD])], axis=1)
    k2 = jnp.concatenate([ck[..., C_HEAD:], rope(lk[..., C_HEAD:])], axis=1)
    vv = jnp.concatenate([cv, lv], axis=1)
    lf = diff_lambda.astype(f32)
    lam = jnp.exp(jnp.sum(lf[0] * lf[1])) - jnp.exp(jnp.sum(lf[2] * lf[3])) + lam_init

    def c_post(o):
        o = _rmsnorm(o, diff_subln, DIFF_EPS) * (1.0 - lam_init)
        return o.reshape(o.shape[:2] + (C_DIM,))

    c_lat = c_post(_blocked_diff_attend(rope(lq[..., :C_HEAD]), rope(lq[..., C_HEAD:]), k1, k2, vv, lam))

    sc = D_KDIM ** -0.5
    lg_f, lg_b = _retention_log_gammas()
    ck_d = _hsplit(p_ctx, D_K_OFF, D_HEADS, D_KDIM).astype(f32) * sc
    cv_d = _hsplit(p_ctx, D_V_OFF, D_HEADS, D_VDIM).astype(f32)
    cq_d = _hsplit(p_ctx, D_Q_OFF, D_HEADS, D_KDIM).astype(f32) if need_ctx else None
    lq_d = rope(_hsplit(p_lat, D_Q_OFF, D_HEADS, D_KDIM)).astype(f32)
    lk_d = rope(_hsplit(p_lat, D_K_OFF, D_HEADS, D_KDIM)).astype(f32) * sc
    lv_d = _hsplit(p_lat, D_V_OFF, D_HEADS, D_VDIM).astype(f32)
    zero = jnp.zeros((h_lat.shape[0], D_HEADS, D_KDIM, D_VDIM), f32)
    r_f, of_c = _retention_scan(zero, cq_d, ck_d, cv_d, lg_f)
    r_b, ob_c = _retention_scan(zero, _flip(cq_d), _flip(ck_d), _flip(cv_d), lg_b)
    _, of_l = _retention_scan(r_f, lq_d, lk_d, lv_d, lg_f)
    _, ob_l = _retention_scan(r_b, _flip(lq_d), _flip(lk_d), _flip(lv_d), lg_b)
    ret_g = ret_norm.reshape(D_HEADS, D_VDIM)

    def d_post(o, p):
        o = _rmsnorm(o, ret_g)
        return o.reshape(o.shape[:2] + (D_DIM,)) * jax.nn.silu(p[..., D_G_OFF:D_G_OFF + D_DIM])

    d_lat = d_post(of_l + _flip(ob_l), p_lat)
    y_lat = jnp.concatenate([c_lat, d_lat.astype(c_lat.dtype)], axis=-1) @ w_out
    y_ctx = None
    if need_ctx:
        cq = _hsplit(p_ctx, C_Q_OFF, C_HEADS, 2 * C_HEAD)
        c_ctx_o = c_post(_diff_attend(cq[..., :C_HEAD], cq[..., C_HEAD:], ck[..., :C_HEAD], ck[..., C_HEAD:], cv, lam))
        d_ctx_o = d_post(of_c + _flip(ob_c), p_ctx)
        y_ctx = jnp.concatenate([c_ctx_o, d_ctx_o.astype(c_ctx_o.dtype)], axis=-1) @ w_out
    return y_ctx, y_lat


def _expert_dispatch(h, eidx, gates, w_gate, w_up, w_down):
    N, D = h.shape
    K = eidx.shape[1]
    E = w_gate.shape[0]
    flat_e = eidx.reshape(-1).astype(jnp.int32)
    order = jnp.argsort(flat_e)
    se = flat_e[order]
    tok = (order // K).astype(jnp.int32)
    counts = jnp.bincount(flat_e, length=E)
    padded = (counts + EXPERT_BLOCK - 1) // EXPERT_BLOCK * EXPERT_BLOCK
    pad_end = jnp.cumsum(padded)
    pad_start = pad_end - padded
    start = jnp.cumsum(counts) - counts
    dest = pad_start[se] + jnp.arange(N * K, dtype=jnp.int32) - start[se]
    n_blocks = -(-(N * K) // EXPERT_BLOCK) + E
    rows = jnp.full((n_blocks * EXPERT_BLOCK,), N, jnp.int32).at[dest].set(tok)
    block_e = jnp.minimum(jnp.searchsorted(pad_end, jnp.arange(n_blocks, dtype=jnp.int32) * EXPERT_BLOCK, side='right'), E - 1)
    xb = jnp.concatenate([h, jnp.zeros((1, D), h.dtype)], axis=0)[rows].reshape(n_blocks, EXPERT_BLOCK, D)

    def run(args):
        xe, e = args
        return (jax.nn.silu(xe @ w_gate[e]) * (xe @ w_up[e])) @ w_down[e]

    yb = lax.map(run, (xb, block_e)).reshape(-1, D)
    contrib = yb[dest] * gates.reshape(-1)[order][:, None].astype(h.dtype)
    return jnp.zeros_like(h).at[tok].add(contrib)


def _hier_moe(h, w_rg, w_re, w_gate, w_up, w_down):
    N = h.shape[0]
    lg = (h @ w_rg).astype(jnp.float32)
    grp = jnp.argmax(lg, axis=-1).astype(jnp.int32)
    gw = jnp.take_along_axis(jax.nn.softmax(lg, axis=-1), grp[:, None], axis=-1)
    le = (h @ w_re).astype(jnp.float32).reshape(N, MOE_GROUPS, MOE_PER_GROUP)
    le = jnp.take_along_axis(le, grp[:, None, None], axis=1)[:, 0]
    p_in, e_in = lax.top_k(jax.nn.softmax(le, axis=-1), MOE_TOP_K)
    gates = gw * p_in / jnp.sum(p_in, axis=-1, keepdims=True)
    eidx = grp[:, None] * MOE_PER_GROUP + e_in.astype(jnp.int32)
    return _expert_dispatch(h, eidx, gates, w_gate, w_up, w_down)


def setup_inputs(seed: int = 0) -> dict:
    key = jax.random.key(seed)
    ks = iter(jax.random.split(key, 48))
    D = D_MODEL

    def nrm(shape, s):
        return jax.random.normal(next(ks), shape, jnp.float32) * s

    def gain(shape):
        return 1.0 + nrm(shape, 0.1)

    return {
        'x': nrm((BATCH, SEQ, D), 1.0),
        'c': nrm((BATCH, D), 1.0),
        'ctx': nrm((BATCH, CTX_LEN, D), 1.0),
        'c_ctx': nrm((D,), 1.0),
        'ada_w': nrm((DEPTH, D, 6 * D), 0.5 * D ** -0.5),
        'ada_b': nrm((DEPTH, 6 * D), 0.01),
        'norm_mix': gain((DEPTH, D)),
        'norm_ffn': gain((DEPTH, D)),
        'ev_w_in': nrm((N_EVEN, D, EVEN_IN_COLS), D ** -0.5),
        'ev_w_out': nrm((N_EVEN, MIX_W, D), MIX_W ** -0.5),
        'rwkv_shift': jnp.array([0.25, 0.5, 0.25], jnp.float32)[None, :, None] + nrm((N_EVEN, 3, 3 * A_DIM), 0.05),
        'rwkv_w0': nrm((N_EVEN, 2, A_DIM), 1.0) - 1.5,
        'rwkv_w_up': nrm((N_EVEN, 2, LORA_W, A_DIM), 0.5 * LORA_W ** -0.5),
        'rwkv_a0': nrm((N_EVEN, 2, A_DIM), 0.5),
        'rwkv_a_up': nrm((N_EVEN, 2, LORA_A, A_DIM), 0.5 * LORA_A ** -0.5),
        'rwkv_g_up': nrm((N_EVEN, LORA_G, A_DIM), LORA_G ** -0.5),
        'rwkv_k_k': gain((N_EVEN, A_DIM)),
        'rwkv_k_a': gain((N_EVEN, A_DIM)),
        'rwkv_r_k': nrm((N_EVEN, A_DIM), 0.1),
        'rwkv_ln_g': gain((N_EVEN, A_DIM)),
        'rwkv_ln_b': nrm((N_EVEN, A_DIM), 0.01),
        'pool_w': nrm((N_EVEN, len(POOL_WINDOWS), POOL_GROUP, POOL_GROUP), POOL_GROUP ** -0.5),
        'pool_scale': gain((N_EVEN, B_DIM)),
        'od_w_in': nrm((N_ODD, D, ODD_IN_COLS), D ** -0.5),
        'od_w_out': nrm((N_ODD, MIX_W, D), MIX_W ** -0.5),
        'diff_lambda': nrm((N_ODD, 4, C_HEAD), 0.1),
        'diff_subln': gain((N_ODD, C_VDIM)),
        'ret_norm': gain((N_ODD, D_DIM)),
        'moe_router_group': nrm((DEPTH, D, MOE_GROUPS), D ** -0.5),
        'moe_router_expert': nrm((DEPTH, D, MOE_EXPERTS), D ** -0.5),
        'moe_w_gate': nrm((DEPTH, MOE_EXPERTS, D, D_EXPERT), D ** -0.5),
        'moe_w_up': nrm((DEPTH, MOE_EXPERTS, D, D_EXPERT), D ** -0.5),
        'moe_w_down': nrm((DEPTH, MOE_EXPERTS, D_EXPERT, D), D_EXPERT ** -0.5),
        'final_norm': gain((D,)),
    }


def reference(x, c, ctx, c_ctx, ada_w, ada_b, norm_mix, norm_ffn, ev_w_in, ev_w_out, rwkv_shift,
              rwkv_w0, rwkv_w_up, rwkv_a0, rwkv_a_up, rwkv_g_up, rwkv_k_k, rwkv_k_a, rwkv_r_k,
              rwkv_ln_g, rwkv_ln_b, pool_w, pool_scale, od_w_in, od_w_out, diff_lambda, diff_subln,
              ret_norm, moe_router_group, moe_router_expert, moe_w_gate, moe_w_up, moe_w_down, final_norm):
    Bn, T, D = x.shape
    ROWS = T // GRID_W
    row = jnp.repeat(jnp.arange(ROWS, dtype=jnp.int32), GRID_W)
    col = jnp.tile(jnp.arange(GRID_W, dtype=jnp.int32), ROWS)
    xc = ctx
    for i in range(DEPTH):
        last = i == DEPTH - 1
        mod = jax.nn.silu(c) @ ada_w[i] + ada_b[i]
        mod_c = jax.nn.silu(c_ctx) @ ada_w[i] + ada_b[i]
        sh1, sc1, g1, sh2, sc2, g2 = jnp.split(mod[:, None, :], 6, axis=-1)
        csh1, csc1, cg1, csh2, csc2, cg2 = jnp.split(mod_c, 6)
        h_lat = _rmsnorm(x, norm_mix[i]) * (1.0 + sc1) + sh1
        h_ctx = _rmsnorm(xc, norm_mix[i]) * (1.0 + csc1) + csh1
        j = i // 2
        if i % 2 == 0:
            y_ctx, y_lat = _even_mixer(h_ctx, h_lat, not last, ev_w_in[j], ev_w_out[j], rwkv_shift[j],
                                       rwkv_w0[j], rwkv_w_up[j], rwkv_a0[j], rwkv_a_up[j], rwkv_g_up[j],
                                       rwkv_k_k[j], rwkv_k_a[j], rwkv_r_k[j], rwkv_ln_g[j], rwkv_ln_b[j],
                                       pool_w[j], pool_scale[j])
        else:
            lam_init = 0.8 - 0.6 * math.exp(-0.3 * i)
            y_ctx, y_lat = _odd_mixer(h_ctx, h_lat, not last, lam_init, row, col, od_w_in[j], od_w_out[j],
                                      diff_lambda[j], diff_subln[j], ret_norm[j])
        x = x + g1 * y_lat
        hf_lat = _rmsnorm(x, norm_ffn[i]) * (1.0 + sc2) + sh2
        moe_p = (moe_router_group[i], moe_router_expert[i], moe_w_gate[i], moe_w_up[i], moe_w_down[i])
        if last:
            x = x + g2 * _hier_moe(hf_lat.reshape(-1, D), *moe_p).reshape(x.shape)
        else:
            xc = xc + cg1 * y_ctx
            hf_ctx = _rmsnorm(xc, norm_ffn[i]) * (1.0 + csc2) + csh2
            n_ctx = xc.shape[0] * xc.shape[1]
            f = _hier_moe(jnp.concatenate([hf_ctx.reshape(-1, D), hf_lat.reshape(-1, D)], axis=0), *moe_p)
            xc = xc + cg2 * f[:n_ctx].reshape(xc.shape)
            x = x + g2 * f[n_ctx:].reshape(x.shape)
    return _rmsnorm(x, final_norm)
```

```python
import functools
import math

import jax
import jax.numpy as jnp
from jax import lax
from jax.experimental import pallas as pl
from jax.experimental.pallas import tpu as pltpu

F32 = jnp.float32
BF16 = jnp.bfloat16
HI = lax.Precision.HIGHEST

GRID_W = 64
NORM_EPS = 1e-6
ROPE_BASE = 10000.0
A_HEADS, A_HEAD = 8, 64
A_DIM = A_HEADS * A_HEAD
LORA_W, LORA_A, LORA_G = 64, 64, 128
W_DECAY_SCALE = 0.606531
RWKV_GN_EPS = 64e-5
POOL_WINDOWS = (2, 4, 8, 16)
POOL_GROUP = 128
B_DIM = POOL_GROUP * 4
C_HEADS, C_HEAD = 4, 64
C_VDIM = 2 * C_HEAD
C_DIM = C_HEADS * C_VDIM
DIFF_EPS = 1e-5
D_HEADS, D_KDIM, D_VDIM = 4, 64, 128
D_DIM = D_HEADS * D_VDIM
A_K_OFF, A_V_OFF = 0, A_DIM
W_LORA_OFF = 2 * A_DIM
A_LORA_OFF = W_LORA_OFF + LORA_W
A_R_OFF = A_LORA_OFF + LORA_A
G_LORA_OFF = A_R_OFF + A_DIM
POOL_OFF = G_LORA_OFF + LORA_G
EVEN_IN_COLS = POOL_OFF + B_DIM
C_K_OFF = 0
C_V_OFF = C_HEADS * 2 * C_HEAD
D_K_OFF = C_V_OFF + C_DIM
D_V_OFF = D_K_OFF + D_HEADS * D_KDIM
C_Q_OFF = D_V_OFF + D_DIM
D_Q_OFF = C_Q_OFF + C_HEADS * 2 * C_HEAD
D_G_OFF = D_Q_OFF + D_HEADS * D_KDIM
ODD_IN_COLS = D_G_OFF + D_DIM
MOE_GROUPS, MOE_PER_GROUP = 4, 8
MOE_EXPERTS = MOE_GROUPS * MOE_PER_GROUP
MOE_TOP_K = 2
EXPERT_BLOCK = 256

LANES = 128
SUBLANES = 8
TB = 256
RWKV_CHUNK = 64
RET_CHUNK = 128
ATT_TQ = 256
VMEM_LIMIT = 56 * 1024 * 1024


def _cparams(sem):
    return pltpu.CompilerParams(dimension_semantics=sem, vmem_limit_bytes=VMEM_LIMIT)


def _modnorm(x, gain, scale, shift, eps=NORM_EPS):
    ms = jnp.mean(x * x, axis=-1, keepdims=True)
    return x * lax.rsqrt(ms + eps) * gain * (1.0 + scale) + shift


def _bdot(a, b, dims=((1,), (0,))):
    return lax.dot_general(a.astype(BF16), b.astype(BF16), (dims, ((), ())), preferred_element_type=F32)


def _ada_kernel(c_ref, w_ref, b_ref, o_ref):
    c = c_ref[...]
    s = c * jax.nn.sigmoid(c)
    o_ref[0] = jnp.dot(s, w_ref[0], precision=HI, preferred_element_type=F32) + b_ref[0]


def _ada(cpad, ada_w, ada_b):
    L, D, N = ada_w.shape
    tn = 1536
    return pl.pallas_call(
        _ada_kernel,
        grid=(L, N // tn),
        in_specs=[pl.BlockSpec((16, D), lambda l, n: (0, 0)),
                  pl.BlockSpec((1, D, tn), lambda l, n: (l, 0, n)),
                  pl.BlockSpec((1, 1, tn), lambda l, n: (l, 0, n))],
        out_specs=pl.BlockSpec((1, 16, tn), lambda l, n: (l, 0, n)),
        out_shape=jax.ShapeDtypeStruct((L, 16, N), F32),
        compiler_params=_cparams(("arbitrary", "arbitrary")),
        name="ada_mod",
    )(cpad, ada_w, ada_b.reshape(L, 1, N))


def _mod_spec(D, all_lat=False):
    if all_lat:
        return pl.BlockSpec((1, 1, 6, D), lambda b, j: (b, 1, 0, 0))
    return pl.BlockSpec((1, 1, 6, D), lambda b, j: (b, jnp.minimum(j, 1), 0, 0))


def _proj_kernel(x_ref, mod_ref, g_ref, w_ref, *rest, n_out, rope_chunks, kscale_chunks, out_dtype):
    if rope_chunks:
        cos_ref, s1_ref, s2_ref, o_ref = rest
    else:
        (o_ref,) = rest
    m = mod_ref[0, 0]
    h = _modnorm(x_ref[0], g_ref[...], m[1:2], m[0:1]).astype(BF16)
    cw = 512
    for n0 in range(0, n_out, cw):
        p = jnp.dot(h, w_ref[:, n0:n0 + cw], preferred_element_type=F32)
        if not rope_chunks:
            o_ref[0, :, n0:n0 + cw] = p.astype(out_dtype)
            continue
        for q in range(cw // LANES):
            gi = n0 // LANES + q
            sub = p[:, q * LANES:(q + 1) * LANES]
            if gi in rope_chunks:
                sub = (sub * cos_ref[...] + pltpu.roll(sub, LANES - 16, 1) * s1_ref[...]
                       + pltpu.roll(sub, 16, 1) * s2_ref[...])
            if gi in kscale_chunks:
                sub = sub * (D_KDIM ** -0.5)
            o_ref[0, :, n0 + q * LANES:n0 + (q + 1) * LANES] = sub.astype(out_dtype)


def _proj(X, mod, gain, w_bf16, out_dtype, rope=None):
    B, S, D = X.shape
    N = w_bf16.shape[1]
    in_specs = [pl.BlockSpec((1, TB, D), lambda b, j: (b, j, 0)),
                _mod_spec(D),
                pl.BlockSpec((1, D), lambda b, j: (0, 0)),
                pl.BlockSpec((D, N), lambda b, j: (0, 0))]
    args = [X, mod, gain.reshape(1, D), w_bf16]
    rope_chunks, kscale_chunks = (), ()
    if rope is not None:
        cos, s1, s2 = rope
        in_specs += [pl.BlockSpec((TB, LANES), lambda b, j: (j, 0))] * 3
        args += [cos, s1, s2]
        ranges = [(C_K_OFF, C_V_OFF), (D_K_OFF, D_V_OFF), (C_Q_OFF, D_G_OFF)]
        rope_chunks = tuple(g for lo, hi in ranges for g in range(lo // LANES, hi // LANES))
        kscale_chunks = tuple(range(D_K_OFF // LANES, D_V_OFF // LANES))
    return pl.pallas_call(
        functools.partial(_proj_kernel, n_out=N, rope_chunks=rope_chunks, kscale_chunks=kscale_chunks,
                          out_dtype=out_dtype),
        grid=(B, S // TB),
        in_specs=in_specs,
        out_specs=pl.BlockSpec((1, TB, N), lambda b, j: (b, j, 0)),
        out_shape=jax.ShapeDtypeStruct((B, S, N), out_dtype),
        compiler_params=_cparams(("arbitrary", "arbitrary")),
        name="proj_rope" if rope is not None else "proj",
    )(*args)


def _even_prep_kernel(p_ref, pp_ref, pn_ref, sh_ref, wa0_ref, lora_ref, gup_ref, kkg_ref, ka_ref, rk_ref,
                      pw_ref, ps_ref, eh_ref,
                      lw_ref, kt_ref, bb_ref, v_ref, kkn_ref, r_ref, bonus_ref, g_ref, yb_ref,
                      *, nblk, t_ctx, t_lat):
    j = pl.program_id(1)
    prev_ok = j >= 2
    next_ok = jnp.logical_and(j >= 1, j < nblk - 1)
    n_ext = TB + 2 * SUBLANES

    def ext(c0, c1):
        pv = jnp.where(prev_ok, pp_ref[0, :, c0:c1], 0.0)
        nx = jnp.where(next_ok, pn_ref[0, :, c0:c1], 0.0)
        return jnp.concatenate([pv, p_ref[0, :, c0:c1], nx], axis=0)

    def shift3(c0, c1, k0):
        e = ext(c0, c1)
        kern = sh_ref[:, k0:k0 + (c1 - c0)]
        y = kern[0:1] * pltpu.roll(e, 1, 0) + kern[1:2] * e + kern[2:3] * pltpu.roll(e, n_ext - 1, 0)
        return y[SUBLANES:SUBLANES + TB]

    k = shift3(A_K_OFF, A_K_OFF + A_DIM, 0)
    v = shift3(A_V_OFF, A_V_OFF + A_DIM, A_DIM)
    r = shift3(A_R_OFF, A_R_OFF + A_DIM, 2 * A_DIM)

    c = p_ref[0, :, W_LORA_OFF:W_LORA_OFF + LANES]
    lane = lax.broadcasted_iota(jnp.int32, c.shape, 1)
    lin = jnp.where(lane < LORA_W, jnp.tanh(c), c)
    z = _bdot(lin, lora_ref[...]) + wa0_ref[...]

    eh = eh_ref[...]
    kk0 = k * kkg_ref[...]
    ss = jnp.dot(kk0 * kk0, eh, precision=HI, preferred_element_type=F32)
    kk = kk0 * lax.rsqrt(jnp.maximum(ss, 1e-12))
    kt_sum = None
    for d in range(2):
        zw = z[:, (2 * d) * A_DIM:(2 * d + 1) * A_DIM]
        za = z[:, (2 * d + 1) * A_DIM:(2 * d + 2) * A_DIM]
        a = jax.nn.sigmoid(za)
        kt = k * (1.0 + (a - 1.0) * ka_ref[...])
        lw_ref[d, 0] = -W_DECAY_SCALE * jax.nn.sigmoid(zw)
        kt_ref[d, 0] = kt
        bb_ref[d, 0] = a * kk
        kt_sum = kt if kt_sum is None else kt_sum + kt
    v_ref[0] = v
    kkn_ref[0] = kk
    r_ref[0] = r
    bonus_ref[0] = jnp.dot(r * kt_sum * rk_ref[...], eh, precision=HI, preferred_element_type=F32) * v
    gl = p_ref[0, :, G_LORA_OFF:G_LORA_OFF + LORA_G]
    g_ref[0] = _bdot(jax.nn.sigmoid(gl), gup_ref[...])

    row = lax.broadcasted_iota(jnp.int32, (TB, POOL_GROUP), 0)
    tpos = row + jnp.maximum(j - 1, 0) * TB
    tseg = jnp.where(j == 0, t_ctx, t_lat)
    for gi, win in enumerate(POOL_WINDOWS):
        c0 = POOL_OFF + gi * POOL_GROUP
        e = ext(c0, c0 + POOL_GROUP)
        s = e + pltpu.roll(e, 1, 0)
        sh = 1
        while 2 * sh < win:
            s = pltpu.roll(s, sh, 0) + pltpu.roll(s, n_ext - sh, 0)
            sh *= 2
        s = s[SUBLANES:SUBLANES + TB]
        lo = jnp.clip(tpos - win // 2, 0, tseg)
        hi = jnp.clip(tpos - win // 2 + win, 0, tseg)
        mean = s / (hi - lo).astype(F32)
        dd = mean - p_ref[0, :, c0:c0 + POOL_GROUP]
        y = _bdot(dd, pw_ref[gi])
        yb_ref[0, :, gi * POOL_GROUP:(gi + 1) * POOL_GROUP] = y * ps_ref[:, gi * POOL_GROUP:(gi + 1) * POOL_GROUP]


def _even_prep(P, t_ctx, shift_k, wa0, lora_w, g_up, k_k, k_a, r_k, pool_w, pool_scale, ehead):
    B, S, N = P.shape
    nblk = S // TB
    hb = TB // SUBLANES
    const2 = lambda b, j: (0, 0)
    in_specs = [
        pl.BlockSpec((1, TB, N), lambda b, j: (b, j, 0)),
        pl.BlockSpec((1, SUBLANES, N), lambda b, j: (b, jnp.maximum(j * hb - 1, 0), 0)),
        pl.BlockSpec((1, SUBLANES, N), lambda b, j: (b, jnp.minimum((j + 1) * hb, S // SUBLANES - 1), 0)),
        pl.BlockSpec((3, 3 * A_DIM), const2),
        pl.BlockSpec((1, 4 * A_DIM), const2),
        pl.BlockSpec((LANES, 4 * A_DIM), const2),
        pl.BlockSpec((LORA_G, A_DIM), const2),
        pl.BlockSpec((1, A_DIM), const2),
        pl.BlockSpec((1, A_DIM), const2),
        pl.BlockSpec((1, A_DIM), const2),
        pl.BlockSpec((4, POOL_GROUP, POOL_GROUP), lambda b, j: (0, 0, 0)),
        pl.BlockSpec((1, B_DIM), const2),
        pl.BlockSpec((A_DIM, A_DIM), const2),
    ]
    two = jax.ShapeDtypeStruct((2, B, S, A_DIM), F32)
    one = jax.ShapeDtypeStruct((B, S, A_DIM), F32)
    two_spec = pl.BlockSpec((2, 1, TB, A_DIM), lambda b, j: (0, b, j, 0))
    one_spec = pl.BlockSpec((1, TB, A_DIM), lambda b, j: (b, j, 0))
    return pl.pallas_call(
        functools.partial(_even_prep_kernel, nblk=nblk, t_ctx=t_ctx, t_lat=S - t_ctx),
        grid=(B, nblk),
        in_specs=in_specs,
        out_specs=[two_spec] * 3 + [one_spec] * 6,
        out_shape=[two] * 3 + [one] * 6,
        compiler_params=_cparams(("arbitrary", "arbitrary")),
        name="even_prep",
    )(P, P, P, shift_k, wa0, lora_w, g_up, k_k.reshape(1, -1), k_a.reshape(1, -1), r_k.reshape(1, -1),
      pool_w, pool_scale.reshape(1, -1), ehead)


def _rwkv_scan_kernel(lw_ref, kt_ref, bb_ref, v_ref, kk_ref, r_ref, y_ref, s_ref):
    d = pl.program_id(1)
    c = pl.program_id(2)

    @pl.when(c == 0)
    def _():
        s_ref[...] = jnp.zeros_like(s_ref)

    C = RWKV_CHUNK
    tt = lax.broadcasted_iota(jnp.int32, (C, C), 0)
    ii = lax.broadcasted_iota(jnp.int32, (C, C), 1)
    rel = (tt - ii) * (1 - 2 * d)
    strict = rel > 0
    incl = rel >= 0
    lw = lw_ref[0, 0]
    cum = jnp.dot(incl.astype(F32), lw, precision=HI, preferred_element_type=F32)
    tot = jnp.sum(lw, axis=0, keepdims=True)
    p_inc = jnp.exp(cum)
    p_exc = jnp.exp(cum - lw)
    p_inv = jnp.exp(-cum)
    p_rest = jnp.exp(tot - cum)
    p_tot = jnp.exp(tot)
    bb = bb_ref[0, 0]
    kt = kt_ref[0, 0]
    vv = v_ref[0]
    at_all = (-kk_ref[0] * p_exc).astype(BF16)
    rt_all = (r_ref[0] * p_inc).astype(BF16)
    bt_all = (bb * p_inv).astype(BF16)
    kti_all = (kt * p_inv).astype(BF16)
    bp_all = (bb * p_rest).astype(BF16)
    kp_all = (kt * p_rest).astype(BF16)
    v_all = vv.astype(BF16)
    n_dbl = int(math.log2(C))
    outs = []
    for h in range(A_HEADS):
        hs = slice(h * A_HEAD, (h + 1) * A_HEAD)
        s0 = s_ref[h]
        ar = jnp.concatenate([at_all[:, hs], rt_all[:, hs]], axis=0)
        bk = jnp.concatenate([bt_all[:, hs], kti_all[:, hs]], axis=0)
        vh = v_all[:, hs]
        m = _bdot(ar, bk, ((1,), (1,)))
        arh = _bdot(ar, s0, ((1,), (1,)))
        a_ab = jnp.where(strict, m[:C, :C], 0.0)
        a_ak = jnp.where(strict, m[:C, C:], 0.0)
        a_rb = jnp.where(incl, m[C:, :C], 0.0)
        a_rk = jnp.where(incl, m[C:, C:], 0.0)
        u = arh[:C] + _bdot(a_ak, vh)
        lp = a_ab
        for q in range(n_dbl):
            u = u + _bdot(lp, u)
            if q < n_dbl - 1:
                lp = _bdot(lp, lp)
        uv = jnp.concatenate([u.astype(BF16), vh], axis=0)
        o = arh[C:] + _bdot(jnp.concatenate([a_rb, a_rk], axis=1), uv)
        bkp = jnp.concatenate([bp_all[:, hs], kp_all[:, hs]], axis=0)
        s_ref[h] = s0 * p_tot[:, hs] + _bdot(uv, bkp, ((0,), (0,)))
        outs.append(o)
    y_ref[0, 0] = jnp.concatenate(outs, axis=1)


def _rwkv_scan(lw, kt, bb, v, kk, r, t_ctx):
    _, B, S, _ = lw.shape
    C = RWKV_CHUNK
    nc = S // C
    nctx = t_ctx // C

    def cidx(d, c):
        back = jnp.where(c < nctx, nctx - 1 - c, nc - 1 + nctx - c)
        return jnp.where(d == 0, c, back)

    two_spec = pl.BlockSpec((1, 1, C, A_DIM), lambda b, d, c: (d, b, cidx(d, c), 0))
    one_spec = pl.BlockSpec((1, C, A_DIM), lambda b, d, c: (b, cidx(d, c), 0))
    return pl.pallas_call(
        _rwkv_scan_kernel,
        grid=(B, 2, nc),
        in_specs=[two_spec] * 3 + [one_spec] * 3,
        out_specs=two_spec,
        out_shape=jax.ShapeDtypeStruct((2, B, S, A_DIM), F32),
        scratch_shapes=[pltpu.VMEM((A_HEADS, A_HEAD, A_HEAD), F32)],
        compiler_params=_cparams(("arbitrary", "arbitrary", "arbitrary")),
        name="rwkv_scan",
    )(lw, kt, bb, v, kk, r)


def _even_merge_kernel(y2_ref, bonus_ref, g_ref, yb_ref, x_ref, mod_ref, lng_ref, lnb_ref, wo_ref, eh_ref, o_ref):
    y = y2_ref[0, 0] + y2_ref[1, 0]
    ehm = eh_ref[...] * (1.0 / A_HEAD)
    mu = jnp.dot(y, ehm, precision=HI, preferred_element_type=F32)
    yc = y - mu
    var = jnp.dot(yc * yc, ehm, precision=HI, preferred_element_type=F32)
    yn = yc * lax.rsqrt(var + RWKV_GN_EPS) * lng_ref[...] + lnb_ref[...]
    ya = (yn + bonus_ref[0]) * g_ref[0]
    out = _bdot(ya, wo_ref[0:A_DIM, :]) + _bdot(yb_ref[0], wo_ref[A_DIM:A_DIM + B_DIM, :])
    m = mod_ref[0, 0]
    o_ref[0] = x_ref[0] + m[2:3] * out


def _even_merge(y2, bonus, g, yb, X, mod, ln_g, ln_b, wo_bf16, ehead):
    B, S, D = X.shape
    one_spec = pl.BlockSpec((1, TB, A_DIM), lambda b, j: (b, j, 0))
    const2 = lambda b, j: (0, 0)
    return pl.pallas_call(
        _even_merge_kernel,
        grid=(B, S // TB),
        in_specs=[pl.BlockSpec((2, 1, TB, A_DIM), lambda b, j: (0, b, j, 0)), one_spec, one_spec, one_spec,
                  pl.BlockSpec((1, TB, D), lambda b, j: (b, j, 0)), _mod_spec(D),
                  pl.BlockSpec((1, A_DIM), const2), pl.BlockSpec((1, A_DIM), const2),
                  pl.BlockSpec((A_DIM + B_DIM, D), const2), pl.BlockSpec((A_DIM, A_DIM), const2)],
        out_specs=pl.BlockSpec((1, TB, D), lambda b, j: (b, j, 0)),
        out_shape=jax.ShapeDtypeStruct((B, S, D), F32),
        compiler_params=_cparams(("arbitrary", "arbitrary")),
        name="even_merge",
    )(y2, bonus, g, yb, X, mod, ln_g.reshape(1, -1), ln_b.reshape(1, -1), wo_bf16, ehead)


def _router_kernel(x_ref, mod_ref, g_ref, wr_ref, hf_ref, meta_ref, cnt_ref, carry_ref):
    first = jnp.logical_and(pl.program_id(0) == 0, pl.program_id(1) == 0)

    @pl.when(first)
    def _():
        carry_ref[...] = jnp.zeros_like(carry_ref)

    m = mod_ref[0, 0]
    hf = _modnorm(x_ref[0], g_ref[...], m[4:5], m[3:4])
    hf_ref[0] = hf
    logits = jnp.dot(hf, wr_ref[...], precision=HI, preferred_element_type=F32)
    lane = lax.broadcasted_iota(jnp.int32, logits.shape, 1)
    neg = -jnp.inf
    lg = jnp.where(lane < MOE_GROUPS, logits, neg)
    mg = jnp.max(lg, axis=1, keepdims=True)
    grp = jnp.min(jnp.where(lg == mg, lane, LANES), axis=1, keepdims=True)
    gw = 1.0 / jnp.sum(jnp.exp(lg - mg), axis=1, keepdims=True)
    el = lane - MOE_GROUPS
    in_grp = jnp.logical_and(el >= grp * MOE_PER_GROUP, el < (grp + 1) * MOE_PER_GROUP)
    le = jnp.where(in_grp, logits, neg)
    m0 = jnp.max(le, axis=1, keepdims=True)
    i0 = jnp.min(jnp.where(le == m0, lane, LANES), axis=1, keepdims=True)
    le1 = jnp.where(lane == i0, neg, le)
    m1 = jnp.max(le1, axis=1, keepdims=True)
    i1 = jnp.min(jnp.where(le1 == m1, lane, LANES), axis=1, keepdims=True)
    p1 = jnp.exp(m1 - m0)
    gate0 = gw / (1.0 + p1)
    gate1 = gw * p1 / (1.0 + p1)
    e0 = i0 - MOE_GROUPS
    e1 = i1 - MOE_GROUPS
    oh0 = (lane == e0).astype(F32)
    oh1 = (lane == e1).astype(F32)
    cnt = oh0 + oh1
    rr = lax.broadcasted_iota(jnp.int32, (TB, TB), 0)
    cc = lax.broadcasted_iota(jnp.int32, (TB, TB), 1)
    before = _bdot((cc < rr).astype(F32), cnt) + carry_ref[...]
    rank0 = jnp.sum(oh0 * before, axis=1, keepdims=True)
    rank1 = jnp.sum(oh1 * before, axis=1, keepdims=True)
    carry_ref[...] = carry_ref[...] + jnp.sum(cnt, axis=0, keepdims=True)
    cnt_ref[...] = carry_ref[...]
    meta = jnp.where(lane == 0, e0.astype(F32), 0.0)
    meta = jnp.where(lane == 1, e1.astype(F32), meta)
    meta = jnp.where(lane == 2, rank0, meta)
    meta = jnp.where(lane == 3, rank1, meta)
    meta = jnp.where(lane == 4, gate0, meta)
    meta = jnp.where(lane == 5, gate1, meta)
    meta_ref[0] = meta


def _router(X, mod, gain, w_router, all_lat):
    B, S, D = X.shape
    return pl.pallas_call(
        _router_kernel,
        grid=(B, S // TB),
        in_specs=[pl.BlockSpec((1, TB, D), lambda b, j: (b, j, 0)), _mod_spec(D, all_lat),
                  pl.BlockSpec((1, D), lambda b, j: (0, 0)), pl.BlockSpec((D, LANES), lambda b, j: (0, 0))],
        out_specs=[pl.BlockSpec((1, TB, D), lambda b, j: (b, j, 0)),
                   pl.BlockSpec((1, TB, LANES), lambda b, j: (b, j, 0)),
                   pl.BlockSpec((1, LANES), lambda b, j: (0, 0))],
        out_shape=[jax.ShapeDtypeStruct((B, S, D), F32), jax.ShapeDtypeStruct((B, S, LANES), F32),
                   jax.ShapeDtypeStruct((1, LANES), F32)],
        scratch_shapes=[pltpu.VMEM((1, LANES), F32)],
        compiler_params=_cparams(("arbitrary", "arbitrary")),
        name="moe_router",
    )(X, mod, gain.reshape(1, D), w_router)


def _routing_tables(meta, counts):
    n_tok = meta.shape[0] * meta.shape[1]
    mi = meta.reshape(n_tok, LANES)[:, :4].astype(jnp.int32)
    cnt = counts[0, :MOE_EXPERTS].astype(jnp.int32)
    padded = (cnt + EXPERT_BLOCK - 1) // EXPERT_BLOCK * EXPERT_BLOCK
    pad_end = jnp.cumsum(padded)
    pad_start = pad_end - padded
    dest = pad_start[mi[:, 0:2]] + mi[:, 2:4]
    n_blocks = -(-(n_tok * MOE_TOP_K) // EXPERT_BLOCK) + MOE_EXPERTS
    tok = jnp.arange(n_tok, dtype=jnp.int32)
    rows = jnp.zeros((n_blocks * EXPERT_BLOCK,), jnp.int32).at[dest[:, 0]].set(tok).at[dest[:, 1]].set(tok)
    block_e = jnp.minimum(jnp.searchsorted(pad_end, jnp.arange(n_blocks, dtype=jnp.int32) * EXPERT_BLOCK,
                                           side='right'), MOE_EXPERTS - 1).astype(jnp.int32)
    n_used = (pad_end[-1] // EXPERT_BLOCK).astype(jnp.int32).reshape(1)
    return dest.reshape(-1).astype(jnp.int32), rows, block_e, n_used, n_blocks


def _expert_kernel(be_ref, nu_ref, rows_ref, hf_ref, wg_ref, wu_ref, wd_ref, o_ref, xbuf, sem):
    j = pl.program_id(0)
    n_used = nu_ref[0]
    EB = EXPERT_BLOCK

    def row_copy(blk, slot, r):
        row = rows_ref[blk * EB + r]
        return pltpu.make_async_copy(hf_ref.at[pl.ds(row, 1), :], xbuf.at[slot, pl.ds(r, 1), :], sem.at[slot])

    def issue(blk, slot):
        def body(r, carry):
            row_copy(blk, slot, r).start()
            return carry
        lax.fori_loop(0, EB, body, 0)

    @pl.when(j == 0)
    def _():
        issue(0, 0)

    @pl.when(j + 1 < n_used)
    def _():
        issue(j + 1, (j + 1) % 2)

    @pl.when(j < n_used)
    def _():
        slot = j % 2

        def body(r, carry):
            row_copy(j, slot, r).wait()
            return carry
        lax.fori_loop(0, EB, body, 0)
        xb = xbuf[slot].astype(BF16)
        hg = jnp.dot(xb, wg_ref[0].astype(BF16), preferred_element_type=F32)
        hu = jnp.dot(xb, wu_ref[0].astype(BF16), preferred_element_type=F32)
        hm = (hg * jax.nn.sigmoid(hg) * hu).astype(BF16)
        o_ref[...] = jnp.dot(hm, wd_ref[0].astype(BF16), preferred_element_type=F32)

    @pl.when(j >= n_used)
    def _():
        o_ref[...] = jnp.zeros_like(o_ref)


def _experts(hf2d, rows, block_e, n_used, n_blocks, w_gate, w_up, w_down):
    N, D = hf2d.shape
    E, _, De = w_gate.shape
    EB = EXPERT_BLOCK
    grid_spec = pltpu.PrefetchScalarGridSpec(
        num_scalar_prefetch=3,
        grid=(n_blocks,),
        in_specs=[pl.BlockSpec(memory_space=pl.ANY),
                  pl.BlockSpec((1, D, De), lambda j, be, nu, rw: (be[j], 0, 0)),
                  pl.BlockSpec((1, D, De), lambda j, be, nu, rw: (be[j], 0, 0)),
                  pl.BlockSpec((1, De, D), lambda j, be, nu, rw: (be[j], 0, 0))],
        out_specs=pl.BlockSpec((EB, D), lambda j, be, nu, rw: (j, 0)),
        scratch_shapes=[pltpu.VMEM((2, EB, D), F32), pltpu.SemaphoreType.DMA((2,))],
    )
    return pl.pallas_call(
        _expert_kernel,
        grid_spec=grid_spec,
        out_shape=jax.ShapeDtypeStruct((n_blocks * EB, D), F32),
        compiler_params=_cparams(("arbitrary",)),
        name="moe_experts",
    )(block_e, n_used, rows, hf2d, w_gate, w_up, w_down)


def _combine_kernel(dest_ref, yb_ref, x_ref, meta_ref, mod_ref, *rest, nj, final):
    if final:
        fn_ref, o_ref, ybuf, sem = rest
    else:
        o_ref, ybuf, sem = rest
    b = pl.program_id(0)
    j = pl.program_id(1)
    step = b * nj + j
    n_steps = pl.num_programs(0) * nj

    def row_copy(st, slot, r, k):
        dst_row = dest_ref[2 * (st * TB + r) + k]
        return pltpu.make_async_copy(yb_ref.at[pl.ds(dst_row, 1), :], ybuf.at[slot, k, pl.ds(r, 1), :], sem.at[slot])

    def issue(st, slot):
        def body(r, carry):
            row_copy(st, slot, r, 0).start()
            row_copy(st, slot, r, 1).start()
            return carry
        lax.fori_loop(0, TB, body, 0)

    @pl.when(step == 0)
    def _():
        issue(0, 0)

    @pl.when(step + 1 < n_steps)
    def _():
        issue(step + 1, (step + 1) % 2)

    slot = step % 2

    def wbody(r, carry):
        row_copy(step, slot, r, 0).wait()
        row_copy(step, slot, r, 1).wait()
        return carry
    lax.fori_loop(0, TB, wbody, 0)
    meta = meta_ref[0]
    f = meta[:, 4:5] * ybuf[slot, 0] + meta[:, 5:6] * ybuf[slot, 1]
    m = mod_ref[0, 0]
    xn = x_ref[0] + m[5:6] * f
    if final:
        ms = jnp.mean(xn * xn, axis=-1, keepdims=True)
        xn = xn * lax.rsqrt(ms + NORM_EPS) * fn_ref[...]
    o_ref[0] = xn


def _combine(dest, yb, X, meta, mod, all_lat, final_gain=None):
    B, S, D = X.shape
    nj = S // TB
    final = final_gain is not None
    in_specs = [pl.BlockSpec(memory_space=pl.ANY),
                pl.BlockSpec((1, TB, D), lambda b, j, ds: (b, j, 0)),
                pl.BlockSpec((1, TB, LANES), lambda b, j, ds: (b, j, 0)),
                (pl.BlockSpec((1, 1, 6, D), lambda b, j, ds: (b, 1, 0, 0)) if all_lat else
                 pl.BlockSpec((1, 1, 6, D), lambda b, j, ds: (b, jnp.minimum(j, 1), 0, 0)))]
    args = [dest, yb, X, meta, mod]
    if final:
        in_specs.append(pl.BlockSpec((1, D), lambda b, j, ds: (0, 0)))
        args.append(final_gain.reshape(1, D))
    grid_spec = pltpu.PrefetchScalarGridSpec(
        num_scalar_prefetch=1,
        grid=(B, nj),
        in_specs=in_specs,
        out_specs=pl.BlockSpec((1, TB, D), lambda b, j, ds: (b, j, 0)),
        scratch_shapes=[pltpu.VMEM((2, 2, TB, D), F32), pltpu.SemaphoreType.DMA((2,))],
    )
    return pl.pallas_call(
        functools.partial(_combine_kernel, nj=nj, final=final),
        grid_spec=grid_spec,
        out_shape=jax.ShapeDtypeStruct((B, S, D), F32),
        compiler_params=_cparams(("arbitrary", "arbitrary")),
        name="moe_combine_final" if final else "moe_combine",
    )(*args)


def _moe(X, mod, gain, w_router, w_gate, w_up, w_down, all_lat, final_gain=None):
    B, S, D = X.shape
    hf, meta, counts = _router(X, mod, gain, w_router, all_lat)
    dest, rows, block_e, n_used, n_blocks = _routing_tables(meta, counts)
    yb = _experts(hf.reshape(B * S, D), rows, block_e, n_used, n_blocks, w_gate, w_up, w_down)
    return _combine(dest, yb, X, meta, mod, all_lat, final_gain)


def _diff_attn_kernel(q_ref, k_ref, v_ref, dl_ref, sg_ref, o_ref, *, lam_init):
    q = q_ref[0]
    k = k_ref[0]
    lane = lax.broadcasted_iota(jnp.int32, q.shape, 1)
    zero = jnp.zeros_like(q)
    sc = C_HEAD ** -0.5
    q1 = jnp.where(lane < C_HEAD, q, zero)
    q2 = jnp.where(lane >= C_HEAD, q, zero)
    dl = dl_ref[...]
    lam = (jnp.exp(jnp.sum(dl[0:1] * dl[1:2], axis=1, keepdims=True))
           - jnp.exp(jnp.sum(dl[2:3] * dl[3:4], axis=1, keepdims=True)) + lam_init)

    def soft(qq):
        s = lax.dot_general(qq, k, (((1,), (1,)), ((), ())), preferred_element_type=F32) * sc
        e = jnp.exp(s - jnp.max(s, axis=1, keepdims=True))
        return e * (1.0 / jnp.sum(e, axis=1, keepdims=True))

    a = soft(q1) - lam * soft(q2)
    o = jnp.dot(a.astype(BF16), v_ref[0], preferred_element_type=F32)
    ms = jnp.mean(o * o, axis=-1, keepdims=True)
    o_ref[0] = (o * lax.rsqrt(ms + DIFF_EPS) * sg_ref[...] * (1.0 - lam_init)).astype(o_ref.dtype)


def _diff_attn(P2, t_ctx, diff_lambda, diff_subln, lam_init):
    B, S, _ = P2.shape
    T = S - t_ctx
    qoff = t_ctx // ATT_TQ
    return pl.pallas_call(
        functools.partial(_diff_attn_kernel, lam_init=lam_init),
        grid=(B, C_HEADS, T // ATT_TQ),
        in_specs=[pl.BlockSpec((1, ATT_TQ, LANES), lambda b, h, q: (b, q + qoff, C_Q_OFF // LANES + h)),
                  pl.BlockSpec((1, S, LANES), lambda b, h, q: (b, 0, C_K_OFF // LANES + h)),
                  pl.BlockSpec((1, S, LANES), lambda b, h, q: (b, 0, C_V_OFF // LANES + h)),
                  pl.BlockSpec((4, C_HEAD), lambda b, h, q: (0, 0)),
                  pl.BlockSpec((1, C_VDIM), lambda b, h, q: (0, 0))],
        out_specs=pl.BlockSpec((1, ATT_TQ, LANES), lambda b, h, q: (b, q, h)),
        out_shape=jax.ShapeDtypeStruct((B, T, C_DIM), BF16),
        compiler_params=_cparams(("arbitrary", "arbitrary", "arbitrary")),
        name="diff_attn",
    )(P2, P2, P2, diff_lambda, diff_subln.reshape(1, -1))


def _retention_kernel(q_ref, k_ref, v0_ref, v1_ref, o_ref, r_ref):
    d = pl.program_id(1)
    c = pl.program_id(2)

    @pl.when(c == 0)
    def _():
        r_ref[...] = jnp.zeros_like(r_ref)

    C = RET_CHUNK
    ti =lax.broadcasted_iota(jnp.int32, (C, C), 0)
    si = lax.broadcasted_iota(jnp.int32, (C, C), 1)
    rel = ((ti - si) * (1 - 2 * d)).astype(F32)
    col = lax.broadcasted_iota(jnp.int32, (C, 1), 0)
    pos = jnp.where(d == 0, col, C - 1 - col).astype(F32)
    outs = []
    for h in range(D_HEADS):
        lg_f = math.log(1.0 - 2.0 ** (-5.0 - h))
        lg_b = math.log(1.0 - 2.0 ** (-5.0 - (D_HEADS - 1 - h)))
        lg = jnp.where(d == 0, jnp.full((1, 1), lg_f, F32), jnp.full((1, 1), lg_b, F32))
        q = q_ref[0, :, h * D_KDIM:(h + 1) * D_KDIM]
        k = k_ref[0, :, h * D_KDIM:(h + 1) * D_KDIM]
        vref = v0_ref if h < 2 else v1_ref
        v = vref[0, :, (h % 2) * D_VDIM:(h % 2 + 1) * D_VDIM]
        inner_dec = jnp.where(rel >= 0, jnp.exp(lg * jnp.maximum(rel, 0.0)), 0.0)
        q_dec = jnp.exp(lg * (pos + 1.0))
        k_dec = jnp.exp(lg * (C - 1.0 - pos))
        r0 = r_ref[h]
        inner = lax.dot_general(q, k, (((1,), (1,)), ((), ())), preferred_element_type=F32) * inner_dec
        o = _bdot(inner, v) + _bdot(q, r0) * q_dec
        r_ref[h] = r0 * jnp.exp(lg * C) + _bdot(k.astype(F32) * k_dec, v, ((0,), (0,)))
        outs.append(o)
    o_ref[0, 0] = jnp.concatenate(outs, axis=1)


def _retention(P2, t_ctx):
    B, S, _ = P2.shape
    C = RET_CHUNK
    nc = S // C
    nctx = t_ctx // C

    def cidx(d, c):
        back = jnp.where(c < nctx, nctx - 1 - c, nc - 1 + nctx - c)
        return jnp.where(d == 0, c, back)

    w = D_HEADS * D_KDIM
    return pl.pallas_call(
        _retention_kernel,
        grid=(B, 2, nc),
        in_specs=[pl.BlockSpec((1, C, w), lambda b, d, c: (b, cidx(d, c), D_Q_OFF // w)),
                  pl.BlockSpec((1, C, w), lambda b, d, c: (b, cidx(d, c), D_K_OFF // w)),
                  pl.BlockSpec((1, C, w), lambda b, d, c: (b, cidx(d, c), D_V_OFF // w)),
                  pl.BlockSpec((1, C, w), lambda b, d, c: (b, cidx(d, c), D_V_OFF // w + 1))],
        out_specs=pl.BlockSpec((1, 1, C, D_DIM), lambda b, d, c: (d, b, cidx(d, c), 0)),
        out_shape=jax.ShapeDtypeStruct((2, B, S, D_DIM), F32),
        scratch_shapes=[pltpu.VMEM((D_HEADS, D_KDIM, D_VDIM), F32)],
        compiler_params=_cparams(("arbitrary", "arbitrary", "arbitrary")),
        name="retention",
    )(P2, P2, P2, P2)


def _odd_merge_kernel(c_ref, o2_ref, gate_ref, x_ref, mod_ref, rn_ref, wo_ref, o_ref):
    o = o2_ref[0, 0] + o2_ref[1, 0]
    parts = []
    for h in range(D_HEADS):
        oh = o[:, h * D_VDIM:(h + 1) * D_VDIM]
        ms = jnp.mean(oh * oh, axis=-1, keepdims=True)
        parts.append(oh * lax.rsqrt(ms + NORM_EPS))
    gt = gate_ref[0].astype(F32)
    dl = jnp.concatenate(parts, axis=1) * rn_ref[...] * (gt * jax.nn.sigmoid(gt))
    out = (jnp.dot(c_ref[0], wo_ref[0:C_DIM, :], preferred_element_type=F32)
           + _bdot(dl, wo_ref[C_DIM:C_DIM + D_DIM, :]))
    m = mod_ref[0, 0]
    o_ref[0] = x_ref[0] + m[2:3] * out


def _odd_merge(c_lat, o2, P2, X, mod, ret_norm, wo_bf16, t_ctx):
    B, S, D = X.shape
    T = S - t_ctx
    off = t_ctx // TB
    return pl.pallas_call(
        _odd_merge_kernel,
        grid=(B, T // TB),
        in_specs=[pl.BlockSpec((1, TB, C_DIM), lambda b, j: (b, j, 0)),
                  pl.BlockSpec((2, 1, TB, D_DIM), lambda b, j: (0, b, j + off, 0)),
                  pl.BlockSpec((1, TB, D_DIM), lambda b, j: (b, j + off, D_G_OFF // D_DIM)),
                  pl.BlockSpec((1, TB, D), lambda b, j: (b, j + off, 0)),
                  _mod_spec(D, all_lat=True),
                  pl.BlockSpec((1, D_DIM), lambda b, j: (0, 0)),
                  pl.BlockSpec((C_DIM + D_DIM, D), lambda b, j: (0, 0))],
        out_specs=pl.BlockSpec((1, TB, D), lambda b, j: (b, j, 0)),
        out_shape=jax.ShapeDtypeStruct((B, T, D), F32),
        compiler_params=_cparams(("arbitrary", "arbitrary")),
        name="odd_merge",
    )(c_lat, o2, P2, X, mod, ret_norm.reshape(1, -1), wo_bf16)


def _rope_tables(t_ctx, T):
    half = C_HEAD // 4
    inv = ROPE_BASE ** (-jnp.arange(half, dtype=F32) / half)
    t = jnp.arange(T, dtype=jnp.int32)
    ar = (t // GRID_W).astype(F32)[:, None] * inv[None, :]
    ac = (t % GRID_W).astype(F32)[:, None] * inv[None, :]
    z = jnp.zeros_like(ar)
    cos = jnp.concatenate([jnp.cos(ar), jnp.cos(ar), jnp.cos(ac), jnp.cos(ac)], axis=1)
    s1 = jnp.concatenate([-jnp.sin(ar), z, -jnp.sin(ac), z], axis=1)
    s2 = jnp.concatenate([z, jnp.sin(ar), z, jnp.sin(ac)], axis=1)

    def full(tab, ctx_val):
        tab = jnp.concatenate([jnp.full((t_ctx, C_HEAD), ctx_val, F32), tab], axis=0)
        return jnp.tile(tab, (1, LANES // C_HEAD))

    return full(cos, 1.0), full(s1, 0.0), full(s2, 0.0)


def _head_indicator(n, width):
    i = jnp.arange(n) // width
    return (i[:, None] == i[None, :]).astype(F32)


def kernel(x, c, ctx, c_ctx, ada_w, ada_b, norm_mix, norm_ffn, ev_w_in, ev_w_out, rwkv_shift, rwkv_w0, rwkv_w_up, rwkv_a0, rwkv_a_up, rwkv_g_up, rwkv_k_k, rwkv_k_a, rwkv_r_k, rwkv_ln_g, rwkv_ln_b, pool_w, pool_scale, od_w_in, od_w_out, diff_lambda, diff_subln, ret_norm, moe_router_group, moe_router_expert, moe_w_gate, moe_w_up, moe_w_down, final_norm):
    B, T, D = x.shape
    t_ctx = ctx.shape[1]
    assert ada_w.shape[0] == 2 and t_ctx == TB and T % TB == 0 and T % GRID_W == 0 and B < 16
    X = jnp.concatenate([ctx, x], axis=1)

    cpad = jnp.zeros((16, D), F32).at[:B].set(c).at[B].set(c_ctx)
    mods = _ada(cpad, ada_w, ada_b)

    def layer_mod(i):
        lat = mods[i, :B]
        cm = jnp.broadcast_to(mods[i, B][None], lat.shape)
        return jnp.stack([cm, lat], axis=1).reshape(B, 2, 6, D)

    def router_w(i):
        w = jnp.concatenate([moe_router_group[i], moe_router_expert[i]], axis=1)
        return jnp.pad(w, ((0, 0), (0, LANES - w.shape[1])))

    ehead = _head_indicator(A_DIM, A_HEAD)

    mod0 = layer_mod(0)
    P = _proj(X, mod0, norm_mix[0], ev_w_in[0].astype(BF16), F32)
    zl = jnp.zeros((LORA_W, A_DIM), F32)
    lora_blocks = []
    for d in range(2):
        lora_blocks.append(jnp.concatenate([rwkv_w_up[0, d], zl], axis=0))
        lora_blocks.append(jnp.concatenate([zl, rwkv_a_up[0, d]], axis=0))
    lora_w = jnp.concatenate(lora_blocks, axis=1).astype(BF16)
    wa0 = jnp.concatenate([rwkv_w0[0, 0], rwkv_a0[0, 0], rwkv_w0[0, 1], rwkv_a0[0, 1]]).reshape(1, -1)
    lw, kt, bb, v, kk, r, bonus, g, yb = _even_prep(
        P, t_ctx, rwkv_shift[0], wa0, lora_w, rwkv_g_up[0].astype(BF16), rwkv_k_k[0], rwkv_k_a[0], rwkv_r_k[0],
        pool_w[0].astype(BF16), pool_scale[0], ehead)
    y2 = _rwkv_scan(lw, kt, bb, v, kk, r, t_ctx)
    X = _even_merge(y2, bonus, g, yb, X, mod0, rwkv_ln_g[0], rwkv_ln_b[0], ev_w_out[0].astype(BF16), ehead)
    X = _moe(X, mod0, norm_ffn[0], router_w(0), moe_w_gate[0], moe_w_up[0], moe_w_down[0], all_lat=False)

    mod1 = layer_mod(1)
    lam_init = 0.8 - 0.6 * math.exp(-0.3 * 1)
    P2 = _proj(X, mod1, norm_mix[1], od_w_in[0].astype(BF16), BF16, rope=_rope_tables(t_ctx, T))
    c_lat = _diff_attn(P2, t_ctx, diff_lambda[0], diff_subln[0], lam_init)
    o2 = _retention(P2, t_ctx)
    XL = _odd_merge(c_lat, o2, P2, X, mod1, ret_norm[0], od_w_out[0].astype(BF16), t_ctx)
    return _moe(XL, mod1, norm_ffn[1], router_w(1), moe_w_gate[1], moe_w_up[1], moe_w_down[1], all_lat=True,
                final_gain=final_norm)
```

```python
import functools
import math

import jax
import jax.numpy as jnp
from jax import lax
from jax.experimental import pallas as pl
from jax.experimental.pallas import tpu as pltpu

F32 = jnp.float32
BF16 = jnp.bfloat16
HI = lax.Precision.HIGHEST

GRID_W = 64
NORM_EPS = 1e-6
ROPE_BASE = 10000.0
A_HEADS, A_HEAD = 8, 64
A_DIM = A_HEADS * A_HEAD
LORA_W, LORA_A, LORA_G = 64, 64, 128
W_DECAY_SCALE = 0.606531
RWKV_GN_EPS = 64e-5
POOL_WINDOWS = (2, 4, 8, 16)
POOL_GROUP = 128
B_DIM = POOL_GROUP * 4
C_HEADS, C_HEAD = 4, 64
C_VDIM = 2 * C_HEAD
C_DIM = C_HEADS * C_VDIM
DIFF_EPS = 1e-5
D_HEADS, D_KDIM, D_VDIM = 4, 64, 128
D_DIM = D_HEADS * D_VDIM
A_K_OFF, A_V_OFF = 0, A_DIM
W_LORA_OFF = 2 * A_DIM
A_LORA_OFF = W_LORA_OFF + LORA_W
A_R_OFF = A_LORA_OFF + LORA_A
G_LORA_OFF = A_R_OFF + A_DIM
POOL_OFF = G_LORA_OFF + LORA_G
EVEN_IN_COLS = POOL_OFF + B_DIM
C_K_OFF = 0
C_V_OFF = C_HEADS * 2 * C_HEAD
D_K_OFF = C_V_OFF + C_DIM
D_V_OFF = D_K_OFF + D_HEADS * D_KDIM
C_Q_OFF = D_V_OFF + D_DIM
D_Q_OFF = C_Q_OFF + C_HEADS * 2 * C_HEAD
D_G_OFF = D_Q_OFF + D_HEADS * D_KDIM
ODD_IN_COLS = D_G_OFF + D_DIM
MOE_GROUPS, MOE_PER_GROUP = 4, 8
MOE_EXPERTS = MOE_GROUPS * MOE_PER_GROUP
MOE_TOP_K = 2
EXPERT_BLOCK = 256

LANES = 128
SUBLANES = 8
TB = 256
RWKV_CHUNK = 64
RET_CHUNK = 128
ATT_TQ = 256
VMEM_LIMIT = 56 * 1024 * 1024
DMA_UNROLL = 8


def _cparams(sem):
    return pltpu.CompilerParams(dimension_semantics=sem, vmem_limit_bytes=VMEM_LIMIT)


def _modnorm(x, gain, scale, shift, eps=NORM_EPS):
    ms = jnp.mean(x * x, axis=-1, keepdims=True)
    return x * lax.rsqrt(ms + eps) * gain * (1.0 + scale) + shift


def _bdot(a, b, dims=((1,), (0,))):
    return lax.dot_general(a.astype(BF16), b.astype(BF16), (dims, ((), ())), preferred_element_type=F32)


def _ada_kernel(c_ref, w_ref, b_ref, o_ref):
    c = c_ref[...]
    s = c * jax.nn.sigmoid(c)
    o_ref[0] = jnp.dot(s, w_ref[0], precision=HI, preferred_element_type=F32) + b_ref[0]


def _ada(cpad, ada_w, ada_b):
    L, D, N = ada_w.shape
    tn = 1536
    return pl.pallas_call(
        _ada_kernel,
        grid=(L, N // tn),
        in_specs=[pl.BlockSpec((16, D), lambda l, n: (0, 0)),
                  pl.BlockSpec((1, D, tn), lambda l, n: (l, 0, n)),
                  pl.BlockSpec((1, 1, tn), lambda l, n: (l, 0, n))],
        out_specs=pl.BlockSpec((1, 16, tn), lambda l, n: (l, 0, n)),
        out_shape=jax.ShapeDtypeStruct((L, 16, N), F32),
        compiler_params=_cparams(("arbitrary", "arbitrary")),
        name="ada_mod",
    )(cpad, ada_w, ada_b.reshape(L, 1, N))


def _mod_spec(D, all_lat=False):
    if all_lat:
        return pl.BlockSpec((1, 1, 6, D), lambda b, j: (b, 1, 0, 0))
    return pl.BlockSpec((1, 1, 6, D), lambda b, j: (b, jnp.minimum(j, 1), 0, 0))


def _proj_kernel(x_ref, mod_ref, g_ref, w_ref, *rest, n_out, rope_chunks, kscale_chunks, out_dtype):
    if rope_chunks:
        cos_ref, s1_ref, s2_ref, o_ref = rest
    else:
        (o_ref,) = rest
    m = mod_ref[0, 0]
    h = _modnorm(x_ref[0], g_ref[...], m[1:2], m[0:1]).astype(BF16)
    cw = 512
    for n0 in range(0, n_out, cw):
        p = jnp.dot(h, w_ref[:, n0:n0 + cw], preferred_element_type=F32)
        if not rope_chunks:
            o_ref[0, :, n0:n0 + cw] = p.astype(out_dtype)
            continue
        for q in range(cw // LANES):
            gi = n0 // LANES + q
            sub = p[:, q * LANES:(q + 1) * LANES]
            if gi in rope_chunks:
                sub = (sub * cos_ref[...] + pltpu.roll(sub, LANES - 16, 1) * s1_ref[...]
                       + pltpu.roll(sub, 16, 1) * s2_ref[...])
            if gi in kscale_chunks:
                sub = sub * (D_KDIM ** -0.5)
            o_ref[0, :, n0 + q * LANES:n0 + (q + 1) * LANES] = sub.astype(out_dtype)


def _proj(X, mod, gain, w_bf16, out_dtype, rope=None):
    B, S, D = X.shape
    N = w_bf16.shape[1]
    in_specs = [pl.BlockSpec((1, TB, D), lambda b, j: (b, j, 0)),
                _mod_spec(D),
                pl.BlockSpec((1, D), lambda b, j: (0, 0)),
                pl.BlockSpec((D, N), lambda b, j: (0, 0))]
    args = [X, mod, gain.reshape(1, D), w_bf16]
    rope_chunks, kscale_chunks = (), ()
    if rope is not None:
        cos, s1, s2 = rope
        in_specs += [pl.BlockSpec((TB, LANES), lambda b, j: (j, 0))] * 3
        args += [cos, s1, s2]
        ranges = [(C_K_OFF, C_V_OFF), (D_K_OFF, D_V_OFF), (C_Q_OFF, D_G_OFF)]
        rope_chunks = tuple(g for lo, hi in ranges for g in range(lo // LANES, hi // LANES))
        kscale_chunks = tuple(range(D_K_OFF // LANES, D_V_OFF // LANES))
    return pl.pallas_call(
        functools.partial(_proj_kernel, n_out=N, rope_chunks=rope_chunks, kscale_chunks=kscale_chunks,
                          out_dtype=out_dtype),
        grid=(B, S // TB),
        in_specs=in_specs,
        out_specs=pl.BlockSpec((1, TB, N), lambda b, j: (b, j, 0)),
        out_shape=jax.ShapeDtypeStruct((B, S, N), out_dtype),
        compiler_params=_cparams(("arbitrary", "arbitrary")),
        name="proj_rope" if rope is not None else "proj",
    )(*args)


def _even_prep_kernel(p_ref, pp_ref, pn_ref, sh_ref, wa0_ref, lora_ref, gup_ref, kkg_ref, ka_ref, rk_ref,
                      pw_ref, ps_ref, eh_ref,
                      lw_ref, kt_ref, bb_ref, v_ref, kkn_ref, r_ref, bonus_ref, g_ref, yb_ref,
                      *, nblk, t_ctx, t_lat):
    j = pl.program_id(1)
    prev_ok = j >= 2
    next_ok = jnp.logical_and(j >= 1, j < nblk - 1)
    n_ext = TB + 2 * SUBLANES

    def ext(c0, c1):
        pv = jnp.where(prev_ok, pp_ref[0, :, c0:c1], 0.0)
        nx = jnp.where(next_ok, pn_ref[0, :, c0:c1], 0.0)
        return jnp.concatenate([pv, p_ref[0, :, c0:c1], nx], axis=0)

    def shift3(c0, c1, k0):
        e = ext(c0, c1)
        kern = sh_ref[:, k0:k0 + (c1 - c0)]
        y = kern[0:1] * pltpu.roll(e, 1, 0) + kern[1:2] * e + kern[2:3] * pltpu.roll(e, n_ext - 1, 0)
        return y[SUBLANES:SUBLANES + TB]

    k = shift3(A_K_OFF, A_K_OFF + A_DIM, 0)
    v = shift3(A_V_OFF, A_V_OFF + A_DIM, A_DIM)
    r = shift3(A_R_OFF, A_R_OFF + A_DIM, 2 * A_DIM)

    c = p_ref[0, :, W_LORA_OFF:W_LORA_OFF + LANES]
    lane = lax.broadcasted_iota(jnp.int32, c.shape, 1)
    lin = jnp.where(lane < LORA_W, jnp.tanh(c), c)
    z = _bdot(lin, lora_ref[...]) + wa0_ref[...]

    eh = eh_ref[...]
    kk0 = k * kkg_ref[...]
    ss = jnp.dot(kk0 * kk0, eh, precision=HI, preferred_element_type=F32)
    kk = kk0 * lax.rsqrt(jnp.maximum(ss, 1e-12))
    kt_sum = None
    for d in range(2):
        zw = z[:, (2 * d) * A_DIM:(2 * d + 1) * A_DIM]
        za = z[:, (2 * d + 1) * A_DIM:(2 * d + 2) * A_DIM]
        a = jax.nn.sigmoid(za)
        kt = k * (1.0 + (a - 1.0) * ka_ref[...])
        lw_ref[d, 0] = -W_DECAY_SCALE * jax.nn.sigmoid(zw)
        kt_ref[d, 0] = kt
        bb_ref[d, 0] = a * kk
        kt_sum = kt if kt_sum is None else kt_sum + kt
    v_ref[0] = v
    kkn_ref[0] = kk
    r_ref[0] = r
    bonus_ref[0] = jnp.dot(r * kt_sum * rk_ref[...], eh, precision=HI, preferred_element_type=F32) * v
    gl = p_ref[0, :, G_LORA_OFF:G_LORA_OFF + LORA_G]
    g_ref[0] = _bdot(jax.nn.sigmoid(gl), gup_ref[...])

    row = lax.broadcasted_iota(jnp.int32, (TB, POOL_GROUP), 0)
    tpos = row + jnp.maximum(j - 1, 0) * TB
    tseg = jnp.where(j == 0, t_ctx, t_lat)
    for gi, win in enumerate(POOL_WINDOWS):
        c0 = POOL_OFF + gi * POOL_GROUP
        e = ext(c0, c0 + POOL_GROUP)
        s = e + pltpu.roll(e, 1, 0)
        sh = 1
        while 2 * sh < win:
            s = pltpu.roll(s, sh, 0) + pltpu.roll(s, n_ext - sh, 0)
            sh *= 2
        s = s[SUBLANES:SUBLANES + TB]
        lo = jnp.clip(tpos - win // 2, 0, tseg)
        hi = jnp.clip(tpos - win // 2 + win, 0, tseg)
        mean = s / (hi - lo).astype(F32)
        dd = mean - p_ref[0, :, c0:c0 + POOL_GROUP]
        y = _bdot(dd, pw_ref[gi])
        yb_ref[0, :, gi * POOL_GROUP:(gi + 1) * POOL_GROUP] = y * ps_ref[:, gi * POOL_GROUP:(gi + 1) * POOL_GROUP]


def _even_prep(P, t_ctx, shift_k, wa0, lora_w, g_up, k_k, k_a, r_k, pool_w, pool_scale, ehead):
    B, S, N = P.shape
    nblk = S // TB
    hb = TB // SUBLANES
    const2 = lambda b, j: (0, 0)
    in_specs = [
        pl.BlockSpec((1, TB, N), lambda b, j: (b, j, 0)),
        pl.BlockSpec((1, SUBLANES, N), lambda b, j: (b, jnp.maximum(j * hb - 1, 0), 0)),
        pl.BlockSpec((1, SUBLANES, N), lambda b, j: (b, jnp.minimum((j + 1) * hb, S // SUBLANES - 1), 0)),
        pl.BlockSpec((3, 3 * A_DIM), const2),
        pl.BlockSpec((1, 4 * A_DIM), const2),
        pl.BlockSpec((LANES, 4 * A_DIM), const2),
        pl.BlockSpec((LORA_G, A_DIM), const2),
        pl.BlockSpec((1, A_DIM), const2),
        pl.BlockSpec((1, A_DIM), const2),
        pl.BlockSpec((1, A_DIM), const2),
        pl.BlockSpec((4, POOL_GROUP, POOL_GROUP), lambda b, j: (0, 0, 0)),
        pl.BlockSpec((1, B_DIM), const2),
        pl.BlockSpec((A_DIM, A_DIM), const2),
    ]
    two = jax.ShapeDtypeStruct((2, B, S, A_DIM), F32)
    one = jax.ShapeDtypeStruct((B, S, A_DIM), F32)
    two_spec = pl.BlockSpec((2, 1, TB, A_DIM), lambda b, j: (0, b, j, 0))
    one_spec = pl.BlockSpec((1, TB, A_DIM), lambda b, j: (b, j, 0))
    return pl.pallas_call(
        functools.partial(_even_prep_kernel, nblk=nblk, t_ctx=t_ctx, t_lat=S - t_ctx),
        grid=(B, nblk),
        in_specs=in_specs,
        out_specs=[two_spec] * 3 + [one_spec] * 6,
        out_shape=[two] * 3 + [one] * 6,
        compiler_params=_cparams(("arbitrary", "arbitrary")),
        name="even_prep",
    )(P, P, P, shift_k, wa0, lora_w, g_up, k_k.reshape(1, -1), k_a.reshape(1, -1), r_k.reshape(1, -1),
      pool_w, pool_scale.reshape(1, -1), ehead)


def _rwkv_scan_kernel(lw_ref, kt_ref, bb_ref, v_ref, kk_ref, r_ref, y_ref, s_ref):
    d = pl.program_id(1)
    c = pl.program_id(2)

    @pl.when(c == 0)
    def _():
        s_ref[...] = jnp.zeros_like(s_ref)

    C = RWKV_CHUNK
    tt = lax.broadcasted_iota(jnp.int32, (C, C), 0)
    ii = lax.broadcasted_iota(jnp.int32, (C, C), 1)
    rel = (tt - ii) * (1 - 2 * d)
    strict = rel > 0
    incl = rel >= 0
    lw = lw_ref[0, 0]
    cum = jnp.dot(incl.astype(F32), lw, precision=HI, preferred_element_type=F32)
    tot = jnp.sum(lw, axis=0, keepdims=True)
    p_inc = jnp.exp(cum)
    p_exc = jnp.exp(cum - lw)
    p_inv = jnp.exp(-cum)
    p_rest = jnp.exp(tot - cum)
    p_tot = jnp.exp(tot)
    bb = bb_ref[0, 0]
    kt = kt_ref[0, 0]
    vv = v_ref[0]
    at_all = (-kk_ref[0] * p_exc).astype(BF16)
    rt_all = (r_ref[0] * p_inc).astype(BF16)
    bt_all = (bb * p_inv).astype(BF16)
    kti_all = (kt * p_inv).astype(BF16)
    bp_all = (bb * p_rest).astype(BF16)
    kp_all = (kt * p_rest).astype(BF16)
    v_all = vv.astype(BF16)
    n_dbl = int(math.log2(C))
    heads = range(A_HEADS)
    hsl = [slice(h * A_HEAD, (h + 1) * A_HEAD) for h in heads]
    s0 = [s_ref[h] for h in heads]
    ar = [jnp.concatenate([at_all[:, s], rt_all[:, s]], axis=0) for s in hsl]
    bk = [jnp.concatenate([bt_all[:, s], kti_all[:, s]], axis=0) for s in hsl]
    vh = [v_all[:, s] for s in hsl]
    m = [_bdot(ar[h], bk[h], ((1,), (1,))) for h in heads]
    arh = [_bdot(ar[h], s0[h], ((1,), (1,))) for h in heads]
    lp = [jnp.where(strict, m[h][:C, :C], 0.0) for h in heads]
    a_ak = [jnp.where(strict, m[h][:C, C:], 0.0) for h in heads]
    t2 = lax.broadcasted_iota(jnp.int32, (C, 2 * C), 0)
    i2 = lax.broadcasted_iota(jnp.int32, (C, 2 * C), 1) & (C - 1)
    incl2 = (t2 - i2) * (1 - 2 * d) >= 0
    a_r = [jnp.where(incl2, m[h][C:, :], 0.0) for h in heads]
    u = [arh[h][:C] + _bdot(a_ak[h], vh[h]) for h in heads]
    for q in range(n_dbl):
        u = [u[h] + _bdot(lp[h], u[h]) for h in heads]
        if q < n_dbl - 1:
            lp = [_bdot(lp[h], lp[h]) for h in heads]
    uv = [jnp.concatenate([u[h].astype(BF16), vh[h]], axis=0) for h in heads]
    outs = [arh[h][C:] + _bdot(a_r[h], uv[h]) for h in heads]
    bkp = [jnp.concatenate([bp_all[:, s], kp_all[:, s]], axis=0) for s in hsl]
    s_new = [s0[h] * p_tot[:, hsl[h]] + _bdot(uv[h], bkp[h], ((0,), (0,))) for h in heads]
    for h in heads:
        s_ref[h] = s_new[h]
    y_ref[0, 0] = jnp.concatenate(outs, axis=1)


def _rwkv_scan(lw, kt, bb, v, kk, r, t_ctx):
    _, B, S, _ = lw.shape
    C = RWKV_CHUNK
    nc = S // C
    nctx = t_ctx // C

    def cidx(d, c):
        back = jnp.where(c < nctx, nctx - 1 - c, nc - 1 + nctx - c)
        return jnp.where(d == 0, c, back)

    two_spec = pl.BlockSpec((1, 1, C, A_DIM), lambda b, d, c: (d, b, cidx(d, c), 0))
    one_spec = pl.BlockSpec((1, C, A_DIM), lambda b, d, c: (b, cidx(d, c), 0))
    return pl.pallas_call(
        _rwkv_scan_kernel,
        grid=(B, 2, nc),
        in_specs=[two_spec] * 3 + [one_spec] * 3,
        out_specs=two_spec,
        out_shape=jax.ShapeDtypeStruct((2, B, S, A_DIM), F32),
        scratch_shapes=[pltpu.VMEM((A_HEADS, A_HEAD, A_HEAD), F32)],
        compiler_params=_cparams(("arbitrary", "arbitrary", "arbitrary")),
        name="rwkv_scan",
    )(lw, kt, bb, v, kk, r)


def _even_merge_kernel(y2_ref, bonus_ref, g_ref, yb_ref, x_ref, mod_ref, lng_ref, lnb_ref, wo_ref, eh_ref, o_ref):
    y = y2_ref[0, 0] + y2_ref[1, 0]
    ehm = eh_ref[...] * (1.0 / A_HEAD)
    mu = jnp.dot(y, ehm, precision=HI, preferred_element_type=F32)
    yc = y - mu
    var = jnp.dot(yc * yc, ehm, precision=HI, preferred_element_type=F32)
    yn = yc * lax.rsqrt(var + RWKV_GN_EPS) * lng_ref[...] + lnb_ref[...]
    ya = (yn + bonus_ref[0]) * g_ref[0]
    out = _bdot(ya, wo_ref[0:A_DIM, :]) + _bdot(yb_ref[0], wo_ref[A_DIM:A_DIM + B_DIM, :])
    m = mod_ref[0, 0]
    o_ref[0] = x_ref[0] + m[2:3] * out


def _even_merge(y2, bonus, g, yb, X, mod, ln_g, ln_b, wo_bf16, ehead):
    B, S, D = X.shape
    one_spec = pl.BlockSpec((1, TB, A_DIM), lambda b, j: (b, j, 0))
    const2 = lambda b, j: (0, 0)
    return pl.pallas_call(
        _even_merge_kernel,
        grid=(B, S // TB),
        in_specs=[pl.BlockSpec((2, 1, TB, A_DIM), lambda b, j: (0, b, j, 0)), one_spec, one_spec, one_spec,
                  pl.BlockSpec((1, TB, D), lambda b, j: (b, j, 0)), _mod_spec(D),
                  pl.BlockSpec((1, A_DIM), const2), pl.BlockSpec((1, A_DIM), const2),
                  pl.BlockSpec((A_DIM + B_DIM, D), const2), pl.BlockSpec((A_DIM, A_DIM), const2)],
        out_specs=pl.BlockSpec((1, TB, D), lambda b, j: (b, j, 0)),
        out_shape=jax.ShapeDtypeStruct((B, S, D), F32),
        compiler_params=_cparams(("arbitrary", "arbitrary")),
        name="even_merge",
    )(y2, bonus, g, yb, X, mod, ln_g.reshape(1, -1), ln_b.reshape(1, -1), wo_bf16, ehead)


def _router_kernel(x_ref, mod_ref, g_ref, wr_ref, hf_ref, meta_ref, metat_ref, cnt_ref, carry_ref):
    first = jnp.logical_and(pl.program_id(0) == 0, pl.program_id(1) == 0)

    @pl.when(first)
    def _():
        carry_ref[...] = jnp.zeros_like(carry_ref)

    m = mod_ref[0, 0]
    hf = _modnorm(x_ref[0], g_ref[...], m[4:5], m[3:4])
    hf_ref[0] = hf
    logits = jnp.dot(hf, wr_ref[...], precision=HI, preferred_element_type=F32)
    lane = lax.broadcasted_iota(jnp.int32, logits.shape, 1)
    neg = -jnp.inf
    lg = jnp.where(lane < MOE_GROUPS, logits, neg)
    mg = jnp.max(lg, axis=1, keepdims=True)
    grp = jnp.min(jnp.where(lg == mg, lane, LANES), axis=1, keepdims=True)
    gw = 1.0 / jnp.sum(jnp.exp(lg - mg), axis=1, keepdims=True)
    el = lane - MOE_GROUPS
    in_grp = jnp.logical_and(el >= grp * MOE_PER_GROUP, el < (grp + 1) * MOE_PER_GROUP)
    le = jnp.where(in_grp, logits, neg)
    m0 = jnp.max(le, axis=1, keepdims=True)
    i0 = jnp.min(jnp.where(le == m0, lane, LANES), axis=1, keepdims=True)
    le1 = jnp.where(lane == i0, neg, le)
    m1 = jnp.max(le1, axis=1, keepdims=True)
    i1 = jnp.min(jnp.where(le1 == m1, lane, LANES), axis=1, keepdims=True)
    p1 = jnp.exp(m1 - m0)
    gate0 = gw / (1.0 + p1)
    gate1 = gw * p1 / (1.0 + p1)
    e0 = i0 - MOE_GROUPS
    e1 = i1 - MOE_GROUPS
    oh0 = (lane == e0).astype(F32)
    oh1 = (lane == e1).astype(F32)
    cnt = oh0 + oh1
    rr = lax.broadcasted_iota(jnp.int32, (TB, TB), 0)
    cc = lax.broadcasted_iota(jnp.int32, (TB, TB), 1)
    before = _bdot((cc < rr).astype(F32), cnt) + carry_ref[...]
    rank0 = jnp.sum(oh0 * before, axis=1, keepdims=True)
    rank1 = jnp.sum(oh1 * before, axis=1, keepdims=True)
    carry_ref[...] = carry_ref[...] + jnp.sum(cnt, axis=0, keepdims=True)
    cnt_ref[...] = carry_ref[...]
    meta = jnp.where(lane == 0, e0.astype(F32), 0.0)
    meta = jnp.where(lane == 1, e1.astype(F32), meta)
    meta = jnp.where(lane == 2, rank0, meta)
    meta = jnp.where(lane == 3, rank1, meta)
    meta = jnp.where(lane == 4, gate0, meta)
    meta = jnp.where(lane == 5, gate1, meta)
    meta_ref[0] = meta
    metat_ref[0, 0] = meta.T[0:SUBLANES]


def _router(X, mod, gain, w_router, all_lat):
    B, S, D = X.shape
    return pl.pallas_call(
        _router_kernel,
        grid=(B, S // TB),
        in_specs=[pl.BlockSpec((1, TB, D), lambda b, j: (b, j, 0)), _mod_spec(D, all_lat),
                  pl.BlockSpec((1, D), lambda b, j: (0, 0)), pl.BlockSpec((D, LANES), lambda b, j: (0, 0))],
        out_specs=[pl.BlockSpec((1, TB, D), lambda b, j: (b, j, 0)),
                   pl.BlockSpec((1, TB, LANES), lambda b, j: (b, j, 0)),
                   pl.BlockSpec((1, 1, SUBLANES, TB), lambda b, j: (b, j, 0, 0)),
                   pl.BlockSpec((1, LANES), lambda b, j: (0, 0))],
        out_shape=[jax.ShapeDtypeStruct((B, S, D), F32), jax.ShapeDtypeStruct((B, S, LANES), F32),
                   jax.ShapeDtypeStruct((B, S // TB, SUBLANES, TB), F32), jax.ShapeDtypeStruct((1, LANES), F32)],
        scratch_shapes=[pltpu.VMEM((1, LANES), F32)],
        compiler_params=_cparams(("arbitrary", "arbitrary")),
        name="moe_router",
    )(X, mod, gain.reshape(1, D), w_router)


def _routing_tables(metat, counts):
    B, nj = metat.shape[:2]
    n_tok = B * nj * TB
    mi = jnp.transpose(metat[:, :, 0:4, :], (2, 0, 1, 3)).reshape(4, n_tok).astype(jnp.int32)
    cnt = counts[0, :MOE_EXPERTS].astype(jnp.int32)
    padded = (cnt + EXPERT_BLOCK - 1) // EXPERT_BLOCK * EXPERT_BLOCK
    pad_end = jnp.cumsum(padded)
    pad_start = pad_end - padded
    dest = jnp.concatenate([pad_start[mi[0]] + mi[2], pad_start[mi[1]] + mi[3]])
    n_blocks = -(-(n_tok * MOE_TOP_K) // EXPERT_BLOCK) + MOE_EXPERTS
    tok = jnp.arange(n_tok, dtype=jnp.int32)
    rows = jnp.zeros((n_blocks * EXPERT_BLOCK,), jnp.int32).at[dest].set(jnp.concatenate([tok, tok]))
    starts = jnp.arange(n_blocks, dtype=jnp.int32) * EXPERT_BLOCK
    block_e = jnp.minimum(jnp.sum((pad_end[None, :] <= starts[:, None]).astype(jnp.int32), axis=1), MOE_EXPERTS - 1)
    n_used = (pad_end[-1] // EXPERT_BLOCK).astype(jnp.int32).reshape(1)
    return dest, rows, block_e, n_used, n_blocks


def _expert_kernel(be_ref, nu_ref, rows_ref, hf_ref, wg_ref, wu_ref, wd_ref, o_ref, xbuf, wg_bf, wu_bf, wd_bf, sem):
    j = pl.program_id(0)
    n_used = nu_ref[0]
    EB = EXPERT_BLOCK

    def row_copy(blk, slot, r):
        row = rows_ref[blk * EB + r]
        return pltpu.make_async_copy(hf_ref.at[pl.ds(row, 1), :], xbuf.at[slot, pl.ds(r, 1), :], sem.at[slot])

    def issue(blk, slot):
        def body(r, carry):
            row_copy(blk, slot, r).start()
            return carry
        lax.fori_loop(0, EB, body, 0, unroll=DMA_UNROLL)

    @pl.when(j == 0)
    def _():
        issue(0, 0)

    @pl.when(j + 1 < n_used)
    def _():
        issue(j + 1, (j + 1) % 2)

    @pl.when(j < n_used)
    def _():
        slot = j % 2

        def body(r, carry):
            row_copy(j, slot, r).wait()
            return carry
        lax.fori_loop(0, EB, body, 0, unroll=DMA_UNROLL)

        @pl.when(jnp.logical_or(j == 0, be_ref[j] != be_ref[jnp.maximum(j - 1, 0)]))
        def _():
            wg_bf[...] = wg_ref[0].astype(BF16)
            wu_bf[...] = wu_ref[0].astype(BF16)
            wd_bf[...] = wd_ref[0].astype(BF16)

        xb = xbuf[slot].astype(BF16)
        hg = jnp.dot(xb, wg_bf[...], preferred_element_type=F32)
        hu = jnp.dot(xb, wu_bf[...], preferred_element_type=F32)
        hm = (hg * jax.nn.sigmoid(hg) * hu).astype(BF16)
        o_ref[...] = jnp.dot(hm, wd_bf[...], preferred_element_type=F32)

    @pl.when(j >= n_used)
    def _():
        o_ref[...] = jnp.zeros_like(o_ref)


def _experts(hf2d, rows, block_e, n_used, n_blocks, w_gate, w_up, w_down):
    N, D = hf2d.shape
    E, _, De = w_gate.shape
    EB = EXPERT_BLOCK
    grid_spec = pltpu.PrefetchScalarGridSpec(
        num_scalar_prefetch=3,
        grid=(n_blocks,),
        in_specs=[pl.BlockSpec(memory_space=pl.ANY),
                  pl.BlockSpec((1, D, De), lambda j, be, nu, rw: (be[j], 0, 0)),
                  pl.BlockSpec((1, D, De), lambda j, be, nu, rw: (be[j], 0, 0)),
                  pl.BlockSpec((1, De, D), lambda j, be, nu, rw: (be[j], 0, 0))],
        out_specs=pl.BlockSpec((EB, D), lambda j, be, nu, rw: (j, 0)),
        scratch_shapes=[pltpu.VMEM((2, EB, D), F32), pltpu.VMEM((D, De), BF16), pltpu.VMEM((D, De), BF16),
                        pltpu.VMEM((De, D), BF16), pltpu.SemaphoreType.DMA((2,))],
    )
    return pl.pallas_call(
        _expert_kernel,
        grid_spec=grid_spec,
        out_shape=jax.ShapeDtypeStruct((n_blocks * EB, D), F32),
        compiler_params=_cparams(("arbitrary",)),
        name="moe_experts",
    )(block_e, n_used, rows, hf2d, w_gate, w_up, w_down)


def _combine_kernel(dest_ref, yb_ref, x_ref, meta_ref, mod_ref, *rest, nj, final):
    if final:
        fn_ref, o_ref, ybuf, sem = rest
    else:
        o_ref, ybuf, sem = rest
    b = pl.program_id(0)
    j = pl.program_id(1)
    step = b * nj + j
    n_steps = pl.num_programs(0) * nj
    n_tok = n_steps * TB

    def row_copy(st, slot, r, k):
        dst_row = dest_ref[k * n_tok + st * TB + r]
        return pltpu.make_async_copy(yb_ref.at[pl.ds(dst_row, 1), :], ybuf.at[slot, k, pl.ds(r, 1), :], sem.at[slot])

    def issue(st, slot):
        def body(r, carry):
            row_copy(st, slot, r, 0).start()
            row_copy(st, slot, r, 1).start()
            return carry
        lax.fori_loop(0, TB, body, 0, unroll=DMA_UNROLL)

    @pl.when(step == 0)
    def _():
        issue(0, 0)

    @pl.when(step + 1 < n_steps)
    def _():
        issue(step + 1, (step + 1) % 2)

    slot = step % 2

    def wbody(r, carry):
        row_copy(step, slot, r, 0).wait()
        row_copy(step, slot, r, 1).wait()
        return carry
    lax.fori_loop(0, TB, wbody, 0, unroll=DMA_UNROLL)
    meta = meta_ref[0]
    f = meta[:, 4:5] * ybuf[slot, 0] + meta[:, 5:6] * ybuf[slot, 1]
    m = mod_ref[0, 0]
    xn = x_ref[0] + m[5:6] * f
    if final:
        ms = jnp.mean(xn * xn, axis=-1, keepdims=True)
        xn = xn * lax.rsqrt(ms + NORM_EPS) * fn_ref[...]
    o_ref[0] = xn


def _combine(dest, yb, X, meta, mod, all_lat, final_gain=None):
    B, S, D = X.shape
    nj = S // TB
    final = final_gain is not None
    in_specs = [pl.BlockSpec(memory_space=pl.ANY),
                pl.BlockSpec((1, TB, D), lambda b, j, ds: (b, j, 0)),
                pl.BlockSpec((1, TB, LANES), lambda b, j, ds: (b, j, 0)),
                (pl.BlockSpec((1, 1, 6, D), lambda b, j, ds: (b, 1, 0, 0)) if all_lat else
                 pl.BlockSpec((1, 1, 6, D), lambda b, j, ds: (b, jnp.minimum(j, 1), 0, 0)))]
    args = [dest, yb, X, meta, mod]
    if final:
        in_specs.append(pl.BlockSpec((1, D), lambda b, j, ds: (0, 0)))
        args.append(final_gain.reshape(1, D))
    grid_spec = pltpu.PrefetchScalarGridSpec(
        num_scalar_prefetch=1,
        grid=(B, nj),
        in_specs=in_specs,
        out_specs=pl.BlockSpec((1, TB, D), lambda b, j, ds: (b, j, 0)),
        scratch_shapes=[pltpu.VMEM((2, 2, TB, D), F32), pltpu.SemaphoreType.DMA((2,))],
    )
    return pl.pallas_call(
        functools.partial(_combine_kernel, nj=nj, final=final),
        grid_spec=grid_spec,
        out_shape=jax.ShapeDtypeStruct((B, S, D), F32),
        compiler_params=_cparams(("arbitrary", "arbitrary")),
        name="moe_combine_final" if final else "moe_combine",
    )(*args)


def _moe(X, mod, gain, w_router, w_gate, w_up, w_down, all_lat, final_gain=None):
    B, S, D = X.shape
    hf, meta, metat, counts = _router(X, mod, gain, w_router, all_lat)
    dest, rows, block_e, n_used, n_blocks = _routing_tables(metat, counts)
    yb = _experts(hf.reshape(B * S, D), rows, block_e, n_used, n_blocks, w_gate, w_up, w_down)
    return _combine(dest, yb, X, meta, mod, all_lat, final_gain)


def _diff_attn_kernel(q_ref, k_ref, v_ref, dl_ref, sg_ref, o_ref, *, lam_init):
    q = q_ref[0]
    k = k_ref[0]
    lane = lax.broadcasted_iota(jnp.int32, q.shape, 1)
    zero = jnp.zeros_like(q)
    sc = C_HEAD ** -0.5
    q1 = jnp.where(lane < C_HEAD, q, zero)
    q2 = jnp.where(lane >= C_HEAD, q, zero)
    dl = dl_ref[...]
    lam = (jnp.exp(jnp.sum(dl[0:1] * dl[1:2], axis=1, keepdims=True))
           - jnp.exp(jnp.sum(dl[2:3] * dl[3:4], axis=1, keepdims=True)) + lam_init)

    def soft(qq):
        s = lax.dot_general(qq, k, (((1,), (1,)), ((), ())), preferred_element_type=F32) * sc
        e = jnp.exp(s - jnp.max(s, axis=1, keepdims=True))
        return e * (1.0 / jnp.sum(e, axis=1, keepdims=True))

    a = soft(q1) - lam * soft(q2)
    o = jnp.dot(a.astype(BF16), v_ref[0], preferred_element_type=F32)
    ms = jnp.mean(o * o, axis=-1, keepdims=True)
    o_ref[0] = (o * lax.rsqrt(ms + DIFF_EPS) * sg_ref[...] * (1.0 - lam_init)).astype(o_ref.dtype)


def _diff_attn(P2, t_ctx, diff_lambda, diff_subln, lam_init):
    B, S, _ = P2.shape
    T = S - t_ctx
    qoff = t_ctx // ATT_TQ
    return pl.pallas_call(
        functools.partial(_diff_attn_kernel, lam_init=lam_init),
        grid=(B, C_HEADS, T // ATT_TQ),
        in_specs=[pl.BlockSpec((1, ATT_TQ, LANES), lambda b, h, q: (b, q + qoff, C_Q_OFF // LANES + h)),
                  pl.BlockSpec((1, S, LANES), lambda b, h, q: (b, 0, C_K_OFF // LANES + h)),
                  pl.BlockSpec((1, S, LANES), lambda b, h, q: (b, 0, C_V_OFF // LANES + h)),
                  pl.BlockSpec((4, C_HEAD), lambda b, h, q: (0, 0)),
                  pl.BlockSpec((1, C_VDIM), lambda b, h, q: (0, 0))],
        out_specs=pl.BlockSpec((1, ATT_TQ, LANES), lambda b, h, q: (b, q, h)),
        out_shape=jax.ShapeDtypeStruct((B, T, C_DIM), BF16),
        compiler_params=_cparams(("arbitrary", "arbitrary", "arbitrary")),
        name="diff_attn",
    )(P2, P2, P2, diff_lambda, diff_subln.reshape(1, -1))


def _retention_kernel(q_ref, k_ref, v0_ref, v1_ref, o_ref, r_ref):
    d = pl.program_id(1)
    c = pl.program_id(2)

    @pl.when(c == 0)
    def _():
        r_ref[...] = jnp.zeros_like(r_ref)

    C = RET_CHUNK
    ti =lax.broadcasted_iota(jnp.int32, (C, C), 0)
    si = lax.broadcasted_iota(jnp.int32, (C, C), 1)
    rel = ((ti - si) * (1 - 2 * d)).astype(F32)
    col = lax.broadcasted_iota(jnp.int32, (C, 1), 0)
    pos = jnp.where(d == 0, col, C - 1 - col).astype(F32)
    r0s = [r_ref[h] for h in range(D_HEADS)]
    outs, r_new = [], []
    for h in range(D_HEADS):
        lg_f = math.log(1.0 - 2.0 ** (-5.0 - h))
        lg_b = math.log(1.0 - 2.0 ** (-5.0 - (D_HEADS - 1 - h)))
        lg = jnp.where(d == 0, jnp.full((1, 1), lg_f, F32), jnp.full((1, 1), lg_b, F32))
        q = q_ref[0, :, h * D_KDIM:(h + 1) * D_KDIM]
        k = k_ref[0, :, h * D_KDIM:(h + 1) * D_KDIM]
        vref = v0_ref if h < 2 else v1_ref
        v = vref[0, :, (h % 2) * D_VDIM:(h % 2 + 1) * D_VDIM]
        inner_dec = jnp.where(rel >= 0, jnp.exp(lg * jnp.maximum(rel, 0.0)), 0.0)
        q_dec = jnp.exp(lg * (pos + 1.0))
        k_dec = jnp.exp(lg * (C - 1.0 - pos))
        inner = lax.dot_general(q, k, (((1,), (1,)), ((), ())), preferred_element_type=F32) * inner_dec
        outs.append(_bdot(inner, v) + _bdot(q, r0s[h]) * q_dec)
        r_new.append(r0s[h] * jnp.exp(lg * C) + _bdot(k.astype(F32) * k_dec, v, ((0,), (0,))))
    for h in range(D_HEADS):
        r_ref[h] = r_new[h]
    o_ref[0, 0] = jnp.concatenate(outs, axis=1)


def _retention(P2, t_ctx):
    B, S, _ = P2.shape
    C = RET_CHUNK
    nc = S // C
    nctx = t_ctx // C

    def cidx(d, c):
        back = jnp.where(c < nctx, nctx - 1 - c, nc - 1 + nctx - c)
        return jnp.where(d == 0, c, back)

    w = D_HEADS * D_KDIM
    return pl.pallas_call(
        _retention_kernel,
        grid=(B, 2, nc),
        in_specs=[pl.BlockSpec((1, C, w), lambda b, d, c: (b, cidx(d, c), D_Q_OFF // w)),
                  pl.BlockSpec((1, C, w), lambda b, d, c: (b, cidx(d, c), D_K_OFF // w)),
                  pl.BlockSpec((1, C, w), lambda b, d, c: (b, cidx(d, c), D_V_OFF // w)),
                  pl.BlockSpec((1, C, w), lambda b, d, c: (b, cidx(d, c), D_V_OFF // w + 1))],
        out_specs=pl.BlockSpec((1, 1, C, D_DIM), lambda b, d, c: (d, b, cidx(d, c), 0)),
        out_shape=jax.ShapeDtypeStruct((2, B, S, D_DIM), F32),
        scratch_shapes=[pltpu.VMEM((D_HEADS, D_KDIM, D_VDIM), F32)],
        compiler_params=_cparams(("arbitrary", "arbitrary", "arbitrary")),
        name="retention",
    )(P2, P2, P2, P2)


def _odd_merge_kernel(c_ref, o2_ref, gate_ref, x_ref, mod_ref, rn_ref, wo_ref, o_ref):
    o = o2_ref[0, 0] + o2_ref[1, 0]
    parts = []
    for h in range(D_HEADS):
        oh = o[:, h * D_VDIM:(h + 1) * D_VDIM]
        ms = jnp.mean(oh * oh, axis=-1, keepdims=True)
        parts.append(oh * lax.rsqrt(ms + NORM_EPS))
    gt = gate_ref[0].astype(F32)
    dl = jnp.concatenate(parts, axis=1) * rn_ref[...] * (gt * jax.nn.sigmoid(gt))
    out = (jnp.dot(c_ref[0], wo_ref[0:C_DIM, :], preferred_element_type=F32)
           + _bdot(dl, wo_ref[C_DIM:C_DIM + D_DIM, :]))
    m = mod_ref[0, 0]
    o_ref[0] = x_ref[0] + m[2:3] * out


def _odd_merge(c_lat, o2, P2, X, mod, ret_norm, wo_bf16, t_ctx):
    B, S, D = X.shape
    T = S - t_ctx
    off = t_ctx // TB
    return pl.pallas_call(
        _odd_merge_kernel,
        grid=(B, T // TB),
        in_specs=[pl.BlockSpec((1, TB, C_DIM), lambda b, j: (b, j, 0)),
                  pl.BlockSpec((2, 1, TB, D_DIM), lambda b, j: (0, b, j + off, 0)),
                  pl.BlockSpec((1, TB, D_DIM), lambda b, j: (b, j + off, D_G_OFF // D_DIM)),
                  pl.BlockSpec((1, TB, D), lambda b, j: (b, j + off, 0)),
                  _mod_spec(D, all_lat=True),
                  pl.BlockSpec((1, D_DIM), lambda b, j: (0, 0)),
                  pl.BlockSpec((C_DIM + D_DIM, D), lambda b, j: (0, 0))],
        out_specs=pl.BlockSpec((1, TB, D), lambda b, j: (b, j, 0)),
        out_shape=jax.ShapeDtypeStruct((B, T, D), F32),
        compiler_params=_cparams(("arbitrary", "arbitrary")),
        name="odd_merge",
    )(c_lat, o2, P2, X, mod, ret_norm.reshape(1, -1), wo_bf16)


def _rope_tables(t_ctx, T):
    half = C_HEAD // 4
    inv = ROPE_BASE ** (-jnp.arange(half, dtype=F32) / half)
    t = jnp.arange(T, dtype=jnp.int32)
    ar = (t // GRID_W).astype(F32)[:, None] * inv[None, :]
    ac = (t % GRID_W).astype(F32)[:, None] * inv[None, :]
    z = jnp.zeros_like(ar)
    cos = jnp.concatenate([jnp.cos(ar), jnp.cos(ar), jnp.cos(ac), jnp.cos(ac)], axis=1)
    s1 = jnp.concatenate([-jnp.sin(ar), z, -jnp.sin(ac), z], axis=1)
    s2 = jnp.concatenate([z, jnp.sin(ar), z, jnp.sin(ac)], axis=1)

    def full(tab, ctx_val):
        tab = jnp.concatenate([jnp.full((t_ctx, C_HEAD), ctx_val, F32), tab], axis=0)
        return jnp.tile(tab, (1, LANES // C_HEAD))

    return full(cos, 1.0), full(s1, 0.0), full(s2, 0.0)


def _head_indicator(n, width):
    i = jnp.arange(n) // width
    return (i[:, None] == i[None, :]).astype(F32)


def kernel(x, c, ctx, c_ctx, ada_w, ada_b, norm_mix, norm_ffn, ev_w_in, ev_w_out, rwkv_shift, rwkv_w0, rwkv_w_up, rwkv_a0, rwkv_a_up, rwkv_g_up, rwkv_k_k, rwkv_k_a, rwkv_r_k, rwkv_ln_g, rwkv_ln_b, pool_w, pool_scale, od_w_in, od_w_out, diff_lambda, diff_subln, ret_norm, moe_router_group, moe_router_expert, moe_w_gate, moe_w_up, moe_w_down, final_norm):
    B, T, D = x.shape
    t_ctx = ctx.shape[1]
    assert ada_w.shape[0] == 2 and t_ctx == TB and T % TB == 0 and T % GRID_W == 0 and B < 16
    X = jnp.concatenate([ctx, x], axis=1)

    cpad = jnp.zeros((16, D), F32).at[:B].set(c).at[B].set(c_ctx)
    mods = _ada(cpad, ada_w, ada_b)

    def layer_mod(i):
        lat = mods[i, :B]
        cm = jnp.broadcast_to(mods[i, B][None], lat.shape)
        return jnp.stack([cm, lat], axis=1).reshape(B, 2, 6, D)

    def router_w(i):
        w = jnp.concatenate([moe_router_group[i], moe_router_expert[i]], axis=1)
        return jnp.pad(w, ((0, 0), (0, LANES - w.shape[1])))

    ehead = _head_indicator(A_DIM, A_HEAD)

    mod0 = layer_mod(0)
    P = _proj(X, mod0, norm_mix[0], ev_w_in[0].astype(BF16), F32)
    zl = jnp.zeros((LORA_W, A_DIM), F32)
    lora_blocks = []
    for d in range(2):
        lora_blocks.append(jnp.concatenate([rwkv_w_up[0, d], zl], axis=0))
        lora_blocks.append(jnp.concatenate([zl, rwkv_a_up[0, d]], axis=0))
    lora_w = jnp.concatenate(lora_blocks, axis=1).astype(BF16)
    wa0 = jnp.concatenate([rwkv_w0[0, 0], rwkv_a0[0, 0], rwkv_w0[0, 1], rwkv_a0[0, 1]]).reshape(1, -1)
    lw, kt, bb, v, kk, r, bonus, g, yb = _even_prep(
        P, t_ctx, rwkv_shift[0], wa0, lora_w, rwkv_g_up[0].astype(BF16), rwkv_k_k[0], rwkv_k_a[0], rwkv_r_k[0],
        pool_w[0].astype(BF16), pool_scale[0], ehead)
    y2 = _rwkv_scan(lw, kt, bb, v, kk, r, t_ctx)
    X = _even_merge(y2, bonus, g, yb, X, mod0, rwkv_ln_g[0], rwkv_ln_b[0], ev_w_out[0].astype(BF16), ehead)
    X = _moe(X, mod0, norm_ffn[0], router_w(0), moe_w_gate[0], moe_w_up[0], moe_w_down[0], all_lat=False)

    mod1 = layer_mod(1)
    lam_init = 0.8 - 0.6 * math.exp(-0.3 * 1)
    P2 = _proj(X, mod1, norm_mix[1], od_w_in[0].astype(BF16), BF16, rope=_rope_tables(t_ctx, T))
    c_lat = _diff_attn(P2, t_ctx, diff_lambda[0], diff_subln[0], lam_init)
    o2 = _retention(P2, t_ctx)
    XL = _odd_merge(c_lat, o2, P2, X, mod1, ret_norm[0], od_w_out[0].astype(BF16), t_ctx)
    return _moe(XL, mod1, norm_ffn[1], router_w(1), moe_w_gate[1], moe_w_up[1], moe_w_down[1], all_lat=True,
                final_gain=final_norm)
```

```python
import functools
import math

import jax
import jax.numpy as jnp
from jax import lax
from jax.experimental import pallas as pl
from jax.experimental.pallas import tpu as pltpu

F32 = jnp.float32
BF16 = jnp.bfloat16
HI = lax.Precision.HIGHEST

GRID_W = 64
NORM_EPS = 1e-6
ROPE_BASE = 10000.0
A_HEADS, A_HEAD = 8, 64
A_DIM = A_HEADS * A_HEAD
LORA_W, LORA_A, LORA_G = 64, 64, 128
W_DECAY_SCALE = 0.606531
RWKV_GN_EPS = 64e-5
POOL_WINDOWS = (2, 4, 8, 16)
POOL_GROUP = 128
B_DIM = POOL_GROUP * 4
C_HEADS, C_HEAD = 4, 64
C_VDIM = 2 * C_HEAD
C_DIM = C_HEADS * C_VDIM
DIFF_EPS = 1e-5
D_HEADS, D_KDIM, D_VDIM = 4, 64, 128
D_DIM = D_HEADS * D_VDIM
A_K_OFF, A_V_OFF = 0, A_DIM
W_LORA_OFF = 2 * A_DIM
A_LORA_OFF = W_LORA_OFF + LORA_W
A_R_OFF = A_LORA_OFF + LORA_A
G_LORA_OFF = A_R_OFF + A_DIM
POOL_OFF = G_LORA_OFF + LORA_G
EVEN_IN_COLS = POOL_OFF + B_DIM
C_K_OFF = 0
C_V_OFF = C_HEADS * 2 * C_HEAD
D_K_OFF = C_V_OFF + C_DIM
D_V_OFF = D_K_OFF + D_HEADS * D_KDIM
C_Q_OFF = D_V_OFF + D_DIM
D_Q_OFF = C_Q_OFF + C_HEADS * 2 * C_HEAD
D_G_OFF = D_Q_OFF + D_HEADS * D_KDIM
ODD_IN_COLS = D_G_OFF + D_DIM
MOE_GROUPS, MOE_PER_GROUP = 4, 8
MOE_EXPERTS = MOE_GROUPS * MOE_PER_GROUP
MOE_TOP_K = 2
EXPERT_BLOCK = 256

LANES = 128
SUBLANES = 8
TB = 256
RWKV_CHUNK = 64
RET_CHUNK = 128
SCAN_NB = 4
ATT_TQ = 256
VMEM_LIMIT = 56 * 1024 * 1024
DMA_UNROLL = 8


def _cparams(sem):
    return pltpu.CompilerParams(dimension_semantics=sem, vmem_limit_bytes=VMEM_LIMIT)


def _modnorm(x, gain, scale, shift, eps=NORM_EPS):
    ms = jnp.mean(x * x, axis=-1, keepdims=True)
    return x * lax.rsqrt(ms + eps) * gain * (1.0 + scale) + shift


def _bdot(a, b, dims=((1,), (0,))):
    return lax.dot_general(a.astype(BF16), b.astype(BF16), (dims, ((), ())), preferred_element_type=F32)


def _bf16_terms(x, terms):
    parts = []
    for _ in range(terms):
        p = x.astype(BF16)
        parts.append(p)
        x = x - p.astype(F32)
    return parts


def _dot_split_lhs(x, w_bf16, terms=2):
    return sum(jnp.dot(p, w_bf16, preferred_element_type=F32) for p in _bf16_terms(x, terms))


def _ada_kernel(c_ref, w_ref, b_ref, o_ref):
    c = c_ref[...]
    s = c * jax.nn.sigmoid(c)
    o_ref[0] = jnp.dot(s, w_ref[0], precision=HI, preferred_element_type=F32) + b_ref[0]


def _ada(cpad, ada_w, ada_b):
    L, D, N = ada_w.shape
    tn = 1536
    return pl.pallas_call(
        _ada_kernel,
        grid=(L, N // tn),
        in_specs=[pl.BlockSpec((16, D), lambda l, n: (0, 0)),
                  pl.BlockSpec((1, D, tn), lambda l, n: (l, 0, n)),
                  pl.BlockSpec((1, 1, tn), lambda l, n: (l, 0, n))],
        out_specs=pl.BlockSpec((1, 16, tn), lambda l, n: (l, 0, n)),
        out_shape=jax.ShapeDtypeStruct((L, 16, N), F32),
        compiler_params=_cparams(("arbitrary", "arbitrary")),
        name="ada_mod",
    )(cpad, ada_w, ada_b.reshape(L, 1, N))


def _mod_spec(D, all_lat=False):
    if all_lat:
        return pl.BlockSpec((1, 1, 6, D), lambda b, j: (b, 1, 0, 0))
    return pl.BlockSpec((1, 1, 6, D), lambda b, j: (b, jnp.minimum(j, 1), 0, 0))


def _proj_kernel(x_ref, mod_ref, g_ref, w_ref, *rest, n_out, rope_chunks, kscale_chunks, out_dtype):
    if rope_chunks:
        cos_ref, s1_ref, s2_ref, o_ref = rest
    else:
        (o_ref,) = rest
    m = mod_ref[0, 0]
    h = _modnorm(x_ref[0], g_ref[...], m[1:2], m[0:1]).astype(BF16)
    cw = 512
    for n0 in range(0, n_out, cw):
        p = jnp.dot(h, w_ref[:, n0:n0 + cw], preferred_element_type=F32)
        if not rope_chunks:
            o_ref[0, :, n0:n0 + cw] = p.astype(out_dtype)
            continue
        for q in range(cw // LANES):
            gi = n0 // LANES + q
            sub = p[:, q * LANES:(q + 1) * LANES]
            if gi in rope_chunks:
                sub = (sub * cos_ref[...] + pltpu.roll(sub, LANES - 16, 1) * s1_ref[...]
                       + pltpu.roll(sub, 16, 1) * s2_ref[...])
            if gi in kscale_chunks:
                sub = sub * (D_KDIM ** -0.5)
            o_ref[0, :, n0 + q * LANES:n0 + (q + 1) * LANES] = sub.astype(out_dtype)


def _proj(X, mod, gain, w_bf16, out_dtype, rope=None):
    B, S, D = X.shape
    N = w_bf16.shape[1]
    in_specs = [pl.BlockSpec((1, TB, D), lambda b, j: (b, j, 0)),
                _mod_spec(D),
                pl.BlockSpec((1, D), lambda b, j: (0, 0)),
                pl.BlockSpec((D, N), lambda b, j: (0, 0))]
    args = [X, mod, gain.reshape(1, D), w_bf16]
    rope_chunks, kscale_chunks = (), ()
    if rope is not None:
        cos, s1, s2 = rope
        in_specs += [pl.BlockSpec((TB, LANES), lambda b, j: (j, 0))] * 3
        args += [cos, s1, s2]
        ranges = [(C_K_OFF, C_V_OFF), (D_K_OFF, D_V_OFF), (C_Q_OFF, D_G_OFF)]
        rope_chunks = tuple(g for lo, hi in ranges for g in range(lo // LANES, hi // LANES))
        kscale_chunks = tuple(range(D_K_OFF // LANES, D_V_OFF // LANES))
    return pl.pallas_call(
        functools.partial(_proj_kernel, n_out=N, rope_chunks=rope_chunks, kscale_chunks=kscale_chunks,
                          out_dtype=out_dtype),
        grid=(B, S // TB),
        in_specs=in_specs,
        out_specs=pl.BlockSpec((1, TB, N), lambda b, j: (b, j, 0)),
        out_shape=jax.ShapeDtypeStruct((B, S, N), out_dtype),
        compiler_params=_cparams(("arbitrary", "arbitrary")),
        name="proj_rope" if rope is not None else "proj",
    )(*args)


def _even_prep_kernel(p_ref, pp_ref, pn_ref, sh_ref, wa0_ref, lora_ref, gup_ref, kkg_ref, ka_ref, rk_ref,
                      pw_ref, ps_ref, eh_ref,
                      lw_ref, kt_ref, bb_ref, v_ref, kkn_ref, r_ref, bonus_ref, g_ref, yb_ref,
                      *, nblk, t_ctx, t_lat):
    j = pl.program_id(1)
    prev_ok = j >= 2
    next_ok = jnp.logical_and(j >= 1, j < nblk - 1)
    n_ext = TB + 2 * SUBLANES

    def ext(c0, c1):
        pv = jnp.where(prev_ok, pp_ref[0, :, c0:c1], 0.0)
        nx = jnp.where(next_ok, pn_ref[0, :, c0:c1], 0.0)
        return jnp.concatenate([pv, p_ref[0, :, c0:c1], nx], axis=0)

    def shift3(c0, c1, k0):
        e = ext(c0, c1)
        kern = sh_ref[:, k0:k0 + (c1 - c0)]
        y = kern[0:1] * pltpu.roll(e, 1, 0) + kern[1:2] * e + kern[2:3] * pltpu.roll(e, n_ext - 1, 0)
        return y[SUBLANES:SUBLANES + TB]

    k = shift3(A_K_OFF, A_K_OFF + A_DIM, 0)
    v = shift3(A_V_OFF, A_V_OFF + A_DIM, A_DIM)
    r = shift3(A_R_OFF, A_R_OFF + A_DIM, 2 * A_DIM)

    c = p_ref[0, :, W_LORA_OFF:W_LORA_OFF + LANES]
    lane = lax.broadcasted_iota(jnp.int32, c.shape, 1)
    lin = jnp.where(lane < LORA_W, jnp.tanh(c), c)
    z = _bdot(lin, lora_ref[...]) + wa0_ref[...]

    eh = eh_ref[...]
    kk0 = k * kkg_ref[...]
    ss = _dot_split_lhs(kk0 * kk0, eh)
    kk = kk0 * lax.rsqrt(jnp.maximum(ss, 1e-12))
    kt_sum = None
    for d in range(2):
        zw = z[:, (2 * d) * A_DIM:(2 * d + 1) * A_DIM]
        za = z[:, (2 * d + 1) * A_DIM:(2 * d + 2) * A_DIM]
        a = jax.nn.sigmoid(za)
        kt = k * (1.0 + (a - 1.0) * ka_ref[...])
        lw_ref[d, 0] = -W_DECAY_SCALE * jax.nn.sigmoid(zw)
        kt_ref[d, 0] = kt
        bb_ref[d, 0] = a * kk
        kt_sum = kt if kt_sum is None else kt_sum + kt
    v_ref[0] = v
    kkn_ref[0] = kk
    r_ref[0] = r
    bonus_ref[0] = _dot_split_lhs(r * kt_sum * rk_ref[...], eh) * v
    gl = p_ref[0, :, G_LORA_OFF:G_LORA_OFF + LORA_G]
    g_ref[0] = _bdot(jax.nn.sigmoid(gl), gup_ref[...])

    row = lax.broadcasted_iota(jnp.int32, (TB, POOL_GROUP), 0)
    tpos = row + jnp.maximum(j - 1, 0) * TB
    tseg = jnp.where(j == 0, t_ctx, t_lat)
    for gi, win in enumerate(POOL_WINDOWS):
        c0 = POOL_OFF + gi * POOL_GROUP
        e = ext(c0, c0 + POOL_GROUP)
        s = e + pltpu.roll(e, 1, 0)
        sh = 1
        while 2 * sh < win:
            s = pltpu.roll(s, sh, 0) + pltpu.roll(s, n_ext - sh, 0)
            sh *= 2
        s = s[SUBLANES:SUBLANES + TB]
        lo = jnp.clip(tpos - win // 2, 0, tseg)
        hi = jnp.clip(tpos - win // 2 + win, 0, tseg)
        mean = s / (hi - lo).astype(F32)
        dd = mean - p_ref[0, :, c0:c0 + POOL_GROUP]
        y = _bdot(dd, pw_ref[gi])
        yb_ref[0, :, gi * POOL_GROUP:(gi + 1) * POOL_GROUP] = y * ps_ref[:, gi * POOL_GROUP:(gi + 1) * POOL_GROUP]


def _even_prep(P, t_ctx, shift_k, wa0, lora_w, g_up, k_k, k_a, r_k, pool_w, pool_scale, ehead):
    B, S, N = P.shape
    nblk = S // TB
    hb = TB // SUBLANES
    const2 = lambda b, j: (0, 0)
    in_specs = [
        pl.BlockSpec((1, TB, N), lambda b, j: (b, j, 0)),
        pl.BlockSpec((1, SUBLANES, N), lambda b, j: (b, jnp.maximum(j * hb - 1, 0), 0)),
        pl.BlockSpec((1, SUBLANES, N), lambda b, j: (b, jnp.minimum((j + 1) * hb, S // SUBLANES - 1), 0)),
        pl.BlockSpec((3, 3 * A_DIM), const2),
        pl.BlockSpec((1, 4 * A_DIM), const2),
        pl.BlockSpec((LANES, 4 * A_DIM), const2),
        pl.BlockSpec((LORA_G, A_DIM), const2),
        pl.BlockSpec((1, A_DIM), const2),
        pl.BlockSpec((1, A_DIM), const2),
        pl.BlockSpec((1, A_DIM), const2),
        pl.BlockSpec((4, POOL_GROUP, POOL_GROUP), lambda b, j: (0, 0, 0)),
        pl.BlockSpec((1, B_DIM), const2),
        pl.BlockSpec((A_DIM, A_DIM), const2),
    ]
    two = jax.ShapeDtypeStruct((2, B, S, A_DIM), F32)
    one = jax.ShapeDtypeStruct((B, S, A_DIM), F32)
    two_spec = pl.BlockSpec((2, 1, TB, A_DIM), lambda b, j: (0, b, j, 0))
    one_spec = pl.BlockSpec((1, TB, A_DIM), lambda b, j: (b, j, 0))
    return pl.pallas_call(
        functools.partial(_even_prep_kernel, nblk=nblk, t_ctx=t_ctx, t_lat=S - t_ctx),
        grid=(B, nblk),
        in_specs=in_specs,
        out_specs=[two_spec] * 3 + [one_spec] * 6,
        out_shape=[two] * 3 + [one] * 6,
        compiler_params=_cparams(("arbitrary", "arbitrary")),
        name="even_prep",
    )(P, P, P, shift_k, wa0, lora_w, g_up, k_k.reshape(1, -1), k_a.reshape(1, -1), r_k.reshape(1, -1),
      pool_w, pool_scale.reshape(1, -1), ehead)


def _rwkv_scan_kernel(lw_ref, kt_ref, bb_ref, v_ref, kk_ref, r_ref, y_ref, s_ref):
    d = pl.program_id(1)
    c = pl.program_id(2)

    @pl.when(c == 0)
    def _():
        s_ref[...] = jnp.zeros_like(s_ref)

    C = RWKV_CHUNK
    tt = lax.broadcasted_iota(jnp.int32, (C, C), 0)
    ii = lax.broadcasted_iota(jnp.int32, (C, C), 1)
    rel = (tt - ii) * (1 - 2 * d)
    strict = rel > 0
    incl = rel >= 0
    t2 = lax.broadcasted_iota(jnp.int32, (C, 2 * C), 0)
    i2 = lax.broadcasted_iota(jnp.int32, (C, 2 * C), 1) & (C - 1)
    incl2 = (t2 - i2) * (1 - 2 * d) >= 0
    tri = jnp.where(incl, 1.0, 0.0).astype(BF16)
    n_dbl = int(math.log2(C))
    hsl = [slice(h * A_HEAD, (h + 1) * A_HEAD) for h in range(A_HEADS)]
    ar, bk, vh, bkp, ptot = [], [], [], [], []
    for bi in range(SCAN_NB):
        lw = lw_ref[0, bi]
        cum = sum(jnp.dot(tri, p, preferred_element_type=F32) for p in _bf16_terms(lw, 3))
        tot = jnp.sum(lw, axis=0, keepdims=True)
        p_inv = jnp.exp(-cum)
        p_rest = jnp.exp(tot - cum)
        p_tot = jnp.exp(tot)
        bb = bb_ref[0, bi]
        kt = kt_ref[0, bi]
        at_all = (-kk_ref[bi] * jnp.exp(cum - lw)).astype(BF16)
        rt_all = (r_ref[bi] * jnp.exp(cum)).astype(BF16)
        bt_all = (bb * p_inv).astype(BF16)
        kti_all = (kt * p_inv).astype(BF16)
        bp_all = (bb * p_rest).astype(BF16)
        kp_all = (kt * p_rest).astype(BF16)
        v_all = v_ref[bi].astype(BF16)
        for s in hsl:
            ar.append(jnp.concatenate([at_all[:, s], rt_all[:, s]], axis=0))
            bk.append(jnp.concatenate([bt_all[:, s], kti_all[:, s]], axis=0))
            bkp.append(jnp.concatenate([bp_all[:, s], kp_all[:, s]], axis=0))
            vh.append(v_all[:, s])
            ptot.append(p_tot[:, s])
    chains = range(SCAN_NB * A_HEADS)
    s0 = [s_ref[n] for n in chains]
    m = [_bdot(ar[n], bk[n], ((1,), (1,))) for n in chains]
    arh = [_bdot(ar[n], s0[n], ((1,), (1,))) for n in chains]
    lp = [jnp.where(strict, m[n][:C, :C], 0.0) for n in chains]
    a_ak = [jnp.where(strict, m[n][:C, C:], 0.0) for n in chains]
    a_r = [jnp.where(incl2, m[n][C:, :], 0.0) for n in chains]
    u = [arh[n][:C] + _bdot(a_ak[n], vh[n]) for n in chains]
    for q in range(n_dbl):
        u = [u[n] + _bdot(lp[n], u[n]) for n in chains]
        if q < n_dbl - 1:
            lp = [_bdot(lp[n], lp[n]) for n in chains]
    uv = [jnp.concatenate([u[n].astype(BF16), vh[n]], axis=0) for n in chains]
    outs = [arh[n][C:] + _bdot(a_r[n], uv[n]) for n in chains]
    s_new = [s0[n] * ptot[n] + _bdot(uv[n], bkp[n], ((0,), (0,))) for n in chains]
    for n in chains:
        s_ref[n] = s_new[n]
    for bi in range(SCAN_NB):
        y_ref[0, bi] = jnp.concatenate(outs[bi * A_HEADS:(bi + 1) * A_HEADS], axis=1)


def _rwkv_scan(lw, kt, bb, v, kk, r, t_ctx):
    _, B, S, _ = lw.shape
    C = RWKV_CHUNK
    nc = S // C
    nctx = t_ctx // C

    def cidx(d, c):
        back = jnp.where(c < nctx, nctx - 1 - c, nc - 1 + nctx - c)
        return jnp.where(d == 0, c, back)

    two_spec = pl.BlockSpec((1, SCAN_NB, C, A_DIM), lambda b, d, c: (d, b, cidx(d, c), 0))
    one_spec = pl.BlockSpec((SCAN_NB, C, A_DIM), lambda b, d, c: (b, cidx(d, c), 0))
    return pl.pallas_call(
        _rwkv_scan_kernel,
        grid=(B // SCAN_NB, 2, nc),
        in_specs=[two_spec] * 3 + [one_spec] * 3,
        out_specs=two_spec,
        out_shape=jax.ShapeDtypeStruct((2, B, S, A_DIM), F32),
        scratch_shapes=[pltpu.VMEM((SCAN_NB * A_HEADS, A_HEAD, A_HEAD), F32)],
        compiler_params=_cparams(("arbitrary", "arbitrary", "arbitrary")),
        name="rwkv_scan",
    )(lw, kt, bb, v, kk, r)


def _even_merge_kernel(y2_ref, bonus_ref, g_ref, yb_ref, x_ref, mod_ref, lng_ref, lnb_ref, wo_ref, eh_ref, o_ref):
    y = y2_ref[0, 0] + y2_ref[1, 0]
    eh = eh_ref[...]
    mu = _dot_split_lhs(y, eh) * (1.0 / A_HEAD)
    yc = y - mu
    var = _dot_split_lhs(yc * yc, eh) * (1.0 / A_HEAD)
    yn = yc * lax.rsqrt(var + RWKV_GN_EPS) * lng_ref[...] + lnb_ref[...]
    ya = (yn + bonus_ref[0]) * g_ref[0]
    out = _bdot(ya, wo_ref[0:A_DIM, :]) + _bdot(yb_ref[0], wo_ref[A_DIM:A_DIM + B_DIM, :])
    m = mod_ref[0, 0]
    o_ref[0] = x_ref[0] + m[2:3] * out


def _even_merge(y2, bonus, g, yb, X, mod, ln_g, ln_b, wo_bf16, ehead):
    B, S, D = X.shape
    one_spec = pl.BlockSpec((1, TB, A_DIM), lambda b, j: (b, j, 0))
    const2 = lambda b, j: (0, 0)
    return pl.pallas_call(
        _even_merge_kernel,
        grid=(B, S // TB),
        in_specs=[pl.BlockSpec((2, 1, TB, A_DIM), lambda b, j: (0, b, j, 0)), one_spec, one_spec, one_spec,
                  pl.BlockSpec((1, TB, D), lambda b, j: (b, j, 0)), _mod_spec(D),
                  pl.BlockSpec((1, A_DIM), const2), pl.BlockSpec((1, A_DIM), const2),
                  pl.BlockSpec((A_DIM + B_DIM, D), const2), pl.BlockSpec((A_DIM, A_DIM), const2)],
        out_specs=pl.BlockSpec((1, TB, D), lambda b, j: (b, j, 0)),
        out_shape=jax.ShapeDtypeStruct((B, S, D), F32),
        compiler_params=_cparams(("arbitrary", "arbitrary")),
        name="even_merge",
    )(y2, bonus, g, yb, X, mod, ln_g.reshape(1, -1), ln_b.reshape(1, -1), wo_bf16, ehead)


def _router_kernel(x_ref, mod_ref, g_ref, wr_ref, hf_ref, meta_ref, metat_ref, cnt_ref, carry_ref):
    first = jnp.logical_and(pl.program_id(0) == 0, pl.program_id(1) == 0)

    @pl.when(first)
    def _():
        carry_ref[...] = jnp.zeros_like(carry_ref)

    m = mod_ref[0, 0]
    hf = _modnorm(x_ref[0], g_ref[...], m[4:5], m[3:4])
    hf_ref[0] = hf
    h0, h1 = _bf16_terms(hf, 2)
    w0, w1 = _bf16_terms(wr_ref[...], 2)
    logits = (jnp.dot(h0, w0, preferred_element_type=F32) + jnp.dot(h0, w1, preferred_element_type=F32)
              + jnp.dot(h1, w0, preferred_element_type=F32))
    lane = lax.broadcasted_iota(jnp.int32, logits.shape, 1)
    neg = -jnp.inf
    lg = jnp.where(lane < MOE_GROUPS, logits, neg)
    mg = jnp.max(lg, axis=1, keepdims=True)
    grp = jnp.min(jnp.where(lg == mg, lane, LANES), axis=1, keepdims=True)
    gw = 1.0 / jnp.sum(jnp.exp(lg - mg), axis=1, keepdims=True)
    el = lane - MOE_GROUPS
    in_grp = jnp.logical_and(el >= grp * MOE_PER_GROUP, el < (grp + 1) * MOE_PER_GROUP)
    le = jnp.where(in_grp, logits, neg)
    m0 = jnp.max(le, axis=1, keepdims=True)
    i0 = jnp.min(jnp.where(le == m0, lane, LANES), axis=1, keepdims=True)
    le1 = jnp.where(lane == i0, neg, le)
    m1 = jnp.max(le1, axis=1, keepdims=True)
    i1 = jnp.min(jnp.where(le1 == m1, lane, LANES), axis=1, keepdims=True)
    p1 = jnp.exp(m1 - m0)
    gate0 = gw / (1.0 + p1)
    gate1 = gw * p1 / (1.0 + p1)
    e0 = i0 - MOE_GROUPS
    e1 = i1 - MOE_GROUPS
    oh0 = (lane == e0).astype(F32)
    oh1 = (lane == e1).astype(F32)
    cnt = oh0 + oh1
    rr = lax.broadcasted_iota(jnp.int32, (TB, TB), 0)
    cc = lax.broadcasted_iota(jnp.int32, (TB, TB), 1)
    before = _bdot((cc < rr).astype(F32), cnt) + carry_ref[...]
    rank0 = jnp.sum(oh0 * before, axis=1, keepdims=True)
    rank1 = jnp.sum(oh1 * before, axis=1, keepdims=True)
    carry_ref[...] = carry_ref[...] + jnp.sum(cnt, axis=0, keepdims=True)
    cnt_ref[...] = carry_ref[...]
    meta = jnp.where(lane == 0, e0.astype(F32), 0.0)
    meta = jnp.where(lane == 1, e1.astype(F32), meta)
    meta = jnp.where(lane == 2, rank0, meta)
    meta = jnp.where(lane == 3, rank1, meta)
    meta = jnp.where(lane == 4, gate0, meta)
    meta = jnp.where(lane == 5, gate1, meta)
    meta_ref[0] = meta
    metat_ref[0, 0] = meta.T[0:SUBLANES]


def _router(X, mod, gain, w_router, all_lat):
    B, S, D = X.shape
    return pl.pallas_call(
        _router_kernel,
        grid=(B, S // TB),
        in_specs=[pl.BlockSpec((1, TB, D), lambda b, j: (b, j, 0)), _mod_spec(D, all_lat),
                  pl.BlockSpec((1, D), lambda b, j: (0, 0)), pl.BlockSpec((D, LANES), lambda b, j: (0, 0))],
        out_specs=[pl.BlockSpec((1, TB, D), lambda b, j: (b, j, 0)),
                   pl.BlockSpec((1, TB, LANES), lambda b, j: (b, j, 0)),
                   pl.BlockSpec((1, 1, SUBLANES, TB), lambda b, j: (b, j, 0, 0)),
                   pl.BlockSpec((1, LANES), lambda b, j: (0, 0))],
        out_shape=[jax.ShapeDtypeStruct((B, S, D), F32), jax.ShapeDtypeStruct((B, S, LANES), F32),
                   jax.ShapeDtypeStruct((B, S // TB, SUBLANES, TB), F32), jax.ShapeDtypeStruct((1, LANES), F32)],
        scratch_shapes=[pltpu.VMEM((1, LANES), F32)],
        compiler_params=_cparams(("arbitrary", "arbitrary")),
        name="moe_router",
    )(X, mod, gain.reshape(1, D), w_router)


def _routing_tables(metat, counts):
    B, nj = metat.shape[:2]
    n_tok = B * nj * TB
    mi = jnp.transpose(metat[:, :, 0:4, :], (2, 0, 1, 3)).reshape(4, n_tok).astype(jnp.int32)
    cnt = counts[0, :MOE_EXPERTS].astype(jnp.int32)
    padded = (cnt + EXPERT_BLOCK - 1) // EXPERT_BLOCK * EXPERT_BLOCK
    pad_end = jnp.cumsum(padded)
    pad_start = pad_end - padded
    dest = jnp.concatenate([pad_start[mi[0]] + mi[2], pad_start[mi[1]] + mi[3]])
    n_blocks = -(-(n_tok * MOE_TOP_K) // EXPERT_BLOCK) + MOE_EXPERTS
    tok = jnp.arange(n_tok, dtype=jnp.int32)
    rows = jnp.zeros((n_blocks * EXPERT_BLOCK,), jnp.int32).at[dest].set(jnp.concatenate([tok, tok]))
    starts = jnp.arange(n_blocks, dtype=jnp.int32) * EXPERT_BLOCK
    block_e = jnp.minimum(jnp.sum((pad_end[None, :] <= starts[:, None]).astype(jnp.int32), axis=1), MOE_EXPERTS - 1)
    n_used = (pad_end[-1] // EXPERT_BLOCK).astype(jnp.int32).reshape(1)
    return dest, rows, block_e, n_used, n_blocks


def _expert_kernel(be_ref, nu_ref, rows_ref, hf_ref, wg_ref, wu_ref, wd_ref, o_ref, xbuf, wg_bf, wu_bf, wd_bf, sem):
    j = pl.program_id(0)
    n_used = nu_ref[0]
    EB = EXPERT_BLOCK

    def row_copy(blk, slot, r):
        row = rows_ref[blk * EB + r]
        return pltpu.make_async_copy(hf_ref.at[pl.ds(row, 1), :], xbuf.at[slot, pl.ds(r, 1), :], sem.at[slot])

    def issue(blk, slot):
        def body(r, carry):
            row_copy(blk, slot, r).start()
            return carry
        lax.fori_loop(0, EB, body, 0, unroll=DMA_UNROLL)

    @pl.when(j == 0)
    def _():
        issue(0, 0)

    def compute(prefetch_next):
        slot = j % 2

        def body(r, carry):
            row_copy(j, slot, r).wait()
            return carry
        lax.fori_loop(0, EB, body, 0, unroll=DMA_UNROLL)

        @pl.when(jnp.logical_or(j == 0, be_ref[j] != be_ref[jnp.maximum(j - 1, 0)]))
        def _():
            wg_bf[...] = wg_ref[0, 0].astype(BF16)
            wu_bf[...] = wu_ref[0, 0].astype(BF16)
            wd_bf[...] = wd_ref[0, 0].astype(BF16)

        xb = xbuf[slot].astype(BF16)
        if prefetch_next:
            for r in range(EB):
                row_copy(j + 1, 1 - slot, r).start()
        hg = jnp.dot(xb, wg_bf[...], preferred_element_type=F32)
        hu = jnp.dot(xb, wu_bf[...], preferred_element_type=F32)
        hm = (hg * jax.nn.sigmoid(hg) * hu).astype(BF16)
        o_ref[...] = jnp.dot(hm, wd_bf[...], preferred_element_type=F32)

    @pl.when(j + 1 < n_used)
    def _():
        compute(True)

    @pl.when(j + 1 == n_used)
    def _():
        compute(False)

    @pl.when(j >= n_used)
    def _():
        o_ref[...] = jnp.zeros_like(o_ref)


def _experts(hf2d, rows, block_e, n_used, n_blocks, layer, w_gate, w_up, w_down):
    N, D = hf2d.shape
    _, E, _, De = w_gate.shape
    EB = EXPERT_BLOCK
    grid_spec = pltpu.PrefetchScalarGridSpec(
        num_scalar_prefetch=3,
        grid=(n_blocks,),
        in_specs=[pl.BlockSpec(memory_space=pl.ANY),
                  pl.BlockSpec((1, 1, D, De), lambda j, be, nu, rw: (layer, be[j], 0, 0)),
                  pl.BlockSpec((1, 1, D, De), lambda j, be, nu, rw: (layer, be[j], 0, 0)),
                  pl.BlockSpec((1, 1, De, D), lambda j, be, nu, rw: (layer, be[j], 0, 0))],
        out_specs=pl.BlockSpec((EB, D), lambda j, be, nu, rw: (j, 0)),
        scratch_shapes=[pltpu.VMEM((2, EB, D), F32), pltpu.VMEM((D, De), BF16), pltpu.VMEM((D, De), BF16),
                        pltpu.VMEM((De, D), BF16), pltpu.SemaphoreType.DMA((2,))],
    )
    return pl.pallas_call(
        _expert_kernel,
        grid_spec=grid_spec,
        out_shape=jax.ShapeDtypeStruct((n_blocks * EB, D), F32),
        compiler_params=_cparams(("arbitrary",)),
        name="moe_experts",
    )(block_e, n_used, rows, hf2d, w_gate, w_up, w_down)


def _combine_kernel(dest_ref, yb_ref, x_ref, meta_ref, mod_ref, *rest, nj, final):
    if final:
        fn_ref, o_ref, ybuf, sem = rest
    else:
        o_ref, ybuf, sem = rest
    b = pl.program_id(0)
    j = pl.program_id(1)
    step = b * nj + j
    n_steps = pl.num_programs(0) * nj
    n_tok = n_steps * TB

    def row_copy(st, slot, r, k):
        dst_row = dest_ref[k * n_tok + st * TB + r]
        return pltpu.make_async_copy(yb_ref.at[pl.ds(dst_row, 1), :], ybuf.at[slot, k, pl.ds(r, 1), :], sem.at[slot])

    def issue(st, slot):
        def body(r, carry):
            row_copy(st, slot, r, 0).start()
            row_copy(st, slot, r, 1).start()
            return carry
        lax.fori_loop(0, TB, body, 0, unroll=DMA_UNROLL)

    @pl.when(step == 0)
    def _():
        issue(0, 0)

    slot = step % 2

    @pl.when(step + 1 < n_steps)
    def _():
        for r in range(TB):
            row_copy(step + 1, 1 - slot, r, 0).start()
            row_copy(step + 1, 1 - slot, r, 1).start()

    def wbody(r, carry):
        row_copy(step, slot, r, 0).wait()
        row_copy(step, slot, r, 1).wait()
        return carry
    lax.fori_loop(0, TB, wbody, 0, unroll=DMA_UNROLL)
    meta = meta_ref[0]
    f = meta[:, 4:5] * ybuf[slot, 0] + meta[:, 5:6] * ybuf[slot, 1]
    m = mod_ref[0, 0]
    xn = x_ref[0] + m[5:6] * f
    if final:
        ms = jnp.mean(xn * xn, axis=-1, keepdims=True)
        xn = xn * lax.rsqrt(ms + NORM_EPS) * fn_ref[...]
    o_ref[0] = xn


def _combine(dest, yb, X, meta, mod, all_lat, final_gain=None):
    B, S, D = X.shape
    nj = S // TB
    final = final_gain is not None
    in_specs = [pl.BlockSpec(memory_space=pl.ANY),
                pl.BlockSpec((1, TB, D), lambda b, j, ds: (b, j, 0)),
                pl.BlockSpec((1, TB, LANES), lambda b, j, ds: (b, j, 0)),
                (pl.BlockSpec((1, 1, 6, D), lambda b, j, ds: (b, 1, 0, 0)) if all_lat else
                 pl.BlockSpec((1, 1, 6, D), lambda b, j, ds: (b, jnp.minimum(j, 1), 0, 0)))]
    args = [dest, yb, X, meta, mod]
    if final:
        in_specs.append(pl.BlockSpec((1, D), lambda b, j, ds: (0, 0)))
        args.append(final_gain.reshape(1, D))
    grid_spec = pltpu.PrefetchScalarGridSpec(
        num_scalar_prefetch=1,
        grid=(B, nj),
        in_specs=in_specs,
        out_specs=pl.BlockSpec((1, TB, D), lambda b, j, ds: (b, j, 0)),
        scratch_shapes=[pltpu.VMEM((2, 2, TB, D), F32), pltpu.SemaphoreType.DMA((2,))],
    )
    return pl.pallas_call(
        functools.partial(_combine_kernel, nj=nj, final=final),
        grid_spec=grid_spec,
        out_shape=jax.ShapeDtypeStruct((B, S, D), F32),
        compiler_params=_cparams(("arbitrary", "arbitrary")),
        name="moe_combine_final" if final else "moe_combine",
    )(*args)


def _moe(X, mod, gain, w_router, layer, w_gate, w_up, w_down, all_lat, final_gain=None):
    B, S, D = X.shape
    hf, meta, metat, counts = _router(X, mod, gain, w_router, all_lat)
    dest, rows, block_e, n_used, n_blocks = _routing_tables(metat, counts)
    yb = _experts(hf.reshape(B * S, D), rows, block_e, n_used, n_blocks, layer, w_gate, w_up, w_down)
    return _combine(dest, yb, X, meta, mod, all_lat, final_gain)


def _diff_attn_kernel(q_ref, k_ref, v_ref, dl_ref, sg_ref, o_ref, vx_ref, *, lam_init):
    @pl.when(pl.program_id(2) == 0)
    def _():
        vlane = lax.broadcasted_iota(jnp.int32, (vx_ref.shape[0], LANES), 1)
        vx_ref[:, 0:C_VDIM] = v_ref[0]
        vx_ref[:, C_VDIM:C_VDIM + LANES] = jnp.where(vlane == 0, 1.0, 0.0).astype(BF16)

    q = q_ref[0] * (C_HEAD ** -0.5)
    k = k_ref[0]
    lane = lax.broadcasted_iota(jnp.int32, q.shape, 1)
    zero = jnp.zeros_like(q)
    q1 = jnp.where(lane < C_HEAD, q, zero)
    q2 = jnp.where(lane >= C_HEAD, q, zero)
    dl = dl_ref[...]
    lam = (jnp.exp(jnp.sum(dl[0:1] * dl[1:2], axis=1, keepdims=True))
           - jnp.exp(jnp.sum(dl[2:3] * dl[3:4], axis=1, keepdims=True)) + lam_init)

    s = [lax.dot_general(qq, k, (((1,), (1,)), ((), ())), preferred_element_type=F32) for qq in (q1, q2)]
    e = [jnp.exp(si - jnp.max(si, axis=1, keepdims=True)).astype(BF16) for si in s]
    oz = [jnp.dot(ei, vx_ref[...], preferred_element_type=F32) for ei in e]
    on = [ozi[:, 0:C_VDIM] * (1.0 / ozi[:, C_VDIM:C_VDIM + 1]) for ozi in oz]
    o = on[0] - lam * on[1]
    ms = jnp.mean(o * o, axis=-1, keepdims=True)
    o_ref[0] = (o * lax.rsqrt(ms + DIFF_EPS) * sg_ref[...] * (1.0 - lam_init)).astype(o_ref.dtype)


def _diff_attn(P2, t_ctx, diff_lambda, diff_subln, lam_init):
    B, S, _ = P2.shape
    T = S - t_ctx
    qoff = t_ctx // ATT_TQ
    return pl.pallas_call(
        functools.partial(_diff_attn_kernel, lam_init=lam_init),
        grid=(B, C_HEADS, T // ATT_TQ),
        in_specs=[pl.BlockSpec((1, ATT_TQ, LANES), lambda b, h, q: (b, q + qoff, C_Q_OFF // LANES + h)),
                  pl.BlockSpec((1, S, LANES), lambda b, h, q: (b, 0, C_K_OFF // LANES + h)),
                  pl.BlockSpec((1, S, LANES), lambda b, h, q: (b, 0, C_V_OFF // LANES + h)),
                  pl.BlockSpec((4, C_HEAD), lambda b, h, q: (0, 0)),
                  pl.BlockSpec((1, C_VDIM), lambda b, h, q: (0, 0))],
        out_specs=pl.BlockSpec((1, ATT_TQ, LANES), lambda b, h, q: (b, q, h)),
        out_shape=jax.ShapeDtypeStruct((B, T, C_DIM), BF16),
        scratch_shapes=[pltpu.VMEM((S, C_VDIM + LANES), BF16)],
        compiler_params=_cparams(("arbitrary", "arbitrary", "arbitrary")),
        name="diff_attn",
    )(P2, P2, P2, diff_lambda, diff_subln.reshape(1, -1))


def _retention_kernel(q_ref, k_ref, v0_ref, v1_ref, o_ref, r_ref):
    d = pl.program_id(1)
    c = pl.program_id(2)

    @pl.when(c == 0)
    def _():
        r_ref[...] = jnp.zeros_like(r_ref)

    C = RET_CHUNK
    ti =lax.broadcasted_iota(jnp.int32, (C, C), 0)
    si = lax.broadcasted_iota(jnp.int32, (C, C), 1)
    rel = ((ti - si) * (1 - 2 * d)).astype(F32)
    col = lax.broadcasted_iota(jnp.int32, (C, 1), 0)
    pos = jnp.where(d == 0, col, C - 1 - col).astype(F32)
    r0s = [r_ref[h] for h in range(D_HEADS)]
    outs, r_new = [], []
    for h in range(D_HEADS):
        lg_f = math.log(1.0 - 2.0 ** (-5.0 - h))
        lg_b = math.log(1.0 - 2.0 ** (-5.0 - (D_HEADS - 1 - h)))
        lg = jnp.where(d == 0, jnp.full((1, 1), lg_f, F32), jnp.full((1, 1), lg_b, F32))
        q = q_ref[0, :, h * D_KDIM:(h + 1) * D_KDIM]
        k = k_ref[0, :, h * D_KDIM:(h + 1) * D_KDIM]
        vref = v0_ref if h < 2 else v1_ref
        v = vref[0, :, (h % 2) * D_VDIM:(h % 2 + 1) * D_VDIM]
        inner_dec = jnp.where(rel >= 0, jnp.exp(lg * jnp.maximum(rel, 0.0)), 0.0)
        q_dec = jnp.exp(lg * (pos + 1.0))
        k_dec = jnp.exp(lg * (C - 1.0 - pos))
        inner = lax.dot_general(q, k, (((1,), (1,)), ((), ())), preferred_element_type=F32) * inner_dec
        outs.append(_bdot(inner, v) + _bdot(q, r0s[h]) * q_dec)
        r_new.append(r0s[h] * jnp.exp(lg * C) + _bdot(k.astype(F32) * k_dec, v, ((0,), (0,))))
    for h in range(D_HEADS):
        r_ref[h] = r_new[h]
    o_ref[0, 0] = jnp.concatenate(outs, axis=1)


def _retention(P2, t_ctx):
    B, S, _ = P2.shape
    C = RET_CHUNK
    nc = S // C
    nctx = t_ctx // C

    def cidx(d, c):
        back = jnp.where(c < nctx, nctx - 1 - c, nc - 1 + nctx - c)
        return jnp.where(d == 0, c, back)

    w = D_HEADS * D_KDIM
    return pl.pallas_call(
        _retention_kernel,
        grid=(B, 2, nc),
        in_specs=[pl.BlockSpec((1, C, w), lambda b, d, c: (b, cidx(d, c), D_Q_OFF // w)),
                  pl.BlockSpec((1, C, w), lambda b, d, c: (b, cidx(d, c), D_K_OFF // w)),
                  pl.BlockSpec((1, C, w), lambda b, d, c: (b, cidx(d, c), D_V_OFF // w)),
                  pl.BlockSpec((1, C, w), lambda b, d, c: (b, cidx(d, c), D_V_OFF // w + 1))],
        out_specs=pl.BlockSpec((1, 1, C, D_DIM), lambda b, d, c: (d, b, cidx(d, c), 0)),
        out_shape=jax.ShapeDtypeStruct((2, B, S, D_DIM), F32),
        scratch_shapes=[pltpu.VMEM((D_HEADS, D_KDIM, D_VDIM), F32)],
        compiler_params=_cparams(("arbitrary", "arbitrary", "arbitrary")),
        name="retention",
    )(P2, P2, P2, P2)


def _odd_merge_kernel(c_ref, o2_ref, gate_ref, x_ref, mod_ref, rn_ref, wo_ref, o_ref):
    o = o2_ref[0, 0] + o2_ref[1, 0]
    parts = []
    for h in range(D_HEADS):
        oh = o[:, h * D_VDIM:(h + 1) * D_VDIM]
        ms = jnp.mean(oh * oh, axis=-1, keepdims=True)
        parts.append(oh * lax.rsqrt(ms + NORM_EPS))
    gt = gate_ref[0].astype(F32)
    dl = jnp.concatenate(parts, axis=1) * rn_ref[...] * (gt * jax.nn.sigmoid(gt))
    out = (jnp.dot(c_ref[0], wo_ref[0:C_DIM, :], preferred_element_type=F32)
           + _bdot(dl, wo_ref[C_DIM:C_DIM + D_DIM, :]))
    m = mod_ref[0, 0]
    o_ref[0] = x_ref[0] + m[2:3] * out


def _odd_merge(c_lat, o2, P2, X, mod, ret_norm, wo_bf16, t_ctx):
    B, S, D = X.shape
    T = S - t_ctx
    off = t_ctx // TB
    return pl.pallas_call(
        _odd_merge_kernel,
        grid=(B, T // TB),
        in_specs=[pl.BlockSpec((1, TB, C_DIM), lambda b, j: (b, j, 0)),
                  pl.BlockSpec((2, 1, TB, D_DIM), lambda b, j: (0, b, j + off, 0)),
                  pl.BlockSpec((1, TB, D_DIM), lambda b, j: (b, j + off, D_G_OFF // D_DIM)),
                  pl.BlockSpec((1, TB, D), lambda b, j: (b, j + off, 0)),
                  _mod_spec(D, all_lat=True),
                  pl.BlockSpec((1, D_DIM), lambda b, j: (0, 0)),
                  pl.BlockSpec((C_DIM + D_DIM, D), lambda b, j: (0, 0))],
        out_specs=pl.BlockSpec((1, TB, D), lambda b, j: (b, j, 0)),
        out_shape=jax.ShapeDtypeStruct((B, T, D), F32),
        compiler_params=_cparams(("arbitrary", "arbitrary")),
        name="odd_merge",
    )(c_lat, o2, P2, X, mod, ret_norm.reshape(1, -1), wo_bf16)


def _rope_tables(t_ctx, T):
    half = C_HEAD // 4
    inv = ROPE_BASE ** (-jnp.arange(half, dtype=F32) / half)
    t = jnp.arange(T, dtype=jnp.int32)
    ar = (t // GRID_W).astype(F32)[:, None] * inv[None, :]
    ac = (t % GRID_W).astype(F32)[:, None] * inv[None, :]
    z = jnp.zeros_like(ar)
    cos = jnp.concatenate([jnp.cos(ar), jnp.cos(ar), jnp.cos(ac), jnp.cos(ac)], axis=1)
    s1 = jnp.concatenate([-jnp.sin(ar), z, -jnp.sin(ac), z], axis=1)
    s2 = jnp.concatenate([z, jnp.sin(ar), z, jnp.sin(ac)], axis=1)

    def full(tab, ctx_val):
        tab = jnp.concatenate([jnp.full((t_ctx, C_HEAD), ctx_val, F32), tab], axis=0)
        return jnp.tile(tab, (1, LANES // C_HEAD))

    return full(cos, 1.0), full(s1, 0.0), full(s2, 0.0)


def _head_indicator(n, width):
    i = jnp.arange(n) // width
    return (i[:, None] == i[None, :]).astype(BF16)


def kernel(x, c, ctx, c_ctx, ada_w, ada_b, norm_mix, norm_ffn, ev_w_in, ev_w_out, rwkv_shift, rwkv_w0, rwkv_w_up, rwkv_a0, rwkv_a_up, rwkv_g_up, rwkv_k_k, rwkv_k_a, rwkv_r_k, rwkv_ln_g, rwkv_ln_b, pool_w, pool_scale, od_w_in, od_w_out, diff_lambda, diff_subln, ret_norm, moe_router_group, moe_router_expert, moe_w_gate, moe_w_up, moe_w_down, final_norm):
    B, T, D = x.shape
    t_ctx = ctx.shape[1]
    assert ada_w.shape[0] == 2 and t_ctx == TB and T % TB == 0 and T % GRID_W == 0 and B < 16
    X = jnp.concatenate([ctx, x], axis=1)

    cpad = jnp.zeros((16, D), F32).at[:B].set(c).at[B].set(c_ctx)
    mods = _ada(cpad, ada_w, ada_b)

    def layer_mod(i):
        lat = mods[i, :B]
        cm = jnp.broadcast_to(mods[i, B][None], lat.shape)
        return jnp.stack([cm, lat], axis=1).reshape(B, 2, 6, D)

    def router_w(i):
        w = jnp.concatenate([moe_router_group[i], moe_router_expert[i]], axis=1)
        return jnp.pad(w, ((0, 0), (0, LANES - w.shape[1])))

    ehead = _head_indicator(A_DIM, A_HEAD)

    mod0 = layer_mod(0)
    P = _proj(X, mod0, norm_mix[0], ev_w_in[0].astype(BF16), F32)
    zl = jnp.zeros((LORA_W, A_DIM), F32)
    lora_blocks = []
    for d in range(2):
        lora_blocks.append(jnp.concatenate([rwkv_w_up[0, d], zl], axis=0))
        lora_blocks.append(jnp.concatenate([zl, rwkv_a_up[0, d]], axis=0))
    lora_w = jnp.concatenate(lora_blocks, axis=1).astype(BF16)
    wa0 = jnp.concatenate([rwkv_w0[0, 0], rwkv_a0[0, 0], rwkv_w0[0, 1], rwkv_a0[0, 1]]).reshape(1, -1)
    lw, kt, bb, v, kk, r, bonus, g, yb = _even_prep(
        P, t_ctx, rwkv_shift[0], wa0, lora_w, rwkv_g_up[0].astype(BF16), rwkv_k_k[0], rwkv_k_a[0], rwkv_r_k[0],
        pool_w[0].astype(BF16), pool_scale[0], ehead)
    y2 = _rwkv_scan(lw, kt, bb, v, kk, r, t_ctx)
    X = _even_merge(y2, bonus, g, yb, X, mod0, rwkv_ln_g[0], rwkv_ln_b[0], ev_w_out[0].astype(BF16), ehead)
    X = _moe(X, mod0, norm_ffn[0], router_w(0), 0, moe_w_gate, moe_w_up, moe_w_down, all_lat=False)

    mod1 = layer_mod(1)
    lam_init = 0.8 - 0.6 * math.exp(-0.3 * 1)
    P2 = _proj(X, mod1, norm_mix[1], od_w_in[0].astype(BF16), BF16, rope=_rope_tables(t_ctx, T))
    c_lat = _diff_attn(P2, t_ctx, diff_lambda[0], diff_subln[0], lam_init)
    o2 = _retention(P2, t_ctx)
    XL = _odd_merge(c_lat, o2, P2, X, mod1, ret_norm[0], od_w_out[0].astype(BF16), t_ctx)
    return _moe(XL, mod1, norm_ffn[1], router_w(1), 1, moe_w_gate, moe_w_up, moe_w_down, all_lat=True,
                final_gain=final_norm)
```

```python
import functools
import math

import jax
import jax.numpy as jnp
from jax import lax
from jax.experimental import pallas as pl
from jax.experimental.pallas import tpu as pltpu

F32 = jnp.float32
BF16 = jnp.bfloat16
HI = lax.Precision.HIGHEST

GRID_W = 64
NORM_EPS = 1e-6
ROPE_BASE = 10000.0
A_HEADS, A_HEAD = 8, 64
A_DIM = A_HEADS * A_HEAD
LORA_W, LORA_A, LORA_G = 64, 64, 128
W_DECAY_SCALE = 0.606531
RWKV_GN_EPS = 64e-5
POOL_WINDOWS = (2, 4, 8, 16)
POOL_GROUP = 128
B_DIM = POOL_GROUP * 4
C_HEADS, C_HEAD = 4, 64
C_VDIM = 2 * C_HEAD
C_DIM = C_HEADS * C_VDIM
DIFF_EPS = 1e-5
D_HEADS, D_KDIM, D_VDIM = 4, 64, 128
D_DIM = D_HEADS * D_VDIM
A_K_OFF, A_V_OFF = 0, A_DIM
W_LORA_OFF = 2 * A_DIM
A_LORA_OFF = W_LORA_OFF + LORA_W
A_R_OFF = A_LORA_OFF + LORA_A
G_LORA_OFF = A_R_OFF + A_DIM
POOL_OFF = G_LORA_OFF + LORA_G
EVEN_IN_COLS = POOL_OFF + B_DIM
C_K_OFF = 0
C_V_OFF = C_HEADS * 2 * C_HEAD
D_K_OFF = C_V_OFF + C_DIM
D_V_OFF = D_K_OFF + D_HEADS * D_KDIM
C_Q_OFF = D_V_OFF + D_DIM
D_Q_OFF = C_Q_OFF + C_HEADS * 2 * C_HEAD
D_G_OFF = D_Q_OFF + D_HEADS * D_KDIM
ODD_IN_COLS = D_G_OFF + D_DIM
MOE_GROUPS, MOE_PER_GROUP = 4, 8
MOE_EXPERTS = MOE_GROUPS * MOE_PER_GROUP
MOE_TOP_K = 2
EXPERT_BLOCK = 256

LANES = 128
SUBLANES = 8
TB = 256
RWKV_CHUNK = 64
RET_CHUNK = 256
RET_NB = 2
SCAN_NB = 4
ATT_TQ = 256
VMEM_LIMIT = 56 * 1024 * 1024
DMA_UNROLL = 8
GATHER_SLOTS = 3


def _cparams(sem):
    return pltpu.CompilerParams(dimension_semantics=sem, vmem_limit_bytes=VMEM_LIMIT)


def _modnorm(x, gain, scale, shift, eps=NORM_EPS):
    ms = jnp.mean(x * x, axis=-1, keepdims=True)
    return x * lax.rsqrt(ms + eps) * gain * (1.0 + scale) + shift


def _bdot(a, b, dims=((1,), (0,))):
    return lax.dot_general(a.astype(BF16), b.astype(BF16), (dims, ((), ())), preferred_element_type=F32)


def _bf16_terms(x, terms):
    parts = []
    for _ in range(terms):
        p = x.astype(BF16)
        parts.append(p)
        x = x - p.astype(F32)
    return parts


def _dot_split_lhs(x, w_bf16, terms=2):
    return sum(jnp.dot(p, w_bf16, preferred_element_type=F32) for p in _bf16_terms(x, terms))


def _ada_kernel(c_ref, w_ref, b_ref, o_ref):
    c = c_ref[...]
    s = c * jax.nn.sigmoid(c)
    o_ref[0] = jnp.dot(s, w_ref[0], precision=HI, preferred_element_type=F32) + b_ref[0]


def _ada(cpad, ada_w, ada_b):
    L, D, N = ada_w.shape
    tn = 1536
    return pl.pallas_call(
        _ada_kernel,
        grid=(L, N // tn),
        in_specs=[pl.BlockSpec((16, D), lambda l, n: (0, 0)),
                  pl.BlockSpec((1, D, tn), lambda l, n: (l, 0, n)),
                  pl.BlockSpec((1, 1, tn), lambda l, n: (l, 0, n))],
        out_specs=pl.BlockSpec((1, 16, tn), lambda l, n: (l, 0, n)),
        out_shape=jax.ShapeDtypeStruct((L, 16, N), F32),
        compiler_params=_cparams(("arbitrary", "arbitrary")),
        name="ada_mod",
    )(cpad, ada_w, ada_b.reshape(L, 1, N))


def _mod_spec(D, all_lat=False):
    if all_lat:
        return pl.BlockSpec((1, 1, 6, D), lambda b, j: (b, 1, 0, 0))
    return pl.BlockSpec((1, 1, 6, D), lambda b, j: (b, jnp.minimum(j, 1), 0, 0))


def _proj_kernel(x_ref, mod_ref, g_ref, w_ref, *rest, n_out, rope_chunks, kscale_chunks, out_dtype):
    if rope_chunks:
        cos_ref, s1_ref, s2_ref, o_ref = rest
    else:
        (o_ref,) = rest
    m = mod_ref[0, 0]
    h = _modnorm(x_ref[0], g_ref[...], m[1:2], m[0:1]).astype(BF16)
    cw = 512
    for n0 in range(0, n_out, cw):
        p = jnp.dot(h, w_ref[:, n0:n0 + cw], preferred_element_type=F32)
        if not rope_chunks:
            o_ref[0, :, n0:n0 + cw] = p.astype(out_dtype)
            continue
        for q in range(cw // LANES):
            gi = n0 // LANES + q
            sub = p[:, q * LANES:(q + 1) * LANES]
            if gi in rope_chunks:
                sub = (sub * cos_ref[...] + pltpu.roll(sub, LANES - 16, 1) * s1_ref[...]
                       + pltpu.roll(sub, 16, 1) * s2_ref[...])
            if gi in kscale_chunks:
                sub = sub * (D_KDIM ** -0.5)
            o_ref[0, :, n0 + q * LANES:n0 + (q + 1) * LANES] = sub.astype(out_dtype)


def _proj(X, mod, gain, w_bf16, out_dtype, rope=None):
    B, S, D = X.shape
    N = w_bf16.shape[1]
    in_specs = [pl.BlockSpec((1, TB, D), lambda b, j: (b, j, 0)),
                _mod_spec(D),
                pl.BlockSpec((1, D), lambda b, j: (0, 0)),
                pl.BlockSpec((D, N), lambda b, j: (0, 0))]
    args = [X, mod, gain.reshape(1, D), w_bf16]
    rope_chunks, kscale_chunks = (), ()
    if rope is not None:
        cos, s1, s2 = rope
        in_specs += [pl.BlockSpec((TB, LANES), lambda b, j: (j, 0))] * 3
        args += [cos, s1, s2]
        ranges = [(C_K_OFF, C_V_OFF), (D_K_OFF, D_V_OFF), (C_Q_OFF, D_G_OFF)]
        rope_chunks = tuple(g for lo, hi in ranges for g in range(lo // LANES, hi // LANES))
        kscale_chunks = tuple(range(D_K_OFF // LANES, D_V_OFF // LANES))
    return pl.pallas_call(
        functools.partial(_proj_kernel, n_out=N, rope_chunks=rope_chunks, kscale_chunks=kscale_chunks,
                          out_dtype=out_dtype),
        grid=(B, S // TB),
        in_specs=in_specs,
        out_specs=pl.BlockSpec((1, TB, N), lambda b, j: (b, j, 0)),
        out_shape=jax.ShapeDtypeStruct((B, S, N), out_dtype),
        compiler_params=_cparams(("arbitrary", "arbitrary")),
        name="proj_rope" if rope is not None else "proj",
    )(*args)


def _even_prep_kernel(p_ref, pp_ref, pn_ref, sh_ref, wa0_ref, lora_ref, gup_ref, kkg_ref, ka_ref, rk_ref,
                      pw_ref, ps_ref, eh_ref,
                      lw_ref, kt_ref, bb_ref, v_ref, kkn_ref, r_ref, bonus_ref, g_ref, yb_ref,
                      *, nblk, t_ctx, t_lat):
    j = pl.program_id(1)
    prev_ok = j >= 2
    next_ok = jnp.logical_and(j >= 1, j < nblk - 1)
    n_ext = TB + 2 * SUBLANES

    def ext(c0, c1):
        pv = jnp.where(prev_ok, pp_ref[0, :, c0:c1], 0.0)
        nx = jnp.where(next_ok, pn_ref[0, :, c0:c1], 0.0)
        return jnp.concatenate([pv, p_ref[0, :, c0:c1], nx], axis=0)

    def shift3(c0, c1, k0):
        e = ext(c0, c1)
        kern = sh_ref[:, k0:k0 + (c1 - c0)]
        y = kern[0:1] * pltpu.roll(e, 1, 0) + kern[1:2] * e + kern[2:3] * pltpu.roll(e, n_ext - 1, 0)
        return y[SUBLANES:SUBLANES + TB]

    k = shift3(A_K_OFF, A_K_OFF + A_DIM, 0)
    v = shift3(A_V_OFF, A_V_OFF + A_DIM, A_DIM)
    r = shift3(A_R_OFF, A_R_OFF + A_DIM, 2 * A_DIM)

    c = p_ref[0, :, W_LORA_OFF:W_LORA_OFF + LANES]
    lane = lax.broadcasted_iota(jnp.int32, c.shape, 1)
    lin = jnp.where(lane < LORA_W, jnp.tanh(c), c)
    z = _bdot(lin, lora_ref[...]) + wa0_ref[...]

    eh = eh_ref[...]
    kk0 = k * kkg_ref[...]
    ss = _dot_split_lhs(kk0 * kk0, eh)
    kk = kk0 * lax.rsqrt(jnp.maximum(ss, 1e-12))
    kt_sum = None
    for d in range(2):
        zw = z[:, (2 * d) * A_DIM:(2 * d + 1) * A_DIM]
        za = z[:, (2 * d + 1) * A_DIM:(2 * d + 2) * A_DIM]
        a = jax.nn.sigmoid(za)
        kt = k * (1.0 + (a - 1.0) * ka_ref[...])
        lw_ref[d, 0] = -W_DECAY_SCALE * jax.nn.sigmoid(zw)
        kt_ref[d, 0] = kt
        bb_ref[d, 0] = a * kk
        kt_sum = kt if kt_sum is None else kt_sum + kt
    v_ref[0] = v
    kkn_ref[0] = kk
    r_ref[0] = r
    bonus_ref[0] = _dot_split_lhs(r * kt_sum * rk_ref[...], eh) * v
    gl = p_ref[0, :, G_LORA_OFF:G_LORA_OFF + LORA_G]
    g_ref[0] = _bdot(jax.nn.sigmoid(gl), gup_ref[...])

    row = lax.broadcasted_iota(jnp.int32, (TB, POOL_GROUP), 0)
    tpos = row + jnp.maximum(j - 1, 0) * TB
    tseg = jnp.where(j == 0, t_ctx, t_lat)
    for gi, win in enumerate(POOL_WINDOWS):
        c0 = POOL_OFF + gi * POOL_GROUP
        e = ext(c0, c0 + POOL_GROUP)
        s = e + pltpu.roll(e, 1, 0)
        sh = 1
        while 2 * sh < win:
            s = pltpu.roll(s, sh, 0) + pltpu.roll(s, n_ext - sh, 0)
            sh *= 2
        s = s[SUBLANES:SUBLANES + TB]
        lo = jnp.clip(tpos - win // 2, 0, tseg)
        hi = jnp.clip(tpos - win // 2 + win, 0, tseg)
        mean = s / (hi - lo).astype(F32)
        dd = mean - p_ref[0, :, c0:c0 + POOL_GROUP]
        y = _bdot(dd, pw_ref[gi])
        yb_ref[0, :, gi * POOL_GROUP:(gi + 1) * POOL_GROUP] = y * ps_ref[:, gi * POOL_GROUP:(gi + 1) * POOL_GROUP]


def _even_prep(P, t_ctx, shift_k, wa0, lora_w, g_up, k_k, k_a, r_k, pool_w, pool_scale, ehead):
    B, S, N = P.shape
    nblk = S // TB
    hb = TB // SUBLANES
    const2 = lambda b, j: (0, 0)
    in_specs = [
        pl.BlockSpec((1, TB, N), lambda b, j: (b, j, 0)),
        pl.BlockSpec((1, SUBLANES, N), lambda b, j: (b, jnp.maximum(j * hb - 1, 0), 0)),
        pl.BlockSpec((1, SUBLANES, N), lambda b, j: (b, jnp.minimum((j + 1) * hb, S // SUBLANES - 1), 0)),
        pl.BlockSpec((3, 3 * A_DIM), const2),
        pl.BlockSpec((1, 4 * A_DIM), const2),
        pl.BlockSpec((LANES, 4 * A_DIM), const2),
        pl.BlockSpec((LORA_G, A_DIM), const2),
        pl.BlockSpec((1, A_DIM), const2),
        pl.BlockSpec((1, A_DIM), const2),
        pl.BlockSpec((1, A_DIM), const2),
        pl.BlockSpec((4, POOL_GROUP, POOL_GROUP), lambda b, j: (0, 0, 0)),
        pl.BlockSpec((1, B_DIM), const2),
        pl.BlockSpec((A_DIM, A_DIM), const2),
    ]
    two = jax.ShapeDtypeStruct((2, B, S, A_DIM), F32)
    one = jax.ShapeDtypeStruct((B, S, A_DIM), F32)
    two_spec = pl.BlockSpec((2, 1, TB, A_DIM), lambda b, j: (0, b, j, 0))
    one_spec = pl.BlockSpec((1, TB, A_DIM), lambda b, j: (b, j, 0))
    return pl.pallas_call(
        functools.partial(_even_prep_kernel, nblk=nblk, t_ctx=t_ctx, t_lat=S - t_ctx),
        grid=(B, nblk),
        in_specs=in_specs,
        out_specs=[two_spec] * 3 + [one_spec] * 6,
        out_shape=[two] * 3 + [one] * 6,
        compiler_params=_cparams(("arbitrary", "arbitrary")),
        name="even_prep",
    )(P, P, P, shift_k, wa0, lora_w, g_up, k_k.reshape(1, -1), k_a.reshape(1, -1), r_k.reshape(1, -1),
      pool_w, pool_scale.reshape(1, -1), ehead)


def _rwkv_scan_kernel(lw_ref, kt_ref, bb_ref, v_ref, kk_ref, r_ref, y_ref, s_ref):
    d = pl.program_id(1)
    c = pl.program_id(2)

    @pl.when(c == 0)
    def _():
        s_ref[...] = jnp.zeros_like(s_ref)

    C = RWKV_CHUNK
    tt = lax.broadcasted_iota(jnp.int32, (C, C), 0)
    ii = lax.broadcasted_iota(jnp.int32, (C, C), 1)
    rel = (tt - ii) * (1 - 2 * d)
    strict = rel > 0
    incl = rel >= 0
    t2 = lax.broadcasted_iota(jnp.int32, (C, 2 * C), 0)
    i2 = lax.broadcasted_iota(jnp.int32, (C, 2 * C), 1) & (C - 1)
    incl2 = (t2 - i2) * (1 - 2 * d) >= 0
    tri = jnp.where(incl, 1.0, 0.0).astype(BF16)
    n_dbl = int(math.log2(C))
    hsl = [slice(h * A_HEAD, (h + 1) * A_HEAD) for h in range(A_HEADS)]
    ar, bk, vh, bkp, ptot = [], [], [], [], []
    for bi in range(SCAN_NB):
        lw = lw_ref[0, bi]
        cum = sum(jnp.dot(tri, p, preferred_element_type=F32) for p in _bf16_terms(lw, 3))
        tot = jnp.sum(lw, axis=0, keepdims=True)
        p_inv = jnp.exp(-cum)
        p_rest = jnp.exp(tot - cum)
        p_tot = jnp.exp(tot)
        bb = bb_ref[0, bi]
        kt = kt_ref[0, bi]
        at_all = (-kk_ref[bi] * jnp.exp(cum - lw)).astype(BF16)
        rt_all = (r_ref[bi] * jnp.exp(cum)).astype(BF16)
        bt_all = (bb * p_inv).astype(BF16)
        kti_all = (kt * p_inv).astype(BF16)
        bp_all = (bb * p_rest).astype(BF16)
        kp_all = (kt * p_rest).astype(BF16)
        v_all = v_ref[bi].astype(BF16)
        for s in hsl:
            ar.append(jnp.concatenate([at_all[:, s], rt_all[:, s]], axis=0))
            bk.append(jnp.concatenate([bt_all[:, s], kti_all[:, s]], axis=0))
            bkp.append(jnp.concatenate([bp_all[:, s], kp_all[:, s]], axis=0))
            vh.append(v_all[:, s])
            ptot.append(p_tot[:, s])
    chains = range(SCAN_NB * A_HEADS)
    s0 = [s_ref[n] for n in chains]
    m = [_bdot(ar[n], bk[n], ((1,), (1,))) for n in chains]
    arh = [_bdot(ar[n], s0[n], ((1,), (1,))) for n in chains]
    lp = [jnp.where(strict, m[n][:C, :C], 0.0) for n in chains]
    a_ak = [jnp.where(strict, m[n][:C, C:], 0.0) for n in chains]
    a_r = [jnp.where(incl2, m[n][C:, :], 0.0) for n in chains]
    u = [arh[n][:C] + _bdot(a_ak[n], vh[n]) for n in chains]
    for q in range(n_dbl):
        u = [u[n] + _bdot(lp[n], u[n]) for n in chains]
        if q < n_dbl - 1:
            lp = [_bdot(lp[n], lp[n]) for n in chains]
    uv = [jnp.concatenate([u[n].astype(BF16), vh[n]], axis=0) for n in chains]
    outs = [arh[n][C:] + _bdot(a_r[n], uv[n]) for n in chains]
    s_new = [s0[n] * ptot[n] + _bdot(uv[n], bkp[n], ((0,), (0,))) for n in chains]
    for n in chains:
        s_ref[n] = s_new[n]
    for bi in range(SCAN_NB):
        y_ref[0, bi] = jnp.concatenate(outs[bi * A_HEADS:(bi + 1) * A_HEADS], axis=1)


def _rwkv_scan(lw, kt, bb, v, kk, r, t_ctx):
    _, B, S, _ = lw.shape
    C = RWKV_CHUNK
    nc = S // C
    nctx = t_ctx // C

    def cidx(d, c):
        back = jnp.where(c < nctx, nctx - 1 - c, nc - 1 + nctx - c)
        return jnp.where(d == 0, c, back)

    two_spec = pl.BlockSpec((1, SCAN_NB, C, A_DIM), lambda b, d, c: (d, b, cidx(d, c), 0))
    one_spec = pl.BlockSpec((SCAN_NB, C, A_DIM), lambda b, d, c: (b, cidx(d, c), 0))
    return pl.pallas_call(
        _rwkv_scan_kernel,
        grid=(B // SCAN_NB, 2, nc),
        in_specs=[two_spec] * 3 + [one_spec] * 3,
        out_specs=two_spec,
        out_shape=jax.ShapeDtypeStruct((2, B, S, A_DIM), F32),
        scratch_shapes=[pltpu.VMEM((SCAN_NB * A_HEADS, A_HEAD, A_HEAD), F32)],
        compiler_params=_cparams(("arbitrary", "arbitrary", "arbitrary")),
        name="rwkv_scan",
    )(lw, kt, bb, v, kk, r)


def _even_merge_kernel(y2_ref, bonus_ref, g_ref, yb_ref, x_ref, mod_ref, lng_ref, lnb_ref, wo_ref, eh_ref, o_ref):
    y = y2_ref[0, 0] + y2_ref[1, 0]
    eh = eh_ref[...]
    mu = _dot_split_lhs(y, eh) * (1.0 / A_HEAD)
    yc = y - mu
    var = _dot_split_lhs(yc * yc, eh) * (1.0 / A_HEAD)
    yn = yc * lax.rsqrt(var + RWKV_GN_EPS) * lng_ref[...] + lnb_ref[...]
    ya = (yn + bonus_ref[0]) * g_ref[0]
    out = _bdot(ya, wo_ref[0:A_DIM, :]) + _bdot(yb_ref[0], wo_ref[A_DIM:A_DIM + B_DIM, :])
    m = mod_ref[0, 0]
    o_ref[0] = x_ref[0] + m[2:3] * out


def _even_merge(y2, bonus, g, yb, X, mod, ln_g, ln_b, wo_bf16, ehead):
    B, S, D = X.shape
    one_spec = pl.BlockSpec((1, TB, A_DIM), lambda b, j: (b, j, 0))
    const2 = lambda b, j: (0, 0)
    return pl.pallas_call(
        _even_merge_kernel,
        grid=(B, S // TB),
        in_specs=[pl.BlockSpec((2, 1, TB, A_DIM), lambda b, j: (0, b, j, 0)), one_spec, one_spec, one_spec,
                  pl.BlockSpec((1, TB, D), lambda b, j: (b, j, 0)), _mod_spec(D),
                  pl.BlockSpec((1, A_DIM), const2), pl.BlockSpec((1, A_DIM), const2),
                  pl.BlockSpec((A_DIM + B_DIM, D), const2), pl.BlockSpec((A_DIM, A_DIM), const2)],
        out_specs=pl.BlockSpec((1, TB, D), lambda b, j: (b, j, 0)),
        out_shape=jax.ShapeDtypeStruct((B, S, D), F32),
        compiler_params=_cparams(("arbitrary", "arbitrary")),
        name="even_merge",
    )(y2, bonus, g, yb, X, mod, ln_g.reshape(1, -1), ln_b.reshape(1, -1), wo_bf16, ehead)


def _router_kernel(x_ref, mod_ref, g_ref, wr_ref, hf_ref, meta_ref, metat_ref, cnt_ref, carry_ref):
    first = jnp.logical_and(pl.program_id(0) == 0, pl.program_id(1) == 0)

    @pl.when(first)
    def _():
        carry_ref[...] = jnp.zeros_like(carry_ref)

    m = mod_ref[0, 0]
    hf = _modnorm(x_ref[0], g_ref[...], m[4:5], m[3:4])
    hf_ref[0] = hf
    h0, h1 = _bf16_terms(hf, 2)
    w0, w1 = _bf16_terms(wr_ref[...], 2)
    logits = (jnp.dot(h0, w0, preferred_element_type=F32) + jnp.dot(h0, w1, preferred_element_type=F32)
              + jnp.dot(h1, w0, preferred_element_type=F32))
    lane = lax.broadcasted_iota(jnp.int32, logits.shape, 1)
    neg = -jnp.inf
    lg = jnp.where(lane < MOE_GROUPS, logits, neg)
    mg = jnp.max(lg, axis=1, keepdims=True)
    grp = jnp.min(jnp.where(lg == mg, lane, LANES), axis=1, keepdims=True)
    gw = 1.0 / jnp.sum(jnp.exp(lg - mg), axis=1, keepdims=True)
    el = lane - MOE_GROUPS
    in_grp = jnp.logical_and(el >= grp * MOE_PER_GROUP, el < (grp + 1) * MOE_PER_GROUP)
    le = jnp.where(in_grp, logits, neg)
    m0 = jnp.max(le, axis=1, keepdims=True)
    i0 = jnp.min(jnp.where(le == m0, lane, LANES), axis=1, keepdims=True)
    le1 = jnp.where(lane == i0, neg, le)
    m1 = jnp.max(le1, axis=1, keepdims=True)
    i1 = jnp.min(jnp.where(le1 == m1, lane, LANES), axis=1, keepdims=True)
    p1 = jnp.exp(m1 - m0)
    gate0 = gw / (1.0 + p1)
    gate1 = gw * p1 / (1.0 + p1)
    e0 = i0 - MOE_GROUPS
    e1 = i1 - MOE_GROUPS
    oh0 = (lane == e0).astype(F32)
    oh1 = (lane == e1).astype(F32)
    cnt = oh0 + oh1
    rr = lax.broadcasted_iota(jnp.int32, (TB, TB), 0)
    cc = lax.broadcasted_iota(jnp.int32, (TB, TB), 1)
    before = _bdot((cc < rr).astype(F32), cnt) + carry_ref[...]
    rank0 = jnp.sum(oh0 * before, axis=1, keepdims=True)
    rank1 = jnp.sum(oh1 * before, axis=1, keepdims=True)
    carry_ref[...] = carry_ref[...] + jnp.sum(cnt, axis=0, keepdims=True)
    cnt_ref[...] = carry_ref[...]
    meta = jnp.where(lane == 0, e0.astype(F32), 0.0)
    meta = jnp.where(lane == 1, e1.astype(F32), meta)
    meta = jnp.where(lane == 2, rank0, meta)
    meta = jnp.where(lane == 3, rank1, meta)
    meta = jnp.where(lane == 4, gate0, meta)
    meta = jnp.where(lane == 5, gate1, meta)
    meta_ref[0] = meta
    metat_ref[0, 0] = meta.T[0:SUBLANES]


def _router(X, mod, gain, w_router, all_lat):
    B, S, D = X.shape
    return pl.pallas_call(
        _router_kernel,
        grid=(B, S // TB),
        in_specs=[pl.BlockSpec((1, TB, D), lambda b, j: (b, j, 0)), _mod_spec(D, all_lat),
                  pl.BlockSpec((1, D), lambda b, j: (0, 0)), pl.BlockSpec((D, LANES), lambda b, j: (0, 0))],
        out_specs=[pl.BlockSpec((1, TB, D), lambda b, j: (b, j, 0)),
                   pl.BlockSpec((1, TB, LANES), lambda b, j: (b, j, 0)),
                   pl.BlockSpec((1, 1, SUBLANES, TB), lambda b, j: (b, j, 0, 0)),
                   pl.BlockSpec((1, LANES), lambda b, j: (0, 0))],
        out_shape=[jax.ShapeDtypeStruct((B, S, D), F32), jax.ShapeDtypeStruct((B, S, LANES), F32),
                   jax.ShapeDtypeStruct((B, S // TB, SUBLANES, TB), F32), jax.ShapeDtypeStruct((1, LANES), F32)],
        scratch_shapes=[pltpu.VMEM((1, LANES), F32)],
        compiler_params=_cparams(("arbitrary", "arbitrary")),
        name="moe_router",
    )(X, mod, gain.reshape(1, D), w_router)


def _routing_tables(metat, counts):
    B, nj = metat.shape[:2]
    n_tok = B * nj * TB
    mi = jnp.transpose(metat[:, :, 0:4, :], (2, 0, 1, 3)).reshape(4, n_tok).astype(jnp.int32)
    cnt = counts[0, :MOE_EXPERTS].astype(jnp.int32)
    padded = (cnt + EXPERT_BLOCK - 1) // EXPERT_BLOCK * EXPERT_BLOCK
    pad_end = jnp.cumsum(padded)
    pad_start = pad_end - padded
    dest = jnp.concatenate([pad_start[mi[0]] + mi[2], pad_start[mi[1]] + mi[3]])
    n_blocks = -(-(n_tok * MOE_TOP_K) // EXPERT_BLOCK) + MOE_EXPERTS
    tok = jnp.arange(n_tok, dtype=jnp.int32)
    rows = jnp.zeros((n_blocks * EXPERT_BLOCK,), jnp.int32).at[dest].set(jnp.concatenate([tok, tok]))
    starts = jnp.arange(n_blocks, dtype=jnp.int32) * EXPERT_BLOCK
    block_e = jnp.minimum(jnp.sum((pad_end[None, :] <= starts[:, None]).astype(jnp.int32), axis=1), MOE_EXPERTS - 1)
    n_used = (pad_end[-1] // EXPERT_BLOCK).astype(jnp.int32).reshape(1)
    return dest, rows, block_e, n_used, n_blocks


def _expert_kernel(be_ref, nu_ref, rows_ref, hf_ref, wg_ref, wu_ref, wd_ref, o_ref, xbuf, wg_bf, wu_bf, wd_bf, sem):
    j = pl.program_id(0)
    n_used = nu_ref[0]
    EB = EXPERT_BLOCK

    def row_copy(blk, slot, r):
        row = rows_ref[blk * EB + r]
        return pltpu.make_async_copy(hf_ref.at[pl.ds(row, 1), :], xbuf.at[slot, pl.ds(r, 1), :], sem.at[slot])

    def issue(blk, slot):
        def body(r, carry):
            row_copy(blk, slot, r).start()
            return carry
        lax.fori_loop(0, EB, body, 0, unroll=DMA_UNROLL)

    @pl.when(j == 0)
    def _():
        issue(0, 0)

        @pl.when(n_used > 1)
        def _():
            issue(1, 1)

    def compute(prefetch_ahead):
        slot = lax.rem(j, GATHER_SLOTS)

        def body(r, carry):
            row_copy(j, slot, r).wait()
            return carry
        lax.fori_loop(0, EB, body, 0, unroll=DMA_UNROLL)

        @pl.when(jnp.logical_or(j == 0, be_ref[j] != be_ref[jnp.maximum(j - 1, 0)]))
        def _():
            wg_bf[...] = wg_ref[0, 0].astype(BF16)
            wu_bf[...] = wu_ref[0, 0].astype(BF16)
            wd_bf[...] = wd_ref[0, 0].astype(BF16)

        xb = xbuf[slot].astype(BF16)
        if prefetch_ahead:
            ahead = lax.rem(j + 2, GATHER_SLOTS)
            for r in range(EB):
                row_copy(j + 2, ahead, r).start()
        hg = jnp.dot(xb, wg_bf[...], preferred_element_type=F32)
        hu = jnp.dot(xb, wu_bf[...], preferred_element_type=F32)
        hm = (hg * jax.nn.sigmoid(hg) * hu).astype(BF16)
        o_ref[...] = jnp.dot(hm, wd_bf[...], preferred_element_type=F32)

    @pl.when(j + 2 < n_used)
    def _():
        compute(True)

    @pl.when(jnp.logical_and(j < n_used, j + 2 >= n_used))
    def _():
        compute(False)

    @pl.when(j >= n_used)
    def _():
        o_ref[...] = jnp.zeros_like(o_ref)


def _experts(hf2d, rows, block_e, n_used, n_blocks, layer, w_gate, w_up, w_down):
    N, D = hf2d.shape
    _, E, _, De = w_gate.shape
    EB = EXPERT_BLOCK
    grid_spec = pltpu.PrefetchScalarGridSpec(
        num_scalar_prefetch=3,
        grid=(n_blocks,),
        in_specs=[pl.BlockSpec(memory_space=pl.ANY),
                  pl.BlockSpec((1, 1, D, De), lambda j, be, nu, rw: (layer, be[j], 0, 0)),
                  pl.BlockSpec((1, 1, D, De), lambda j, be, nu, rw: (layer, be[j], 0, 0)),
                  pl.BlockSpec((1, 1, De, D), lambda j, be, nu, rw: (layer, be[j], 0, 0))],
        out_specs=pl.BlockSpec((EB, D), lambda j, be, nu, rw: (j, 0)),
        scratch_shapes=[pltpu.VMEM((GATHER_SLOTS, EB, D), F32), pltpu.VMEM((D, De), BF16), pltpu.VMEM((D, De), BF16),
                        pltpu.VMEM((De, D), BF16), pltpu.SemaphoreType.DMA((GATHER_SLOTS,))],
    )
    return pl.pallas_call(
        _expert_kernel,
        grid_spec=grid_spec,
        out_shape=jax.ShapeDtypeStruct((n_blocks * EB, D), F32),
        compiler_params=_cparams(("arbitrary",)),
        name="moe_experts",
    )(block_e, n_used, rows, hf2d, w_gate, w_up, w_down)


def _combine_kernel(dest_ref, yb_ref, x_ref, meta_ref, mod_ref, *rest, nj, final):
    if final:
        fn_ref, o_ref, ybuf, sem = rest
    else:
        o_ref, ybuf, sem = rest
    b = pl.program_id(0)
    j = pl.program_id(1)
    step = b * nj + j
    n_steps = pl.num_programs(0) * nj
    n_tok = n_steps * TB

    def row_copy(st, slot, r, k):
        dst_row = dest_ref[k * n_tok + st * TB + r]
        return pltpu.make_async_copy(yb_ref.at[pl.ds(dst_row, 1), :], ybuf.at[slot, k, pl.ds(r, 1), :], sem.at[slot])

    def issue(st, slot):
        def body(r, carry):
            row_copy(st, slot, r, 0).start()
            row_copy(st, slot, r, 1).start()
            return carry
        lax.fori_loop(0, TB, body, 0, unroll=DMA_UNROLL)

    @pl.when(step == 0)
    def _():
        issue(0, 0)
        issue(1, 1)

    slot = lax.rem(step, GATHER_SLOTS)

    @pl.when(step + 2 < n_steps)
    def _():
        ahead = lax.rem(step + 2, GATHER_SLOTS)
        for r in range(TB):
            row_copy(step + 2, ahead, r, 0).start()
            row_copy(step + 2, ahead, r, 1).start()

    def wbody(r, carry):
        row_copy(step, slot, r, 0).wait()
        row_copy(step, slot, r, 1).wait()
        return carry
    lax.fori_loop(0, TB, wbody, 0, unroll=DMA_UNROLL)
    meta = meta_ref[0]
    f = meta[:, 4:5] * ybuf[slot, 0] + meta[:, 5:6] * ybuf[slot, 1]
    m = mod_ref[0, 0]
    xn = x_ref[0] + m[5:6] * f
    if final:
        ms = jnp.mean(xn * xn, axis=-1, keepdims=True)
        xn = xn * lax.rsqrt(ms + NORM_EPS) * fn_ref[...]
    o_ref[0] = xn


def _combine(dest, yb, X, meta, mod, all_lat, final_gain=None):
    B, S, D = X.shape
    nj = S // TB
    final = final_gain is not None
    in_specs = [pl.BlockSpec(memory_space=pl.ANY),
                pl.BlockSpec((1, TB, D), lambda b, j, ds: (b, j, 0)),
                pl.BlockSpec((1, TB, LANES), lambda b, j, ds: (b, j, 0)),
                (pl.BlockSpec((1, 1, 6, D), lambda b, j, ds: (b, 1, 0, 0)) if all_lat else
                 pl.BlockSpec((1, 1, 6, D), lambda b, j, ds: (b, jnp.minimum(j, 1), 0, 0)))]
    args = [dest, yb, X, meta, mod]
    if final:
        in_specs.append(pl.BlockSpec((1, D), lambda b, j, ds: (0, 0)))
        args.append(final_gain.reshape(1, D))
    grid_spec = pltpu.PrefetchScalarGridSpec(
        num_scalar_prefetch=1,
        grid=(B, nj),
        in_specs=in_specs,
        out_specs=pl.BlockSpec((1, TB, D), lambda b, j, ds: (b, j, 0)),
        scratch_shapes=[pltpu.VMEM((GATHER_SLOTS, 2, TB, D), F32), pltpu.SemaphoreType.DMA((GATHER_SLOTS,))],
    )
    return pl.pallas_call(
        functools.partial(_combine_kernel, nj=nj, final=final),
        grid_spec=grid_spec,
        out_shape=jax.ShapeDtypeStruct((B, S, D), F32),
        compiler_params=_cparams(("arbitrary", "arbitrary")),
        name="moe_combine_final" if final else "moe_combine",
    )(*args)


def _moe(X, mod, gain, w_router, layer, w_gate, w_up, w_down, all_lat, final_gain=None):
    B, S, D = X.shape
    hf, meta, metat, counts = _router(X, mod, gain, w_router, all_lat)
    dest, rows, block_e, n_used, n_blocks = _routing_tables(metat, counts)
    yb = _experts(hf.reshape(B * S, D), rows, block_e, n_used, n_blocks, layer, w_gate, w_up, w_down)
    return _combine(dest, yb, X, meta, mod, all_lat, final_gain)


def _diff_attn_kernel(q_ref, k_ref, v_ref, dl_ref, sg_ref, o_ref, vx_ref, *, lam_init):
    @pl.when(pl.program_id(2) == 0)
    def _():
        vlane = lax.broadcasted_iota(jnp.int32, (vx_ref.shape[0], LANES), 1)
        vx_ref[:, 0:C_VDIM] = v_ref[0]
        vx_ref[:, C_VDIM:C_VDIM + LANES] = jnp.where(vlane == 0, 1.0, 0.0).astype(BF16)

    q = q_ref[0] * (C_HEAD ** -0.5)
    k = k_ref[0]
    lane = lax.broadcasted_iota(jnp.int32, q.shape, 1)
    zero = jnp.zeros_like(q)
    q1 = jnp.where(lane < C_HEAD, q, zero)
    q2 = jnp.where(lane >= C_HEAD, q, zero)
    dl = dl_ref[...]
    lam = (jnp.exp(jnp.sum(dl[0:1] * dl[1:2], axis=1, keepdims=True))
           - jnp.exp(jnp.sum(dl[2:3] * dl[3:4], axis=1, keepdims=True)) + lam_init)

    s = [lax.dot_general(qq, k, (((1,), (1,)), ((), ())), preferred_element_type=F32) for qq in (q1, q2)]
    e = [jnp.exp(si - jnp.max(si, axis=1, keepdims=True)).astype(BF16) for si in s]
    oz = [jnp.dot(ei, vx_ref[...], preferred_element_type=F32) for ei in e]
    on = [ozi[:, 0:C_VDIM] * (1.0 / ozi[:, C_VDIM:C_VDIM + 1]) for ozi in oz]
    o = on[0] - lam * on[1]
    ms = jnp.mean(o * o, axis=-1, keepdims=True)
    o_ref[0] = (o * lax.rsqrt(ms + DIFF_EPS) * sg_ref[...] * (1.0 - lam_init)).astype(o_ref.dtype)


def _diff_attn(P2, t_ctx, diff_lambda, diff_subln, lam_init):
    B, S, _ = P2.shape
    T = S - t_ctx
    qoff = t_ctx // ATT_TQ
    return pl.pallas_call(
        functools.partial(_diff_attn_kernel, lam_init=lam_init),
        grid=(B, C_HEADS, T // ATT_TQ),
        in_specs=[pl.BlockSpec((1, ATT_TQ, LANES), lambda b, h, q: (b, q + qoff, C_Q_OFF // LANES + h)),
                  pl.BlockSpec((1, S, LANES), lambda b, h, q: (b, 0, C_K_OFF // LANES + h)),
                  pl.BlockSpec((1, S, LANES), lambda b, h, q: (b, 0, C_V_OFF // LANES + h)),
                  pl.BlockSpec((4, C_HEAD), lambda b, h, q: (0, 0)),
                  pl.BlockSpec((1, C_VDIM), lambda b, h, q: (0, 0))],
        out_specs=pl.BlockSpec((1, ATT_TQ, LANES), lambda b, h, q: (b, q, h)),
        out_shape=jax.ShapeDtypeStruct((B, T, C_DIM), BF16),
        scratch_shapes=[pltpu.VMEM((S, C_VDIM + LANES), BF16)],
        compiler_params=_cparams(("arbitrary", "arbitrary", "arbitrary")),
        name="diff_attn",
    )(P2, P2, P2, diff_lambda, diff_subln.reshape(1, -1))


def _retention_kernel(q_ref, k_ref, v0_ref, v1_ref, o_ref, r_ref):
    d = pl.program_id(1)
    c = pl.program_id(2)

    @pl.when(c == 0)
    def _():
        r_ref[...] = jnp.zeros_like(r_ref)

    C = RET_CHUNK
    ti = lax.broadcasted_iota(jnp.int32, (C, C), 0)
    si = lax.broadcasted_iota(jnp.int32, (C, C), 1)
    rel = ((ti - si) * (1 - 2 * d)).astype(F32)
    col = lax.broadcasted_iota(jnp.int32, (C, 1), 0)
    pos = jnp.where(d == 0, col, C - 1 - col).astype(F32)
    chains = [(bi, h) for bi in range(RET_NB) for h in range(D_HEADS)]
    r0s = [r_ref[bi * D_HEADS + h] for bi, h in chains]
    decs = []
    for h in range(D_HEADS):
        lg_f = math.log(1.0 - 2.0 ** (-5.0 - h))
        lg_b = math.log(1.0 - 2.0 ** (-5.0 - (D_HEADS - 1 - h)))
        lg = jnp.where(d == 0, jnp.full((1, 1), lg_f, F32), jnp.full((1, 1), lg_b, F32))
        decs.append((jnp.where(rel >= 0, jnp.exp(lg * jnp.maximum(rel, 0.0)), 0.0),
                     jnp.exp(lg * (pos + 1.0)), jnp.exp(lg * (C - 1.0 - pos)), jnp.exp(lg * C)))
    outs, r_new = [], []
    for n, (bi, h) in enumerate(chains):
        inner_dec, q_dec, k_dec, c_dec = decs[h]
        q = q_ref[bi, :, h * D_KDIM:(h + 1) * D_KDIM]
        k = k_ref[bi, :, h * D_KDIM:(h + 1) * D_KDIM]
        vref = v0_ref if h < 2 else v1_ref
        v = vref[bi, :, (h % 2) * D_VDIM:(h % 2 + 1) * D_VDIM]
        inner = lax.dot_general(q, k, (((1,), (1,)), ((), ())), preferred_element_type=F32) * inner_dec
        outs.append(_bdot(inner, v) + _bdot(q, r0s[n]) * q_dec)
        r_new.append(r0s[n] * c_dec + _bdot(k.astype(F32) * k_dec, v, ((0,), (0,))))
    for n in range(len(chains)):
        r_ref[n] = r_new[n]
    for bi in range(RET_NB):
        o_ref[0, bi] = jnp.concatenate(outs[bi * D_HEADS:(bi + 1) * D_HEADS], axis=1)


def _retention(P2, t_ctx):
    B, S, _ = P2.shape
    C = RET_CHUNK
    nc = S // C
    nctx = t_ctx // C

    def cidx(d, c):
        back = jnp.where(c < nctx, nctx - 1 - c, nc - 1 + nctx - c)
        return jnp.where(d == 0, c, back)

    w = D_HEADS * D_KDIM
    return pl.pallas_call(
        _retention_kernel,
        grid=(B // RET_NB, 2, nc),
        in_specs=[pl.BlockSpec((RET_NB, C, w), lambda b, d, c: (b, cidx(d, c), D_Q_OFF // w)),
                  pl.BlockSpec((RET_NB, C, w), lambda b, d, c: (b, cidx(d, c), D_K_OFF // w)),
                  pl.BlockSpec((RET_NB, C, w), lambda b, d, c: (b, cidx(d, c), D_V_OFF // w)),
                  pl.BlockSpec((RET_NB, C, w), lambda b, d, c: (b, cidx(d, c), D_V_OFF // w + 1))],
        out_specs=pl.BlockSpec((1, RET_NB, C, D_DIM), lambda b, d, c: (d, b, cidx(d, c), 0)),
        out_shape=jax.ShapeDtypeStruct((2, B, S, D_DIM), F32),
        scratch_shapes=[pltpu.VMEM((RET_NB * D_HEADS, D_KDIM, D_VDIM), F32)],
        compiler_params=_cparams(("arbitrary", "arbitrary", "arbitrary")),
        name="retention",
    )(P2, P2, P2, P2)


def _odd_merge_kernel(c_ref, o2_ref, gate_ref, x_ref, mod_ref, rn_ref, wo_ref, o_ref):
    o = o2_ref[0, 0] + o2_ref[1, 0]
    parts = []
    for h in range(D_HEADS):
        oh = o[:, h * D_VDIM:(h + 1) * D_VDIM]
        ms = jnp.mean(oh * oh, axis=-1, keepdims=True)
        parts.append(oh * lax.rsqrt(ms + NORM_EPS))
    gt = gate_ref[0].astype(F32)
    dl = jnp.concatenate(parts, axis=1) * rn_ref[...] * (gt * jax.nn.sigmoid(gt))
    out = (jnp.dot(c_ref[0], wo_ref[0:C_DIM, :], preferred_element_type=F32)
           + _bdot(dl, wo_ref[C_DIM:C_DIM + D_DIM, :]))
    m = mod_ref[0, 0]
    o_ref[0] = x_ref[0] + m[2:3] * out


def _odd_merge(c_lat, o2, P2, X, mod, ret_norm, wo_bf16, t_ctx):
    B, S, D = X.shape
    T = S - t_ctx
    off = t_ctx // TB
    return pl.pallas_call(
        _odd_merge_kernel,
        grid=(B, T // TB),
        in_specs=[pl.BlockSpec((1, TB, C_DIM), lambda b, j: (b, j, 0)),
                  pl.BlockSpec((2, 1, TB, D_DIM), lambda b, j: (0, b, j + off, 0)),
                  pl.BlockSpec((1, TB, D_DIM), lambda b, j: (b, j + off, D_G_OFF // D_DIM)),
                  pl.BlockSpec((1, TB, D), lambda b, j: (b, j + off, 0)),
                  _mod_spec(D, all_lat=True),
                  pl.BlockSpec((1, D_DIM), lambda b, j: (0, 0)),
                  pl.BlockSpec((C_DIM + D_DIM, D), lambda b, j: (0, 0))],
        out_specs=pl.BlockSpec((1, TB, D), lambda b, j: (b, j, 0)),
        out_shape=jax.ShapeDtypeStruct((B, T, D), F32),
        compiler_params=_cparams(("arbitrary", "arbitrary")),
        name="odd_merge",
    )(c_lat, o2, P2, X, mod, ret_norm.reshape(1, -1), wo_bf16)


def _rope_tables(t_ctx, T):
    half = C_HEAD // 4
    inv = ROPE_BASE ** (-jnp.arange(half, dtype=F32) / half)
    t = jnp.arange(T, dtype=jnp.int32)
    ar = (t // GRID_W).astype(F32)[:, None] * inv[None, :]
    ac = (t % GRID_W).astype(F32)[:, None] * inv[None, :]
    z = jnp.zeros_like(ar)
    cos = jnp.concatenate([jnp.cos(ar), jnp.cos(ar), jnp.cos(ac), jnp.cos(ac)], axis=1)
    s1 = jnp.concatenate([-jnp.sin(ar), z, -jnp.sin(ac), z], axis=1)
    s2 = jnp.concatenate([z, jnp.sin(ar), z, jnp.sin(ac)], axis=1)

    def full(tab, ctx_val):
        tab = jnp.concatenate([jnp.full((t_ctx, C_HEAD), ctx_val, F32), tab], axis=0)
        return jnp.tile(tab, (1, LANES // C_HEAD))

    return full(cos, 1.0), full(s1, 0.0), full(s2, 0.0)


def _head_indicator(n, width):
    i = jnp.arange(n) // width
    return (i[:, None] == i[None, :]).astype(BF16)


def kernel(x, c, ctx, c_ctx, ada_w, ada_b, norm_mix, norm_ffn, ev_w_in, ev_w_out, rwkv_shift, rwkv_w0, rwkv_w_up, rwkv_a0, rwkv_a_up, rwkv_g_up, rwkv_k_k, rwkv_k_a, rwkv_r_k, rwkv_ln_g, rwkv_ln_b, pool_w, pool_scale, od_w_in, od_w_out, diff_lambda, diff_subln, ret_norm, moe_router_group, moe_router_expert, moe_w_gate, moe_w_up, moe_w_down, final_norm):
    B, T, D = x.shape
    t_ctx = ctx.shape[1]
    assert ada_w.shape[0] == 2 and t_ctx == TB and T % TB == 0 and T % GRID_W == 0 and B < 16
    X = jnp.concatenate([ctx, x], axis=1)

    cpad = jnp.zeros((16, D), F32).at[:B].set(c).at[B].set(c_ctx)
    mods = _ada(cpad, ada_w, ada_b)

    def layer_mod(i):
        lat = mods[i, :B]
        cm = jnp.broadcast_to(mods[i, B][None], lat.shape)
        return jnp.stack([cm, lat], axis=1).reshape(B, 2, 6, D)

    def router_w(i):
        w = jnp.concatenate([moe_router_group[i], moe_router_expert[i]], axis=1)
        return jnp.pad(w, ((0, 0), (0, LANES - w.shape[1])))

    ehead = _head_indicator(A_DIM, A_HEAD)

    mod0 = layer_mod(0)
    P = _proj(X, mod0, norm_mix[0], ev_w_in[0].astype(BF16), F32)
    zl = jnp.zeros((LORA_W, A_DIM), F32)
    lora_blocks = []
    for d in range(2):
        lora_blocks.append(jnp.concatenate([rwkv_w_up[0, d], zl], axis=0))
        lora_blocks.append(jnp.concatenate([zl, rwkv_a_up[0, d]], axis=0))
    lora_w = jnp.concatenate(lora_blocks, axis=1).astype(BF16)
    wa0 = jnp.concatenate([rwkv_w0[0, 0], rwkv_a0[0, 0], rwkv_w0[0, 1], rwkv_a0[0, 1]]).reshape(1, -1)
    lw, kt, bb, v, kk, r, bonus, g, yb = _even_prep(
        P, t_ctx, rwkv_shift[0], wa0, lora_w, rwkv_g_up[0].astype(BF16), rwkv_k_k[0], rwkv_k_a[0], rwkv_r_k[0],
        pool_w[0].astype(BF16), pool_scale[0], ehead)
    y2 = _rwkv_scan(lw, kt, bb, v, kk, r, t_ctx)
    X = _even_merge(y2, bonus, g, yb, X, mod0, rwkv_ln_g[0], rwkv_ln_b[0], ev_w_out[0].astype(BF16), ehead)
    X = _moe(X, mod0, norm_ffn[0], router_w(0), 0, moe_w_gate, moe_w_up, moe_w_down, all_lat=False)

    mod1 = layer_mod(1)
    lam_init = 0.8 - 0.6 * math.exp(-0.3 * 1)
    P2 = _proj(X, mod1, norm_mix[1], od_w_in[0].astype(BF16), BF16, rope=_rope_tables(t_ctx, T))
    c_lat = _diff_attn(P2, t_ctx, diff_lambda[0], diff_subln[0], lam_init)
    o2 = _retention(P2, t_ctx)
    XL = _odd_merge(c_lat, o2, P2, X, mod1, ret_norm[0], od_w_out[0].astype(BF16), t_ctx)
    return _moe(XL, mod1, norm_ffn[1], router_w(1), 1, moe_w_gate, moe_w_up, moe_w_down, all_lat=True,
                final_gain=final_norm)
```

```python
import functools
import math

import jax
import jax.numpy as jnp
from jax import lax
from jax.experimental import pallas as pl
from jax.experimental.pallas import tpu as pltpu

F32 = jnp.float32
BF16 = jnp.bfloat16
HI = lax.Precision.HIGHEST

GRID_W = 64
NORM_EPS = 1e-6
ROPE_BASE = 10000.0
A_HEADS, A_HEAD = 8, 64
A_DIM = A_HEADS * A_HEAD
LORA_W, LORA_A, LORA_G = 64, 64, 128
W_DECAY_SCALE = 0.606531
RWKV_GN_EPS = 64e-5
POOL_WINDOWS = (2, 4, 8, 16)
POOL_GROUP = 128
B_DIM = POOL_GROUP * 4
C_HEADS, C_HEAD = 4, 64
C_VDIM = 2 * C_HEAD
C_DIM = C_HEADS * C_VDIM
DIFF_EPS = 1e-5
D_HEADS, D_KDIM, D_VDIM = 4, 64, 128
D_DIM = D_HEADS * D_VDIM
A_K_OFF, A_V_OFF = 0, A_DIM
W_LORA_OFF = 2 * A_DIM
A_LORA_OFF = W_LORA_OFF + LORA_W
A_R_OFF = A_LORA_OFF + LORA_A
G_LORA_OFF = A_R_OFF + A_DIM
POOL_OFF = G_LORA_OFF + LORA_G
EVEN_IN_COLS = POOL_OFF + B_DIM
C_K_OFF = 0
C_V_OFF = C_HEADS * 2 * C_HEAD
D_K_OFF = C_V_OFF + C_DIM
D_V_OFF = D_K_OFF + D_HEADS * D_KDIM
C_Q_OFF = D_V_OFF + D_DIM
D_Q_OFF = C_Q_OFF + C_HEADS * 2 * C_HEAD
D_G_OFF = D_Q_OFF + D_HEADS * D_KDIM
ODD_IN_COLS = D_G_OFF + D_DIM
MOE_GROUPS, MOE_PER_GROUP = 4, 8
MOE_EXPERTS = MOE_GROUPS * MOE_PER_GROUP
MOE_TOP_K = 2
EXPERT_BLOCK = 256

LANES = 128
SUBLANES = 8
TB = 256
RWKV_CHUNK = 64
RET_CHUNK = 256
RET_NB = 2
SCAN_NB = 4
ATT_TQ = 256
VMEM_LIMIT = 56 * 1024 * 1024
DMA_UNROLL = 8
GATHER_SLOTS = 3
GATHER_DMA_PRIORITY = 1


def _cparams(sem):
    return pltpu.CompilerParams(dimension_semantics=sem, vmem_limit_bytes=VMEM_LIMIT)


def _modnorm(x, gain, scale, shift, eps=NORM_EPS):
    ms = jnp.mean(x * x, axis=-1, keepdims=True)
    return x * lax.rsqrt(ms + eps) * gain * (1.0 + scale) + shift


def _bdot(a, b, dims=((1,), (0,))):
    return lax.dot_general(a.astype(BF16), b.astype(BF16), (dims, ((), ())), preferred_element_type=F32)


def _bf16_terms(x, terms):
    parts = []
    for _ in range(terms):
        p = x.astype(BF16)
        parts.append(p)
        x = x - p.astype(F32)
    return parts


def _dot_split_lhs(x, w_bf16, terms=2):
    return sum(jnp.dot(p, w_bf16, preferred_element_type=F32) for p in _bf16_terms(x, terms))


def _ada_kernel(c_ref, w_ref, b_ref, o_ref):
    c = c_ref[...]
    s = c * jax.nn.sigmoid(c)
    o_ref[0] = jnp.dot(s, w_ref[0], precision=HI, preferred_element_type=F32) + b_ref[0]


def _ada(cpad, ada_w, ada_b):
    L, D, N = ada_w.shape
    tn = 1536
    return pl.pallas_call(
        _ada_kernel,
        grid=(L, N // tn),
        in_specs=[pl.BlockSpec((16, D), lambda l, n: (0, 0)),
                  pl.BlockSpec((1, D, tn), lambda l, n: (l, 0, n)),
                  pl.BlockSpec((1, 1, tn), lambda l, n: (l, 0, n))],
        out_specs=pl.BlockSpec((1, 16, tn), lambda l, n: (l, 0, n)),
        out_shape=jax.ShapeDtypeStruct((L, 16, N), F32),
        compiler_params=_cparams(("arbitrary", "arbitrary")),
        name="ada_mod",
    )(cpad, ada_w, ada_b.reshape(L, 1, N))


def _mod_spec(D, all_lat=False):
    if all_lat:
        return pl.BlockSpec((1, 1, 6, D), lambda b, j: (b, 1, 0, 0))
    return pl.BlockSpec((1, 1, 6, D), lambda b, j: (b, jnp.minimum(j, 1), 0, 0))


def _rope_plan():
    ranges = [(C_K_OFF, C_V_OFF), (D_K_OFF, D_V_OFF), (C_Q_OFF, D_G_OFF)]
    rope_chunks = tuple(g for lo, hi in ranges for g in range(lo // LANES, hi // LANES))
    kscale_chunks = tuple(range(D_K_OFF // LANES, D_V_OFF // LANES))
    return rope_chunks, kscale_chunks


def _project_block(h, w_ref, o_ref, rope_refs=None, before_piece=None):
    n_out = w_ref.shape[1]
    rope_chunks, kscale_chunks = _rope_plan() if rope_refs is not None else ((), ())
    cw = 512
    for n0 in range(0, n_out, cw):
        if before_piece is not None:
            before_piece(n0 // cw, n_out // cw)
        p = jnp.dot(h, w_ref[:, n0:n0 + cw], preferred_element_type=F32)
        if rope_refs is None:
            o_ref[0, :, n0:n0 + cw] = p.astype(o_ref.dtype)
            continue
        cos_ref, s1_ref, s2_ref = rope_refs
        for q in range(cw // LANES):
            gi = n0 // LANES + q
            sub = p[:, q * LANES:(q + 1) * LANES]
            if gi in rope_chunks:
                sub = (sub * cos_ref[...] + pltpu.roll(sub, LANES - 16, 1) * s1_ref[...]
                       + pltpu.roll(sub, 16, 1) * s2_ref[...])
            if gi in kscale_chunks:
                sub = sub * (D_KDIM ** -0.5)
            o_ref[0, :, n0 + q * LANES:n0 + (q + 1) * LANES] = sub.astype(o_ref.dtype)


def _proj_kernel(x_ref, mod_ref, g_ref, w_ref, o_ref):
    m = mod_ref[0, 0]
    h = _modnorm(x_ref[0], g_ref[...], m[1:2], m[0:1]).astype(BF16)
    _project_block(h, w_ref, o_ref)


def _proj(X, mod, gain, w_bf16, out_dtype):
    B, S, D = X.shape
    N = w_bf16.shape[1]
    return pl.pallas_call(
        _proj_kernel,
        grid=(B, S // TB),
        in_specs=[pl.BlockSpec((1, TB, D), lambda b, j: (b, j, 0)),
                  _mod_spec(D),
                  pl.BlockSpec((1, D), lambda b, j: (0, 0)),
                  pl.BlockSpec((D, N), lambda b, j: (0, 0))],
        out_specs=pl.BlockSpec((1, TB, N), lambda b, j: (b, j, 0)),
        out_shape=jax.ShapeDtypeStruct((B, S, N), out_dtype),
        compiler_params=_cparams(("arbitrary", "arbitrary")),
        name="proj",
    )(X, mod, gain.reshape(1, D), w_bf16)


def _even_prep_kernel(p_ref, pp_ref, pn_ref, sh_ref, wa0_ref, lora_ref, gup_ref, kkg_ref, ka_ref, rk_ref,
                      pw_ref, ps_ref, eh_ref,
                      lw_ref, kt_ref, bb_ref, v_ref, kkn_ref, r_ref, bonus_ref, g_ref, yb_ref,
                      *, nblk, t_ctx, t_lat):
    j = pl.program_id(1)
    prev_ok = j >= 2
    next_ok = jnp.logical_and(j >= 1, j < nblk - 1)
    n_ext = TB + 2 * SUBLANES

    def ext(c0, c1):
        pv = jnp.where(prev_ok, pp_ref[0, :, c0:c1], 0.0)
        nx = jnp.where(next_ok, pn_ref[0, :, c0:c1], 0.0)
        return jnp.concatenate([pv, p_ref[0, :, c0:c1], nx], axis=0)

    def shift3(c0, c1, k0):
        e = ext(c0, c1)
        kern = sh_ref[:, k0:k0 + (c1 - c0)]
        y = kern[0:1] * pltpu.roll(e, 1, 0) + kern[1:2] * e + kern[2:3] * pltpu.roll(e, n_ext - 1, 0)
        return y[SUBLANES:SUBLANES + TB]

    k = shift3(A_K_OFF, A_K_OFF + A_DIM, 0)
    v = shift3(A_V_OFF, A_V_OFF + A_DIM, A_DIM)
    r = shift3(A_R_OFF, A_R_OFF + A_DIM, 2 * A_DIM)

    c = p_ref[0, :, W_LORA_OFF:W_LORA_OFF + LANES]
    lane = lax.broadcasted_iota(jnp.int32, c.shape, 1)
    lin = jnp.where(lane < LORA_W, jnp.tanh(c), c)
    z = _bdot(lin, lora_ref[...]) + wa0_ref[...]

    eh = eh_ref[...]
    kk0 = k * kkg_ref[...]
    ss = _dot_split_lhs(kk0 * kk0, eh)
    kk = kk0 * lax.rsqrt(jnp.maximum(ss, 1e-12))
    kt_sum = None
    for d in range(2):
        zw = z[:, (2 * d) * A_DIM:(2 * d + 1) * A_DIM]
        za = z[:, (2 * d + 1) * A_DIM:(2 * d + 2) * A_DIM]
        a = jax.nn.sigmoid(za)
        kt = k * (1.0 + (a - 1.0) * ka_ref[...])
        lw_ref[d, 0] = -W_DECAY_SCALE * jax.nn.sigmoid(zw)
        kt_ref[d, 0] = kt
        bb_ref[d, 0] = a * kk
        kt_sum = kt if kt_sum is None else kt_sum + kt
    v_ref[0] = v
    kkn_ref[0] = kk
    r_ref[0] = r
    bonus_ref[0] = _dot_split_lhs(r * kt_sum * rk_ref[...], eh) * v
    gl = p_ref[0, :, G_LORA_OFF:G_LORA_OFF + LORA_G]
    g_ref[0] = _bdot(jax.nn.sigmoid(gl), gup_ref[...])

    row = lax.broadcasted_iota(jnp.int32, (TB, POOL_GROUP), 0)
    tpos = row + jnp.maximum(j - 1, 0) * TB
    tseg = jnp.where(j == 0, t_ctx, t_lat)
    for gi, win in enumerate(POOL_WINDOWS):
        c0 = POOL_OFF + gi * POOL_GROUP
        e = ext(c0, c0 + POOL_GROUP)
        s = e + pltpu.roll(e, 1, 0)
        sh = 1
        while 2 * sh < win:
            s = pltpu.roll(s, sh, 0) + pltpu.roll(s, n_ext - sh, 0)
            sh *= 2
        s = s[SUBLANES:SUBLANES + TB]
        lo = jnp.clip(tpos - win // 2, 0, tseg)
        hi = jnp.clip(tpos - win // 2 + win, 0, tseg)
        mean = s / (hi - lo).astype(F32)
        dd = mean - p_ref[0, :, c0:c0 + POOL_GROUP]
        y = _bdot(dd, pw_ref[gi])
        yb_ref[0, :, gi * POOL_GROUP:(gi + 1) * POOL_GROUP] = y * ps_ref[:, gi * POOL_GROUP:(gi + 1) * POOL_GROUP]


def _even_prep(P, t_ctx, shift_k, wa0, lora_w, g_up, k_k, k_a, r_k, pool_w, pool_scale, ehead):
    B, S, N = P.shape
    nblk = S // TB
    hb = TB // SUBLANES
    const2 = lambda b, j: (0, 0)
    in_specs = [
        pl.BlockSpec((1, TB, N), lambda b, j: (b, j, 0)),
        pl.BlockSpec((1, SUBLANES, N), lambda b, j: (b, jnp.maximum(j * hb - 1, 0), 0)),
        pl.BlockSpec((1, SUBLANES, N), lambda b, j: (b, jnp.minimum((j + 1) * hb, S // SUBLANES - 1), 0)),
        pl.BlockSpec((3, 3 * A_DIM), const2),
        pl.BlockSpec((1, 4 * A_DIM), const2),
        pl.BlockSpec((LANES, 4 * A_DIM), const2),
        pl.BlockSpec((LORA_G, A_DIM), const2),
        pl.BlockSpec((1, A_DIM), const2),
        pl.BlockSpec((1, A_DIM), const2),
        pl.BlockSpec((1, A_DIM), const2),
        pl.BlockSpec((4, POOL_GROUP, POOL_GROUP), lambda b, j: (0, 0, 0)),
        pl.BlockSpec((1, B_DIM), const2),
        pl.BlockSpec((A_DIM, A_DIM), const2),
    ]
    two = jax.ShapeDtypeStruct((2, B, S, A_DIM), F32)
    one = jax.ShapeDtypeStruct((B, S, A_DIM), F32)
    two_spec = pl.BlockSpec((2, 1, TB, A_DIM), lambda b, j: (0, b, j, 0))
    one_spec = pl.BlockSpec((1, TB, A_DIM), lambda b, j: (b, j, 0))
    return pl.pallas_call(
        functools.partial(_even_prep_kernel, nblk=nblk, t_ctx=t_ctx, t_lat=S - t_ctx),
        grid=(B, nblk),
        in_specs=in_specs,
        out_specs=[two_spec] * 3 + [one_spec] * 6,
        out_shape=[two] * 3 + [one] * 6,
        compiler_params=_cparams(("arbitrary", "arbitrary")),
        name="even_prep",
    )(P, P, P, shift_k, wa0, lora_w, g_up, k_k.reshape(1, -1), k_a.reshape(1, -1), r_k.reshape(1, -1),
      pool_w, pool_scale.reshape(1, -1), ehead)


def _rwkv_scan_kernel(lw_ref, kt_ref, bb_ref, v_ref, kk_ref, r_ref, y_ref, s_ref):
    d = pl.program_id(1)
    c = pl.program_id(2)

    @pl.when(c == 0)
    def _():
        s_ref[...] = jnp.zeros_like(s_ref)

    C = RWKV_CHUNK
    tt = lax.broadcasted_iota(jnp.int32, (C, C), 0)
    ii = lax.broadcasted_iota(jnp.int32, (C, C), 1)
    rel = (tt - ii) * (1 - 2 * d)
    strict = rel > 0
    incl = rel >= 0
    t2 = lax.broadcasted_iota(jnp.int32, (C, 2 * C), 0)
    i2 = lax.broadcasted_iota(jnp.int32, (C, 2 * C), 1) & (C - 1)
    incl2 = (t2 - i2) * (1 - 2 * d) >= 0
    tri = jnp.where(incl, 1.0, 0.0).astype(BF16)
    n_dbl = int(math.log2(C))
    hsl = [slice(h * A_HEAD, (h + 1) * A_HEAD) for h in range(A_HEADS)]
    ar, bk, vh, bkp, ptot = [], [], [], [], []
    for bi in range(SCAN_NB):
        lw = lw_ref[0, bi]
        cum = sum(jnp.dot(tri, p, preferred_element_type=F32) for p in _bf16_terms(lw, 3))
        tot = jnp.sum(lw, axis=0, keepdims=True)
        p_inv = jnp.exp(-cum)
        p_rest = jnp.exp(tot - cum)
        p_tot = jnp.exp(tot)
        bb = bb_ref[0, bi]
        kt = kt_ref[0, bi]
        at_all = (-kk_ref[bi] * jnp.exp(cum - lw)).astype(BF16)
        rt_all = (r_ref[bi] * jnp.exp(cum)).astype(BF16)
        bt_all = (bb * p_inv).astype(BF16)
        kti_all = (kt * p_inv).astype(BF16)
        bp_all = (bb * p_rest).astype(BF16)
        kp_all = (kt * p_rest).astype(BF16)
        v_all = v_ref[bi].astype(BF16)
        for s in hsl:
            ar.append(jnp.concatenate([at_all[:, s], rt_all[:, s]], axis=0))
            bk.append(jnp.concatenate([bt_all[:, s], kti_all[:, s]], axis=0))
            bkp.append(jnp.concatenate([bp_all[:, s], kp_all[:, s]], axis=0))
            vh.append(v_all[:, s])
            ptot.append(p_tot[:, s])
    chains = range(SCAN_NB * A_HEADS)
    s0 = [s_ref[n] for n in chains]
    m = [_bdot(ar[n], bk[n], ((1,), (1,))) for n in chains]
    arh = [_bdot(ar[n], s0[n], ((1,), (1,))) for n in chains]
    lp = [jnp.where(strict, m[n][:C, :C], 0.0) for n in chains]
    a_ak = [jnp.where(strict, m[n][:C, C:], 0.0) for n in chains]
    a_r = [jnp.where(incl2, m[n][C:, :], 0.0) for n in chains]
    u = [arh[n][:C] + _bdot(a_ak[n], vh[n]) for n in chains]
    for q in range(n_dbl):
        u = [u[n] + _bdot(lp[n], u[n]) for n in chains]
        if q < n_dbl - 1:
            lp = [_bdot(lp[n], lp[n]) for n in chains]
    uv = [jnp.concatenate([u[n].astype(BF16), vh[n]], axis=0) for n in chains]
    outs = [arh[n][C:] + _bdot(a_r[n], uv[n]) for n in chains]
    s_new = [s0[n] * ptot[n] + _bdot(uv[n], bkp[n], ((0,), (0,))) for n in chains]
    for n in chains:
        s_ref[n] = s_new[n]
    for bi in range(SCAN_NB):
        y_ref[0, bi] = jnp.concatenate(outs[bi * A_HEADS:(bi + 1) * A_HEADS], axis=1)


def _rwkv_scan(lw, kt, bb, v, kk, r, t_ctx):
    _, B, S, _ = lw.shape
    C = RWKV_CHUNK
    nc = S // C
    nctx = t_ctx // C

    def cidx(d, c):
        back = jnp.where(c < nctx, nctx - 1 - c, nc - 1 + nctx - c)
        return jnp.where(d == 0, c, back)

    two_spec = pl.BlockSpec((1, SCAN_NB, C, A_DIM), lambda b, d, c: (d, b, cidx(d, c), 0))
    one_spec = pl.BlockSpec((SCAN_NB, C, A_DIM), lambda b, d, c: (b, cidx(d, c), 0))
    return pl.pallas_call(
        _rwkv_scan_kernel,
        grid=(B // SCAN_NB, 2, nc),
        in_specs=[two_spec] * 3 + [one_spec] * 3,
        out_specs=two_spec,
        out_shape=jax.ShapeDtypeStruct((2, B, S, A_DIM), F32),
        scratch_shapes=[pltpu.VMEM((SCAN_NB * A_HEADS, A_HEAD, A_HEAD), F32)],
        compiler_params=_cparams(("arbitrary", "arbitrary", "arbitrary")),
        name="rwkv_scan",
    )(lw, kt, bb, v, kk, r)


def _even_merge_kernel(y2_ref, bonus_ref, g_ref, yb_ref, x_ref, mod_ref, lng_ref, lnb_ref, wo_ref, eh_ref, o_ref):
    y = y2_ref[0, 0] + y2_ref[1, 0]
    eh = eh_ref[...]
    mu = _dot_split_lhs(y, eh) * (1.0 / A_HEAD)
    yc = y - mu
    var = _dot_split_lhs(yc * yc, eh) * (1.0 / A_HEAD)
    yn = yc * lax.rsqrt(var + RWKV_GN_EPS) * lng_ref[...] + lnb_ref[...]
    ya = (yn + bonus_ref[0]) * g_ref[0]
    out = _bdot(ya, wo_ref[0:A_DIM, :]) + _bdot(yb_ref[0], wo_ref[A_DIM:A_DIM + B_DIM, :])
    m = mod_ref[0, 0]
    o_ref[0] = x_ref[0] + m[2:3] * out


def _even_merge(y2, bonus, g, yb, X, mod, ln_g, ln_b, wo_bf16, ehead):
    B, S, D = X.shape
    one_spec = pl.BlockSpec((1, TB, A_DIM), lambda b, j: (b, j, 0))
    const2 = lambda b, j: (0, 0)
    return pl.pallas_call(
        _even_merge_kernel,
        grid=(B, S // TB),
        in_specs=[pl.BlockSpec((2, 1, TB, A_DIM), lambda b, j: (0, b, j, 0)), one_spec, one_spec, one_spec,
                  pl.BlockSpec((1, TB, D), lambda b, j: (b, j, 0)), _mod_spec(D),
                  pl.BlockSpec((1, A_DIM), const2), pl.BlockSpec((1, A_DIM), const2),
                  pl.BlockSpec((A_DIM + B_DIM, D), const2), pl.BlockSpec((A_DIM, A_DIM), const2)],
        out_specs=pl.BlockSpec((1, TB, D), lambda b, j: (b, j, 0)),
        out_shape=jax.ShapeDtypeStruct((B, S, D), F32),
        compiler_params=_cparams(("arbitrary", "arbitrary")),
        name="even_merge",
    )(y2, bonus, g, yb, X, mod, ln_g.reshape(1, -1), ln_b.reshape(1, -1), wo_bf16, ehead)


def _router_kernel(x_ref, mod_ref, g_ref, wr_ref, hf_ref, meta_ref, metat_ref, cnt_ref, carry_ref):
    first = jnp.logical_and(pl.program_id(0) == 0, pl.program_id(1) == 0)

    @pl.when(first)
    def _():
        carry_ref[...] = jnp.zeros_like(carry_ref)

    m = mod_ref[0, 0]
    hf = _modnorm(x_ref[0], g_ref[...], m[4:5], m[3:4])
    hf_ref[0] = hf
    h0, h1 = _bf16_terms(hf, 2)
    w0, w1 = _bf16_terms(wr_ref[...], 2)
    logits = (jnp.dot(h0, w0, preferred_element_type=F32) + jnp.dot(h0, w1, preferred_element_type=F32)
              + jnp.dot(h1, w0, preferred_element_type=F32))
    lane = lax.broadcasted_iota(jnp.int32, logits.shape, 1)
    neg = -jnp.inf
    lg = jnp.where(lane < MOE_GROUPS, logits, neg)
    mg = jnp.max(lg, axis=1, keepdims=True)
    grp = jnp.min(jnp.where(lg == mg, lane, LANES), axis=1, keepdims=True)
    gw = 1.0 / jnp.sum(jnp.exp(lg - mg), axis=1, keepdims=True)
    el = lane - MOE_GROUPS
    in_grp = jnp.logical_and(el >= grp * MOE_PER_GROUP, el < (grp + 1) * MOE_PER_GROUP)
    le = jnp.where(in_grp, logits, neg)
    m0 = jnp.max(le, axis=1, keepdims=True)
    i0 = jnp.min(jnp.where(le == m0, lane, LANES), axis=1, keepdims=True)
    le1 = jnp.where(lane == i0, neg, le)
    m1 = jnp.max(le1, axis=1, keepdims=True)
    i1 = jnp.min(jnp.where(le1 == m1, lane, LANES), axis=1, keepdims=True)
    p1 = jnp.exp(m1 - m0)
    gate0 = gw / (1.0 + p1)
    gate1 = gw * p1 / (1.0 + p1)
    e0 = i0 - MOE_GROUPS
    e1 = i1 - MOE_GROUPS
    oh0 = (lane == e0).astype(F32)
    oh1 = (lane == e1).astype(F32)
    cnt = oh0 + oh1
    rr = lax.broadcasted_iota(jnp.int32, (TB, TB), 0)
    cc = lax.broadcasted_iota(jnp.int32, (TB, TB), 1)
    before = _bdot((cc < rr).astype(F32), cnt) + carry_ref[...]
    rank0 = jnp.sum(oh0 * before, axis=1, keepdims=True)
    rank1 = jnp.sum(oh1 * before, axis=1, keepdims=True)
    carry_ref[...] = carry_ref[...] + jnp.sum(cnt, axis=0, keepdims=True)
    cnt_ref[...] = carry_ref[...]
    meta = jnp.where(lane == 0, e0.astype(F32), 0.0)
    meta = jnp.where(lane == 1, e1.astype(F32), meta)
    meta = jnp.where(lane == 2, rank0, meta)
    meta = jnp.where(lane == 3, rank1, meta)
    meta = jnp.where(lane == 4, gate0, meta)
    meta = jnp.where(lane == 5, gate1, meta)
    meta_ref[0] = meta
    metat_ref[0, 0] = meta.T[0:SUBLANES]


def _router(X, mod, gain, w_router, all_lat):
    B, S, D = X.shape
    return pl.pallas_call(
        _router_kernel,
        grid=(B, S // TB),
        in_specs=[pl.BlockSpec((1, TB, D), lambda b, j: (b, j, 0)), _mod_spec(D, all_lat),
                  pl.BlockSpec((1, D), lambda b, j: (0, 0)), pl.BlockSpec((D, LANES), lambda b, j: (0, 0))],
        out_specs=[pl.BlockSpec((1, TB, D), lambda b, j: (b, j, 0)),
                   pl.BlockSpec((1, TB, LANES), lambda b, j: (b, j, 0)),
                   pl.BlockSpec((1, 1, SUBLANES, TB), lambda b, j: (b, j, 0, 0)),
                   pl.BlockSpec((1, LANES), lambda b, j: (0, 0))],
        out_shape=[jax.ShapeDtypeStruct((B, S, D), F32), jax.ShapeDtypeStruct((B, S, LANES), F32),
                   jax.ShapeDtypeStruct((B, S // TB, SUBLANES, TB), F32), jax.ShapeDtypeStruct((1, LANES), F32)],
        scratch_shapes=[pltpu.VMEM((1, LANES), F32)],
        compiler_params=_cparams(("arbitrary", "arbitrary")),
        name="moe_router",
    )(X, mod, gain.reshape(1, D), w_router)


def _routing_tables(metat, counts):
    B, nj = metat.shape[:2]
    n_tok = B * nj * TB
    mi = jnp.transpose(metat[:, :, 0:4, :], (2, 0, 1, 3)).reshape(4, n_tok).astype(jnp.int32)
    cnt = counts[0, :MOE_EXPERTS].astype(jnp.int32)
    padded = (cnt + EXPERT_BLOCK - 1) // EXPERT_BLOCK * EXPERT_BLOCK
    pad_end = jnp.cumsum(padded)
    pad_start = pad_end - padded
    dest = jnp.concatenate([pad_start[mi[0]] + mi[2], pad_start[mi[1]] + mi[3]])
    n_blocks = -(-(n_tok * MOE_TOP_K) // EXPERT_BLOCK) + MOE_EXPERTS
    tok = jnp.arange(n_tok, dtype=jnp.int32)
    fill = jnp.arange(n_blocks * EXPERT_BLOCK, dtype=jnp.int32) % n_tok
    rows = fill.at[dest].set(jnp.concatenate([tok, tok]))
    starts = jnp.arange(n_blocks, dtype=jnp.int32) * EXPERT_BLOCK
    block_e = jnp.minimum(jnp.sum((pad_end[None, :] <= starts[:, None]).astype(jnp.int32), axis=1), MOE_EXPERTS - 1)
    n_used = (pad_end[-1] // EXPERT_BLOCK).astype(jnp.int32).reshape(1)
    return dest, rows, block_e, n_used, n_blocks


def _expert_kernel(be_ref, nu_ref, rows_ref, hf_ref, wg_ref, wu_ref, wd_ref, o_ref, xbuf, wg_bf, wu_bf, wd_bf, sem):
    j = pl.program_id(0)
    n_used = nu_ref[0]
    EB = EXPERT_BLOCK

    def row_copy(blk, slot, r):
        row = rows_ref[blk * EB + r]
        return pltpu.make_async_copy(hf_ref.at[pl.ds(row, 1), :], xbuf.at[slot, pl.ds(r, 1), :], sem.at[slot])

    def issue(blk, slot):
        def body(r, carry):
            row_copy(blk, slot, r).start()
            return carry
        lax.fori_loop(0, EB, body, 0, unroll=DMA_UNROLL)

    @pl.when(j == 0)
    def _():
        issue(0, 0)

        @pl.when(n_used > 1)
        def _():
            issue(1, 1)

    def compute(prefetch_ahead):
        slot = lax.rem(j, GATHER_SLOTS)

        def body(r, carry):
            row_copy(j, slot, r).wait()
            return carry
        lax.fori_loop(0, EB, body, 0, unroll=DMA_UNROLL)

        @pl.when(jnp.logical_or(j == 0, be_ref[j] != be_ref[jnp.maximum(j - 1, 0)]))
        def _():
            wg_bf[...] = wg_ref[0, 0].astype(BF16)
            wu_bf[...] = wu_ref[0, 0].astype(BF16)
            wd_bf[...] = wd_ref[0, 0].astype(BF16)

        xb = xbuf[slot].astype(BF16)
        if prefetch_ahead:
            ahead = lax.rem(j + 2, GATHER_SLOTS)
            for r in range(EB):
                row_copy(j + 2, ahead, r).start(priority=GATHER_DMA_PRIORITY)
        hg = jnp.dot(xb, wg_bf[...], preferred_element_type=F32)
        hu = jnp.dot(xb, wu_bf[...], preferred_element_type=F32)
        hm = (hg * jax.nn.sigmoid(hg) * hu).astype(BF16)
        o_ref[...] = jnp.dot(hm, wd_bf[...], preferred_element_type=F32)

    @pl.when(j + 2 < n_used)
    def _():
        compute(True)

    @pl.when(jnp.logical_and(j < n_used, j + 2 >= n_used))
    def _():
        compute(False)

    @pl.when(j >= n_used)
    def _():
        o_ref[...] = jnp.zeros_like(o_ref)


def _experts(hf2d, rows, block_e, n_used, n_blocks, layer, w_gate, w_up, w_down):
    N, D = hf2d.shape
    _, E, _, De = w_gate.shape
    EB = EXPERT_BLOCK
    grid_spec = pltpu.PrefetchScalarGridSpec(
        num_scalar_prefetch=3,
        grid=(n_blocks,),
        in_specs=[pl.BlockSpec(memory_space=pl.ANY),
                  pl.BlockSpec((1, 1, D, De), lambda j, be, nu, rw: (layer, be[j], 0, 0)),
                  pl.BlockSpec((1, 1, D, De), lambda j, be, nu, rw: (layer, be[j], 0, 0)),
                  pl.BlockSpec((1, 1, De, D), lambda j, be, nu, rw: (layer, be[j], 0, 0))],
        out_specs=pl.BlockSpec((EB, D), lambda j, be, nu, rw: (j, 0)),
        scratch_shapes=[pltpu.VMEM((GATHER_SLOTS, EB, D), F32), pltpu.VMEM((D, De), BF16), pltpu.VMEM((D, De), BF16),
                        pltpu.VMEM((De, D), BF16), pltpu.SemaphoreType.DMA((GATHER_SLOTS,))],
    )
    return pl.pallas_call(
        _expert_kernel,
        grid_spec=grid_spec,
        out_shape=jax.ShapeDtypeStruct((n_blocks * EB, D), F32),
        compiler_params=_cparams(("arbitrary",)),
        name="moe_experts",
    )(block_e, n_used, rows, hf2d, w_gate, w_up, w_down)


def _combine_kernel(dest_ref, yb_ref, x_ref, meta_ref, mod_ref, *rest, nj, project):
    if project:
        mod1_ref, g1_ref, w_ref, cos_ref, s1_ref, s2_ref, o_ref, p_ref, ybuf, sem = rest
    else:
        fn_ref, o_ref, ybuf, sem = rest
    b = pl.program_id(0)
    j = pl.program_id(1)
    step = b * nj + j
    n_steps = pl.num_programs(0) * nj
    n_tok = n_steps * TB

    def row_copy(st, slot, r, k):
        dst_row = dest_ref[k * n_tok + st * TB + r]
        return pltpu.make_async_copy(yb_ref.at[pl.ds(dst_row, 1), :], ybuf.at[slot, k, pl.ds(r, 1), :], sem.at[slot])

    def issue(st, slot):
        def body(r, carry):
            row_copy(st, slot, r, 0).start()
            row_copy(st, slot, r, 1).start()
            return carry
        lax.fori_loop(0, TB, body, 0, unroll=DMA_UNROLL)

    @pl.when(step == 0)
    def _():
        issue(0, 0)
        issue(1, 1)

    slot = lax.rem(step, GATHER_SLOTS)

    def body(prefetch_ahead):
        def wbody(r, carry):
            row_copy(step, slot, r, 0).wait()
            row_copy(step, slot, r, 1).wait()
            return carry
        lax.fori_loop(0, TB, wbody, 0, unroll=DMA_UNROLL)
        meta = meta_ref[0]
        f = meta[:, 4:5] * ybuf[slot, 0] + meta[:, 5:6] * ybuf[slot, 1]
        m = mod_ref[0, 0]
        xn = x_ref[0] + m[5:6] * f
        ahead = lax.rem(step + 2, GATHER_SLOTS)

        def issue_rows(i, n):
            if prefetch_ahead:
                for r in range(i * TB // n, (i + 1) * TB // n):
                    row_copy(step + 2, ahead, r, 0).start()
                    row_copy(step + 2, ahead, r, 1).start()

        if project:
            o_ref[0] = xn
            m1 = mod1_ref[0, 0]
            h = _modnorm(xn, g1_ref[...], m1[1:2], m1[0:1]).astype(BF16)
            _project_block(h, w_ref, p_ref, (cos_ref, s1_ref, s2_ref), before_piece=issue_rows)
        else:
            issue_rows(0, 1)
            ms = jnp.mean(xn * xn, axis=-1, keepdims=True)
            o_ref[0] = xn * lax.rsqrt(ms + NORM_EPS) * fn_ref[...]

    @pl.when(step + 2 < n_steps)
    def _():
        body(True)

    @pl.when(step + 2 >= n_steps)
    def _():
        body(False)


def _combine(dest, yb, X, meta, mod, all_lat, final_gain=None, project=None):
    B, S, D = X.shape
    nj = S // TB

    def mspec(lat_only):
        if lat_only:
            return pl.BlockSpec((1, 1, 6, D), lambda b, j, ds: (b, 1, 0, 0))
        return pl.BlockSpec((1, 1, 6, D), lambda b, j, ds: (b, jnp.minimum(j, 1), 0, 0))

    row_spec = pl.BlockSpec((1, TB, D), lambda b, j, ds: (b, j, 0))
    in_specs = [pl.BlockSpec(memory_space=pl.ANY), row_spec,
                pl.BlockSpec((1, TB, LANES), lambda b, j, ds: (b, j, 0)), mspec(all_lat)]
    args = [dest, yb, X, meta, mod]
    if project is None:
        in_specs.append(pl.BlockSpec((1, D), lambda b, j, ds: (0, 0)))
        args.append(final_gain.reshape(1, D))
        out_specs = row_spec
        out_shape = jax.ShapeDtypeStruct((B, S, D), F32)
    else:
        mod1, gain1, w_bf16, (cos, s1, s2) = project
        N = w_bf16.shape[1]
        in_specs += [mspec(False), pl.BlockSpec((1, D), lambda b, j, ds: (0, 0)),
                     pl.BlockSpec((D, N), lambda b, j, ds: (0, 0))]
        in_specs += [pl.BlockSpec((TB, LANES), lambda b, j, ds: (j, 0))] * 3
        args += [mod1, gain1.reshape(1, D), w_bf16, cos, s1, s2]
        out_specs = [row_spec, pl.BlockSpec((1, TB, N), lambda b, j, ds: (b, j, 0))]
        out_shape = [jax.ShapeDtypeStruct((B, S, D), F32), jax.ShapeDtypeStruct((B, S, N), BF16)]
    grid_spec = pltpu.PrefetchScalarGridSpec(
        num_scalar_prefetch=1,
        grid=(B, nj),
        in_specs=in_specs,
        out_specs=out_specs,
        scratch_shapes=[pltpu.VMEM((GATHER_SLOTS, 2, TB, D), F32), pltpu.SemaphoreType.DMA((GATHER_SLOTS,))],
    )
    return pl.pallas_call(
        functools.partial(_combine_kernel, nj=nj, project=project is not None),
        grid_spec=grid_spec,
        out_shape=out_shape,
        compiler_params=_cparams(("arbitrary", "arbitrary")),
        name="moe_combine_final" if project is None else "moe_combine_proj",
    )(*args)


def _moe(X, mod, gain, w_router, layer, w_gate, w_up, w_down, all_lat, final_gain=None, project=None):
    B, S, D = X.shape
    hf, meta, metat, counts = _router(X, mod, gain, w_router, all_lat)
    dest, rows, block_e, n_used, n_blocks = _routing_tables(metat, counts)
    yb = _experts(hf.reshape(B * S, D), rows, block_e, n_used, n_blocks, layer, w_gate, w_up, w_down)
    return _combine(dest, yb, X, meta, mod, all_lat, final_gain, project)


def _diff_attn_kernel(q_ref, k_ref, v_ref, dl_ref, sg_ref, o_ref, vx_ref, *, lam_init):
    @pl.when(pl.program_id(2) == 0)
    def _():
        vlane = lax.broadcasted_iota(jnp.int32, (vx_ref.shape[0], LANES), 1)
        vx_ref[:, 0:C_VDIM] = v_ref[0]
        vx_ref[:, C_VDIM:C_VDIM + LANES] = jnp.where(vlane == 0, 1.0, 0.0).astype(BF16)

    q = q_ref[0] * (C_HEAD ** -0.5)
    k = k_ref[0]
    lane = lax.broadcasted_iota(jnp.int32, q.shape, 1)
    zero = jnp.zeros_like(q)
    q1 = jnp.where(lane < C_HEAD, q, zero)
    q2 = jnp.where(lane >= C_HEAD, q, zero)
    dl = dl_ref[...]
    lam = (jnp.exp(jnp.sum(dl[0:1] * dl[1:2], axis=1, keepdims=True))
           - jnp.exp(jnp.sum(dl[2:3] * dl[3:4], axis=1, keepdims=True)) + lam_init)

    s = [lax.dot_general(qq, k, (((1,), (1,)), ((), ())), preferred_element_type=F32) for qq in (q1, q2)]
    e = [jnp.exp(si - jnp.max(si, axis=1, keepdims=True)).astype(BF16) for si in s]
    oz = [jnp.dot(ei, vx_ref[...], preferred_element_type=F32) for ei in e]
    on = [ozi[:, 0:C_VDIM] * (1.0 / ozi[:, C_VDIM:C_VDIM + 1]) for ozi in oz]
    o = on[0] - lam * on[1]
    ms = jnp.mean(o * o, axis=-1, keepdims=True)
    o_ref[0] = (o * lax.rsqrt(ms + DIFF_EPS) * sg_ref[...] * (1.0 - lam_init)).astype(o_ref.dtype)


def _diff_attn(P2, t_ctx, diff_lambda, diff_subln, lam_init):
    B, S, _ = P2.shape
    T = S - t_ctx
    qoff = t_ctx // ATT_TQ
    return pl.pallas_call(
        functools.partial(_diff_attn_kernel, lam_init=lam_init),
        grid=(B, C_HEADS, T // ATT_TQ),
        in_specs=[pl.BlockSpec((1, ATT_TQ, LANES), lambda b, h, q: (b, q + qoff, C_Q_OFF // LANES + h)),
                  pl.BlockSpec((1, S, LANES), lambda b, h, q: (b, 0, C_K_OFF // LANES + h)),
                  pl.BlockSpec((1, S, LANES), lambda b, h, q: (b, 0, C_V_OFF // LANES + h)),
                  pl.BlockSpec((4, C_HEAD), lambda b, h, q: (0, 0)),
                  pl.BlockSpec((1, C_VDIM), lambda b, h, q: (0, 0))],
        out_specs=pl.BlockSpec((1, ATT_TQ, LANES), lambda b, h, q: (b, q, h)),
        out_shape=jax.ShapeDtypeStruct((B, T, C_DIM), BF16),
        scratch_shapes=[pltpu.VMEM((S, C_VDIM + LANES), BF16)],
        compiler_params=_cparams(("arbitrary", "arbitrary", "arbitrary")),
        name="diff_attn",
    )(P2, P2, P2, diff_lambda, diff_subln.reshape(1, -1))


def _retention_kernel(q_ref, k_ref, v0_ref, v1_ref, o_ref, r_ref):
    d = pl.program_id(1)
    c = pl.program_id(2)

    @pl.when(c == 0)
    def _():
        r_ref[...] = jnp.zeros_like(r_ref)

    C = RET_CHUNK
    ti = lax.broadcasted_iota(jnp.int32, (C, C), 0)
    si = lax.broadcasted_iota(jnp.int32, (C, C), 1)
    rel = ((ti - si) * (1 - 2 * d)).astype(F32)
    col = lax.broadcasted_iota(jnp.int32, (C, 1), 0)
    pos = jnp.where(d == 0, col, C - 1 - col).astype(F32)
    chains = [(bi, h) for bi in range(RET_NB) for h in range(D_HEADS)]
    r0s = [r_ref[bi * D_HEADS + h] for bi, h in chains]
    decs = []
    for h in range(D_HEADS):
        lg_f = math.log(1.0 - 2.0 ** (-5.0 - h))
        lg_b = math.log(1.0 - 2.0 ** (-5.0 - (D_HEADS - 1 - h)))
        lg = jnp.where(d == 0, jnp.full((1, 1), lg_f, F32), jnp.full((1, 1), lg_b, F32))
        decs.append((jnp.where(rel >= 0, jnp.exp(lg * jnp.maximum(rel, 0.0)), 0.0),
                     jnp.exp(lg * (pos + 1.0)), jnp.exp(lg * (C - 1.0 - pos)), jnp.exp(lg * C)))
    outs, r_new = [], []
    for n, (bi, h) in enumerate(chains):
        inner_dec, q_dec, k_dec, c_dec = decs[h]
        q = q_ref[bi, :, h * D_KDIM:(h + 1) * D_KDIM]
        k = k_ref[bi, :, h * D_KDIM:(h + 1) * D_KDIM]
        vref = v0_ref if h < 2 else v1_ref
        v = vref[bi, :, (h % 2) * D_VDIM:(h % 2 + 1) * D_VDIM]
        inner = lax.dot_general(q, k, (((1,), (1,)), ((), ())), preferred_element_type=F32) * inner_dec
        outs.append(_bdot(inner, v) + _bdot(q, r0s[n]) * q_dec)
        r_new.append(r0s[n] * c_dec + _bdot(k.astype(F32) * k_dec, v, ((0,), (0,))))
    for n in range(len(chains)):
        r_ref[n] = r_new[n]
    for bi in range(RET_NB):
        o_ref[0, bi] = jnp.concatenate(outs[bi * D_HEADS:(bi + 1) * D_HEADS], axis=1)


def _retention(P2, t_ctx):
    B, S, _ = P2.shape
    C = RET_CHUNK
    nc = S // C
    nctx = t_ctx // C

    def cidx(d, c):
        back = jnp.where(c < nctx, nctx - 1 - c, nc - 1 + nctx - c)
        return jnp.where(d == 0, c, back)

    w = D_HEADS * D_KDIM
    return pl.pallas_call(
        _retention_kernel,
        grid=(B // RET_NB, 2, nc),
        in_specs=[pl.BlockSpec((RET_NB, C, w), lambda b, d, c: (b, cidx(d, c), D_Q_OFF // w)),
                  pl.BlockSpec((RET_NB, C, w), lambda b, d, c: (b, cidx(d, c), D_K_OFF // w)),
                  pl.BlockSpec((RET_NB, C, w), lambda b, d, c: (b, cidx(d, c), D_V_OFF // w)),
                  pl.BlockSpec((RET_NB, C, w), lambda b, d, c: (b, cidx(d, c), D_V_OFF // w + 1))],
        out_specs=pl.BlockSpec((1, RET_NB, C, D_DIM), lambda b, d, c: (d, b, cidx(d, c), 0)),
        out_shape=jax.ShapeDtypeStruct((2, B, S, D_DIM), F32),
        scratch_shapes=[pltpu.VMEM((RET_NB * D_HEADS, D_KDIM, D_VDIM), F32)],
        compiler_params=_cparams(("arbitrary", "arbitrary", "arbitrary")),
        name="retention",
    )(P2, P2, P2, P2)


def _odd_merge_kernel(c_ref, o2_ref, gate_ref, x_ref, mod_ref, rn_ref, wo_ref, o_ref):
    o = o2_ref[0, 0] + o2_ref[1, 0]
    parts = []
    for h in range(D_HEADS):
        oh = o[:, h * D_VDIM:(h + 1) * D_VDIM]
        ms = jnp.mean(oh * oh, axis=-1, keepdims=True)
        parts.append(oh * lax.rsqrt(ms + NORM_EPS))
    gt = gate_ref[0].astype(F32)
    dl = jnp.concatenate(parts, axis=1) * rn_ref[...] * (gt * jax.nn.sigmoid(gt))
    out = (jnp.dot(c_ref[0], wo_ref[0:C_DIM, :], preferred_element_type=F32)
           + _bdot(dl, wo_ref[C_DIM:C_DIM + D_DIM, :]))
    m = mod_ref[0, 0]
    o_ref[0] = x_ref[0] + m[2:3] * out


def _odd_merge(c_lat, o2, P2, X, mod, ret_norm, wo_bf16, t_ctx):
    B, S, D = X.shape
    T = S - t_ctx
    off = t_ctx // TB
    return pl.pallas_call(
        _odd_merge_kernel,
        grid=(B, T // TB),
        in_specs=[pl.BlockSpec((1, TB, C_DIM), lambda b, j: (b, j, 0)),
                  pl.BlockSpec((2, 1, TB, D_DIM), lambda b, j: (0, b, j + off, 0)),
                  pl.BlockSpec((1, TB, D_DIM), lambda b, j: (b, j + off, D_G_OFF // D_DIM)),
                  pl.BlockSpec((1, TB, D), lambda b, j: (b, j + off, 0)),
                  _mod_spec(D, all_lat=True),
                  pl.BlockSpec((1, D_DIM), lambda b, j: (0, 0)),
                  pl.BlockSpec((C_DIM + D_DIM, D), lambda b, j: (0, 0))],
        out_specs=pl.BlockSpec((1, TB, D), lambda b, j: (b, j, 0)),
        out_shape=jax.ShapeDtypeStruct((B, T, D), F32),
        compiler_params=_cparams(("arbitrary", "arbitrary")),
        name="odd_merge",
    )(c_lat, o2, P2, X, mod, ret_norm.reshape(1, -1), wo_bf16)


def _rope_tables(t_ctx, T):
    half = C_HEAD // 4
    inv = ROPE_BASE ** (-jnp.arange(half, dtype=F32) / half)
    t = jnp.arange(T, dtype=jnp.int32)
    ar = (t // GRID_W).astype(F32)[:, None] * inv[None, :]
    ac = (t % GRID_W).astype(F32)[:, None] * inv[None, :]
    z = jnp.zeros_like(ar)
    cos = jnp.concatenate([jnp.cos(ar), jnp.cos(ar), jnp.cos(ac), jnp.cos(ac)], axis=1)
    s1 = jnp.concatenate([-jnp.sin(ar), z, -jnp.sin(ac), z], axis=1)
    s2 = jnp.concatenate([z, jnp.sin(ar), z, jnp.sin(ac)], axis=1)

    def full(tab, ctx_val):
        tab = jnp.concatenate([jnp.full((t_ctx, C_HEAD), ctx_val, F32), tab], axis=0)
        return jnp.tile(tab, (1, LANES // C_HEAD))

    return full(cos, 1.0), full(s1, 0.0), full(s2, 0.0)


def _head_indicator(n, width):
    i = jnp.arange(n) // width
    return (i[:, None] == i[None, :]).astype(BF16)


def kernel(x, c, ctx, c_ctx, ada_w, ada_b, norm_mix, norm_ffn, ev_w_in, ev_w_out, rwkv_shift, rwkv_w0, rwkv_w_up, rwkv_a0, rwkv_a_up, rwkv_g_up, rwkv_k_k, rwkv_k_a, rwkv_r_k, rwkv_ln_g, rwkv_ln_b, pool_w, pool_scale, od_w_in, od_w_out, diff_lambda, diff_subln, ret_norm, moe_router_group, moe_router_expert, moe_w_gate, moe_w_up, moe_w_down, final_norm):
    B, T, D = x.shape
    t_ctx = ctx.shape[1]
    assert ada_w.shape[0] == 2 and t_ctx == TB and T % TB == 0 and T % GRID_W == 0 and B < 16
    X = jnp.concatenate([ctx, x], axis=1)

    cpad = jnp.zeros((16, D), F32).at[:B].set(c).at[B].set(c_ctx)
    mods = _ada(cpad, ada_w, ada_b)

    def layer_mod(i):
        lat = mods[i, :B]
        cm = jnp.broadcast_to(mods[i, B][None], lat.shape)
        return jnp.stack([cm, lat], axis=1).reshape(B, 2, 6, D)

    def router_w(i):
        w = jnp.concatenate([moe_router_group[i], moe_router_expert[i]], axis=1)
        return jnp.pad(w, ((0, 0), (0, LANES - w.shape[1])))

    ehead = _head_indicator(A_DIM, A_HEAD)

    mod0 = layer_mod(0)
    P = _proj(X, mod0, norm_mix[0], ev_w_in[0].astype(BF16), F32)
    zl = jnp.zeros((LORA_W, A_DIM), F32)
    lora_blocks = []
    for d in range(2):
        lora_blocks.append(jnp.concatenate([rwkv_w_up[0, d], zl], axis=0))
        lora_blocks.append(jnp.concatenate([zl, rwkv_a_up[0, d]], axis=0))
    lora_w = jnp.concatenate(lora_blocks, axis=1).astype(BF16)
    wa0 = jnp.concatenate([rwkv_w0[0, 0], rwkv_a0[0, 0], rwkv_w0[0, 1], rwkv_a0[0, 1]]).reshape(1, -1)
    lw, kt, bb, v, kk, r, bonus, g, yb = _even_prep(
        P, t_ctx, rwkv_shift[0], wa0, lora_w, rwkv_g_up[0].astype(BF16), rwkv_k_k[0], rwkv_k_a[0], rwkv_r_k[0],
        pool_w[0].astype(BF16), pool_scale[0], ehead)
    y2 = _rwkv_scan(lw, kt, bb, v, kk, r, t_ctx)
    X = _even_merge(y2, bonus, g, yb, X, mod0, rwkv_ln_g[0], rwkv_ln_b[0], ev_w_out[0].astype(BF16), ehead)
    mod1 = layer_mod(1)
    X, P2 = _moe(X, mod0, norm_ffn[0], router_w(0), 0, moe_w_gate, moe_w_up, moe_w_down, all_lat=False,
                 project=(mod1, norm_mix[1], od_w_in[0].astype(BF16), _rope_tables(t_ctx, T)))

    lam_init = 0.8 - 0.6 * math.exp(-0.3 * 1)
    c_lat = _diff_attn(P2, t_ctx, diff_lambda[0], diff_subln[0], lam_init)
    o2 = _retention(P2, t_ctx)
    XL = _odd_merge(c_lat, o2, P2, X, mod1, ret_norm[0], od_w_out[0].astype(BF16), t_ctx)
    return _moe(XL, mod1, norm_ffn[1], router_w(1), 1, moe_w_gate, moe_w_up, moe_w_down, all_lat=True,
                final_gain=final_norm)
```

```python
import functools
import math

import jax
import jax.numpy as jnp
from jax import lax
from jax.experimental import pallas as pl
from jax.experimental.pallas import tpu as pltpu

F32 = jnp.float32
BF16 = jnp.bfloat16
HI = lax.Precision.HIGHEST

GRID_W = 64
NORM_EPS = 1e-6
ROPE_BASE = 10000.0
A_HEADS, A_HEAD = 8, 64
A_DIM = A_HEADS * A_HEAD
LORA_W, LORA_A, LORA_G = 64, 64, 128
W_DECAY_SCALE = 0.606531
RWKV_GN_EPS = 64e-5
POOL_WINDOWS = (2, 4, 8, 16)
POOL_GROUP = 128
B_DIM = POOL_GROUP * 4
C_HEADS, C_HEAD = 4, 64
C_VDIM = 2 * C_HEAD
C_DIM = C_HEADS * C_VDIM
DIFF_EPS = 1e-5
D_HEADS, D_KDIM, D_VDIM = 4, 64, 128
D_DIM = D_HEADS * D_VDIM
A_K_OFF, A_V_OFF = 0, A_DIM
W_LORA_OFF = 2 * A_DIM
A_LORA_OFF = W_LORA_OFF + LORA_W
A_R_OFF = A_LORA_OFF + LORA_A
G_LORA_OFF = A_R_OFF + A_DIM
POOL_OFF = G_LORA_OFF + LORA_G
EVEN_IN_COLS = POOL_OFF + B_DIM
C_K_OFF = 0
C_V_OFF = C_HEADS * 2 * C_HEAD
D_K_OFF = C_V_OFF + C_DIM
D_V_OFF = D_K_OFF + D_HEADS * D_KDIM
C_Q_OFF = D_V_OFF + D_DIM
D_Q_OFF = C_Q_OFF + C_HEADS * 2 * C_HEAD
D_G_OFF = D_Q_OFF + D_HEADS * D_KDIM
ODD_IN_COLS = D_G_OFF + D_DIM
MOE_GROUPS, MOE_PER_GROUP = 4, 8
MOE_EXPERTS = MOE_GROUPS * MOE_PER_GROUP
MOE_TOP_K = 2
EXPERT_BLOCK = 256

LANES = 128
SUBLANES = 8
HALO = 16
TB = 256
RWKV_CHUNK = 64
RET_CHUNK = 256
RET_NB = 2
SCAN_NB = 4
ATT_TQ = 256
VMEM_LIMIT = 56 * 1024 * 1024
DMA_UNROLL = 8
GATHER_SLOTS = 3
TABLE_UNROLL = 32
GATHER_DMA_PRIORITY = 1


def _cparams(sem):
    return pltpu.CompilerParams(dimension_semantics=sem, vmem_limit_bytes=VMEM_LIMIT)


def _modnorm(x, gain, scale, shift, eps=NORM_EPS):
    ms = jnp.mean(x * x, axis=-1, keepdims=True)
    return x * lax.rsqrt(ms + eps) * gain * (1.0 + scale) + shift


def _bdot(a, b, dims=((1,), (0,))):
    return lax.dot_general(a.astype(BF16), b.astype(BF16), (dims, ((), ())), preferred_element_type=F32)


def _bf16_terms(x, terms):
    parts = []
    for _ in range(terms):
        p = x.astype(BF16)
        parts.append(p)
        x = x - p.astype(F32)
    return parts


def _dot_split_lhs(x, w_bf16, terms=2):
    return sum(jnp.dot(p, w_bf16, preferred_element_type=F32) for p in _bf16_terms(x, terms))


def _ada_kernel(c_ref, w_ref, b_ref, o_ref):
    c = c_ref[...]
    s = c * jax.nn.sigmoid(c)
    o_ref[0] = jnp.dot(s, w_ref[0], precision=HI, preferred_element_type=F32) + b_ref[0]


def _ada(cpad, ada_w, ada_b):
    L, D, N = ada_w.shape
    tn = 1536
    return pl.pallas_call(
        _ada_kernel,
        grid=(L, N // tn),
        in_specs=[pl.BlockSpec((16, D), lambda l, n: (0, 0)),
                  pl.BlockSpec((1, D, tn), lambda l, n: (l, 0, n)),
                  pl.BlockSpec((1, 1, tn), lambda l, n: (l, 0, n))],
        out_specs=pl.BlockSpec((1, 16, tn), lambda l, n: (l, 0, n)),
        out_shape=jax.ShapeDtypeStruct((L, 16, N), F32),
        compiler_params=_cparams(("arbitrary", "arbitrary")),
        name="ada_mod",
    )(cpad, ada_w, ada_b.reshape(L, 1, N))


def _mod_spec(D, all_lat=False):
    if all_lat:
        return pl.BlockSpec((1, 1, 6, D), lambda b, j: (b, 1, 0, 0))
    return pl.BlockSpec((1, 1, 6, D), lambda b, j: (b, jnp.minimum(j, 1), 0, 0))


def _rope_plan():
    ranges = [(C_K_OFF, C_V_OFF), (D_K_OFF, D_V_OFF), (C_Q_OFF, D_G_OFF)]
    rope_chunks = tuple(g for lo, hi in ranges for g in range(lo // LANES, hi // LANES))
    kscale_chunks = tuple(range(D_K_OFF // LANES, D_V_OFF // LANES))
    return rope_chunks, kscale_chunks


def _project_block(h, w_ref, o_ref, rope_refs=None, before_piece=None):
    n_out = w_ref.shape[1]
    rope_chunks, kscale_chunks = _rope_plan() if rope_refs is not None else ((), ())
    cw = 512
    for n0 in range(0, n_out, cw):
        if before_piece is not None:
            before_piece(n0 // cw, n_out // cw)
        p = jnp.dot(h, w_ref[:, n0:n0 + cw], preferred_element_type=F32)
        if rope_refs is None:
            o_ref[0, :, n0:n0 + cw] = p.astype(o_ref.dtype)
            continue
        cos_ref, s1_ref, s2_ref = rope_refs
        for q in range(cw // LANES):
            gi = n0 // LANES + q
            sub = p[:, q * LANES:(q + 1) * LANES]
            if gi in rope_chunks:
                sub = (sub * cos_ref[...] + pltpu.roll(sub, LANES - 16, 1) * s1_ref[...]
                       + pltpu.roll(sub, 16, 1) * s2_ref[...])
            if gi in kscale_chunks:
                sub = sub * (D_KDIM ** -0.5)
            o_ref[0, :, n0 + q * LANES:n0 + (q + 1) * LANES] = sub.astype(o_ref.dtype)


def _stream_specs(D):
    return [pl.BlockSpec((1, TB, D), lambda b, j: (b, 0, 0)),
            pl.BlockSpec((1, TB, D), lambda b, j: (b, jnp.maximum(j - 1, 0), 0))]


def _stream_block(ctx_ref, x_ref):
    return jnp.where(pl.program_id(1) == 0, ctx_ref[0], x_ref[0])


def _proj_kernel(ctx_ref, x_ref, mod_ref, g_ref, w_ref, o_ref):
    m = mod_ref[0, 0]
    h = _modnorm(_stream_block(ctx_ref, x_ref), g_ref[...], m[1:2], m[0:1]).astype(BF16)
    _project_block(h, w_ref, o_ref)


def _proj(ctx, x, mod, gain, w_bf16, out_dtype):
    B, T, D = x.shape
    S = ctx.shape[1] + T
    N = w_bf16.shape[1]
    return pl.pallas_call(
        _proj_kernel,
        grid=(B, S // TB),
        in_specs=_stream_specs(D) + [
                  _mod_spec(D),
                  pl.BlockSpec((1, D), lambda b, j: (0, 0)),
                  pl.BlockSpec((D, N), lambda b, j: (0, 0))],
        out_specs=pl.BlockSpec((1, TB, N), lambda b, j: (b, j, 0)),
        out_shape=jax.ShapeDtypeStruct((B, S, N), out_dtype),
        compiler_params=_cparams(("arbitrary", "arbitrary")),
        name="proj",
    )(ctx, x, mod, gain.reshape(1, D), w_bf16)


def _even_prep_kernel(p_ref, pp_ref, pn_ref, sh_ref, wa0_ref, lora_ref, gup_ref, kkg_ref, ka_ref, rk_ref,
                      pw_ref, ps_ref, eh_ref,
                      lw_ref, kt_ref, bb_ref, v_ref, kkn_ref, r_ref, bonus_ref, g_ref, yb_ref,
                      *, nblk, t_ctx, t_lat):
    j = pl.program_id(1)
    prev_ok = j >= 2
    next_ok = jnp.logical_and(j >= 1, j < nblk - 1)
    n_ext = TB + 2 * HALO

    def ext(c0, c1):
        pv = jnp.where(prev_ok, pp_ref[0, :, c0:c1].astype(F32), 0.0)
        nx = jnp.where(next_ok, pn_ref[0, :, c0:c1].astype(F32), 0.0)
        return jnp.concatenate([pv, p_ref[0, :, c0:c1].astype(F32), nx], axis=0)

    def shift3(c0, c1, k0):
        e = ext(c0, c1)
        kern = sh_ref[:, k0:k0 + (c1 - c0)]
        y = kern[0:1] * pltpu.roll(e, 1, 0) + kern[1:2] * e + kern[2:3] * pltpu.roll(e, n_ext - 1, 0)
        return y[HALO:HALO + TB]

    k = shift3(A_K_OFF, A_K_OFF + A_DIM, 0)
    v = shift3(A_V_OFF, A_V_OFF + A_DIM, A_DIM)
    r = shift3(A_R_OFF, A_R_OFF + A_DIM, 2 * A_DIM)

    c = p_ref[0, :, W_LORA_OFF:W_LORA_OFF + LANES].astype(F32)
    lane = lax.broadcasted_iota(jnp.int32, c.shape, 1)
    lin = jnp.where(lane < LORA_W, jnp.tanh(c), c)
    z = _bdot(lin, lora_ref[...]) + wa0_ref[...]

    eh = eh_ref[...]
    kk0 = k * kkg_ref[...]
    ss = _dot_split_lhs(kk0 * kk0, eh)
    kk = kk0 * lax.rsqrt(jnp.maximum(ss, 1e-12))
    kt_sum = None
    for d in range(2):
        zw = z[:, (2 * d) * A_DIM:(2 * d + 1) * A_DIM]
        za = z[:, (2 * d + 1) * A_DIM:(2 * d + 2) * A_DIM]
        a = jax.nn.sigmoid(za)
        kt = k * (1.0 + (a - 1.0) * ka_ref[...])
        lw_ref[d, 0] = -W_DECAY_SCALE * jax.nn.sigmoid(zw)
        kt_ref[d, 0] = kt.astype(kt_ref.dtype)
        bb_ref[d, 0] = (a * kk).astype(bb_ref.dtype)
        kt_sum = kt if kt_sum is None else kt_sum + kt
    v_ref[0] = v.astype(v_ref.dtype)
    kkn_ref[0] = kk.astype(kkn_ref.dtype)
    r_ref[0] = r.astype(r_ref.dtype)
    bonus_ref[0] = (_dot_split_lhs(r * kt_sum * rk_ref[...], eh) * v).astype(bonus_ref.dtype)
    gl = p_ref[0, :, G_LORA_OFF:G_LORA_OFF + LORA_G].astype(F32)
    g_ref[0] = _bdot(jax.nn.sigmoid(gl), gup_ref[...]).astype(g_ref.dtype)

    row = lax.broadcasted_iota(jnp.int32, (TB, POOL_GROUP), 0)
    tpos = row + jnp.maximum(j - 1, 0) * TB
    tseg = jnp.where(j == 0, t_ctx, t_lat)
    for gi, win in enumerate(POOL_WINDOWS):
        c0 = POOL_OFF + gi * POOL_GROUP
        e = ext(c0, c0 + POOL_GROUP)
        s = e + pltpu.roll(e, 1, 0)
        sh = 1
        while 2 * sh < win:
            s = pltpu.roll(s, sh, 0) + pltpu.roll(s, n_ext - sh, 0)
            sh *= 2
        s = s[HALO:HALO + TB]
        lo = jnp.clip(tpos - win // 2, 0, tseg)
        hi = jnp.clip(tpos - win // 2 + win, 0, tseg)
        mean = s / (hi - lo).astype(F32)
        dd = mean - p_ref[0, :, c0:c0 + POOL_GROUP].astype(F32)
        y = _bdot(dd, pw_ref[gi])
        yb_ref[0, :, gi * POOL_GROUP:(gi + 1) * POOL_GROUP] = (
            y * ps_ref[:, gi * POOL_GROUP:(gi + 1) * POOL_GROUP]).astype(yb_ref.dtype)


def _even_prep(P, t_ctx, shift_k, wa0, lora_w, g_up, k_k, k_a, r_k, pool_w, pool_scale, ehead):
    B, S, N = P.shape
    nblk = S // TB
    hb = TB // HALO
    const2 = lambda b, j: (0, 0)
    in_specs = [
        pl.BlockSpec((1, TB, N), lambda b, j: (b, j, 0)),
        pl.BlockSpec((1, HALO, N), lambda b, j: (b, jnp.maximum(j * hb - 1, 0), 0)),
        pl.BlockSpec((1, HALO, N), lambda b, j: (b, jnp.minimum((j + 1) * hb, S // HALO - 1), 0)),
        pl.BlockSpec((3, 3 * A_DIM), const2),
        pl.BlockSpec((1, 4 * A_DIM), const2),
        pl.BlockSpec((LANES, 4 * A_DIM), const2),
        pl.BlockSpec((LORA_G, A_DIM), const2),
        pl.BlockSpec((1, A_DIM), const2),
        pl.BlockSpec((1, A_DIM), const2),
        pl.BlockSpec((1, A_DIM), const2),
        pl.BlockSpec((4, POOL_GROUP, POOL_GROUP), lambda b, j: (0, 0, 0)),
        pl.BlockSpec((1, B_DIM), const2),
        pl.BlockSpec((A_DIM, A_DIM), const2),
    ]
    two_f32 = jax.ShapeDtypeStruct((2, B, S, A_DIM), F32)
    two = jax.ShapeDtypeStruct((2, B, S, A_DIM), BF16)
    one = jax.ShapeDtypeStruct((B, S, A_DIM), BF16)
    two_spec = pl.BlockSpec((2, 1, TB, A_DIM), lambda b, j: (0, b, j, 0))
    one_spec = pl.BlockSpec((1, TB, A_DIM), lambda b, j: (b, j, 0))
    return pl.pallas_call(
        functools.partial(_even_prep_kernel, nblk=nblk, t_ctx=t_ctx, t_lat=S - t_ctx),
        grid=(B, nblk),
        in_specs=in_specs,
        out_specs=[two_spec] * 3 + [one_spec] * 6,
        out_shape=[two_f32, two, two] + [one] * 6,
        compiler_params=_cparams(("arbitrary", "arbitrary")),
        name="even_prep",
    )(P, P, P, shift_k, wa0, lora_w, g_up, k_k.reshape(1, -1), k_a.reshape(1, -1), r_k.reshape(1, -1),
      pool_w, pool_scale.reshape(1, -1), ehead)


def _rwkv_scan_kernel(lw_ref, kt_ref, bb_ref, v_ref, kk_ref, r_ref, y_ref, s_ref):
    d = pl.program_id(1)
    c = pl.program_id(2)

    @pl.when(c == 0)
    def _():
        s_ref[...] = jnp.zeros_like(s_ref)

    C = RWKV_CHUNK
    tt = lax.broadcasted_iota(jnp.int32, (C, C), 0)
    ii = lax.broadcasted_iota(jnp.int32, (C, C), 1)
    rel = (tt - ii) * (1 - 2 * d)
    strict = rel > 0
    incl = rel >= 0
    t2 = lax.broadcasted_iota(jnp.int32, (C, 2 * C), 0)
    i2 = lax.broadcasted_iota(jnp.int32, (C, 2 * C), 1) & (C - 1)
    incl2 = (t2 - i2) * (1 - 2 * d) >= 0
    tri = jnp.where(incl, 1.0, 0.0).astype(BF16)
    n_dbl = int(math.log2(C))
    hsl = [slice(h * A_HEAD, (h + 1) * A_HEAD) for h in range(A_HEADS)]
    ar, bk, vh, bkp, ptot = [], [], [], [], []
    for bi in range(SCAN_NB):
        lw = lw_ref[0, bi]
        cum = sum(jnp.dot(tri, p, preferred_element_type=F32) for p in _bf16_terms(lw, 3))
        tot = jnp.sum(lw, axis=0, keepdims=True)
        p_inv = jnp.exp(-cum)
        p_rest = jnp.exp(tot - cum)
        p_tot = jnp.exp(tot)
        bb = bb_ref[0, bi]
        kt = kt_ref[0, bi]
        at_all = (-kk_ref[bi] * jnp.exp(cum - lw)).astype(BF16)
        rt_all = (r_ref[bi] * jnp.exp(cum)).astype(BF16)
        bt_all = (bb * p_inv).astype(BF16)
        kti_all = (kt * p_inv).astype(BF16)
        bp_all = (bb * p_rest).astype(BF16)
        kp_all = (kt * p_rest).astype(BF16)
        v_all = v_ref[bi].astype(BF16)
        for s in hsl:
            ar.append(jnp.concatenate([at_all[:, s], rt_all[:, s]], axis=0))
            bk.append(jnp.concatenate([bt_all[:, s], kti_all[:, s]], axis=0))
            bkp.append(jnp.concatenate([bp_all[:, s], kp_all[:, s]], axis=0))
            vh.append(v_all[:, s])
            ptot.append(p_tot[:, s])
    chains = range(SCAN_NB * A_HEADS)
    s0 = [s_ref[n] for n in chains]
    m = [_bdot(ar[n], bk[n], ((1,), (1,))) for n in chains]
    arh = [_bdot(ar[n], s0[n], ((1,), (1,))) for n in chains]
    lp = [jnp.where(strict, m[n][:C, :C], 0.0) for n in chains]
    a_ak = [jnp.where(strict, m[n][:C, C:], 0.0) for n in chains]
    a_r = [jnp.where(incl2, m[n][C:, :], 0.0) for n in chains]
    u = [arh[n][:C] + _bdot(a_ak[n], vh[n]) for n in chains]
    for q in range(n_dbl):
        u = [u[n] + _bdot(lp[n], u[n]) for n in chains]
        if q < n_dbl - 1:
            lp = [_bdot(lp[n], lp[n]) for n in chains]
    uv = [jnp.concatenate([u[n].astype(BF16), vh[n]], axis=0) for n in chains]
    outs = [arh[n][C:] + _bdot(a_r[n], uv[n]) for n in chains]
    s_new = [s0[n] * ptot[n] + _bdot(uv[n], bkp[n], ((0,), (0,))) for n in chains]
    for n in chains:
        s_ref[n] = s_new[n]
    for bi in range(SCAN_NB):
        y_ref[0, bi] = jnp.concatenate(outs[bi * A_HEADS:(bi + 1) * A_HEADS], axis=1).astype(y_ref.dtype)


def _rwkv_scan(lw, kt, bb, v, kk, r, t_ctx):
    _, B, S, _ = lw.shape
    C = RWKV_CHUNK
    nc = S // C
    nctx = t_ctx // C

    def cidx(d, c):
        back = jnp.where(c < nctx, nctx - 1 - c, nc - 1 + nctx - c)
        return jnp.where(d == 0, c, back)

    two_spec = pl.BlockSpec((1, SCAN_NB, C, A_DIM), lambda b, d, c: (d, b, cidx(d, c), 0))
    one_spec = pl.BlockSpec((SCAN_NB, C, A_DIM), lambda b, d, c: (b, cidx(d, c), 0))
    return pl.pallas_call(
        _rwkv_scan_kernel,
        grid=(B // SCAN_NB, 2, nc),
        in_specs=[two_spec] * 3 + [one_spec] * 3,
        out_specs=two_spec,
        out_shape=jax.ShapeDtypeStruct((2, B, S, A_DIM), BF16),
        scratch_shapes=[pltpu.VMEM((SCAN_NB * A_HEADS, A_HEAD, A_HEAD), F32)],
        compiler_params=_cparams(("arbitrary", "arbitrary", "arbitrary")),
        name="rwkv_scan",
    )(lw, kt, bb, v, kk, r)


def _even_merge_kernel(y2_ref, bonus_ref, g_ref, yb_ref, ctx_ref, x_ref, mod_ref, lng_ref, lnb_ref, wo_ref, eh_ref,
                       o_ref):
    y = y2_ref[0, 0].astype(F32) + y2_ref[1, 0].astype(F32)
    eh = eh_ref[...]
    mu = _dot_split_lhs(y, eh) * (1.0 / A_HEAD)
    yc = y - mu
    var = _dot_split_lhs(yc * yc, eh) * (1.0 / A_HEAD)
    yn = yc * lax.rsqrt(var + RWKV_GN_EPS) * lng_ref[...] + lnb_ref[...]
    ya = (yn + bonus_ref[0]) * g_ref[0]
    out = _bdot(ya, wo_ref[0:A_DIM, :]) + _bdot(yb_ref[0], wo_ref[A_DIM:A_DIM + B_DIM, :])
    m = mod_ref[0, 0]
    o_ref[0] = _stream_block(ctx_ref, x_ref) + m[2:3] * out


def _even_merge(y2, bonus, g, yb, ctx, x, mod, ln_g, ln_b, wo_bf16, ehead):
    B, T, D = x.shape
    S = ctx.shape[1] + T
    one_spec = pl.BlockSpec((1, TB, A_DIM), lambda b, j: (b, j, 0))
    const2 = lambda b, j: (0, 0)
    return pl.pallas_call(
        _even_merge_kernel,
        grid=(B, S // TB),
        in_specs=[pl.BlockSpec((2, 1, TB, A_DIM), lambda b, j: (0, b, j, 0)), one_spec, one_spec, one_spec]
        + _stream_specs(D) + [_mod_spec(D),
                  pl.BlockSpec((1, A_DIM), const2), pl.BlockSpec((1, A_DIM), const2),
                  pl.BlockSpec((A_DIM + B_DIM, D), const2), pl.BlockSpec((A_DIM, A_DIM), const2)],
        out_specs=pl.BlockSpec((1, TB, D), lambda b, j: (b, j, 0)),
        out_shape=jax.ShapeDtypeStruct((B, S, D), F32),
        compiler_params=_cparams(("arbitrary", "arbitrary")),
        name="even_merge",
    )(y2, bonus, g, yb, ctx, x, mod, ln_g.reshape(1, -1), ln_b.reshape(1, -1), wo_bf16, ehead)


def _router_kernel(x_ref, mod_ref, g_ref, wr_ref, hf_ref, meta_ref, metat_ref, cnt_ref, carry_ref):
    first = jnp.logical_and(pl.program_id(0) == 0, pl.program_id(1) == 0)

    @pl.when(first)
    def _():
        carry_ref[...] = jnp.zeros_like(carry_ref)

    m = mod_ref[0, 0]
    hf = _modnorm(x_ref[0], g_ref[...], m[4:5], m[3:4])
    hf_ref[0] = hf
    h0, h1 = _bf16_terms(hf, 2)
    w0, w1 = _bf16_terms(wr_ref[...], 2)
    logits = (jnp.dot(h0, w0, preferred_element_type=F32) + jnp.dot(h0, w1, preferred_element_type=F32)
              + jnp.dot(h1, w0, preferred_element_type=F32))
    lane = lax.broadcasted_iota(jnp.int32, logits.shape, 1)
    neg = -jnp.inf
    lg = jnp.where(lane < MOE_GROUPS, logits, neg)
    mg = jnp.max(lg, axis=1, keepdims=True)
    grp = jnp.min(jnp.where(lg == mg, lane, LANES), axis=1, keepdims=True)
    gw = 1.0 / jnp.sum(jnp.exp(lg - mg), axis=1, keepdims=True)
    el = lane - MOE_GROUPS
    in_grp = jnp.logical_and(el >= grp * MOE_PER_GROUP, el < (grp + 1) * MOE_PER_GROUP)
    le = jnp.where(in_grp, logits, neg)
    m0 = jnp.max(le, axis=1, keepdims=True)
    i0 = jnp.min(jnp.where(le == m0, lane, LANES), axis=1, keepdims=True)
    le1 = jnp.where(lane == i0, neg, le)
    m1 = jnp.max(le1, axis=1, keepdims=True)
    i1 = jnp.min(jnp.where(le1 == m1, lane, LANES), axis=1, keepdims=True)
    p1 = jnp.exp(m1 - m0)
    gate0 = gw / (1.0 + p1)
    gate1 = gw * p1 / (1.0 + p1)
    e0 = i0 - MOE_GROUPS
    e1 = i1 - MOE_GROUPS
    oh0 = (lane == e0).astype(F32)
    oh1 = (lane == e1).astype(F32)
    cnt = oh0 + oh1
    rr = lax.broadcasted_iota(jnp.int32, (TB, TB), 0)
    cc = lax.broadcasted_iota(jnp.int32, (TB, TB), 1)
    before = _bdot((cc < rr).astype(F32), cnt) + carry_ref[...]
    rank0 = jnp.sum(oh0 * before, axis=1, keepdims=True)
    rank1 = jnp.sum(oh1 * before, axis=1, keepdims=True)
    carry_ref[...] = carry_ref[...] + jnp.sum(cnt, axis=0, keepdims=True)
    cnt_ref[...] = carry_ref[...]
    meta = jnp.where(lane == 0, e0.astype(F32), 0.0)
    meta = jnp.where(lane == 1, e1.astype(F32), meta)
    meta = jnp.where(lane == 2, rank0, meta)
    meta = jnp.where(lane == 3, rank1, meta)
    meta = jnp.where(lane == 4, gate0, meta)
    meta = jnp.where(lane == 5, gate1, meta)
    meta_ref[0] = meta
    metat_ref[0, 0] = meta.T[0:SUBLANES]


def _router(X, mod, gain, w_router, all_lat):
    B, S, D = X.shape
    return pl.pallas_call(
        _router_kernel,
        grid=(B, S // TB),
        in_specs=[pl.BlockSpec((1, TB, D), lambda b, j: (b, j, 0)), _mod_spec(D, all_lat),
                  pl.BlockSpec((1, D), lambda b, j: (0, 0)), pl.BlockSpec((D, LANES), lambda b, j: (0, 0))],
        out_specs=[pl.BlockSpec((1, TB, D), lambda b, j: (b, j, 0)),
                   pl.BlockSpec((1, TB, LANES), lambda b, j: (b, j, 0)),
                   pl.BlockSpec((1, 1, SUBLANES, TB), lambda b, j: (b, j, 0, 0)),
                   pl.BlockSpec((1, LANES), lambda b, j: (0, 0))],
        out_shape=[jax.ShapeDtypeStruct((B, S, D), F32), jax.ShapeDtypeStruct((B, S, LANES), F32),
                   jax.ShapeDtypeStruct((B, S // TB, SUBLANES, TB), F32), jax.ShapeDtypeStruct((1, LANES), F32)],
        scratch_shapes=[pltpu.VMEM((1, LANES), F32)],
        compiler_params=_cparams(("arbitrary", "arbitrary")),
        name="moe_router",
    )(X, mod, gain.reshape(1, D), w_router)


def _routing_tables(metat, counts):
    B, nj = metat.shape[:2]
    n_tok = B * nj * TB
    mi = jnp.transpose(metat[:, :, 0:4, :], (2, 0, 1, 3)).reshape(4, n_tok).astype(jnp.int32)
    cnt = counts[0, :MOE_EXPERTS].astype(jnp.int32)
    padded = (cnt + EXPERT_BLOCK - 1) // EXPERT_BLOCK * EXPERT_BLOCK
    pad_end = jnp.cumsum(padded)
    pad_start = pad_end - padded
    dest = jnp.concatenate([pad_start[mi[0]] + mi[2], pad_start[mi[1]] + mi[3]])
    n_blocks = -(-(n_tok * MOE_TOP_K) // EXPERT_BLOCK) + MOE_EXPERTS
    starts = jnp.arange(n_blocks, dtype=jnp.int32) * EXPERT_BLOCK
    block_e = jnp.minimum(jnp.sum((pad_end[None, :] <= starts[:, None]).astype(jnp.int32), axis=1), MOE_EXPERTS - 1)
    n_used = (pad_end[-1] // EXPERT_BLOCK).astype(jnp.int32).reshape(1)
    return dest, block_e, n_used, n_blocks


def _expert_kernel(be_ref, nu_ref, dest_ref, hf_ref, wg_ref, wu_ref, wd_ref, o_ref, xbuf, wg_bf, wu_bf, wd_bf, rows_ref,
                   sem, *, n_tok):
    j = pl.program_id(0)
    n_used = nu_ref[0]
    EB = EXPERT_BLOCK
    n_slots = rows_ref.shape[0]

    def row_copy(blk, slot, r):
        row = rows_ref[blk * EB + r]
        return pltpu.make_async_copy(hf_ref.at[pl.ds(row, 1), :], xbuf.at[slot, pl.ds(r, 1), :], sem.at[slot])

    def issue(blk, slot):
        def body(r, carry):
            row_copy(blk, slot, r).start()
            return carry
        lax.fori_loop(0, EB, body, 0, unroll=DMA_UNROLL)

    @pl.when(j == 0)
    def _():
        fill_mask = (1 << (n_tok.bit_length() - 1)) - 1

        def fill(i, carry):
            rows_ref[i] = jnp.bitwise_and(i, fill_mask)
            return carry
        lax.fori_loop(0, n_slots, fill, 0, unroll=TABLE_UNROLL)

        def invert(t, carry):
            rows_ref[dest_ref[t]] = t
            rows_ref[dest_ref[n_tok + t]] = t
            return carry
        lax.fori_loop(0, n_tok, invert, 0, unroll=TABLE_UNROLL)
        issue(0, 0)

        @pl.when(n_used > 1)
        def _():
            issue(1, 1)

    def compute(prefetch_ahead):
        slot = lax.rem(j, GATHER_SLOTS)

        def body(r, carry):
            row_copy(j, slot, r).wait()
            return carry
        lax.fori_loop(0, EB, body, 0, unroll=DMA_UNROLL)

        @pl.when(jnp.logical_or(j == 0, be_ref[j] != be_ref[jnp.maximum(j - 1, 0)]))
        def _():
            wg_bf[...] = wg_ref[0, 0].astype(BF16)
            wu_bf[...] = wu_ref[0, 0].astype(BF16)
            wd_bf[...] = wd_ref[0, 0].astype(BF16)

        xb = xbuf[slot].astype(BF16)
        if prefetch_ahead:
            ahead = lax.rem(j + 2, GATHER_SLOTS)
            for r in range(EB):
                row_copy(j + 2, ahead, r).start(priority=GATHER_DMA_PRIORITY)
        hg = jnp.dot(xb, wg_bf[...], preferred_element_type=F32)
        hu = jnp.dot(xb, wu_bf[...], preferred_element_type=F32)
        hm = (hg * jax.nn.sigmoid(hg) * hu).astype(BF16)
        o_ref[...] = jnp.dot(hm, wd_bf[...], preferred_element_type=F32)

    @pl.when(j + 2 < n_used)
    def _():
        compute(True)

    @pl.when(jnp.logical_and(j < n_used, j + 2 >= n_used))
    def _():
        compute(False)

    @pl.when(j >= n_used)
    def _():
        o_ref[...] = jnp.zeros_like(o_ref)


def _experts(hf2d, dest, block_e, n_used, n_blocks, layer, w_gate, w_up, w_down):
    N, D = hf2d.shape
    _, E, _, De = w_gate.shape
    EB = EXPERT_BLOCK
    grid_spec = pltpu.PrefetchScalarGridSpec(
        num_scalar_prefetch=3,
        grid=(n_blocks,),
        in_specs=[pl.BlockSpec(memory_space=pl.ANY),
                  pl.BlockSpec((1, 1, D, De), lambda j, be, nu, rw: (layer, be[j], 0, 0)),
                  pl.BlockSpec((1, 1, D, De), lambda j, be, nu, rw: (layer, be[j], 0, 0)),
                  pl.BlockSpec((1, 1, De, D), lambda j, be, nu, rw: (layer, be[j], 0, 0))],
        out_specs=pl.BlockSpec((EB, D), lambda j, be, nu, rw: (j, 0)),
        scratch_shapes=[pltpu.VMEM((GATHER_SLOTS, EB, D), F32), pltpu.VMEM((D, De), BF16), pltpu.VMEM((D, De), BF16),
                        pltpu.VMEM((De, D), BF16), pltpu.SMEM((n_blocks * EB,), jnp.int32),
                        pltpu.SemaphoreType.DMA((GATHER_SLOTS,))],
    )
    return pl.pallas_call(
        functools.partial(_expert_kernel, n_tok=N),
        grid_spec=grid_spec,
        out_shape=jax.ShapeDtypeStruct((n_blocks * EB, D), F32),
        compiler_params=_cparams(("arbitrary",)),
        name="moe_experts",
    )(block_e, n_used, dest, hf2d, w_gate, w_up, w_down)


def _combine_kernel(dest_ref, yb_ref, x_ref, meta_ref, mod_ref, *rest, nj, project):
    if project:
        mod1_ref, g1_ref, w_ref, cos_ref, s1_ref, s2_ref, o_ref, p_ref, ybuf, sem = rest
    else:
        fn_ref, o_ref, ybuf, sem = rest
    b = pl.program_id(0)
    j = pl.program_id(1)
    step = b * nj + j
    n_steps = pl.num_programs(0) * nj
    n_tok = n_steps * TB

    def row_copy(st, slot, r, k):
        dst_row = dest_ref[k * n_tok + st * TB + r]
        return pltpu.make_async_copy(yb_ref.at[pl.ds(dst_row, 1), :], ybuf.at[slot, k, pl.ds(r, 1), :], sem.at[slot])

    def issue(st, slot):
        def body(r, carry):
            row_copy(st, slot, r, 0).start()
            row_copy(st, slot, r, 1).start()
            return carry
        lax.fori_loop(0, TB, body, 0, unroll=DMA_UNROLL)

    @pl.when(step == 0)
    def _():
        issue(0, 0)
        issue(1, 1)

    slot = lax.rem(step, GATHER_SLOTS)

    def body(prefetch_ahead):
        def wbody(r, carry):
            row_copy(step, slot, r, 0).wait()
            row_copy(step, slot, r, 1).wait()
            return carry
        lax.fori_loop(0, TB, wbody, 0, unroll=DMA_UNROLL)
        meta = meta_ref[0]
        f = meta[:, 4:5] * ybuf[slot, 0] + meta[:, 5:6] * ybuf[slot, 1]
        m = mod_ref[0, 0]
        xn = x_ref[0] + m[5:6] * f
        ahead = lax.rem(step + 2, GATHER_SLOTS)

        def issue_rows(i, n):
            if prefetch_ahead:
                for r in range(i * TB // n, (i + 1) * TB // n):
                    row_copy(step + 2, ahead, r, 0).start()
                    row_copy(step + 2, ahead, r, 1).start()

        if project:
            o_ref[0] = xn
            m1 = mod1_ref[0, 0]
            h = _modnorm(xn, g1_ref[...], m1[1:2], m1[0:1]).astype(BF16)
            _project_block(h, w_ref, p_ref, (cos_ref, s1_ref, s2_ref), before_piece=issue_rows)
        else:
            issue_rows(0, 1)
            ms = jnp.mean(xn * xn, axis=-1, keepdims=True)
            o_ref[0] = xn * lax.rsqrt(ms + NORM_EPS) * fn_ref[...]

    @pl.when(step + 2 < n_steps)
    def _():
        body(True)

    @pl.when(step + 2 >= n_steps)
    def _():
        body(False)


def _combine(dest, yb, X, meta, mod, all_lat, final_gain=None, project=None):
    B, S, D = X.shape
    nj = S // TB

    def mspec(lat_only):
        if lat_only:
            return pl.BlockSpec((1, 1, 6, D), lambda b, j, ds: (b, 1, 0, 0))
        return pl.BlockSpec((1, 1, 6, D), lambda b, j, ds: (b, jnp.minimum(j, 1), 0, 0))

    row_spec = pl.BlockSpec((1, TB, D), lambda b, j, ds: (b, j, 0))
    in_specs = [pl.BlockSpec(memory_space=pl.ANY), row_spec,
                pl.BlockSpec((1, TB, LANES), lambda b, j, ds: (b, j, 0)), mspec(all_lat)]
    args = [dest, yb, X, meta, mod]
    if project is None:
        in_specs.append(pl.BlockSpec((1, D), lambda b, j, ds: (0, 0)))
        args.append(final_gain.reshape(1, D))
        out_specs = row_spec
        out_shape = jax.ShapeDtypeStruct((B, S, D), F32)
    else:
        mod1, gain1, w_bf16, (cos, s1, s2) = project
        N = w_bf16.shape[1]
        in_specs += [mspec(False), pl.BlockSpec((1, D), lambda b, j, ds: (0, 0)),
                     pl.BlockSpec((D, N), lambda b, j, ds: (0, 0))]
        in_specs += [pl.BlockSpec((TB, LANES), lambda b, j, ds: (j, 0))] * 3
        args += [mod1, gain1.reshape(1, D), w_bf16, cos, s1, s2]
        out_specs = [row_spec, pl.BlockSpec((1, TB, N), lambda b, j, ds: (b, j, 0))]
        out_shape = [jax.ShapeDtypeStruct((B, S, D), F32), jax.ShapeDtypeStruct((B, S, N), BF16)]
    grid_spec = pltpu.PrefetchScalarGridSpec(
        num_scalar_prefetch=1,
        grid=(B, nj),
        in_specs=in_specs,
        out_specs=out_specs,
        scratch_shapes=[pltpu.VMEM((GATHER_SLOTS, 2, TB, D), F32), pltpu.SemaphoreType.DMA((GATHER_SLOTS,))],
    )
    return pl.pallas_call(
        functools.partial(_combine_kernel, nj=nj, project=project is not None),
        grid_spec=grid_spec,
        out_shape=out_shape,
        compiler_params=_cparams(("arbitrary", "arbitrary")),
        name="moe_combine_final" if project is None else "moe_combine_proj",
    )(*args)


def _moe(X, mod, gain, w_router, layer, w_gate, w_up, w_down, all_lat, final_gain=None, project=None):
    B, S, D = X.shape
    hf, meta, metat, counts = _router(X, mod, gain, w_router, all_lat)
    dest, block_e, n_used, n_blocks = _routing_tables(metat, counts)
    yb = _experts(hf.reshape(B * S, D), dest, block_e, n_used, n_blocks, layer, w_gate, w_up, w_down)
    return _combine(dest, yb, X, meta, mod, all_lat, final_gain, project)


def _diff_attn_kernel(q_ref, k_ref, v_ref, dl_ref, sg_ref, o_ref, vx_ref, *, lam_init):
    @pl.when(pl.program_id(2) == 0)
    def _():
        vlane = lax.broadcasted_iota(jnp.int32, (vx_ref.shape[0], LANES), 1)
        vx_ref[:, 0:C_VDIM] = v_ref[0]
        vx_ref[:, C_VDIM:C_VDIM + LANES] = jnp.where(vlane == 0, 1.0, 0.0).astype(BF16)

    q = q_ref[0] * (C_HEAD ** -0.5)
    k = k_ref[0]
    lane = lax.broadcasted_iota(jnp.int32, q.shape, 1)
    zero = jnp.zeros_like(q)
    q1 = jnp.where(lane < C_HEAD, q, zero)
    q2 = jnp.where(lane >= C_HEAD, q, zero)
    dl = dl_ref[...]
    lam = (jnp.exp(jnp.sum(dl[0:1] * dl[1:2], axis=1, keepdims=True))
           - jnp.exp(jnp.sum(dl[2:3] * dl[3:4], axis=1, keepdims=True)) + lam_init)

    s = [lax.dot_general(qq, k, (((1,), (1,)), ((), ())), preferred_element_type=F32) for qq in (q1, q2)]
    e = [jnp.exp(si - jnp.max(si, axis=1, keepdims=True)).astype(BF16) for si in s]
    oz = [jnp.dot(ei, vx_ref[...], preferred_element_type=F32) for ei in e]
    on = [ozi[:, 0:C_VDIM] * (1.0 / ozi[:, C_VDIM:C_VDIM + 1]) for ozi in oz]
    o = on[0] - lam * on[1]
    ms = jnp.mean(o * o, axis=-1, keepdims=True)
    o_ref[0] = (o * lax.rsqrt(ms + DIFF_EPS) * sg_ref[...] * (1.0 - lam_init)).astype(o_ref.dtype)


def _diff_attn(P2, t_ctx, diff_lambda, diff_subln, lam_init):
    B, S, _ = P2.shape
    T = S - t_ctx
    qoff = t_ctx // ATT_TQ
    return pl.pallas_call(
        functools.partial(_diff_attn_kernel, lam_init=lam_init),
        grid=(B, C_HEADS, T // ATT_TQ),
        in_specs=[pl.BlockSpec((1, ATT_TQ, LANES), lambda b, h, q: (b, q + qoff, C_Q_OFF // LANES + h)),
                  pl.BlockSpec((1, S, LANES), lambda b, h, q: (b, 0, C_K_OFF // LANES + h)),
                  pl.BlockSpec((1, S, LANES), lambda b, h, q: (b, 0, C_V_OFF // LANES + h)),
                  pl.BlockSpec((4, C_HEAD), lambda b, h, q: (0, 0)),
                  pl.BlockSpec((1, C_VDIM), lambda b, h, q: (0, 0))],
        out_specs=pl.BlockSpec((1, ATT_TQ, LANES), lambda b, h, q: (b, q, h)),
        out_shape=jax.ShapeDtypeStruct((B, T, C_DIM), BF16),
        scratch_shapes=[pltpu.VMEM((S, C_VDIM + LANES), BF16)],
        compiler_params=_cparams(("arbitrary", "arbitrary", "arbitrary")),
        name="diff_attn",
    )(P2, P2, P2, diff_lambda, diff_subln.reshape(1, -1))


def _retention_kernel(q_ref, k_ref, v0_ref, v1_ref, o_ref, r_ref):
    d = pl.program_id(1)
    c = pl.program_id(2)

    @pl.when(c == 0)
    def _():
        r_ref[...] = jnp.zeros_like(r_ref)

    C = RET_CHUNK
    ti = lax.broadcasted_iota(jnp.int32, (C, C), 0)
    si = lax.broadcasted_iota(jnp.int32, (C, C), 1)
    rel = ((ti - si) * (1 - 2 * d)).astype(F32)
    col = lax.broadcasted_iota(jnp.int32, (C, 1), 0)
    pos = jnp.where(d == 0, col, C - 1 - col).astype(F32)
    chains = [(bi, h) for bi in range(RET_NB) for h in range(D_HEADS)]
    r0s = [r_ref[bi * D_HEADS + h] for bi, h in chains]
    decs = []
    for h in range(D_HEADS):
        lg_f = math.log(1.0 - 2.0 ** (-5.0 - h))
        lg_b = math.log(1.0 - 2.0 ** (-5.0 - (D_HEADS - 1 - h)))
        lg = jnp.where(d == 0, jnp.full((1, 1), lg_f, F32), jnp.full((1, 1), lg_b, F32))
        decs.append((jnp.where(rel >= 0, jnp.exp(lg * jnp.maximum(rel, 0.0)), 0.0),
                     jnp.exp(lg * (pos + 1.0)), jnp.exp(lg * (C - 1.0 - pos)), jnp.exp(lg * C)))
    outs, r_new = [], []
    for n, (bi, h) in enumerate(chains):
        inner_dec, q_dec, k_dec, c_dec = decs[h]
        q = q_ref[bi, :, h * D_KDIM:(h + 1) * D_KDIM]
        k = k_ref[bi, :, h * D_KDIM:(h + 1) * D_KDIM]
        vref = v0_ref if h < 2 else v1_ref
        v = vref[bi, :, (h % 2) * D_VDIM:(h % 2 + 1) * D_VDIM]
        inner = lax.dot_general(q, k, (((1,), (1,)), ((), ())), preferred_element_type=F32) * inner_dec
        outs.append(_bdot(inner, v) + _bdot(q, r0s[n]) * q_dec)
        r_new.append(r0s[n] * c_dec + _bdot(k.astype(F32) * k_dec, v, ((0,), (0,))))
    for n in range(len(chains)):
        r_ref[n] = r_new[n]
    for bi in range(RET_NB):
        o_ref[0, bi] = jnp.concatenate(outs[bi * D_HEADS:(bi + 1) * D_HEADS], axis=1).astype(o_ref.dtype)


def _retention(P2, t_ctx):
    B, S, _ = P2.shape
    C = RET_CHUNK
    nc = S // C
    nctx = t_ctx // C

    def cidx(d, c):
        back = jnp.where(c < nctx, nctx - 1 - c, nc - 1 + nctx - c)
        return jnp.where(d == 0, c, back)

    w = D_HEADS * D_KDIM
    return pl.pallas_call(
        _retention_kernel,
        grid=(B // RET_NB, 2, nc),
        in_specs=[pl.BlockSpec((RET_NB, C, w), lambda b, d, c: (b, cidx(d, c), D_Q_OFF // w)),
                  pl.BlockSpec((RET_NB, C, w), lambda b, d, c: (b, cidx(d, c), D_K_OFF // w)),
                  pl.BlockSpec((RET_NB, C, w), lambda b, d, c: (b, cidx(d, c), D_V_OFF // w)),
                  pl.BlockSpec((RET_NB, C, w), lambda b, d, c: (b, cidx(d, c), D_V_OFF // w + 1))],
        out_specs=pl.BlockSpec((1, RET_NB, C, D_DIM), lambda b, d, c: (d, b, cidx(d, c), 0)),
        out_shape=jax.ShapeDtypeStruct((2, B, S, D_DIM), BF16),
        scratch_shapes=[pltpu.VMEM((RET_NB * D_HEADS, D_KDIM, D_VDIM), F32)],
        compiler_params=_cparams(("arbitrary", "arbitrary", "arbitrary")),
        name="retention",
    )(P2, P2, P2, P2)


def _odd_merge_kernel(c_ref, o2_ref, gate_ref, x_ref, mod_ref, rn_ref, wo_ref, o_ref):
    o = o2_ref[0, 0].astype(F32) + o2_ref[1, 0].astype(F32)
    parts = []
    for h in range(D_HEADS):
        oh = o[:, h * D_VDIM:(h + 1) * D_VDIM]
        ms = jnp.mean(oh * oh, axis=-1, keepdims=True)
        parts.append(oh * lax.rsqrt(ms + NORM_EPS))
    gt = gate_ref[0].astype(F32)
    dl = jnp.concatenate(parts, axis=1) * rn_ref[...] * (gt * jax.nn.sigmoid(gt))
    out = (jnp.dot(c_ref[0], wo_ref[0:C_DIM, :], preferred_element_type=F32)
           + _bdot(dl, wo_ref[C_DIM:C_DIM + D_DIM, :]))
    m = mod_ref[0, 0]
    o_ref[0] = x_ref[0] + m[2:3] * out


def _odd_merge(c_lat, o2, P2, X, mod, ret_norm, wo_bf16, t_ctx):
    B, S, D = X.shape
    T = S - t_ctx
    off = t_ctx // TB
    return pl.pallas_call(
        _odd_merge_kernel,
        grid=(B, T // TB),
        in_specs=[pl.BlockSpec((1, TB, C_DIM), lambda b, j: (b, j, 0)),
                  pl.BlockSpec((2, 1, TB, D_DIM), lambda b, j: (0, b, j + off, 0)),
                  pl.BlockSpec((1, TB, D_DIM), lambda b, j: (b, j + off, D_G_OFF // D_DIM)),
                  pl.BlockSpec((1, TB, D), lambda b, j: (b, j + off, 0)),
                  _mod_spec(D, all_lat=True),
                  pl.BlockSpec((1, D_DIM), lambda b, j: (0, 0)),
                  pl.BlockSpec((C_DIM + D_DIM, D), lambda b, j: (0, 0))],
        out_specs=pl.BlockSpec((1, TB, D), lambda b, j: (b, j, 0)),
        out_shape=jax.ShapeDtypeStruct((B, T, D), F32),
        compiler_params=_cparams(("arbitrary", "arbitrary")),
        name="odd_merge",
    )(c_lat, o2, P2, X, mod, ret_norm.reshape(1, -1), wo_bf16)


def _rope_tables(t_ctx, T):
    half = C_HEAD // 4
    inv = ROPE_BASE ** (-jnp.arange(half, dtype=F32) / half)
    t = jnp.arange(T, dtype=jnp.int32)
    ar = (t // GRID_W).astype(F32)[:, None] * inv[None, :]
    ac = (t % GRID_W).astype(F32)[:, None] * inv[None, :]
    z = jnp.zeros_like(ar)
    cos = jnp.concatenate([jnp.cos(ar), jnp.cos(ar), jnp.cos(ac), jnp.cos(ac)], axis=1)
    s1 = jnp.concatenate([-jnp.sin(ar), z, -jnp.sin(ac), z], axis=1)
    s2 = jnp.concatenate([z, jnp.sin(ar), z, jnp.sin(ac)], axis=1)

    def full(tab, ctx_val):
        tab = jnp.concatenate([jnp.full((t_ctx, C_HEAD), ctx_val, F32), tab], axis=0)
        return jnp.tile(tab, (1, LANES // C_HEAD))

    return full(cos, 1.0), full(s1, 0.0), full(s2, 0.0)


def _head_indicator(n, width):
    i = jnp.arange(n) // width
    return (i[:, None] == i[None, :]).astype(BF16)


def kernel(x, c, ctx, c_ctx, ada_w, ada_b, norm_mix, norm_ffn, ev_w_in, ev_w_out, rwkv_shift, rwkv_w0, rwkv_w_up, rwkv_a0, rwkv_a_up, rwkv_g_up, rwkv_k_k, rwkv_k_a, rwkv_r_k, rwkv_ln_g, rwkv_ln_b, pool_w, pool_scale, od_w_in, od_w_out, diff_lambda, diff_subln, ret_norm, moe_router_group, moe_router_expert, moe_w_gate, moe_w_up, moe_w_down, final_norm):
    B, T, D = x.shape
    t_ctx = ctx.shape[1]
    assert ada_w.shape[0] == 2 and t_ctx == TB and T % TB == 0 and T % GRID_W == 0 and B < 16
    assert B % SCAN_NB == 0 and B % RET_NB == 0

    cpad = jnp.zeros((16, D), F32).at[:B].set(c).at[B].set(c_ctx)
    mods = _ada(cpad, ada_w, ada_b)

    def layer_mod(i):
        lat = mods[i, :B]
        cm = jnp.broadcast_to(mods[i, B][None], lat.shape)
        return jnp.stack([cm, lat], axis=1).reshape(B, 2, 6, D)

    def router_w(i):
        w = jnp.concatenate([moe_router_group[i], moe_router_expert[i]], axis=1)
        return jnp.pad(w, ((0, 0), (0, LANES - w.shape[1])))

    ehead = _head_indicator(A_DIM, A_HEAD)

    mod0 = layer_mod(0)
    P = _proj(ctx, x, mod0, norm_mix[0], ev_w_in[0].astype(BF16), BF16)
    zl = jnp.zeros((LORA_W, A_DIM), F32)
    lora_blocks = []
    for d in range(2):
        lora_blocks.append(jnp.concatenate([rwkv_w_up[0, d], zl], axis=0))
        lora_blocks.append(jnp.concatenate([zl, rwkv_a_up[0, d]], axis=0))
    lora_w = jnp.concatenate(lora_blocks, axis=1).astype(BF16)
    wa0 = jnp.concatenate([rwkv_w0[0, 0], rwkv_a0[0, 0], rwkv_w0[0, 1], rwkv_a0[0, 1]]).reshape(1, -1)
    lw, kt, bb, v, kk, r, bonus, g, yb = _even_prep(
        P, t_ctx, rwkv_shift[0], wa0, lora_w, rwkv_g_up[0].astype(BF16), rwkv_k_k[0], rwkv_k_a[0], rwkv_r_k[0],
        pool_w[0].astype(BF16), pool_scale[0], ehead)
    y2 = _rwkv_scan(lw, kt, bb, v, kk, r, t_ctx)
    X = _even_merge(y2, bonus, g, yb, ctx, x, mod0, rwkv_ln_g[0], rwkv_ln_b[0], ev_w_out[0].astype(BF16), ehead)
    mod1 = layer_mod(1)
    X, P2 = _moe(X, mod0, norm_ffn[0], router_w(0), 0, moe_w_gate, moe_w_up, moe_w_down, all_lat=False,
                 project=(mod1, norm_mix[1], od_w_in[0].astype(BF16), _rope_tables(t_ctx, T)))

    lam_init = 0.8 - 0.6 * math.exp(-0.3 * 1)
    c_lat = _diff_attn(P2, t_ctx, diff_lambda[0], diff_subln[0], lam_init)
    o2 = _retention(P2, t_ctx)
    XL = _odd_merge(c_lat, o2, P2, X, mod1, ret_norm[0], od_w_out[0].astype(BF16), t_ctx)
    return _moe(XL, mod1, norm_ffn[1], router_w(1), 1, moe_w_gate, moe_w_up, moe_w_down, all_lat=True,
                final_gain=final_norm)
```

```python
import functools
import math

import jax
import jax.numpy as jnp
from jax import lax
from jax.experimental import pallas as pl
from jax.experimental.pallas import tpu as pltpu

F32 = jnp.float32
BF16 = jnp.bfloat16
HI = lax.Precision.HIGHEST

GRID_W = 64
NORM_EPS = 1e-6
ROPE_BASE = 10000.0
A_HEADS, A_HEAD = 8, 64
A_DIM = A_HEADS * A_HEAD
LORA_W, LORA_A, LORA_G = 64, 64, 128
W_DECAY_SCALE = 0.606531
RWKV_GN_EPS = 64e-5
POOL_WINDOWS = (2, 4, 8, 16)
POOL_GROUP = 128
B_DIM = POOL_GROUP * 4
C_HEADS, C_HEAD = 4, 64
C_VDIM = 2 * C_HEAD
C_DIM = C_HEADS * C_VDIM
DIFF_EPS = 1e-5
D_HEADS, D_KDIM, D_VDIM = 4, 64, 128
D_DIM = D_HEADS * D_VDIM
A_K_OFF, A_V_OFF = 0, A_DIM
W_LORA_OFF = 2 * A_DIM
A_LORA_OFF = W_LORA_OFF + LORA_W
A_R_OFF = A_LORA_OFF + LORA_A
G_LORA_OFF = A_R_OFF + A_DIM
POOL_OFF = G_LORA_OFF + LORA_G
EVEN_IN_COLS = POOL_OFF + B_DIM
C_K_OFF = 0
C_V_OFF = C_HEADS * 2 * C_HEAD
D_K_OFF = C_V_OFF + C_DIM
D_V_OFF = D_K_OFF + D_HEADS * D_KDIM
C_Q_OFF = D_V_OFF + D_DIM
D_Q_OFF = C_Q_OFF + C_HEADS * 2 * C_HEAD
D_G_OFF = D_Q_OFF + D_HEADS * D_KDIM
ODD_IN_COLS = D_G_OFF + D_DIM
MOE_GROUPS, MOE_PER_GROUP = 4, 8
MOE_EXPERTS = MOE_GROUPS * MOE_PER_GROUP
MOE_TOP_K = 2
EXPERT_BLOCK = 256

LANES = 128
SUBLANES = 8
HALO = 16
TB = 256
RWKV_CHUNK = 64
RET_CHUNK = 256
RET_NB = 2
SCAN_NB = 4
ATT_TQ = 256
VMEM_LIMIT = 56 * 1024 * 1024
DMA_UNROLL = 8
GATHER_SLOTS = 3
TABLE_UNROLL = 32
GATHER_DMA_PRIORITY = 1


def _cparams(sem):
    return pltpu.CompilerParams(dimension_semantics=sem, vmem_limit_bytes=VMEM_LIMIT)


def _modnorm(x, gain, scale, shift, eps=NORM_EPS):
    ms = jnp.mean(x * x, axis=-1, keepdims=True)
    return x * lax.rsqrt(ms + eps) * gain * (1.0 + scale) + shift


def _bdot(a, b, dims=((1,), (0,))):
    return lax.dot_general(a.astype(BF16), b.astype(BF16), (dims, ((), ())), preferred_element_type=F32)


def _bf16_terms(x, terms):
    parts = []
    for _ in range(terms):
        p = x.astype(BF16)
        parts.append(p)
        x = x - p.astype(F32)
    return parts


def _dot_split_lhs(x, w_bf16, terms=2):
    return sum(jnp.dot(p, w_bf16, preferred_element_type=F32) for p in _bf16_terms(x, terms))


def _ada_kernel(c_ref, w_ref, b_ref, o_ref):
    c = c_ref[...]
    s = c * jax.nn.sigmoid(c)
    o_ref[0] = jnp.dot(s, w_ref[0], precision=HI, preferred_element_type=F32) + b_ref[0]


def _ada(cpad, ada_w, ada_b):
    L, D, N = ada_w.shape
    tn = 1536
    return pl.pallas_call(
        _ada_kernel,
        grid=(L, N // tn),
        in_specs=[pl.BlockSpec((16, D), lambda l, n: (0, 0)),
                  pl.BlockSpec((1, D, tn), lambda l, n: (l, 0, n)),
                  pl.BlockSpec((1, 1, tn), lambda l, n: (l, 0, n))],
        out_specs=pl.BlockSpec((1, 16, tn), lambda l, n: (l, 0, n)),
        out_shape=jax.ShapeDtypeStruct((L, 16, N), F32),
        compiler_params=_cparams(("arbitrary", "arbitrary")),
        name="ada_mod",
    )(cpad, ada_w, ada_b.reshape(L, 1, N))


def _mod_spec(D, all_lat=False):
    if all_lat:
        return pl.BlockSpec((1, 1, 6, D), lambda b, j: (b, 1, 0, 0))
    return pl.BlockSpec((1, 1, 6, D), lambda b, j: (b, jnp.minimum(j, 1), 0, 0))


def _rope_plan():
    ranges = [(C_K_OFF, C_V_OFF), (D_K_OFF, D_V_OFF), (C_Q_OFF, D_G_OFF)]
    rope_chunks = tuple(g for lo, hi in ranges for g in range(lo // LANES, hi // LANES))
    kscale_chunks = tuple(range(D_K_OFF // LANES, D_V_OFF // LANES))
    return rope_chunks, kscale_chunks


def _project_block(h, w_ref, o_ref, rope_refs=None, before_piece=None):
    n_out = w_ref.shape[1]
    rope_chunks, kscale_chunks = _rope_plan() if rope_refs is not None else ((), ())
    cw = 512
    for n0 in range(0, n_out, cw):
        if before_piece is not None:
            before_piece(n0 // cw, n_out // cw)
        p = jnp.dot(h, w_ref[:, n0:n0 + cw], preferred_element_type=F32)
        if rope_refs is None:
            o_ref[0, :, n0:n0 + cw] = p.astype(o_ref.dtype)
            continue
        cos_ref, s1_ref, s2_ref = rope_refs
        for q in range(cw // LANES):
            gi = n0 // LANES + q
            sub = p[:, q * LANES:(q + 1) * LANES]
            if gi in rope_chunks:
                sub = (sub * cos_ref[...] + pltpu.roll(sub, LANES - 16, 1) * s1_ref[...]
                       + pltpu.roll(sub, 16, 1) * s2_ref[...])
            if gi in kscale_chunks:
                sub = sub * (D_KDIM ** -0.5)
            o_ref[0, :, n0 + q * LANES:n0 + (q + 1) * LANES] = sub.astype(o_ref.dtype)


def _stream_specs(D):
    return [pl.BlockSpec((1, TB, D), lambda b, j: (b, 0, 0)),
            pl.BlockSpec((1, TB, D), lambda b, j: (b, jnp.maximum(j - 1, 0), 0))]


def _stream_block(ctx_ref, x_ref):
    return jnp.where(pl.program_id(1) == 0, ctx_ref[0], x_ref[0])


def _proj_kernel(ctx_ref, x_ref, mod_ref, g_ref, w_ref, o_ref):
    m = mod_ref[0, 0]
    h = _modnorm(_stream_block(ctx_ref, x_ref), g_ref[...], m[1:2], m[0:1]).astype(BF16)
    _project_block(h, w_ref, o_ref)


def _proj(ctx, x, mod, gain, w_bf16, out_dtype):
    B, T, D = x.shape
    S = ctx.shape[1] + T
    N = w_bf16.shape[1]
    return pl.pallas_call(
        _proj_kernel,
        grid=(B, S // TB),
        in_specs=_stream_specs(D) + [
                  _mod_spec(D),
                  pl.BlockSpec((1, D), lambda b, j: (0, 0)),
                  pl.BlockSpec((D, N), lambda b, j: (0, 0))],
        out_specs=pl.BlockSpec((1, TB, N), lambda b, j: (b, j, 0)),
        out_shape=jax.ShapeDtypeStruct((B, S, N), out_dtype),
        compiler_params=_cparams(("arbitrary", "arbitrary")),
        name="proj",
    )(ctx, x, mod, gain.reshape(1, D), w_bf16)


def _even_prep_kernel(p_ref, pp_ref, pn_ref, sh_ref, wa0_ref, lora_ref, gup_ref, kkg_ref, ka_ref, rk_ref,
                      pw_ref, ps_ref, eh_ref,
                      lw_ref, kt_ref, bb_ref, v_ref, kkn_ref, r_ref, bonus_ref, g_ref, yb_ref,
                      *, nblk, t_ctx, t_lat):
    j = pl.program_id(1)
    prev_ok = j >= 2
    next_ok = jnp.logical_and(j >= 1, j < nblk - 1)
    n_ext = TB + 2 * HALO

    def ext(c0, c1):
        pv = jnp.where(prev_ok, pp_ref[0, :, c0:c1].astype(F32), 0.0)
        nx = jnp.where(next_ok, pn_ref[0, :, c0:c1].astype(F32), 0.0)
        return jnp.concatenate([pv, p_ref[0, :, c0:c1].astype(F32), nx], axis=0)

    def shift3(c0, c1, k0):
        e = ext(c0, c1)
        kern = sh_ref[:, k0:k0 + (c1 - c0)]
        y = kern[0:1] * pltpu.roll(e, 1, 0) + kern[1:2] * e + kern[2:3] * pltpu.roll(e, n_ext - 1, 0)
        return y[HALO:HALO + TB]

    k = shift3(A_K_OFF, A_K_OFF + A_DIM, 0)
    v = shift3(A_V_OFF, A_V_OFF + A_DIM, A_DIM)
    r = shift3(A_R_OFF, A_R_OFF + A_DIM, 2 * A_DIM)

    c = p_ref[0, :, W_LORA_OFF:W_LORA_OFF + LANES].astype(F32)
    lane = lax.broadcasted_iota(jnp.int32, c.shape, 1)
    lin = jnp.where(lane < LORA_W, jnp.tanh(c), c)
    z = _bdot(lin, lora_ref[...]) + wa0_ref[...]

    eh = eh_ref[...]
    kk0 = k * kkg_ref[...]
    ss = _dot_split_lhs(kk0 * kk0, eh)
    kk = kk0 * lax.rsqrt(jnp.maximum(ss, 1e-12))
    kt_sum = None
    for d in range(2):
        zw = z[:, (2 * d) * A_DIM:(2 * d + 1) * A_DIM]
        za = z[:, (2 * d + 1) * A_DIM:(2 * d + 2) * A_DIM]
        a = jax.nn.sigmoid(za)
        kt = k * (1.0 + (a - 1.0) * ka_ref[...])
        lw_ref[d, 0] = -W_DECAY_SCALE * jax.nn.sigmoid(zw)
        kt_ref[d, 0] = kt.astype(kt_ref.dtype)
        bb_ref[d, 0] = (a * kk).astype(bb_ref.dtype)
        kt_sum = kt if kt_sum is None else kt_sum + kt
    v_ref[0] = v.astype(v_ref.dtype)
    kkn_ref[0] = kk.astype(kkn_ref.dtype)
    r_ref[0] = r.astype(r_ref.dtype)
    bonus_ref[0] = (_dot_split_lhs(r * kt_sum * rk_ref[...], eh) * v).astype(bonus_ref.dtype)
    gl = p_ref[0, :, G_LORA_OFF:G_LORA_OFF + LORA_G].astype(F32)
    g_ref[0] = _bdot(jax.nn.sigmoid(gl), gup_ref[...]).astype(g_ref.dtype)

    row = lax.broadcasted_iota(jnp.int32, (TB, POOL_GROUP), 0)
    tpos = row + jnp.maximum(j - 1, 0) * TB
    tseg = jnp.where(j == 0, t_ctx, t_lat)
    for gi, win in enumerate(POOL_WINDOWS):
        c0 = POOL_OFF + gi * POOL_GROUP
        e = ext(c0, c0 + POOL_GROUP)
        s = e + pltpu.roll(e, 1, 0)
        sh = 1
        while 2 * sh < win:
            s = pltpu.roll(s, sh, 0) + pltpu.roll(s, n_ext - sh, 0)
            sh *= 2
        s = s[HALO:HALO + TB]
        lo = jnp.clip(tpos - win // 2, 0, tseg)
        hi = jnp.clip(tpos - win // 2 + win, 0, tseg)
        mean = s / (hi - lo).astype(F32)
        dd = mean - p_ref[0, :, c0:c0 + POOL_GROUP].astype(F32)
        y = _bdot(dd, pw_ref[gi])
        yb_ref[0, :, gi * POOL_GROUP:(gi + 1) * POOL_GROUP] = (
            y * ps_ref[:, gi * POOL_GROUP:(gi + 1) * POOL_GROUP]).astype(yb_ref.dtype)


def _even_prep(P, t_ctx, shift_k, wa0, lora_w, g_up, k_k, k_a, r_k, pool_w, pool_scale, ehead):
    B, S, N = P.shape
    nblk = S // TB
    hb = TB // HALO
    const2 = lambda b, j: (0, 0)
    in_specs = [
        pl.BlockSpec((1, TB, N), lambda b, j: (b, j, 0)),
        pl.BlockSpec((1, HALO, N), lambda b, j: (b, jnp.maximum(j * hb - 1, 0), 0)),
        pl.BlockSpec((1, HALO, N), lambda b, j: (b, jnp.minimum((j + 1) * hb, S // HALO - 1), 0)),
        pl.BlockSpec((3, 3 * A_DIM), const2),
        pl.BlockSpec((1, 4 * A_DIM), const2),
        pl.BlockSpec((LANES, 4 * A_DIM), const2),
        pl.BlockSpec((LORA_G, A_DIM), const2),
        pl.BlockSpec((1, A_DIM), const2),
        pl.BlockSpec((1, A_DIM), const2),
        pl.BlockSpec((1, A_DIM), const2),
        pl.BlockSpec((4, POOL_GROUP, POOL_GROUP), lambda b, j: (0, 0, 0)),
        pl.BlockSpec((1, B_DIM), const2),
        pl.BlockSpec((A_DIM, A_DIM), const2),
    ]
    two_f32 = jax.ShapeDtypeStruct((2, B, S, A_DIM), F32)
    two = jax.ShapeDtypeStruct((2, B, S, A_DIM), BF16)
    one = jax.ShapeDtypeStruct((B, S, A_DIM), BF16)
    two_spec = pl.BlockSpec((2, 1, TB, A_DIM), lambda b, j: (0, b, j, 0))
    one_spec = pl.BlockSpec((1, TB, A_DIM), lambda b, j: (b, j, 0))
    return pl.pallas_call(
        functools.partial(_even_prep_kernel, nblk=nblk, t_ctx=t_ctx, t_lat=S - t_ctx),
        grid=(B, nblk),
        in_specs=in_specs,
        out_specs=[two_spec] * 3 + [one_spec] * 6,
        out_shape=[two_f32, two, two] + [one] * 6,
        compiler_params=_cparams(("arbitrary", "arbitrary")),
        name="even_prep",
    )(P, P, P, shift_k, wa0, lora_w, g_up, k_k.reshape(1, -1), k_a.reshape(1, -1), r_k.reshape(1, -1),
      pool_w, pool_scale.reshape(1, -1), ehead)


def _rwkv_scan_kernel(lw_ref, kt_ref, bb_ref, v_ref, kk_ref, r_ref, y_ref, s_ref):
    d = pl.program_id(1)
    c = pl.program_id(2)

    @pl.when(c == 0)
    def _():
        s_ref[...] = jnp.zeros_like(s_ref)

    C = RWKV_CHUNK
    tt = lax.broadcasted_iota(jnp.int32, (C, C), 0)
    ii = lax.broadcasted_iota(jnp.int32, (C, C), 1)
    rel = (tt - ii) * (1 - 2 * d)
    strict = rel > 0
    incl = rel >= 0
    t2 = lax.broadcasted_iota(jnp.int32, (C, 2 * C), 0)
    i2 = lax.broadcasted_iota(jnp.int32, (C, 2 * C), 1) & (C - 1)
    incl2 = (t2 - i2) * (1 - 2 * d) >= 0
    tri = jnp.where(incl, 1.0, 0.0).astype(BF16)
    n_dbl = int(math.log2(C))
    hsl = [slice(h * A_HEAD, (h + 1) * A_HEAD) for h in range(A_HEADS)]
    ar, bk, vh, bkp, ptot = [], [], [], [], []
    for bi in range(SCAN_NB):
        lw = lw_ref[0, bi]
        cum = sum(jnp.dot(tri, p, preferred_element_type=F32) for p in _bf16_terms(lw, 3))
        tot = jnp.sum(lw, axis=0, keepdims=True)
        p_inv = jnp.exp(-cum)
        p_rest = jnp.exp(tot - cum)
        p_tot = jnp.exp(tot)
        bb = bb_ref[0, bi]
        kt = kt_ref[0, bi]
        at_all = (-kk_ref[bi] * jnp.exp(cum - lw)).astype(BF16)
        rt_all = (r_ref[bi] * jnp.exp(cum)).astype(BF16)
        bt_all = (bb * p_inv).astype(BF16)
        kti_all = (kt * p_inv).astype(BF16)
        bp_all = (bb * p_rest).astype(BF16)
        kp_all = (kt * p_rest).astype(BF16)
        v_all = v_ref[bi].astype(BF16)
        for s in hsl:
            ar.append(jnp.concatenate([at_all[:, s], rt_all[:, s]], axis=0))
            bk.append(jnp.concatenate([bt_all[:, s], kti_all[:, s]], axis=0))
            bkp.append(jnp.concatenate([bp_all[:, s], kp_all[:, s]], axis=0))
            vh.append(v_all[:, s])
            ptot.append(p_tot[:, s])
    chains = range(SCAN_NB * A_HEADS)
    s0 = [s_ref[n] for n in chains]
    m = [_bdot(ar[n], bk[n], ((1,), (1,))) for n in chains]
    arh = [_bdot(ar[n], s0[n], ((1,), (1,))) for n in chains]
    lp = [jnp.where(strict, m[n][:C, :C], 0.0) for n in chains]
    a_ak = [jnp.where(strict, m[n][:C, C:], 0.0) for n in chains]
    a_r = [jnp.where(incl2, m[n][C:, :], 0.0) for n in chains]
    u = [arh[n][:C] + _bdot(a_ak[n], vh[n]) for n in chains]
    for q in range(n_dbl):
        u = [u[n] + _bdot(lp[n], u[n]) for n in chains]
        if q < n_dbl - 1:
            lp = [_bdot(lp[n], lp[n]) for n in chains]
    uv = [jnp.concatenate([u[n].astype(BF16), vh[n]], axis=0) for n in chains]
    outs = [arh[n][C:] + _bdot(a_r[n], uv[n]) for n in chains]
    s_new = [s0[n] * ptot[n] + _bdot(uv[n], bkp[n], ((0,), (0,))) for n in chains]
    for n in chains:
        s_ref[n] = s_new[n]
    for bi in range(SCAN_NB):
        y_ref[0, bi] = jnp.concatenate(outs[bi * A_HEADS:(bi + 1) * A_HEADS], axis=1).astype(y_ref.dtype)


def _rwkv_scan(lw, kt, bb, v, kk, r, t_ctx):
    _, B, S, _ = lw.shape
    C = RWKV_CHUNK
    nc = S // C
    nctx = t_ctx // C

    def cidx(d, c):
        back = jnp.where(c < nctx, nctx - 1 - c, nc - 1 + nctx - c)
        return jnp.where(d == 0, c, back)

    two_spec = pl.BlockSpec((1, SCAN_NB, C, A_DIM), lambda b, d, c: (d, b, cidx(d, c), 0))
    one_spec = pl.BlockSpec((SCAN_NB, C, A_DIM), lambda b, d, c: (b, cidx(d, c), 0))
    return pl.pallas_call(
        _rwkv_scan_kernel,
        grid=(B // SCAN_NB, 2, nc),
        in_specs=[two_spec] * 3 + [one_spec] * 3,
        out_specs=two_spec,
        out_shape=jax.ShapeDtypeStruct((2, B, S, A_DIM), BF16),
        scratch_shapes=[pltpu.VMEM((SCAN_NB * A_HEADS, A_HEAD, A_HEAD), F32)],
        compiler_params=_cparams(("arbitrary", "arbitrary", "arbitrary")),
        name="rwkv_scan",
    )(lw, kt, bb, v, kk, r)


def _even_merge_kernel(y2_ref, bonus_ref, g_ref, yb_ref, ctx_ref, x_ref, mod_ref, lng_ref, lnb_ref, wo_ref, eh_ref,
                       gffn_ref, wr_ref, o_ref, *route_refs):
    y = y2_ref[0, 0].astype(F32) + y2_ref[1, 0].astype(F32)
    eh = eh_ref[...]
    mu = _dot_split_lhs(y, eh) * (1.0 / A_HEAD)
    yc = y - mu
    var = _dot_split_lhs(yc * yc, eh) * (1.0 / A_HEAD)
    yn = yc * lax.rsqrt(var + RWKV_GN_EPS) * lng_ref[...] + lnb_ref[...]
    ya = (yn + bonus_ref[0]) * g_ref[0]
    out = _bdot(ya, wo_ref[0:A_DIM, :]) + _bdot(yb_ref[0], wo_ref[A_DIM:A_DIM + B_DIM, :])
    m = mod_ref[0, 0]
    xn = _stream_block(ctx_ref, x_ref) + m[2:3] * out
    o_ref[0] = xn
    _route_block(xn, m, gffn_ref, wr_ref, *route_refs)


def _even_merge(y2, bonus, g, yb, ctx, x, mod, ln_g, ln_b, wo_bf16, ehead, ffn_gain, w_router):
    B, T, D = x.shape
    S = ctx.shape[1] + T
    r_in, r_out, r_shapes, r_scratch = _route_specs(B, S, D)
    one_spec = pl.BlockSpec((1, TB, A_DIM), lambda b, j: (b, j, 0))
    const2 = lambda b, j: (0, 0)
    return pl.pallas_call(
        _even_merge_kernel,
        grid=(B, S // TB),
        in_specs=[pl.BlockSpec((2, 1, TB, A_DIM), lambda b, j: (0, b, j, 0)), one_spec, one_spec, one_spec]
        + _stream_specs(D) + [_mod_spec(D),
                  pl.BlockSpec((1, A_DIM), const2), pl.BlockSpec((1, A_DIM), const2),
                  pl.BlockSpec((A_DIM + B_DIM, D), const2), pl.BlockSpec((A_DIM, A_DIM), const2)] + r_in,
        out_specs=[pl.BlockSpec((1, TB, D), lambda b, j: (b, j, 0))] + r_out,
        out_shape=[jax.ShapeDtypeStruct((B, S, D), F32)] + r_shapes,
        scratch_shapes=r_scratch,
        compiler_params=_cparams(("arbitrary", "arbitrary")),
        name="even_merge_route",
    )(y2, bonus, g, yb, ctx, x, mod, ln_g.reshape(1, -1), ln_b.reshape(1, -1), wo_bf16, ehead,
      ffn_gain.reshape(1, D), w_router)


def _route_block(xn, m, g_ref, wr_ref, hf_ref, meta_ref, metat_ref, cnt_ref, carry_ref):
    first = jnp.logical_and(pl.program_id(0) == 0, pl.program_id(1) == 0)

    @pl.when(first)
    def _():
        carry_ref[...] = jnp.zeros_like(carry_ref)

    hf = _modnorm(xn, g_ref[...], m[4:5], m[3:4])
    hf_ref[0] = hf
    h0, h1 = _bf16_terms(hf, 2)
    w0, w1 = _bf16_terms(wr_ref[...], 2)
    logits = (jnp.dot(h0, w0, preferred_element_type=F32) + jnp.dot(h0, w1, preferred_element_type=F32)
              + jnp.dot(h1, w0, preferred_element_type=F32))
    lane = lax.broadcasted_iota(jnp.int32, logits.shape, 1)
    neg = -jnp.inf
    lg = jnp.where(lane < MOE_GROUPS, logits, neg)
    mg = jnp.max(lg, axis=1, keepdims=True)
    grp = jnp.min(jnp.where(lg == mg, lane, LANES), axis=1, keepdims=True)
    gw = 1.0 / jnp.sum(jnp.exp(lg - mg), axis=1, keepdims=True)
    el = lane - MOE_GROUPS
    in_grp = jnp.logical_and(el >= grp * MOE_PER_GROUP, el < (grp + 1) * MOE_PER_GROUP)
    le = jnp.where(in_grp, logits, neg)
    m0 = jnp.max(le, axis=1, keepdims=True)
    i0 = jnp.min(jnp.where(le == m0, lane, LANES), axis=1, keepdims=True)
    le1 = jnp.where(lane == i0, neg, le)
    m1 = jnp.max(le1, axis=1, keepdims=True)
    i1 = jnp.min(jnp.where(le1 == m1, lane, LANES), axis=1, keepdims=True)
    p1 = jnp.exp(m1 - m0)
    gate0 = gw / (1.0 + p1)
    gate1 = gw * p1 / (1.0 + p1)
    e0 = i0 - MOE_GROUPS
    e1 = i1 - MOE_GROUPS
    oh0 = (lane == e0).astype(F32)
    oh1 = (lane == e1).astype(F32)
    cnt = oh0 + oh1
    rr = lax.broadcasted_iota(jnp.int32, (TB, TB), 0)
    cc = lax.broadcasted_iota(jnp.int32, (TB, TB), 1)
    before = _bdot((cc < rr).astype(F32), cnt) + carry_ref[...]
    rank0 = jnp.sum(oh0 * before, axis=1, keepdims=True)
    rank1 = jnp.sum(oh1 * before, axis=1, keepdims=True)
    carry_ref[...] = carry_ref[...] + jnp.sum(cnt, axis=0, keepdims=True)
    cnt_ref[...] = carry_ref[...]
    meta = jnp.where(lane == 0, e0.astype(F32), 0.0)
    meta = jnp.where(lane == 1, e1.astype(F32), meta)
    meta = jnp.where(lane == 2, rank0, meta)
    meta = jnp.where(lane == 3, rank1, meta)
    meta = jnp.where(lane == 4, gate0, meta)
    meta = jnp.where(lane == 5, gate1, meta)
    meta_ref[0] = meta
    metat_ref[0, 0] = meta.T[0:SUBLANES]


def _route_specs(B, S, D):
    in_specs = [pl.BlockSpec((1, D), lambda b, j: (0, 0)), pl.BlockSpec((D, LANES), lambda b, j: (0, 0))]
    out_specs = [pl.BlockSpec((1, TB, D), lambda b, j: (b, j, 0)),
                 pl.BlockSpec((1, TB, LANES), lambda b, j: (b, j, 0)),
                 pl.BlockSpec((1, 1, SUBLANES, TB), lambda b, j: (b, j, 0, 0)),
                 pl.BlockSpec((1, LANES), lambda b, j: (0, 0))]
    out_shapes = [jax.ShapeDtypeStruct((B, S, D), F32), jax.ShapeDtypeStruct((B, S, LANES), F32),
                  jax.ShapeDtypeStruct((B, S // TB, SUBLANES, TB), F32), jax.ShapeDtypeStruct((1, LANES), F32)]
    return in_specs, out_specs, out_shapes, [pltpu.VMEM((1, LANES), F32)]


def _routing_tables(metat, counts):
    B, nj = metat.shape[:2]
    n_tok = B * nj * TB
    mi = jnp.transpose(metat[:, :, 0:4, :], (2, 0, 1, 3)).reshape(4, n_tok).astype(jnp.int32)
    cnt = counts[0, :MOE_EXPERTS].astype(jnp.int32)
    padded = (cnt + EXPERT_BLOCK - 1) // EXPERT_BLOCK * EXPERT_BLOCK
    pad_end = jnp.cumsum(padded)
    pad_start = pad_end - padded
    dest = jnp.concatenate([pad_start[mi[0]] + mi[2], pad_start[mi[1]] + mi[3]])
    n_blocks = -(-(n_tok * MOE_TOP_K) // EXPERT_BLOCK) + MOE_EXPERTS
    starts = jnp.arange(n_blocks, dtype=jnp.int32) * EXPERT_BLOCK
    block_e = jnp.minimum(jnp.sum((pad_end[None, :] <= starts[:, None]).astype(jnp.int32), axis=1), MOE_EXPERTS - 1)
    n_used = (pad_end[-1] // EXPERT_BLOCK).astype(jnp.int32).reshape(1)
    return dest, block_e, n_used, n_blocks


def _expert_kernel(be_ref, nu_ref, dest_ref, hf_ref, wg_ref, wu_ref, wd_ref, o_ref, xbuf, wg_bf, wu_bf, wd_bf, rows_ref,
                   sem, *, n_tok):
    j = pl.program_id(0)
    n_used = nu_ref[0]
    EB = EXPERT_BLOCK
    n_slots = rows_ref.shape[0]

    def row_copy(blk, slot, r):
        row = rows_ref[blk * EB + r]
        return pltpu.make_async_copy(hf_ref.at[pl.ds(row, 1), :], xbuf.at[slot, pl.ds(r, 1), :], sem.at[slot])

    def issue(blk, slot):
        def body(r, carry):
            row_copy(blk, slot, r).start()
            return carry
        lax.fori_loop(0, EB, body, 0, unroll=DMA_UNROLL)

    @pl.when(j == 0)
    def _():
        fill_mask = (1 << (n_tok.bit_length() - 1)) - 1

        def fill(i, carry):
            rows_ref[i] = jnp.bitwise_and(i, fill_mask)
            return carry
        lax.fori_loop(0, n_slots, fill, 0, unroll=TABLE_UNROLL)

        def invert(t, carry):
            rows_ref[dest_ref[t]] = t
            rows_ref[dest_ref[n_tok + t]] = t
            return carry
        lax.fori_loop(0, n_tok, invert, 0, unroll=TABLE_UNROLL)
        issue(0, 0)

        @pl.when(n_used > 1)
        def _():
            issue(1, 1)

    def compute(prefetch_ahead):
        slot = lax.rem(j, GATHER_SLOTS)

        def body(r, carry):
            row_copy(j, slot, r).wait()
            return carry
        lax.fori_loop(0, EB, body, 0, unroll=DMA_UNROLL)

        @pl.when(jnp.logical_or(j == 0, be_ref[j] != be_ref[jnp.maximum(j - 1, 0)]))
        def _():
            wg_bf[...] = wg_ref[0, 0].astype(BF16)
            wu_bf[...] = wu_ref[0, 0].astype(BF16)
            wd_bf[...] = wd_ref[0, 0].astype(BF16)

        xb = xbuf[slot].astype(BF16)
        if prefetch_ahead:
            ahead = lax.rem(j + 2, GATHER_SLOTS)
            for r in range(EB):
                row_copy(j + 2, ahead, r).start(priority=GATHER_DMA_PRIORITY)
        hg = jnp.dot(xb, wg_bf[...], preferred_element_type=F32)
        hu = jnp.dot(xb, wu_bf[...], preferred_element_type=F32)
        hm = (hg * jax.nn.sigmoid(hg) * hu).astype(BF16)
        o_ref[...] = jnp.dot(hm, wd_bf[...], preferred_element_type=F32)

    @pl.when(j + 2 < n_used)
    def _():
        compute(True)

    @pl.when(jnp.logical_and(j < n_used, j + 2 >= n_used))
    def _():
        compute(False)

    @pl.when(j >= n_used)
    def _():
        o_ref[...] = jnp.zeros_like(o_ref)


def _experts(hf2d, dest, block_e, n_used, n_blocks, layer, w_gate, w_up, w_down):
    N, D = hf2d.shape
    _, E, _, De = w_gate.shape
    EB = EXPERT_BLOCK
    grid_spec = pltpu.PrefetchScalarGridSpec(
        num_scalar_prefetch=3,
        grid=(n_blocks,),
        in_specs=[pl.BlockSpec(memory_space=pl.ANY),
                  pl.BlockSpec((1, 1, D, De), lambda j, be, nu, rw: (layer, be[j], 0, 0)),
                  pl.BlockSpec((1, 1, D, De), lambda j, be, nu, rw: (layer, be[j], 0, 0)),
                  pl.BlockSpec((1, 1, De, D), lambda j, be, nu, rw: (layer, be[j], 0, 0))],
        out_specs=pl.BlockSpec((EB, D), lambda j, be, nu, rw: (j, 0)),
        scratch_shapes=[pltpu.VMEM((GATHER_SLOTS, EB, D), F32), pltpu.VMEM((D, De), BF16), pltpu.VMEM((D, De), BF16),
                        pltpu.VMEM((De, D), BF16), pltpu.SMEM((n_blocks * EB,), jnp.int32),
                        pltpu.SemaphoreType.DMA((GATHER_SLOTS,))],
    )
    return pl.pallas_call(
        functools.partial(_expert_kernel, n_tok=N),
        grid_spec=grid_spec,
        out_shape=jax.ShapeDtypeStruct((n_blocks * EB, D), F32),
        compiler_params=_cparams(("arbitrary",)),
        name="moe_experts",
    )(block_e, n_used, dest, hf2d, w_gate, w_up, w_down)


def _combine_kernel(dest_ref, yb_ref, x_ref, meta_ref, mod_ref, *rest, nj, project):
    if project:
        mod1_ref, g1_ref, w_ref, cos_ref, s1_ref, s2_ref, o_ref, p_ref, ybuf, sem = rest
    else:
        fn_ref, o_ref, ybuf, sem = rest
    b = pl.program_id(0)
    j = pl.program_id(1)
    step = b * nj + j
    n_steps = pl.num_programs(0) * nj
    n_tok = n_steps * TB

    def row_copy(st, slot, r, k):
        dst_row = dest_ref[k * n_tok + st * TB + r]
        return pltpu.make_async_copy(yb_ref.at[pl.ds(dst_row, 1), :], ybuf.at[slot, k, pl.ds(r, 1), :], sem.at[slot])

    def issue(st, slot):
        def body(r, carry):
            row_copy(st, slot, r, 0).start()
            row_copy(st, slot, r, 1).start()
            return carry
        lax.fori_loop(0, TB, body, 0, unroll=DMA_UNROLL)

    @pl.when(step == 0)
    def _():
        issue(0, 0)
        issue(1, 1)

    slot = lax.rem(step, GATHER_SLOTS)

    def body(prefetch_ahead):
        def wbody(r, carry):
            row_copy(step, slot, r, 0).wait()
            row_copy(step, slot, r, 1).wait()
            return carry
        lax.fori_loop(0, TB, wbody, 0, unroll=DMA_UNROLL)
        meta = meta_ref[0]
        f = meta[:, 4:5] * ybuf[slot, 0] + meta[:, 5:6] * ybuf[slot, 1]
        m = mod_ref[0, 0]
        xn = x_ref[0] + m[5:6] * f
        ahead = lax.rem(step + 2, GATHER_SLOTS)

        def issue_rows(i, n):
            if prefetch_ahead:
                for r in range(i * TB // n, (i + 1) * TB // n):
                    row_copy(step + 2, ahead, r, 0).start()
                    row_copy(step + 2, ahead, r, 1).start()

        if project:
            o_ref[0] = xn
            m1 = mod1_ref[0, 0]
            h = _modnorm(xn, g1_ref[...], m1[1:2], m1[0:1]).astype(BF16)
            _project_block(h, w_ref, p_ref, (cos_ref, s1_ref, s2_ref), before_piece=issue_rows)
        else:
            issue_rows(0, 1)
            ms = jnp.mean(xn * xn, axis=-1, keepdims=True)
            o_ref[0] = xn * lax.rsqrt(ms + NORM_EPS) * fn_ref[...]

    @pl.when(step + 2 < n_steps)
    def _():
        body(True)

    @pl.when(step + 2 >= n_steps)
    def _():
        body(False)


def _combine(dest, yb, X, meta, mod, all_lat, final_gain=None, project=None):
    B, S, D = X.shape
    nj = S // TB

    def mspec(lat_only):
        if lat_only:
            return pl.BlockSpec((1, 1, 6, D), lambda b, j, ds: (b, 1, 0, 0))
        return pl.BlockSpec((1, 1, 6, D), lambda b, j, ds: (b, jnp.minimum(j, 1), 0, 0))

    row_spec = pl.BlockSpec((1, TB, D), lambda b, j, ds: (b, j, 0))
    in_specs = [pl.BlockSpec(memory_space=pl.ANY), row_spec,
                pl.BlockSpec((1, TB, LANES), lambda b, j, ds: (b, j, 0)), mspec(all_lat)]
    args = [dest, yb, X, meta, mod]
    if project is None:
        in_specs.append(pl.BlockSpec((1, D), lambda b, j, ds: (0, 0)))
        args.append(final_gain.reshape(1, D))
        out_specs = row_spec
        out_shape = jax.ShapeDtypeStruct((B, S, D), F32)
    else:
        mod1, gain1, w_bf16, (cos, s1, s2) = project
        N = w_bf16.shape[1]
        in_specs += [mspec(False), pl.BlockSpec((1, D), lambda b, j, ds: (0, 0)),
                     pl.BlockSpec((D, N), lambda b, j, ds: (0, 0))]
        in_specs += [pl.BlockSpec((TB, LANES), lambda b, j, ds: (j, 0))] * 3
        args += [mod1, gain1.reshape(1, D), w_bf16, cos, s1, s2]
        out_specs = [row_spec, pl.BlockSpec((1, TB, N), lambda b, j, ds: (b, j, 0))]
        out_shape = [jax.ShapeDtypeStruct((B, S, D), F32), jax.ShapeDtypeStruct((B, S, N), BF16)]
    grid_spec = pltpu.PrefetchScalarGridSpec(
        num_scalar_prefetch=1,
        grid=(B, nj),
        in_specs=in_specs,
        out_specs=out_specs,
        scratch_shapes=[pltpu.VMEM((GATHER_SLOTS, 2, TB, D), F32), pltpu.SemaphoreType.DMA((GATHER_SLOTS,))],
    )
    return pl.pallas_call(
        functools.partial(_combine_kernel, nj=nj, project=project is not None),
        grid_spec=grid_spec,
        out_shape=out_shape,
        compiler_params=_cparams(("arbitrary", "arbitrary")),
        name="moe_combine_final" if project is None else "moe_combine_proj",
    )(*args)


def _moe(routed, mod, layer, w_gate, w_up, w_down, all_lat, final_gain=None, project=None):
    X, hf, meta, metat, counts = routed
    B, S, D = X.shape
    dest, block_e, n_used, n_blocks = _routing_tables(metat, counts)
    yb = _experts(hf.reshape(B * S, D), dest, block_e, n_used, n_blocks, layer, w_gate, w_up, w_down)
    return _combine(dest, yb, X, meta, mod, all_lat, final_gain, project)


def _diff_attn_kernel(q_ref, k_ref, v_ref, dl_ref, sg_ref, o_ref, vx_ref, *, lam_init):
    @pl.when(pl.program_id(2) == 0)
    def _():
        vlane = lax.broadcasted_iota(jnp.int32, (vx_ref.shape[0], LANES), 1)
        vx_ref[:, 0:C_VDIM] = v_ref[0]
        vx_ref[:, C_VDIM:C_VDIM + LANES] = jnp.where(vlane == 0, 1.0, 0.0).astype(BF16)

    q = q_ref[0] * (C_HEAD ** -0.5)
    k = k_ref[0]
    lane = lax.broadcasted_iota(jnp.int32, q.shape, 1)
    zero = jnp.zeros_like(q)
    q1 = jnp.where(lane < C_HEAD, q, zero)
    q2 = jnp.where(lane >= C_HEAD, q, zero)
    dl = dl_ref[...]
    lam = (jnp.exp(jnp.sum(dl[0:1] * dl[1:2], axis=1, keepdims=True))
           - jnp.exp(jnp.sum(dl[2:3] * dl[3:4], axis=1, keepdims=True)) + lam_init)

    s = [lax.dot_general(qq, k, (((1,), (1,)), ((), ())), preferred_element_type=F32) for qq in (q1, q2)]
    e = [jnp.exp(si - jnp.max(si, axis=1, keepdims=True)).astype(BF16) for si in s]
    oz = [jnp.dot(ei, vx_ref[...], preferred_element_type=F32) for ei in e]
    on = [ozi[:, 0:C_VDIM] * (1.0 / ozi[:, C_VDIM:C_VDIM + 1]) for ozi in oz]
    o = on[0] - lam * on[1]
    ms = jnp.mean(o * o, axis=-1, keepdims=True)
    o_ref[0] = (o * lax.rsqrt(ms + DIFF_EPS) * sg_ref[...] * (1.0 - lam_init)).astype(o_ref.dtype)


def _diff_attn(P2, t_ctx, diff_lambda, diff_subln, lam_init):
    B, S, _ = P2.shape
    T = S - t_ctx
    qoff = t_ctx // ATT_TQ
    return pl.pallas_call(
        functools.partial(_diff_attn_kernel, lam_init=lam_init),
        grid=(B, C_HEADS, T // ATT_TQ),
        in_specs=[pl.BlockSpec((1, ATT_TQ, LANES), lambda b, h, q: (b, q + qoff, C_Q_OFF // LANES + h)),
                  pl.BlockSpec((1, S, LANES), lambda b, h, q: (b, 0, C_K_OFF // LANES + h)),
                  pl.BlockSpec((1, S, LANES), lambda b, h, q: (b, 0, C_V_OFF // LANES + h)),
                  pl.BlockSpec((4, C_HEAD), lambda b, h, q: (0, 0)),
                  pl.BlockSpec((1, C_VDIM), lambda b, h, q: (0, 0))],
        out_specs=pl.BlockSpec((1, ATT_TQ, LANES), lambda b, h, q: (b, q, h)),
        out_shape=jax.ShapeDtypeStruct((B, T, C_DIM), BF16),
        scratch_shapes=[pltpu.VMEM((S, C_VDIM + LANES), BF16)],
        compiler_params=_cparams(("arbitrary", "arbitrary", "arbitrary")),
        name="diff_attn",
    )(P2, P2, P2, diff_lambda, diff_subln.reshape(1, -1))


def _retention_kernel(q_ref, k_ref, v0_ref, v1_ref, o_ref, r_ref):
    d = pl.program_id(1)
    c = pl.program_id(2)

    @pl.when(c == 0)
    def _():
        r_ref[...] = jnp.zeros_like(r_ref)

    C = RET_CHUNK
    ti = lax.broadcasted_iota(jnp.int32, (C, C), 0)
    si = lax.broadcasted_iota(jnp.int32, (C, C), 1)
    rel = ((ti - si) * (1 - 2 * d)).astype(F32)
    col = lax.broadcasted_iota(jnp.int32, (C, 1), 0)
    pos = jnp.where(d == 0, col, C - 1 - col).astype(F32)
    chains = [(bi, h) for bi in range(RET_NB) for h in range(D_HEADS)]
    r0s = [r_ref[bi * D_HEADS + h] for bi, h in chains]
    decs = []
    for h in range(D_HEADS):
        lg_f = math.log(1.0 - 2.0 ** (-5.0 - h))
        lg_b = math.log(1.0 - 2.0 ** (-5.0 - (D_HEADS - 1 - h)))
        lg = jnp.where(d == 0, jnp.full((1, 1), lg_f, F32), jnp.full((1, 1), lg_b, F32))
        decs.append((jnp.where(rel >= 0, jnp.exp(lg * jnp.maximum(rel, 0.0)), 0.0),
                     jnp.exp(lg * (pos + 1.0)), jnp.exp(lg * (C - 1.0 - pos)), jnp.exp(lg * C)))
    outs, r_new = [], []
    for n, (bi, h) in enumerate(chains):
        inner_dec, q_dec, k_dec, c_dec = decs[h]
        q = q_ref[bi, :, h * D_KDIM:(h + 1) * D_KDIM]
        k = k_ref[bi, :, h * D_KDIM:(h + 1) * D_KDIM]
        vref = v0_ref if h < 2 else v1_ref
        v = vref[bi, :, (h % 2) * D_VDIM:(h % 2 + 1) * D_VDIM]
        inner = lax.dot_general(q, k, (((1,), (1,)), ((), ())), preferred_element_type=F32) * inner_dec
        outs.append(_bdot(inner, v) + _bdot(q, r0s[n]) * q_dec)
        r_new.append(r0s[n] * c_dec + _bdot(k.astype(F32) * k_dec, v, ((0,), (0,))))
    for n in range(len(chains)):
        r_ref[n] = r_new[n]
    for bi in range(RET_NB):
        o_ref[0, bi] = jnp.concatenate(outs[bi * D_HEADS:(bi + 1) * D_HEADS], axis=1).astype(o_ref.dtype)


def _retention(P2, t_ctx):
    B, S, _ = P2.shape
    C = RET_CHUNK
    nc = S // C
    nctx = t_ctx // C

    def cidx(d, c):
        back = jnp.where(c < nctx, nctx - 1 - c, nc - 1 + nctx - c)
        return jnp.where(d == 0, c, back)

    w = D_HEADS * D_KDIM
    return pl.pallas_call(
        _retention_kernel,
        grid=(B // RET_NB, 2, nc),
        in_specs=[pl.BlockSpec((RET_NB, C, w), lambda b, d, c: (b, cidx(d, c), D_Q_OFF // w)),
                  pl.BlockSpec((RET_NB, C, w), lambda b, d, c: (b, cidx(d, c), D_K_OFF // w)),
                  pl.BlockSpec((RET_NB, C, w), lambda b, d, c: (b, cidx(d, c), D_V_OFF // w)),
                  pl.BlockSpec((RET_NB, C, w), lambda b, d, c: (b, cidx(d, c), D_V_OFF // w + 1))],
        out_specs=pl.BlockSpec((1, RET_NB, C, D_DIM), lambda b, d, c: (d, b, cidx(d, c), 0)),
        out_shape=jax.ShapeDtypeStruct((2, B, S, D_DIM), BF16),
        scratch_shapes=[pltpu.VMEM((RET_NB * D_HEADS, D_KDIM, D_VDIM), F32)],
        compiler_params=_cparams(("arbitrary", "arbitrary", "arbitrary")),
        name="retention",
    )(P2, P2, P2, P2)


def _odd_merge_kernel(c_ref, o2_ref, gate_ref, x_ref, mod_ref, rn_ref, wo_ref, gffn_ref, wr_ref, o_ref, *route_refs):
    o = o2_ref[0, 0].astype(F32) + o2_ref[1, 0].astype(F32)
    parts = []
    for h in range(D_HEADS):
        oh = o[:, h * D_VDIM:(h + 1) * D_VDIM]
        ms = jnp.mean(oh * oh, axis=-1, keepdims=True)
        parts.append(oh * lax.rsqrt(ms + NORM_EPS))
    gt = gate_ref[0].astype(F32)
    dl = jnp.concatenate(parts, axis=1) * rn_ref[...] * (gt * jax.nn.sigmoid(gt))
    out = (jnp.dot(c_ref[0], wo_ref[0:C_DIM, :], preferred_element_type=F32)
           + _bdot(dl, wo_ref[C_DIM:C_DIM + D_DIM, :]))
    m = mod_ref[0, 0]
    xn = x_ref[0] + m[2:3] * out
    o_ref[0] = xn
    _route_block(xn, m, gffn_ref, wr_ref, *route_refs)


def _odd_merge(c_lat, o2, P2, X, mod, ret_norm, wo_bf16, t_ctx, ffn_gain, w_router):
    B, S, D = X.shape
    T = S - t_ctx
    off = t_ctx // TB
    r_in, r_out, r_shapes, r_scratch = _route_specs(B, T, D)
    return pl.pallas_call(
        _odd_merge_kernel,
        grid=(B, T // TB),
        in_specs=[pl.BlockSpec((1, TB, C_DIM), lambda b, j: (b, j, 0)),
                  pl.BlockSpec((2, 1, TB, D_DIM), lambda b, j: (0, b, j + off, 0)),
                  pl.BlockSpec((1, TB, D_DIM), lambda b, j: (b, j + off, D_G_OFF // D_DIM)),
                  pl.BlockSpec((1, TB, D), lambda b, j: (b, j + off, 0)),
                  _mod_spec(D, all_lat=True),
                  pl.BlockSpec((1, D_DIM), lambda b, j: (0, 0)),
                  pl.BlockSpec((C_DIM + D_DIM, D), lambda b, j: (0, 0))] + r_in,
        out_specs=[pl.BlockSpec((1, TB, D), lambda b, j: (b, j, 0))] + r_out,
        out_shape=[jax.ShapeDtypeStruct((B, T, D), F32)] + r_shapes,
        scratch_shapes=r_scratch,
        compiler_params=_cparams(("arbitrary", "arbitrary")),
        name="odd_merge_route",
    )(c_lat, o2, P2, X, mod, ret_norm.reshape(1, -1), wo_bf16, ffn_gain.reshape(1, D), w_router)


def _rope_tables(t_ctx, T):
    half = C_HEAD // 4
    inv = ROPE_BASE ** (-jnp.arange(half, dtype=F32) / half)
    t = jnp.arange(T, dtype=jnp.int32)
    ar = (t // GRID_W).astype(F32)[:, None] * inv[None, :]
    ac = (t % GRID_W).astype(F32)[:, None] * inv[None, :]
    z = jnp.zeros_like(ar)
    cos = jnp.concatenate([jnp.cos(ar), jnp.cos(ar), jnp.cos(ac), jnp.cos(ac)], axis=1)
    s1 = jnp.concatenate([-jnp.sin(ar), z, -jnp.sin(ac), z], axis=1)
    s2 = jnp.concatenate([z, jnp.sin(ar), z, jnp.sin(ac)], axis=1)

    def full(tab, ctx_val):
        tab = jnp.concatenate([jnp.full((t_ctx, C_HEAD), ctx_val, F32), tab], axis=0)
        return jnp.tile(tab, (1, LANES // C_HEAD))

    return full(cos, 1.0), full(s1, 0.0), full(s2, 0.0)


def _head_indicator(n, width):
    i = jnp.arange(n) // width
    return (i[:, None] == i[None, :]).astype(BF16)


def kernel(x, c, ctx, c_ctx, ada_w, ada_b, norm_mix, norm_ffn, ev_w_in, ev_w_out, rwkv_shift, rwkv_w0, rwkv_w_up, rwkv_a0, rwkv_a_up, rwkv_g_up, rwkv_k_k, rwkv_k_a, rwkv_r_k, rwkv_ln_g, rwkv_ln_b, pool_w, pool_scale, od_w_in, od_w_out, diff_lambda, diff_subln, ret_norm, moe_router_group, moe_router_expert, moe_w_gate, moe_w_up, moe_w_down, final_norm):
    B, T, D = x.shape
    t_ctx = ctx.shape[1]
    assert ada_w.shape[0] == 2 and t_ctx == TB and T % TB == 0 and T % GRID_W == 0 and B < 16
    assert B % SCAN_NB == 0 and B % RET_NB == 0

    cpad = jnp.zeros((16, D), F32).at[:B].set(c).at[B].set(c_ctx)
    mods = _ada(cpad, ada_w, ada_b)

    def layer_mod(i):
        lat = mods[i, :B]
        cm = jnp.broadcast_to(mods[i, B][None], lat.shape)
        return jnp.stack([cm, lat], axis=1).reshape(B, 2, 6, D)

    def router_w(i):
        w = jnp.concatenate([moe_router_group[i], moe_router_expert[i]], axis=1)
        return jnp.pad(w, ((0, 0), (0, LANES - w.shape[1])))

    ehead = _head_indicator(A_DIM, A_HEAD)

    mod0 = layer_mod(0)
    P = _proj(ctx, x, mod0, norm_mix[0], ev_w_in[0].astype(BF16), BF16)
    zl = jnp.zeros((LORA_W, A_DIM), F32)
    lora_blocks = []
    for d in range(2):
        lora_blocks.append(jnp.concatenate([rwkv_w_up[0, d], zl], axis=0))
        lora_blocks.append(jnp.concatenate([zl, rwkv_a_up[0, d]], axis=0))
    lora_w = jnp.concatenate(lora_blocks, axis=1).astype(BF16)
    wa0 = jnp.concatenate([rwkv_w0[0, 0], rwkv_a0[0, 0], rwkv_w0[0, 1], rwkv_a0[0, 1]]).reshape(1, -1)
    lw, kt, bb, v, kk, r, bonus, g, yb = _even_prep(
        P, t_ctx, rwkv_shift[0], wa0, lora_w, rwkv_g_up[0].astype(BF16), rwkv_k_k[0], rwkv_k_a[0], rwkv_r_k[0],
        pool_w[0].astype(BF16), pool_scale[0], ehead)
    y2 = _rwkv_scan(lw, kt, bb, v, kk, r, t_ctx)
    routed = _even_merge(y2, bonus, g, yb, ctx, x, mod0, rwkv_ln_g[0], rwkv_ln_b[0], ev_w_out[0].astype(BF16), ehead,
                         norm_ffn[0], router_w(0))
    mod1 = layer_mod(1)
    X, P2 = _moe(routed, mod0, 0, moe_w_gate, moe_w_up, moe_w_down, all_lat=False,
                 project=(mod1, norm_mix[1], od_w_in[0].astype(BF16), _rope_tables(t_ctx, T)))

    lam_init = 0.8 - 0.6 * math.exp(-0.3 * 1)
    c_lat = _diff_attn(P2, t_ctx, diff_lambda[0], diff_subln[0], lam_init)
    o2 = _retention(P2, t_ctx)
    routed = _odd_merge(c_lat, o2, P2, X, mod1, ret_norm[0], od_w_out[0].astype(BF16), t_ctx, norm_ffn[1], router_w(1))
    return _moe(routed, mod1, 1, moe_w_gate, moe_w_up, moe_w_down, all_lat=True,
                final_gain=final_norm)
```

```python
import functools
import math

import jax
import jax.numpy as jnp
from jax import lax
from jax.experimental import pallas as pl
from jax.experimental.pallas import tpu as pltpu

F32 = jnp.float32
BF16 = jnp.bfloat16
HI = lax.Precision.HIGHEST

GRID_W = 64
NORM_EPS = 1e-6
ROPE_BASE = 10000.0
A_HEADS, A_HEAD = 8, 64
A_DIM = A_HEADS * A_HEAD
LORA_W, LORA_A, LORA_G = 64, 64, 128
W_DECAY_SCALE = 0.606531
RWKV_GN_EPS = 64e-5
POOL_WINDOWS = (2, 4, 8, 16)
POOL_GROUP = 128
B_DIM = POOL_GROUP * 4
C_HEADS, C_HEAD = 4, 64
C_VDIM = 2 * C_HEAD
C_DIM = C_HEADS * C_VDIM
DIFF_EPS = 1e-5
D_HEADS, D_KDIM, D_VDIM = 4, 64, 128
D_DIM = D_HEADS * D_VDIM
A_K_OFF, A_V_OFF = 0, A_DIM
W_LORA_OFF = 2 * A_DIM
A_LORA_OFF = W_LORA_OFF + LORA_W
A_R_OFF = A_LORA_OFF + LORA_A
G_LORA_OFF = A_R_OFF + A_DIM
POOL_OFF = G_LORA_OFF + LORA_G
EVEN_IN_COLS = POOL_OFF + B_DIM
C_K_OFF = 0
C_V_OFF = C_HEADS * 2 * C_HEAD
D_K_OFF = C_V_OFF + C_DIM
D_V_OFF = D_K_OFF + D_HEADS * D_KDIM
C_Q_OFF = D_V_OFF + D_DIM
D_Q_OFF = C_Q_OFF + C_HEADS * 2 * C_HEAD
D_G_OFF = D_Q_OFF + D_HEADS * D_KDIM
ODD_IN_COLS = D_G_OFF + D_DIM
MOE_GROUPS, MOE_PER_GROUP = 4, 8
MOE_EXPERTS = MOE_GROUPS * MOE_PER_GROUP
MOE_TOP_K = 2
EXPERT_BLOCK = 256

LANES = 128
SUBLANES = 8
HALO = 16
TB = 256
RWKV_CHUNK = 64
RET_CHUNK = 256
RET_NB = 2
SCAN_NB = 4
ATT_TQ = 256
ATT_SUB = 8
VMEM_LIMIT = 56 * 1024 * 1024
DMA_UNROLL = 8
GATHER_SLOTS = 3
TABLE_UNROLL = 32
GATHER_DMA_PRIORITY = 1


def _cparams(sem):
    return pltpu.CompilerParams(dimension_semantics=sem, vmem_limit_bytes=VMEM_LIMIT)


def _modnorm(x, gain, scale, shift, eps=NORM_EPS):
    ms = jnp.mean(x * x, axis=-1, keepdims=True)
    return x * lax.rsqrt(ms + eps) * gain * (1.0 + scale) + shift


def _bdot(a, b, dims=((1,), (0,))):
    return lax.dot_general(a.astype(BF16), b.astype(BF16), (dims, ((), ())), preferred_element_type=F32)


def _bf16_terms(x, terms):
    parts = []
    for _ in range(terms):
        p = x.astype(BF16)
        parts.append(p)
        x = x - p.astype(F32)
    return parts


def _dot_split_lhs(x, w_bf16, terms=2):
    return sum(jnp.dot(p, w_bf16, preferred_element_type=F32) for p in _bf16_terms(x, terms))


def _ada_kernel(c_ref, w_ref, b_ref, o_ref):
    c = c_ref[...]
    s = c * jax.nn.sigmoid(c)
    o_ref[0] = jnp.dot(s, w_ref[0], precision=HI, preferred_element_type=F32) + b_ref[0]


def _ada(cpad, ada_w, ada_b):
    L, D, N = ada_w.shape
    tn = 1536
    return pl.pallas_call(
        _ada_kernel,
        grid=(L, N // tn),
        in_specs=[pl.BlockSpec((16, D), lambda l, n: (0, 0)),
                  pl.BlockSpec((1, D, tn), lambda l, n: (l, 0, n)),
                  pl.BlockSpec((1, 1, tn), lambda l, n: (l, 0, n))],
        out_specs=pl.BlockSpec((1, 16, tn), lambda l, n: (l, 0, n)),
        out_shape=jax.ShapeDtypeStruct((L, 16, N), F32),
        compiler_params=_cparams(("arbitrary", "arbitrary")),
        name="ada_mod",
    )(cpad, ada_w, ada_b.reshape(L, 1, N))


def _mod_spec(D, all_lat=False):
    if all_lat:
        return pl.BlockSpec((1, 1, 6, D), lambda b, j: (b, 1, 0, 0))
    return pl.BlockSpec((1, 1, 6, D), lambda b, j: (b, jnp.minimum(j, 1), 0, 0))


def _rope_plan():
    ranges = [(C_K_OFF, C_V_OFF), (D_K_OFF, D_V_OFF), (C_Q_OFF, D_G_OFF)]
    rope_chunks = tuple(g for lo, hi in ranges for g in range(lo // LANES, hi // LANES))
    kscale_chunks = tuple(range(D_K_OFF // LANES, D_V_OFF // LANES))
    return rope_chunks, kscale_chunks


def _project_block(h, w_ref, o_ref, rope_refs=None, before_piece=None):
    n_out = w_ref.shape[1]
    rope_chunks, kscale_chunks = _rope_plan() if rope_refs is not None else ((), ())
    cw = 512
    for n0 in range(0, n_out, cw):
        if before_piece is not None:
            before_piece(n0 // cw, n_out // cw)
        p = jnp.dot(h, w_ref[:, n0:n0 + cw], preferred_element_type=F32)
        if rope_refs is None:
            o_ref[0, :, n0:n0 + cw] = p.astype(o_ref.dtype)
            continue
        cos_ref, s1_ref, s2_ref = rope_refs
        for q in range(cw // LANES):
            gi = n0 // LANES + q
            sub = p[:, q * LANES:(q + 1) * LANES]
            if gi in rope_chunks:
                sub = (sub * cos_ref[...] + pltpu.roll(sub, LANES - 16, 1) * s1_ref[...]
                       + pltpu.roll(sub, 16, 1) * s2_ref[...])
            if gi in kscale_chunks:
                sub = sub * (D_KDIM ** -0.5)
            o_ref[0, :, n0 + q * LANES:n0 + (q + 1) * LANES] = sub.astype(o_ref.dtype)


def _stream_specs(D):
    return [pl.BlockSpec((1, TB, D), lambda b, j: (b, 0, 0)),
            pl.BlockSpec((1, TB, D), lambda b, j: (b, jnp.maximum(j - 1, 0), 0))]


def _stream_block(ctx_ref, x_ref):
    return jnp.where(pl.program_id(1) == 0, ctx_ref[0], x_ref[0])


def _proj_kernel(ctx_ref, x_ref, mod_ref, g_ref, w_ref, o_ref):
    m = mod_ref[0, 0]
    h = _modnorm(_stream_block(ctx_ref, x_ref), g_ref[...], m[1:2], m[0:1]).astype(BF16)
    _project_block(h, w_ref, o_ref)


def _proj(ctx, x, mod, gain, w_bf16, out_dtype):
    B, T, D = x.shape
    S = ctx.shape[1] + T
    N = w_bf16.shape[1]
    return pl.pallas_call(
        _proj_kernel,
        grid=(B, S // TB),
        in_specs=_stream_specs(D) + [
                  _mod_spec(D),
                  pl.BlockSpec((1, D), lambda b, j: (0, 0)),
                  pl.BlockSpec((D, N), lambda b, j: (0, 0))],
        out_specs=pl.BlockSpec((1, TB, N), lambda b, j: (b, j, 0)),
        out_shape=jax.ShapeDtypeStruct((B, S, N), out_dtype),
        compiler_params=_cparams(("arbitrary", "arbitrary")),
        name="proj",
    )(ctx, x, mod, gain.reshape(1, D), w_bf16)


def _even_prep_kernel(p_ref, pp_ref, pn_ref, sh_ref, wa0_ref, lora_ref, gup_ref, kkg_ref, ka_ref, rk_ref,
                      pw_ref, ps_ref, eh_ref,
                      lw_ref, kt_ref, bb_ref, v_ref, kkn_ref, r_ref, bonus_ref, g_ref, yb_ref,
                      *, nblk, t_ctx, t_lat):
    j = pl.program_id(1)
    prev_ok = j >= 2
    next_ok = jnp.logical_and(j >= 1, j < nblk - 1)
    n_ext = TB + 2 * HALO

    def ext(c0, c1):
        pv = jnp.where(prev_ok, pp_ref[0, :, c0:c1].astype(F32), 0.0)
        nx = jnp.where(next_ok, pn_ref[0, :, c0:c1].astype(F32), 0.0)
        return jnp.concatenate([pv, p_ref[0, :, c0:c1].astype(F32), nx], axis=0)

    def shift3(c0, c1, k0):
        e = ext(c0, c1)
        kern = sh_ref[:, k0:k0 + (c1 - c0)]
        y = kern[0:1] * pltpu.roll(e, 1, 0) + kern[1:2] * e + kern[2:3] * pltpu.roll(e, n_ext - 1, 0)
        return y[HALO:HALO + TB]

    k = shift3(A_K_OFF, A_K_OFF + A_DIM, 0)
    v = shift3(A_V_OFF, A_V_OFF + A_DIM, A_DIM)
    r = shift3(A_R_OFF, A_R_OFF + A_DIM, 2 * A_DIM)

    c = p_ref[0, :, W_LORA_OFF:W_LORA_OFF + LANES].astype(F32)
    lane = lax.broadcasted_iota(jnp.int32, c.shape, 1)
    lin = jnp.where(lane < LORA_W, jnp.tanh(c), c)
    z = _bdot(lin, lora_ref[...]) + wa0_ref[...]

    eh = eh_ref[...]
    kk0 = k * kkg_ref[...]
    ss = _dot_split_lhs(kk0 * kk0, eh)
    kk = kk0 * lax.rsqrt(jnp.maximum(ss, 1e-12))
    kt_sum = None
    for d in range(2):
        zw = z[:, (2 * d) * A_DIM:(2 * d + 1) * A_DIM]
        za = z[:, (2 * d + 1) * A_DIM:(2 * d + 2) * A_DIM]
        a = jax.nn.sigmoid(za)
        kt = k * (1.0 + (a - 1.0) * ka_ref[...])
        lw_ref[d, 0] = -W_DECAY_SCALE * jax.nn.sigmoid(zw)
        kt_ref[d, 0] = kt.astype(kt_ref.dtype)
        bb_ref[d, 0] = (a * kk).astype(bb_ref.dtype)
        kt_sum = kt if kt_sum is None else kt_sum + kt
    v_ref[0] = v.astype(v_ref.dtype)
    kkn_ref[0] = kk.astype(kkn_ref.dtype)
    r_ref[0] = r.astype(r_ref.dtype)
    bonus_ref[0] = (_dot_split_lhs(r * kt_sum * rk_ref[...], eh) * v).astype(bonus_ref.dtype)
    gl = p_ref[0, :, G_LORA_OFF:G_LORA_OFF + LORA_G].astype(F32)
    g_ref[0] = _bdot(jax.nn.sigmoid(gl), gup_ref[...]).astype(g_ref.dtype)

    row = lax.broadcasted_iota(jnp.int32, (TB, POOL_GROUP), 0)
    tpos = row + jnp.maximum(j - 1, 0) * TB
    tseg = jnp.where(j == 0, t_ctx, t_lat)
    for gi, win in enumerate(POOL_WINDOWS):
        c0 = POOL_OFF + gi * POOL_GROUP
        e = ext(c0, c0 + POOL_GROUP)
        s = e + pltpu.roll(e, 1, 0)
        sh = 1
        while 2 * sh < win:
            s = pltpu.roll(s, sh, 0) + pltpu.roll(s, n_ext - sh, 0)
            sh *= 2
        s = s[HALO:HALO + TB]
        lo = jnp.clip(tpos - win // 2, 0, tseg)
        hi = jnp.clip(tpos - win // 2 + win, 0, tseg)
        mean = s / (hi - lo).astype(F32)
        dd = mean - p_ref[0, :, c0:c0 + POOL_GROUP].astype(F32)
        y = _bdot(dd, pw_ref[gi])
        yb_ref[0, :, gi * POOL_GROUP:(gi + 1) * POOL_GROUP] = (
            y * ps_ref[:, gi * POOL_GROUP:(gi + 1) * POOL_GROUP]).astype(yb_ref.dtype)


def _even_prep(P, t_ctx, shift_k, wa0, lora_w, g_up, k_k, k_a, r_k, pool_w, pool_scale, ehead):
    B, S, N = P.shape
    nblk = S // TB
    hb = TB // HALO
    const2 = lambda b, j: (0, 0)
    in_specs = [
        pl.BlockSpec((1, TB, N), lambda b, j: (b, j, 0)),
        pl.BlockSpec((1, HALO, N), lambda b, j: (b, jnp.maximum(j * hb - 1, 0), 0)),
        pl.BlockSpec((1, HALO, N), lambda b, j: (b, jnp.minimum((j + 1) * hb, S // HALO - 1), 0)),
        pl.BlockSpec((3, 3 * A_DIM), const2),
        pl.BlockSpec((1, 4 * A_DIM), const2),
        pl.BlockSpec((LANES, 4 * A_DIM), const2),
        pl.BlockSpec((LORA_G, A_DIM), const2),
        pl.BlockSpec((1, A_DIM), const2),
        pl.BlockSpec((1, A_DIM), const2),
        pl.BlockSpec((1, A_DIM), const2),
        pl.BlockSpec((4, POOL_GROUP, POOL_GROUP), lambda b, j: (0, 0, 0)),
        pl.BlockSpec((1, B_DIM), const2),
        pl.BlockSpec((A_DIM, A_DIM), const2),
    ]
    two_f32 = jax.ShapeDtypeStruct((2, B, S, A_DIM), F32)
    two = jax.ShapeDtypeStruct((2, B, S, A_DIM), BF16)
    one = jax.ShapeDtypeStruct((B, S, A_DIM), BF16)
    two_spec = pl.BlockSpec((2, 1, TB, A_DIM), lambda b, j: (0, b, j, 0))
    one_spec = pl.BlockSpec((1, TB, A_DIM), lambda b, j: (b, j, 0))
    return pl.pallas_call(
        functools.partial(_even_prep_kernel, nblk=nblk, t_ctx=t_ctx, t_lat=S - t_ctx),
        grid=(B, nblk),
        in_specs=in_specs,
        out_specs=[two_spec] * 3 + [one_spec] * 6,
        out_shape=[two_f32, two, two] + [one] * 6,
        compiler_params=_cparams(("arbitrary", "arbitrary")),
        name="even_prep",
    )(P, P, P, shift_k, wa0, lora_w, g_up, k_k.reshape(1, -1), k_a.reshape(1, -1), r_k.reshape(1, -1),
      pool_w, pool_scale.reshape(1, -1), ehead)


def _rwkv_scan_kernel(lw_ref, kt_ref, bb_ref, v_ref, kk_ref, r_ref, y_ref, s_ref):
    d = pl.program_id(1)
    c = pl.program_id(2)

    @pl.when(c == 0)
    def _():
        s_ref[...] = jnp.zeros_like(s_ref)

    C = RWKV_CHUNK
    tt = lax.broadcasted_iota(jnp.int32, (C, C), 0)
    ii = lax.broadcasted_iota(jnp.int32, (C, C), 1)
    rel = (tt - ii) * (1 - 2 * d)
    strict = rel > 0
    incl = rel >= 0
    t2 = lax.broadcasted_iota(jnp.int32, (C, 2 * C), 0)
    i2 = lax.broadcasted_iota(jnp.int32, (C, 2 * C), 1) & (C - 1)
    incl2 = (t2 - i2) * (1 - 2 * d) >= 0
    tri = jnp.where(incl, 1.0, 0.0).astype(BF16)
    n_dbl = int(math.log2(C))
    hsl = [slice(h * A_HEAD, (h + 1) * A_HEAD) for h in range(A_HEADS)]
    ar, bk, vh, bkp, ptot = [], [], [], [], []
    for bi in range(SCAN_NB):
        lw = lw_ref[0, bi]
        cum = sum(jnp.dot(tri, p, preferred_element_type=F32) for p in _bf16_terms(lw, 3))
        tot = jnp.sum(lw, axis=0, keepdims=True)
        p_inv = jnp.exp(-cum)
        p_rest = jnp.exp(tot - cum)
        p_tot = jnp.exp(tot)
        bb = bb_ref[0, bi]
        kt = kt_ref[0, bi]
        at_all = (-kk_ref[bi] * jnp.exp(cum - lw)).astype(BF16)
        rt_all = (r_ref[bi] * jnp.exp(cum)).astype(BF16)
        bt_all = (bb * p_inv).astype(BF16)
        kti_all = (kt * p_inv).astype(BF16)
        bp_all = (bb * p_rest).astype(BF16)
        kp_all = (kt * p_rest).astype(BF16)
        v_all = v_ref[bi].astype(BF16)
        for s in hsl:
            ar.append(jnp.concatenate([at_all[:, s], rt_all[:, s]], axis=0))
            bk.append(jnp.concatenate([bt_all[:, s], kti_all[:, s]], axis=0))
            bkp.append(jnp.concatenate([bp_all[:, s], kp_all[:, s]], axis=0))
            vh.append(v_all[:, s])
            ptot.append(p_tot[:, s])
    chains = range(SCAN_NB * A_HEADS)
    s0 = [s_ref[n] for n in chains]
    m = [_bdot(ar[n], bk[n], ((1,), (1,))) for n in chains]
    arh = [_bdot(ar[n], s0[n], ((1,), (1,))) for n in chains]
    lp = [jnp.where(strict, m[n][:C, :C], 0.0) for n in chains]
    a_ak = [jnp.where(strict, m[n][:C, C:], 0.0) for n in chains]
    a_r = [jnp.where(incl2, m[n][C:, :], 0.0) for n in chains]
    u = [arh[n][:C] + _bdot(a_ak[n], vh[n]) for n in chains]
    for q in range(n_dbl):
        u = [u[n] + _bdot(lp[n], u[n]) for n in chains]
        if q < n_dbl - 1:
            lp = [_bdot(lp[n], lp[n]) for n in chains]
    uv = [jnp.concatenate([u[n].astype(BF16), vh[n]], axis=0) for n in chains]
    outs = [arh[n][C:] + _bdot(a_r[n], uv[n]) for n in chains]
    s_new = [s0[n] * ptot[n] + _bdot(uv[n], bkp[n], ((0,), (0,))) for n in chains]
    for n in chains:
        s_ref[n] = s_new[n]
    for bi in range(SCAN_NB):
        y_ref[0, bi] = jnp.concatenate(outs[bi * A_HEADS:(bi + 1) * A_HEADS], axis=1).astype(y_ref.dtype)


def _rwkv_scan(lw, kt, bb, v, kk, r, t_ctx):
    _, B, S, _ = lw.shape
    C = RWKV_CHUNK
    nc = S // C
    nctx = t_ctx // C

    def cidx(d, c):
        back = jnp.where(c < nctx, nctx - 1 - c, nc - 1 + nctx - c)
        return jnp.where(d == 0, c, back)

    two_spec = pl.BlockSpec((1, SCAN_NB, C, A_DIM), lambda b, d, c: (d, b, cidx(d, c), 0))
    one_spec = pl.BlockSpec((SCAN_NB, C, A_DIM), lambda b, d, c: (b, cidx(d, c), 0))
    return pl.pallas_call(
        _rwkv_scan_kernel,
        grid=(B // SCAN_NB, 2, nc),
        in_specs=[two_spec] * 3 + [one_spec] * 3,
        out_specs=two_spec,
        out_shape=jax.ShapeDtypeStruct((2, B, S, A_DIM), BF16),
        scratch_shapes=[pltpu.VMEM((SCAN_NB * A_HEADS, A_HEAD, A_HEAD), F32)],
        compiler_params=_cparams(("arbitrary", "arbitrary", "arbitrary")),
        name="rwkv_scan",
    )(lw, kt, bb, v, kk, r)


def _even_merge_kernel(y2_ref, bonus_ref, g_ref, yb_ref, ctx_ref, x_ref, mod_ref, lng_ref, lnb_ref, wo_ref, eh_ref,
                       gffn_ref, wr_ref, o_ref, *route_refs):
    y = y2_ref[0, 0].astype(F32) + y2_ref[1, 0].astype(F32)
    eh = eh_ref[...]
    mu = _dot_split_lhs(y, eh) * (1.0 / A_HEAD)
    yc = y - mu
    var = _dot_split_lhs(yc * yc, eh) * (1.0 / A_HEAD)
    yn = yc * lax.rsqrt(var + RWKV_GN_EPS) * lng_ref[...] + lnb_ref[...]
    ya = (yn + bonus_ref[0]) * g_ref[0]
    out = _bdot(ya, wo_ref[0:A_DIM, :]) + _bdot(yb_ref[0], wo_ref[A_DIM:A_DIM + B_DIM, :])
    m = mod_ref[0, 0]
    xn = _stream_block(ctx_ref, x_ref) + m[2:3] * out
    o_ref[0] = xn
    _route_block(xn, m, gffn_ref, wr_ref, *route_refs)


def _even_merge(y2, bonus, g, yb, ctx, x, mod, ln_g, ln_b, wo_bf16, ehead, ffn_gain, w_router):
    B, T, D = x.shape
    S = ctx.shape[1] + T
    r_in, r_out, r_shapes, r_scratch = _route_specs(B, S, D)
    one_spec = pl.BlockSpec((1, TB, A_DIM), lambda b, j: (b, j, 0))
    const2 = lambda b, j: (0, 0)
    return pl.pallas_call(
        _even_merge_kernel,
        grid=(B, S // TB),
        in_specs=[pl.BlockSpec((2, 1, TB, A_DIM), lambda b, j: (0, b, j, 0)), one_spec, one_spec, one_spec]
        + _stream_specs(D) + [_mod_spec(D),
                  pl.BlockSpec((1, A_DIM), const2), pl.BlockSpec((1, A_DIM), const2),
                  pl.BlockSpec((A_DIM + B_DIM, D), const2), pl.BlockSpec((A_DIM, A_DIM), const2)] + r_in,
        out_specs=[pl.BlockSpec((1, TB, D), lambda b, j: (b, j, 0))] + r_out,
        out_shape=[jax.ShapeDtypeStruct((B, S, D), F32)] + r_shapes,
        scratch_shapes=r_scratch,
        compiler_params=_cparams(("arbitrary", "arbitrary")),
        name="even_merge_route",
    )(y2, bonus, g, yb, ctx, x, mod, ln_g.reshape(1, -1), ln_b.reshape(1, -1), wo_bf16, ehead,
      ffn_gain.reshape(1, D), w_router)


def _route_block(xn, m, g_ref, wr_ref, hf_ref, meta_ref, metat_ref, cnt_ref, carry_ref):
    first = jnp.logical_and(pl.program_id(0) == 0, pl.program_id(1) == 0)

    @pl.when(first)
    def _():
        carry_ref[...] = jnp.zeros_like(carry_ref)

    hf = _modnorm(xn, g_ref[...], m[4:5], m[3:4])
    hf_ref[0] = hf
    h0, h1 = _bf16_terms(hf, 2)
    w0, w1 = _bf16_terms(wr_ref[...], 2)
    logits = (jnp.dot(h0, w0, preferred_element_type=F32) + jnp.dot(h0, w1, preferred_element_type=F32)
              + jnp.dot(h1, w0, preferred_element_type=F32))
    lane = lax.broadcasted_iota(jnp.int32, logits.shape, 1)
    neg = -jnp.inf
    lg = jnp.where(lane < MOE_GROUPS, logits, neg)
    mg = jnp.max(lg, axis=1, keepdims=True)
    grp = jnp.min(jnp.where(lg == mg, lane, LANES), axis=1, keepdims=True)
    gw = 1.0 / jnp.sum(jnp.exp(lg - mg), axis=1, keepdims=True)
    el = lane - MOE_GROUPS
    in_grp = jnp.logical_and(el >= grp * MOE_PER_GROUP, el < (grp + 1) * MOE_PER_GROUP)
    le = jnp.where(in_grp, logits, neg)
    m0 = jnp.max(le, axis=1, keepdims=True)
    i0 = jnp.min(jnp.where(le == m0, lane, LANES), axis=1, keepdims=True)
    le1 = jnp.where(lane == i0, neg, le)
    m1 = jnp.max(le1, axis=1, keepdims=True)
    i1 = jnp.min(jnp.where(le1 == m1, lane, LANES), axis=1, keepdims=True)
    p1 = jnp.exp(m1 - m0)
    gate0 = gw / (1.0 + p1)
    gate1 = gw * p1 / (1.0 + p1)
    e0 = i0 - MOE_GROUPS
    e1 = i1 - MOE_GROUPS
    oh0 = (lane == e0).astype(F32)
    oh1 = (lane == e1).astype(F32)
    cnt = oh0 + oh1
    rr = lax.broadcasted_iota(jnp.int32, (TB, TB), 0)
    cc = lax.broadcasted_iota(jnp.int32, (TB, TB), 1)
    before = _bdot((cc < rr).astype(F32), cnt) + carry_ref[...]
    rank0 = jnp.sum(oh0 * before, axis=1, keepdims=True)
    rank1 = jnp.sum(oh1 * before, axis=1, keepdims=True)
    carry_ref[...] = carry_ref[...] + jnp.sum(cnt, axis=0, keepdims=True)
    cnt_ref[...] = carry_ref[...]
    meta = jnp.where(lane == 0, e0.astype(F32), 0.0)
    meta = jnp.where(lane == 1, e1.astype(F32), meta)
    meta = jnp.where(lane == 2, rank0, meta)
    meta = jnp.where(lane == 3, rank1, meta)
    meta = jnp.where(lane == 4, gate0, meta)
    meta = jnp.where(lane == 5, gate1, meta)
    meta_ref[0] = meta
    metat_ref[0, 0] = meta.T[0:SUBLANES]


def _route_specs(B, S, D):
    in_specs = [pl.BlockSpec((1, D), lambda b, j: (0, 0)), pl.BlockSpec((D, LANES), lambda b, j: (0, 0))]
    out_specs = [pl.BlockSpec((1, TB, D), lambda b, j: (b, j, 0)),
                 pl.BlockSpec((1, TB, LANES), lambda b, j: (b, j, 0)),
                 pl.BlockSpec((1, 1, SUBLANES, TB), lambda b, j: (b, j, 0, 0)),
                 pl.BlockSpec((1, LANES), lambda b, j: (0, 0))]
    out_shapes = [jax.ShapeDtypeStruct((B, S, D), F32), jax.ShapeDtypeStruct((B, S, LANES), F32),
                  jax.ShapeDtypeStruct((B, S // TB, SUBLANES, TB), F32), jax.ShapeDtypeStruct((1, LANES), F32)]
    return in_specs, out_specs, out_shapes, [pltpu.VMEM((1, LANES), F32)]


def _routing_tables(metat, counts):
    B, nj = metat.shape[:2]
    n_tok = B * nj * TB
    mi = jnp.transpose(metat[:, :, 0:4, :], (2, 0, 1, 3)).reshape(4, n_tok).astype(jnp.int32)
    cnt = counts[0, :MOE_EXPERTS].astype(jnp.int32)
    padded = (cnt + EXPERT_BLOCK - 1) // EXPERT_BLOCK * EXPERT_BLOCK
    pad_end = jnp.cumsum(padded)
    pad_start = pad_end - padded
    dest = jnp.concatenate([pad_start[mi[0]] + mi[2], pad_start[mi[1]] + mi[3]])
    n_blocks = -(-(n_tok * MOE_TOP_K) // EXPERT_BLOCK) + MOE_EXPERTS
    starts = jnp.arange(n_blocks, dtype=jnp.int32) * EXPERT_BLOCK
    block_e = jnp.minimum(jnp.sum((pad_end[None, :] <= starts[:, None]).astype(jnp.int32), axis=1), MOE_EXPERTS - 1)
    n_used = (pad_end[-1] // EXPERT_BLOCK).astype(jnp.int32).reshape(1)
    pads = jnp.concatenate([pad_start + cnt, pad_end])
    return dest, block_e, n_used, pads, n_blocks


def _expert_kernel(be_ref, nu_ref, dest_ref, pads_ref, hf_ref, wg_ref, wu_ref, wd_ref, o_ref, xbuf, wg_bf, wu_bf, wd_bf,
                   rows_ref, sem, *, n_tok):
    j = pl.program_id(0)
    n_used = nu_ref[0]
    EB = EXPERT_BLOCK

    def row_copy(blk, slot, r):
        row = rows_ref[blk * EB + r]
        return pltpu.make_async_copy(hf_ref.at[pl.ds(row, 1), :], xbuf.at[slot, pl.ds(r, 1), :], sem.at[slot])

    def issue(blk, slot):
        def body(r, carry):
            row_copy(blk, slot, r).start()
            return carry
        lax.fori_loop(0, EB, body, 0, unroll=DMA_UNROLL)

    @pl.when(j == 0)
    def _():
        fill_mask = (1 << (n_tok.bit_length() - 1)) - 1

        def fill(i, carry):
            rows_ref[i] = jnp.bitwise_and(i, fill_mask)
            return carry
        for e in range(MOE_EXPERTS):
            lax.fori_loop(pads_ref[e], pads_ref[MOE_EXPERTS + e], fill, 0)

        def invert(t, carry):
            rows_ref[dest_ref[t]] = t
            rows_ref[dest_ref[n_tok + t]] = t
            return carry
        lax.fori_loop(0, n_tok, invert, 0, unroll=TABLE_UNROLL)
        issue(0, 0)

        @pl.when(n_used > 1)
        def _():
            issue(1, 1)

    def compute(prefetch_ahead):
        slot = lax.rem(j, GATHER_SLOTS)

        def body(r, carry):
            row_copy(j, slot, r).wait()
            return carry
        lax.fori_loop(0, EB, body, 0, unroll=DMA_UNROLL)

        @pl.when(jnp.logical_or(j == 0, be_ref[j] != be_ref[jnp.maximum(j - 1, 0)]))
        def _():
            wg_bf[...] = wg_ref[0, 0].astype(BF16)
            wu_bf[...] = wu_ref[0, 0].astype(BF16)
            wd_bf[...] = wd_ref[0, 0].astype(BF16)

        xb = xbuf[slot].astype(BF16)
        if prefetch_ahead:
            ahead = lax.rem(j + 2, GATHER_SLOTS)
            for r in range(EB):
                row_copy(j + 2, ahead, r).start(priority=GATHER_DMA_PRIORITY)
        hg = jnp.dot(xb, wg_bf[...], preferred_element_type=F32)
        hu = jnp.dot(xb, wu_bf[...], preferred_element_type=F32)
        hm = (hg * jax.nn.sigmoid(hg) * hu).astype(BF16)
        o_ref[...] = jnp.dot(hm, wd_bf[...], preferred_element_type=F32)

    @pl.when(j + 2 < n_used)
    def _():
        compute(True)

    @pl.when(jnp.logical_and(j < n_used, j + 2 >= n_used))
    def _():
        compute(False)

    @pl.when(j >= n_used)
    def _():
        o_ref[...] = jnp.zeros_like(o_ref)


def _experts(hf2d, dest, block_e, n_used, pads, n_blocks, layer, w_gate, w_up, w_down):
    N, D = hf2d.shape
    _, E, _, De = w_gate.shape
    EB = EXPERT_BLOCK
    grid_spec = pltpu.PrefetchScalarGridSpec(
        num_scalar_prefetch=4,
        grid=(n_blocks,),
        in_specs=[pl.BlockSpec(memory_space=pl.ANY),
                  pl.BlockSpec((1, 1, D, De), lambda j, be, nu, ds, pd: (layer, be[j], 0, 0)),
                  pl.BlockSpec((1, 1, D, De), lambda j, be, nu, ds, pd: (layer, be[j], 0, 0)),
                  pl.BlockSpec((1, 1, De, D), lambda j, be, nu, ds, pd: (layer, be[j], 0, 0))],
        out_specs=pl.BlockSpec((EB, D), lambda j, be, nu, ds, pd: (j, 0)),
        scratch_shapes=[pltpu.VMEM((GATHER_SLOTS, EB, D), F32), pltpu.VMEM((D, De), BF16), pltpu.VMEM((D, De), BF16),
                        pltpu.VMEM((De, D), BF16), pltpu.SMEM((n_blocks * EB,), jnp.int32),
                        pltpu.SemaphoreType.DMA((GATHER_SLOTS,))],
    )
    return pl.pallas_call(
        functools.partial(_expert_kernel, n_tok=N),
        grid_spec=grid_spec,
        out_shape=jax.ShapeDtypeStruct((n_blocks * EB, D), F32),
        compiler_params=_cparams(("arbitrary",)),
        name="moe_experts",
    )(block_e, n_used, dest, pads, hf2d, w_gate, w_up, w_down)


def _combine_kernel(dest_ref, yb_ref, x_ref, meta_ref, mod_ref, *rest, nj, project):
    if project:
        mod1_ref, g1_ref, w_ref, cos_ref, s1_ref, s2_ref, o_ref, p_ref, ybuf, sem = rest
    else:
        fn_ref, o_ref, ybuf, sem = rest
    b = pl.program_id(0)
    j = pl.program_id(1)
    step = b * nj + j
    n_steps = pl.num_programs(0) * nj
    n_tok = n_steps * TB

    def row_copy(st, slot, r, k):
        dst_row = dest_ref[k * n_tok + st * TB + r]
        return pltpu.make_async_copy(yb_ref.at[pl.ds(dst_row, 1), :], ybuf.at[slot, k, pl.ds(r, 1), :], sem.at[slot])

    def issue(st, slot):
        def body(r, carry):
            row_copy(st, slot, r, 0).start()
            row_copy(st, slot, r, 1).start()
            return carry
        lax.fori_loop(0, TB, body, 0, unroll=DMA_UNROLL)

    @pl.when(step == 0)
    def _():
        issue(0, 0)
        issue(1, 1)

    slot = lax.rem(step, GATHER_SLOTS)

    def body(prefetch_ahead):
        def wbody(r, carry):
            row_copy(step, slot, r, 0).wait()
            row_copy(step, slot, r, 1).wait()
            return carry
        lax.fori_loop(0, TB, wbody, 0, unroll=DMA_UNROLL)
        meta = meta_ref[0]
        f = meta[:, 4:5] * ybuf[slot, 0] + meta[:, 5:6] * ybuf[slot, 1]
        m = mod_ref[0, 0]
        xn = x_ref[0] + m[5:6] * f
        ahead = lax.rem(step + 2, GATHER_SLOTS)

        def issue_rows(i, n):
            if prefetch_ahead:
                for r in range(i * TB // n, (i + 1) * TB // n):
                    row_copy(step + 2, ahead, r, 0).start()
                    row_copy(step + 2, ahead, r, 1).start()

        if project:
            o_ref[0] = xn
            m1 = mod1_ref[0, 0]
            h = _modnorm(xn, g1_ref[...], m1[1:2], m1[0:1]).astype(BF16)
            _project_block(h, w_ref, p_ref, (cos_ref, s1_ref, s2_ref), before_piece=issue_rows)
        else:
            issue_rows(0, 1)
            ms = jnp.mean(xn * xn, axis=-1, keepdims=True)
            o_ref[0] = xn * lax.rsqrt(ms + NORM_EPS) * fn_ref[...]

    @pl.when(step + 2 < n_steps)
    def _():
        body(True)

    @pl.when(step + 2 >= n_steps)
    def _():
        body(False)


def _combine(dest, yb, X, meta, mod, all_lat, final_gain=None, project=None):
    B, S, D = X.shape
    nj = S // TB

    def mspec(lat_only):
        if lat_only:
            return pl.BlockSpec((1, 1, 6, D), lambda b, j, ds: (b, 1, 0, 0))
        return pl.BlockSpec((1, 1, 6, D), lambda b, j, ds: (b, jnp.minimum(j, 1), 0, 0))

    row_spec = pl.BlockSpec((1, TB, D), lambda b, j, ds: (b, j, 0))
    in_specs = [pl.BlockSpec(memory_space=pl.ANY), row_spec,
                pl.BlockSpec((1, TB, LANES), lambda b, j, ds: (b, j, 0)), mspec(all_lat)]
    args = [dest, yb, X, meta, mod]
    if project is None:
        in_specs.append(pl.BlockSpec((1, D), lambda b, j, ds: (0, 0)))
        args.append(final_gain.reshape(1, D))
        out_specs = row_spec
        out_shape = jax.ShapeDtypeStruct((B, S, D), F32)
    else:
        mod1, gain1, w_bf16, (cos, s1, s2) = project
        N = w_bf16.shape[1]
        in_specs += [mspec(False), pl.BlockSpec((1, D), lambda b, j, ds: (0, 0)),
                     pl.BlockSpec((D, N), lambda b, j, ds: (0, 0))]
        in_specs += [pl.BlockSpec((TB, LANES), lambda b, j, ds: (j, 0))] * 3
        args += [mod1, gain1.reshape(1, D), w_bf16, cos, s1, s2]
        out_specs = [row_spec, pl.BlockSpec((1, TB, N), lambda b, j, ds: (b, j, 0))]
        out_shape = [jax.ShapeDtypeStruct((B, S, D), F32), jax.ShapeDtypeStruct((B, S, N), BF16)]
    grid_spec = pltpu.PrefetchScalarGridSpec(
        num_scalar_prefetch=1,
        grid=(B, nj),
        in_specs=in_specs,
        out_specs=out_specs,
        scratch_shapes=[pltpu.VMEM((GATHER_SLOTS, 2, TB, D), F32), pltpu.SemaphoreType.DMA((GATHER_SLOTS,))],
    )
    return pl.pallas_call(
        functools.partial(_combine_kernel, nj=nj, project=project is not None),
        grid_spec=grid_spec,
        out_shape=out_shape,
        compiler_params=_cparams(("arbitrary", "arbitrary")),
        name="moe_combine_final" if project is None else "moe_combine_proj",
    )(*args)


def _moe(routed, mod, layer, w_gate, w_up, w_down, all_lat, final_gain=None, project=None):
    X, hf, meta, metat, counts = routed
    B, S, D = X.shape
    dest, block_e, n_used, pads, n_blocks = _routing_tables(metat, counts)
    yb = _experts(hf.reshape(B * S, D), dest, block_e, n_used, pads, n_blocks, layer, w_gate, w_up, w_down)
    return _combine(dest, yb, X, meta, mod, all_lat, final_gain, project)


def _diff_attn_kernel(*refs, lam_init):
    q_refs = refs[:ATT_SUB]
    k_ref, v_ref, dl_ref, sg_ref, o_ref, vx_ref = refs[ATT_SUB:]

    @pl.when(pl.program_id(2) == 0)
    def _():
        vlane = lax.broadcasted_iota(jnp.int32, (vx_ref.shape[0], LANES), 1)
        vx_ref[:, 0:C_VDIM] = v_ref[0]
        vx_ref[:, C_VDIM:C_VDIM + LANES] = jnp.where(vlane == 0, 1.0, 0.0).astype(BF16)

    k = k_ref[0]
    dl = dl_ref[...]
    lam = (jnp.exp(jnp.sum(dl[0:1] * dl[1:2], axis=1, keepdims=True))
           - jnp.exp(jnp.sum(dl[2:3] * dl[3:4], axis=1, keepdims=True)) + lam_init)

    def scores(t):
        q = q_refs[t][0] * (C_HEAD ** -0.5)
        lane = lax.broadcasted_iota(jnp.int32, q.shape, 1)
        zero = jnp.zeros_like(q)
        qs = (jnp.where(lane < C_HEAD, q, zero), jnp.where(lane >= C_HEAD, q, zero))
        return [lax.dot_general(qq, k, (((1,), (1,)), ((), ())), preferred_element_type=F32) for qq in qs]

    def finish(t, s):
        e = [jnp.exp(si - jnp.max(si, axis=1, keepdims=True)).astype(BF16) for si in s]
        oz = [jnp.dot(ei, vx_ref[...], preferred_element_type=F32) for ei in e]
        on = [ozi[:, 0:C_VDIM] * (1.0 / ozi[:, C_VDIM:C_VDIM + 1]) for ozi in oz]
        o = on[0] - lam * on[1]
        ms = jnp.mean(o * o, axis=-1, keepdims=True)
        o_ref[0, t * ATT_TQ:(t + 1) * ATT_TQ, :] = (
            o * lax.rsqrt(ms + DIFF_EPS) * sg_ref[...] * (1.0 - lam_init)).astype(o_ref.dtype)

    s_next = scores(0)
    for t in range(ATT_SUB):
        s_cur = s_next
        if t + 1 < ATT_SUB:
            s_next = scores(t + 1)
        finish(t, s_cur)


def _diff_attn(P2, t_ctx, diff_lambda, diff_subln, lam_init):
    B, S, _ = P2.shape
    T = S - t_ctx
    qoff = t_ctx // ATT_TQ
    step = ATT_SUB * ATT_TQ

    def q_spec(t):
        return pl.BlockSpec((1, ATT_TQ, LANES), lambda b, h, q: (b, q * ATT_SUB + t + qoff, C_Q_OFF // LANES + h))

    return pl.pallas_call(
        functools.partial(_diff_attn_kernel, lam_init=lam_init),
        grid=(B, C_HEADS, T // step),
        in_specs=[q_spec(t) for t in range(ATT_SUB)] + [
                  pl.BlockSpec((1, S, LANES), lambda b, h, q: (b, 0, C_K_OFF // LANES + h)),
                  pl.BlockSpec((1, S, LANES), lambda b, h, q: (b, 0, C_V_OFF // LANES + h)),
                  pl.BlockSpec((4, C_HEAD), lambda b, h, q: (0, 0)),
                  pl.BlockSpec((1, C_VDIM), lambda b, h, q: (0, 0))],
        out_specs=pl.BlockSpec((1, step, LANES), lambda b, h, q: (b, q, h)),
        out_shape=jax.ShapeDtypeStruct((B, T, C_DIM), BF16),
        scratch_shapes=[pltpu.VMEM((S, C_VDIM + LANES), BF16)],
        compiler_params=_cparams(("arbitrary", "arbitrary", "arbitrary")),
        name="diff_attn",
    )(*([P2] * ATT_SUB), P2, P2, diff_lambda, diff_subln.reshape(1, -1))


def _retention_kernel(q_ref, k_ref, v0_ref, v1_ref, o_ref, r_ref):
    d = pl.program_id(1)
    c = pl.program_id(2)

    @pl.when(c == 0)
    def _():
        r_ref[...] = jnp.zeros_like(r_ref)

    C = RET_CHUNK
    ti = lax.broadcasted_iota(jnp.int32, (C, C), 0)
    si = lax.broadcasted_iota(jnp.int32, (C, C), 1)
    rel = ((ti - si) * (1 - 2 * d)).astype(F32)
    col = lax.broadcasted_iota(jnp.int32, (C, 1), 0)
    pos = jnp.where(d == 0, col, C - 1 - col).astype(F32)
    chains = [(bi, h) for bi in range(RET_NB) for h in range(D_HEADS)]
    r0s = [r_ref[bi * D_HEADS + h] for bi, h in chains]
    decs = []
    for h in range(D_HEADS):
        lg_f = math.log(1.0 - 2.0 ** (-5.0 - h))
        lg_b = math.log(1.0 - 2.0 ** (-5.0 - (D_HEADS - 1 - h)))
        lg = jnp.where(d == 0, jnp.full((1, 1), lg_f, F32), jnp.full((1, 1), lg_b, F32))
        decs.append((jnp.where(rel >= 0, jnp.exp(lg * jnp.maximum(rel, 0.0)), 0.0),
                     jnp.exp(lg * (pos + 1.0)), jnp.exp(lg * (C - 1.0 - pos)), jnp.exp(lg * C)))
    outs, r_new = [], []
    for n, (bi, h) in enumerate(chains):
        inner_dec, q_dec, k_dec, c_dec = decs[h]
        q = q_ref[bi, :, h * D_KDIM:(h + 1) * D_KDIM]
        k = k_ref[bi, :, h * D_KDIM:(h + 1) * D_KDIM]
        vref = v0_ref if h < 2 else v1_ref
        v = vref[bi, :, (h % 2) * D_VDIM:(h % 2 + 1) * D_VDIM]
        inner = lax.dot_general(q, k, (((1,), (1,)), ((), ())), preferred_element_type=F32) * inner_dec
        outs.append(_bdot(inner, v) + _bdot(q, r0s[n]) * q_dec)
        r_new.append(r0s[n] * c_dec + _bdot(k.astype(F32) * k_dec, v, ((0,), (0,))))
    for n in range(len(chains)):
        r_ref[n] = r_new[n]
    for bi in range(RET_NB):
        o_ref[0, bi] = jnp.concatenate(outs[bi * D_HEADS:(bi + 1) * D_HEADS], axis=1).astype(o_ref.dtype)


def _retention(P2, t_ctx):
    B, S, _ = P2.shape
    C = RET_CHUNK
    nc = S // C
    nctx = t_ctx // C

    def cidx(d, c):
        back = jnp.where(c < nctx, nctx - 1 - c, nc - 1 + nctx - c)
        return jnp.where(d == 0, c, back)

    w = D_HEADS * D_KDIM
    return pl.pallas_call(
        _retention_kernel,
        grid=(B // RET_NB, 2, nc),
        in_specs=[pl.BlockSpec((RET_NB, C, w), lambda b, d, c: (b, cidx(d, c), D_Q_OFF // w)),
                  pl.BlockSpec((RET_NB, C, w), lambda b, d, c: (b, cidx(d, c), D_K_OFF // w)),
                  pl.BlockSpec((RET_NB, C, w), lambda b, d, c: (b, cidx(d, c), D_V_OFF // w)),
                  pl.BlockSpec((RET_NB, C, w), lambda b, d, c: (b, cidx(d, c), D_V_OFF // w + 1))],
        out_specs=pl.BlockSpec((1, RET_NB, C, D_DIM), lambda b, d, c: (d, b, cidx(d, c), 0)),
        out_shape=jax.ShapeDtypeStruct((2, B, S, D_DIM), BF16),
        scratch_shapes=[pltpu.VMEM((RET_NB * D_HEADS, D_KDIM, D_VDIM), F32)],
        compiler_params=_cparams(("arbitrary", "arbitrary", "arbitrary")),
        name="retention",
    )(P2, P2, P2, P2)


def _odd_merge_kernel(c_ref, o2_ref, gate_ref, x_ref, mod_ref, rn_ref, wo_ref, gffn_ref, wr_ref, o_ref, *route_refs):
    o = o2_ref[0, 0].astype(F32) + o2_ref[1, 0].astype(F32)
    parts = []
    for h in range(D_HEADS):
        oh = o[:, h * D_VDIM:(h + 1) * D_VDIM]
        ms = jnp.mean(oh * oh, axis=-1, keepdims=True)
        parts.append(oh * lax.rsqrt(ms + NORM_EPS))
    gt = gate_ref[0].astype(F32)
    dl = jnp.concatenate(parts, axis=1) * rn_ref[...] * (gt * jax.nn.sigmoid(gt))
    out = (jnp.dot(c_ref[0], wo_ref[0:C_DIM, :], preferred_element_type=F32)
           + _bdot(dl, wo_ref[C_DIM:C_DIM + D_DIM, :]))
    m = mod_ref[0, 0]
    xn = x_ref[0] + m[2:3] * out
    o_ref[0] = xn
    _route_block(xn, m, gffn_ref, wr_ref, *route_refs)


def _odd_merge(c_lat, o2, P2, X, mod, ret_norm, wo_bf16, t_ctx, ffn_gain, w_router):
    B, S, D = X.shape
    T = S - t_ctx
    off = t_ctx // TB
    r_in, r_out, r_shapes, r_scratch = _route_specs(B, T, D)
    return pl.pallas_call(
        _odd_merge_kernel,
        grid=(B, T // TB),
        in_specs=[pl.BlockSpec((1, TB, C_DIM), lambda b, j: (b, j, 0)),
                  pl.BlockSpec((2, 1, TB, D_DIM), lambda b, j: (0, b, j + off, 0)),
                  pl.BlockSpec((1, TB, D_DIM), lambda b, j: (b, j + off, D_G_OFF // D_DIM)),
                  pl.BlockSpec((1, TB, D), lambda b, j: (b, j + off, 0)),
                  _mod_spec(D, all_lat=True),
                  pl.BlockSpec((1, D_DIM), lambda b, j: (0, 0)),
                  pl.BlockSpec((C_DIM + D_DIM, D), lambda b, j: (0, 0))] + r_in,
        out_specs=[pl.BlockSpec((1, TB, D), lambda b, j: (b, j, 0))] + r_out,
        out_shape=[jax.ShapeDtypeStruct((B, T, D), F32)] + r_shapes,
        scratch_shapes=r_scratch,
        compiler_params=_cparams(("arbitrary", "arbitrary")),
        name="odd_merge_route",
    )(c_lat, o2, P2, X, mod, ret_norm.reshape(1, -1), wo_bf16, ffn_gain.reshape(1, D), w_router)


def _rope_tables(t_ctx, T):
    half = C_HEAD // 4
    inv = ROPE_BASE ** (-jnp.arange(half, dtype=F32) / half)
    t = jnp.arange(T, dtype=jnp.int32)
    ar = (t // GRID_W).astype(F32)[:, None] * inv[None, :]
    ac = (t % GRID_W).astype(F32)[:, None] * inv[None, :]
    z = jnp.zeros_like(ar)
    cos = jnp.concatenate([jnp.cos(ar), jnp.cos(ar), jnp.cos(ac), jnp.cos(ac)], axis=1)
    s1 = jnp.concatenate([-jnp.sin(ar), z, -jnp.sin(ac), z], axis=1)
    s2 = jnp.concatenate([z, jnp.sin(ar), z, jnp.sin(ac)], axis=1)

    def full(tab, ctx_val):
        tab = jnp.concatenate([jnp.full((t_ctx, C_HEAD), ctx_val, F32), tab], axis=0)
        return jnp.tile(tab, (1, LANES // C_HEAD))

    return full(cos, 1.0), full(s1, 0.0), full(s2, 0.0)


def _head_indicator(n, width):
    i = jnp.arange(n) // width
    return (i[:, None] == i[None, :]).astype(BF16)


def kernel(x, c, ctx, c_ctx, ada_w, ada_b, norm_mix, norm_ffn, ev_w_in, ev_w_out, rwkv_shift, rwkv_w0, rwkv_w_up, rwkv_a0, rwkv_a_up, rwkv_g_up, rwkv_k_k, rwkv_k_a, rwkv_r_k, rwkv_ln_g, rwkv_ln_b, pool_w, pool_scale, od_w_in, od_w_out, diff_lambda, diff_subln, ret_norm, moe_router_group, moe_router_expert, moe_w_gate, moe_w_up, moe_w_down, final_norm):
    B, T, D = x.shape
    t_ctx = ctx.shape[1]
    assert ada_w.shape[0] == 2 and t_ctx == TB and T % TB == 0 and T % GRID_W == 0 and B < 16
    assert B % SCAN_NB == 0 and B % RET_NB == 0

    cpad = jnp.zeros((16, D), F32).at[:B].set(c).at[B].set(c_ctx)
    mods = _ada(cpad, ada_w, ada_b)

    def layer_mod(i):
        lat = mods[i, :B]
        cm = jnp.broadcast_to(mods[i, B][None], lat.shape)
        return jnp.stack([cm, lat], axis=1).reshape(B, 2, 6, D)

    def router_w(i):
        w = jnp.concatenate([moe_router_group[i], moe_router_expert[i]], axis=1)
        return jnp.pad(w, ((0, 0), (0, LANES - w.shape[1])))

    ehead = _head_indicator(A_DIM, A_HEAD)

    mod0 = layer_mod(0)
    P = _proj(ctx, x, mod0, norm_mix[0], ev_w_in[0].astype(BF16), BF16)
    zl = jnp.zeros((LORA_W, A_DIM), F32)
    lora_blocks = []
    for d in range(2):
        lora_blocks.append(jnp.concatenate([rwkv_w_up[0, d], zl], axis=0))
        lora_blocks.append(jnp.concatenate([zl, rwkv_a_up[0, d]], axis=0))
    lora_w = jnp.concatenate(lora_blocks, axis=1).astype(BF16)
    wa0 = jnp.concatenate([rwkv_w0[0, 0], rwkv_a0[0, 0], rwkv_w0[0, 1], rwkv_a0[0, 1]]).reshape(1, -1)
    lw, kt, bb, v, kk, r, bonus, g, yb = _even_prep(
        P, t_ctx, rwkv_shift[0], wa0, lora_w, rwkv_g_up[0].astype(BF16), rwkv_k_k[0], rwkv_k_a[0], rwkv_r_k[0],
        pool_w[0].astype(BF16), pool_scale[0], ehead)
    y2 = _rwkv_scan(lw, kt, bb, v, kk, r, t_ctx)
    routed = _even_merge(y2, bonus, g, yb, ctx, x, mod0, rwkv_ln_g[0], rwkv_ln_b[0], ev_w_out[0].astype(BF16), ehead,
                         norm_ffn[0], router_w(0))
    mod1 = layer_mod(1)
    X, P2 = _moe(routed, mod0, 0, moe_w_gate, moe_w_up, moe_w_down, all_lat=False,
                 project=(mod1, norm_mix[1], od_w_in[0].astype(BF16), _rope_tables(t_ctx, T)))

    lam_init = 0.8 - 0.6 * math.exp(-0.3 * 1)
    c_lat = _diff_attn(P2, t_ctx, diff_lambda[0], diff_subln[0], lam_init)
    o2 = _retention(P2, t_ctx)
    routed = _odd_merge(c_lat, o2, P2, X, mod1, ret_norm[0], od_w_out[0].astype(BF16), t_ctx, norm_ffn[1], router_w(1))
    return _moe(routed, mod1, 1, moe_w_gate, moe_w_up, moe_w_down, all_lat=True,
                final_gain=final_norm)
```

```python
import functools
import math

import jax
import jax.numpy as jnp
from jax import lax
from jax.experimental import pallas as pl
from jax.experimental.pallas import tpu as pltpu

F32 = jnp.float32
BF16 = jnp.bfloat16
HI = lax.Precision.HIGHEST

GRID_W = 64
NORM_EPS = 1e-6
ROPE_BASE = 10000.0
A_HEADS, A_HEAD = 8, 64
A_DIM = A_HEADS * A_HEAD
LORA_W, LORA_A, LORA_G = 64, 64, 128
W_DECAY_SCALE = 0.606531
RWKV_GN_EPS = 64e-5
POOL_WINDOWS = (2, 4, 8, 16)
POOL_GROUP = 128
B_DIM = POOL_GROUP * 4
C_HEADS, C_HEAD = 4, 64
C_VDIM = 2 * C_HEAD
C_DIM = C_HEADS * C_VDIM
DIFF_EPS = 1e-5
D_HEADS, D_KDIM, D_VDIM = 4, 64, 128
D_DIM = D_HEADS * D_VDIM
A_K_OFF, A_V_OFF = 0, A_DIM
W_LORA_OFF = 2 * A_DIM
A_LORA_OFF = W_LORA_OFF + LORA_W
A_R_OFF = A_LORA_OFF + LORA_A
G_LORA_OFF = A_R_OFF + A_DIM
POOL_OFF = G_LORA_OFF + LORA_G
EVEN_IN_COLS = POOL_OFF + B_DIM
C_K_OFF = 0
C_V_OFF = C_HEADS * 2 * C_HEAD
D_K_OFF = C_V_OFF + C_DIM
D_V_OFF = D_K_OFF + D_HEADS * D_KDIM
C_Q_OFF = D_V_OFF + D_DIM
D_Q_OFF = C_Q_OFF + C_HEADS * 2 * C_HEAD
D_G_OFF = D_Q_OFF + D_HEADS * D_KDIM
ODD_IN_COLS = D_G_OFF + D_DIM
MOE_GROUPS, MOE_PER_GROUP = 4, 8
MOE_EXPERTS = MOE_GROUPS * MOE_PER_GROUP
MOE_TOP_K = 2
EXPERT_BLOCK = 256

LANES = 128
SUBLANES = 8
HALO = 16
TB = 256
RWKV_CHUNK = 64
RET_CHUNK = 256
RET_NB = 2
SCAN_NB = 4
ATT_TQ = 256
ATT_SUB = 8
VMEM_LIMIT = 56 * 1024 * 1024
DMA_UNROLL = 8
GATHER_SLOTS = 3
TABLE_UNROLL = 32
GATHER_DMA_PRIORITY = 1


def _cparams(sem):
    return pltpu.CompilerParams(dimension_semantics=sem, vmem_limit_bytes=VMEM_LIMIT)


def _modnorm(x, gain, scale, shift, eps=NORM_EPS):
    ms = jnp.mean(x * x, axis=-1, keepdims=True)
    return x * lax.rsqrt(ms + eps) * gain * (1.0 + scale) + shift


def _bdot(a, b, dims=((1,), (0,))):
    return lax.dot_general(a.astype(BF16), b.astype(BF16), (dims, ((), ())), preferred_element_type=F32)


def _bf16_terms(x, terms):
    parts = []
    for _ in range(terms):
        p = x.astype(BF16)
        parts.append(p)
        x = x - p.astype(F32)
    return parts


def _dot_split_lhs(x, w_bf16, terms=2):
    return sum(jnp.dot(p, w_bf16, preferred_element_type=F32) for p in _bf16_terms(x, terms))


def _ada_kernel(c_ref, w_ref, b_ref, o_ref):
    c = c_ref[...]
    s = c * jax.nn.sigmoid(c)
    o_ref[0] = jnp.dot(s, w_ref[0], precision=HI, preferred_element_type=F32) + b_ref[0]


def _ada(cpad, ada_w, ada_b):
    L, D, N = ada_w.shape
    tn = 1536
    return pl.pallas_call(
        _ada_kernel,
        grid=(L, N // tn),
        in_specs=[pl.BlockSpec((16, D), lambda l, n: (0, 0)),
                  pl.BlockSpec((1, D, tn), lambda l, n: (l, 0, n)),
                  pl.BlockSpec((1, 1, tn), lambda l, n: (l, 0, n))],
        out_specs=pl.BlockSpec((1, 16, tn), lambda l, n: (l, 0, n)),
        out_shape=jax.ShapeDtypeStruct((L, 16, N), F32),
        compiler_params=_cparams(("arbitrary", "arbitrary")),
        name="ada_mod",
    )(cpad, ada_w, ada_b.reshape(L, 1, N))


def _mod_spec(D, all_lat=False):
    if all_lat:
        return pl.BlockSpec((1, 1, 6, D), lambda b, j: (b, 1, 0, 0))
    return pl.BlockSpec((1, 1, 6, D), lambda b, j: (b, jnp.minimum(j, 1), 0, 0))


def _rope_plan():
    ranges = [(C_K_OFF, C_V_OFF), (D_K_OFF, D_V_OFF), (C_Q_OFF, D_G_OFF)]
    rope_chunks = tuple(g for lo, hi in ranges for g in range(lo // LANES, hi // LANES))
    kscale_chunks = tuple(range(D_K_OFF // LANES, D_V_OFF // LANES))
    return rope_chunks, kscale_chunks


def _project_block(h, w_ref, o_ref, rope_refs=None, before_piece=None):
    n_out = w_ref.shape[1]
    rope_chunks, kscale_chunks = _rope_plan() if rope_refs is not None else ((), ())
    cw = 512
    for n0 in range(0, n_out, cw):
        if before_piece is not None:
            before_piece(n0 // cw, n_out // cw)
        p = jnp.dot(h, w_ref[:, n0:n0 + cw], preferred_element_type=F32)
        if rope_refs is None:
            o_ref[0, :, n0:n0 + cw] = p.astype(o_ref.dtype)
            continue
        cos_ref, s1_ref, s2_ref = rope_refs
        for q in range(cw // LANES):
            gi = n0 // LANES + q
            sub = p[:, q * LANES:(q + 1) * LANES]
            if gi in rope_chunks:
                sub = (sub * cos_ref[...] + pltpu.roll(sub, LANES - 16, 1) * s1_ref[...]
                       + pltpu.roll(sub, 16, 1) * s2_ref[...])
            if gi in kscale_chunks:
                sub = sub * (D_KDIM ** -0.5)
            o_ref[0, :, n0 + q * LANES:n0 + (q + 1) * LANES] = sub.astype(o_ref.dtype)


def _stream_specs(D):
    return [pl.BlockSpec((1, TB, D), lambda b, j: (b, 0, 0)),
            pl.BlockSpec((1, TB, D), lambda b, j: (b, jnp.maximum(j - 1, 0), 0))]


def _stream_block(ctx_ref, x_ref):
    return jnp.where(pl.program_id(1) == 0, ctx_ref[0], x_ref[0])


def _proj_kernel(ctx_ref, x_ref, mod_ref, g_ref, w_ref, o_ref):
    m = mod_ref[0, 0]
    h = _modnorm(_stream_block(ctx_ref, x_ref), g_ref[...], m[1:2], m[0:1]).astype(BF16)
    _project_block(h, w_ref, o_ref)


def _proj(ctx, x, mod, gain, w_bf16, out_dtype):
    B, T, D = x.shape
    S = ctx.shape[1] + T
    N = w_bf16.shape[1]
    return pl.pallas_call(
        _proj_kernel,
        grid=(B, S // TB),
        in_specs=_stream_specs(D) + [
                  _mod_spec(D),
                  pl.BlockSpec((1, D), lambda b, j: (0, 0)),
                  pl.BlockSpec((D, N), lambda b, j: (0, 0))],
        out_specs=pl.BlockSpec((1, TB, N), lambda b, j: (b, j, 0)),
        out_shape=jax.ShapeDtypeStruct((B, S, N), out_dtype),
        compiler_params=_cparams(("arbitrary", "arbitrary")),
        name="proj",
    )(ctx, x, mod, gain.reshape(1, D), w_bf16)


def _even_prep_kernel(p_ref, pp_ref, pn_ref, sh_ref, wa0_ref, lora_ref, gup_ref, kkg_ref, ka_ref, rk_ref,
                      pw_ref, ps_ref, eh_ref,
                      lw_ref, kt_ref, bb_ref, v_ref, kkn_ref, r_ref, bonus_ref, g_ref, yb_ref,
                      *, nblk, t_ctx, t_lat):
    j = pl.program_id(1)
    prev_ok = j >= 2
    next_ok = jnp.logical_and(j >= 1, j < nblk - 1)
    n_ext = TB + 2 * HALO

    def ext(c0, c1):
        pv = jnp.where(prev_ok, pp_ref[0, :, c0:c1].astype(F32), 0.0)
        nx = jnp.where(next_ok, pn_ref[0, :, c0:c1].astype(F32), 0.0)
        return jnp.concatenate([pv, p_ref[0, :, c0:c1].astype(F32), nx], axis=0)

    def shift3(c0, c1, k0):
        e = ext(c0, c1)
        kern = sh_ref[:, k0:k0 + (c1 - c0)]
        y = kern[0:1] * pltpu.roll(e, 1, 0) + kern[1:2] * e + kern[2:3] * pltpu.roll(e, n_ext - 1, 0)
        return y[HALO:HALO + TB]

    k = shift3(A_K_OFF, A_K_OFF + A_DIM, 0)
    v = shift3(A_V_OFF, A_V_OFF + A_DIM, A_DIM)
    r = shift3(A_R_OFF, A_R_OFF + A_DIM, 2 * A_DIM)

    c = p_ref[0, :, W_LORA_OFF:W_LORA_OFF + LANES].astype(F32)
    lane = lax.broadcasted_iota(jnp.int32, c.shape, 1)
    lin = jnp.where(lane < LORA_W, jnp.tanh(c), c)
    z = _bdot(lin, lora_ref[...]) + wa0_ref[...]

    eh = eh_ref[...]
    kk0 = k * kkg_ref[...]
    ss = _dot_split_lhs(kk0 * kk0, eh)
    kk = kk0 * lax.rsqrt(jnp.maximum(ss, 1e-12))
    kt_sum = None
    for d in range(2):
        zw = z[:, (2 * d) * A_DIM:(2 * d + 1) * A_DIM]
        za = z[:, (2 * d + 1) * A_DIM:(2 * d + 2) * A_DIM]
        a = jax.nn.sigmoid(za)
        kt = k * (1.0 + (a - 1.0) * ka_ref[...])
        lw_ref[d, 0] = -W_DECAY_SCALE * jax.nn.sigmoid(zw)
        kt_ref[d, 0] = kt.astype(kt_ref.dtype)
        bb_ref[d, 0] = (a * kk).astype(bb_ref.dtype)
        kt_sum = kt if kt_sum is None else kt_sum + kt
    v_ref[0] = v.astype(v_ref.dtype)
    kkn_ref[0] = kk.astype(kkn_ref.dtype)
    r_ref[0] = r.astype(r_ref.dtype)
    bonus_ref[0] = (_dot_split_lhs(r * kt_sum * rk_ref[...], eh) * v).astype(bonus_ref.dtype)
    gl = p_ref[0, :, G_LORA_OFF:G_LORA_OFF + LORA_G].astype(F32)
    g_ref[0] = _bdot(jax.nn.sigmoid(gl), gup_ref[...]).astype(g_ref.dtype)

    row = lax.broadcasted_iota(jnp.int32, (TB, POOL_GROUP), 0)
    tpos = row + jnp.maximum(j - 1, 0) * TB
    tseg = jnp.where(j == 0, t_ctx, t_lat)
    for gi, win in enumerate(POOL_WINDOWS):
        c0 = POOL_OFF + gi * POOL_GROUP
        e = ext(c0, c0 + POOL_GROUP)
        s = e + pltpu.roll(e, 1, 0)
        sh = 1
        while 2 * sh < win:
            s = pltpu.roll(s, sh, 0) + pltpu.roll(s, n_ext - sh, 0)
            sh *= 2
        s = s[HALO:HALO + TB]
        lo = jnp.clip(tpos - win // 2, 0, tseg)
        hi = jnp.clip(tpos - win // 2 + win, 0, tseg)
        mean = s / (hi - lo).astype(F32)
        dd = mean - p_ref[0, :, c0:c0 + POOL_GROUP].astype(F32)
        y = _bdot(dd, pw_ref[gi])
        yb_ref[0, :, gi * POOL_GROUP:(gi + 1) * POOL_GROUP] = (
            y * ps_ref[:, gi * POOL_GROUP:(gi + 1) * POOL_GROUP]).astype(yb_ref.dtype)


def _even_prep(P, t_ctx, shift_k, wa0, lora_w, g_up, k_k, k_a, r_k, pool_w, pool_scale, ehead):
    B, S, N = P.shape
    nblk = S // TB
    hb = TB // HALO
    const2 = lambda b, j: (0, 0)
    in_specs = [
        pl.BlockSpec((1, TB, N), lambda b, j: (b, j, 0)),
        pl.BlockSpec((1, HALO, N), lambda b, j: (b, jnp.maximum(j * hb - 1, 0), 0)),
        pl.BlockSpec((1, HALO, N), lambda b, j: (b, jnp.minimum((j + 1) * hb, S // HALO - 1), 0)),
        pl.BlockSpec((3, 3 * A_DIM), const2),
        pl.BlockSpec((1, 4 * A_DIM), const2),
        pl.BlockSpec((LANES, 4 * A_DIM), const2),
        pl.BlockSpec((LORA_G, A_DIM), const2),
        pl.BlockSpec((1, A_DIM), const2),
        pl.BlockSpec((1, A_DIM), const2),
        pl.BlockSpec((1, A_DIM), const2),
        pl.BlockSpec((4, POOL_GROUP, POOL_GROUP), lambda b, j: (0, 0, 0)),
        pl.BlockSpec((1, B_DIM), const2),
        pl.BlockSpec((A_DIM, A_DIM), const2),
    ]
    two_f32 = jax.ShapeDtypeStruct((2, B, S, A_DIM), F32)
    two = jax.ShapeDtypeStruct((2, B, S, A_DIM), BF16)
    one = jax.ShapeDtypeStruct((B, S, A_DIM), BF16)
    two_spec = pl.BlockSpec((2, 1, TB, A_DIM), lambda b, j: (0, b, j, 0))
    one_spec = pl.BlockSpec((1, TB, A_DIM), lambda b, j: (b, j, 0))
    return pl.pallas_call(
        functools.partial(_even_prep_kernel, nblk=nblk, t_ctx=t_ctx, t_lat=S - t_ctx),
        grid=(B, nblk),
        in_specs=in_specs,
        out_specs=[two_spec] * 3 + [one_spec] * 6,
        out_shape=[two_f32, two, two] + [one] * 6,
        compiler_params=_cparams(("arbitrary", "arbitrary")),
        name="even_prep",
    )(P, P, P, shift_k, wa0, lora_w, g_up, k_k.reshape(1, -1), k_a.reshape(1, -1), r_k.reshape(1, -1),
      pool_w, pool_scale.reshape(1, -1), ehead)


def _rwkv_scan_kernel(lw_ref, kt_ref, bb_ref, v_ref, kk_ref, r_ref, y_ref, s_ref):
    d = pl.program_id(1)
    c = pl.program_id(2)

    @pl.when(c == 0)
    def _():
        s_ref[...] = jnp.zeros_like(s_ref)

    C = RWKV_CHUNK
    tt = lax.broadcasted_iota(jnp.int32, (C, C), 0)
    ii = lax.broadcasted_iota(jnp.int32, (C, C), 1)
    rel = (tt - ii) * (1 - 2 * d)
    strict = rel > 0
    incl = rel >= 0
    t2 = lax.broadcasted_iota(jnp.int32, (C, 2 * C), 0)
    i2 = lax.broadcasted_iota(jnp.int32, (C, 2 * C), 1) & (C - 1)
    incl2 = (t2 - i2) * (1 - 2 * d) >= 0
    tri = jnp.where(incl, 1.0, 0.0).astype(BF16)
    n_dbl = int(math.log2(C))
    hsl = [slice(h * A_HEAD, (h + 1) * A_HEAD) for h in range(A_HEADS)]
    ar, bk, vh, bkp, ptot = [], [], [], [], []
    for bi in range(SCAN_NB):
        lw = lw_ref[0, bi]
        cum = sum(jnp.dot(tri, p, preferred_element_type=F32) for p in _bf16_terms(lw, 3))
        tot = jnp.sum(lw, axis=0, keepdims=True)
        p_inv = jnp.exp(-cum)
        p_rest = jnp.exp(tot - cum)
        p_tot = jnp.exp(tot)
        bb = bb_ref[0, bi]
        kt = kt_ref[0, bi]
        at_all = (-kk_ref[bi] * jnp.exp(cum - lw)).astype(BF16)
        rt_all = (r_ref[bi] * jnp.exp(cum)).astype(BF16)
        bt_all = (bb * p_inv).astype(BF16)
        kti_all = (kt * p_inv).astype(BF16)
        bp_all = (bb * p_rest).astype(BF16)
        kp_all = (kt * p_rest).astype(BF16)
        v_all = v_ref[bi].astype(BF16)
        for s in hsl:
            ar.append(jnp.concatenate([at_all[:, s], rt_all[:, s]], axis=0))
            bk.append(jnp.concatenate([bt_all[:, s], kti_all[:, s]], axis=0))
            bkp.append(jnp.concatenate([bp_all[:, s], kp_all[:, s]], axis=0))
            vh.append(v_all[:, s])
            ptot.append(p_tot[:, s])
    chains = range(SCAN_NB * A_HEADS)
    s0 = [s_ref[n] for n in chains]
    m = [_bdot(ar[n], bk[n], ((1,), (1,))) for n in chains]
    arh = [_bdot(ar[n], s0[n], ((1,), (1,))) for n in chains]
    lp = [jnp.where(strict, m[n][:C, :C], 0.0) for n in chains]
    a_ak = [jnp.where(strict, m[n][:C, C:], 0.0) for n in chains]
    a_r = [jnp.where(incl2, m[n][C:, :], 0.0) for n in chains]
    u = [arh[n][:C] + _bdot(a_ak[n], vh[n]) for n in chains]
    for q in range(n_dbl):
        u = [u[n] + _bdot(lp[n], u[n]) for n in chains]
        if q < n_dbl - 1:
            lp = [_bdot(lp[n], lp[n]) for n in chains]
    uv = [jnp.concatenate([u[n].astype(BF16), vh[n]], axis=0) for n in chains]
    outs = [arh[n][C:] + _bdot(a_r[n], uv[n]) for n in chains]
    s_new = [s0[n] * ptot[n] + _bdot(uv[n], bkp[n], ((0,), (0,))) for n in chains]
    for n in chains:
        s_ref[n] = s_new[n]
    for bi in range(SCAN_NB):
        y_ref[0, bi] = jnp.concatenate(outs[bi * A_HEADS:(bi + 1) * A_HEADS], axis=1).astype(y_ref.dtype)


def _rwkv_scan(lw, kt, bb, v, kk, r, t_ctx):
    _, B, S, _ = lw.shape
    C = RWKV_CHUNK
    nc = S // C
    nctx = t_ctx // C

    def cidx(d, c):
        back = jnp.where(c < nctx, nctx - 1 - c, nc - 1 + nctx - c)
        return jnp.where(d == 0, c, back)

    two_spec = pl.BlockSpec((1, SCAN_NB, C, A_DIM), lambda b, d, c: (d, b, cidx(d, c), 0))
    one_spec = pl.BlockSpec((SCAN_NB, C, A_DIM), lambda b, d, c: (b, cidx(d, c), 0))
    return pl.pallas_call(
        _rwkv_scan_kernel,
        grid=(B // SCAN_NB, 2, nc),
        in_specs=[two_spec] * 3 + [one_spec] * 3,
        out_specs=two_spec,
        out_shape=jax.ShapeDtypeStruct((2, B, S, A_DIM), BF16),
        scratch_shapes=[pltpu.VMEM((SCAN_NB * A_HEADS, A_HEAD, A_HEAD), F32)],
        compiler_params=_cparams(("arbitrary", "arbitrary", "arbitrary")),
        name="rwkv_scan",
    )(lw, kt, bb, v, kk, r)


def _even_merge_kernel(y2_ref, bonus_ref, g_ref, yb_ref, ctx_ref, x_ref, mod_ref, lng_ref, lnb_ref, wo_ref, eh_ref,
                       gffn_ref, wr_ref, o_ref, *route_refs):
    y = y2_ref[0, 0].astype(F32) + y2_ref[1, 0].astype(F32)
    eh = eh_ref[...]
    mu = _dot_split_lhs(y, eh) * (1.0 / A_HEAD)
    yc = y - mu
    var = _dot_split_lhs(yc * yc, eh) * (1.0 / A_HEAD)
    yn = yc * lax.rsqrt(var + RWKV_GN_EPS) * lng_ref[...] + lnb_ref[...]
    ya = (yn + bonus_ref[0]) * g_ref[0]
    out = _bdot(ya, wo_ref[0:A_DIM, :]) + _bdot(yb_ref[0], wo_ref[A_DIM:A_DIM + B_DIM, :])
    m = mod_ref[0, 0]
    xn = _stream_block(ctx_ref, x_ref) + m[2:3] * out
    o_ref[0] = xn
    _route_block(xn, m, gffn_ref, wr_ref, *route_refs)


def _even_merge(y2, bonus, g, yb, ctx, x, mod, ln_g, ln_b, wo_bf16, ehead, ffn_gain, w_router):
    B, T, D = x.shape
    S = ctx.shape[1] + T
    r_in, r_out, r_shapes, r_scratch = _route_specs(B, S, D)
    one_spec = pl.BlockSpec((1, TB, A_DIM), lambda b, j: (b, j, 0))
    const2 = lambda b, j: (0, 0)
    return pl.pallas_call(
        _even_merge_kernel,
        grid=(B, S // TB),
        in_specs=[pl.BlockSpec((2, 1, TB, A_DIM), lambda b, j: (0, b, j, 0)), one_spec, one_spec, one_spec]
        + _stream_specs(D) + [_mod_spec(D),
                  pl.BlockSpec((1, A_DIM), const2), pl.BlockSpec((1, A_DIM), const2),
                  pl.BlockSpec((A_DIM + B_DIM, D), const2), pl.BlockSpec((A_DIM, A_DIM), const2)] + r_in,
        out_specs=[pl.BlockSpec((1, TB, D), lambda b, j: (b, j, 0))] + r_out,
        out_shape=[jax.ShapeDtypeStruct((B, S, D), F32)] + r_shapes,
        scratch_shapes=r_scratch,
        compiler_params=_cparams(("arbitrary", "arbitrary")),
        name="even_merge_route",
    )(y2, bonus, g, yb, ctx, x, mod, ln_g.reshape(1, -1), ln_b.reshape(1, -1), wo_bf16, ehead,
      ffn_gain.reshape(1, D), w_router)


def _route_block(xn, m, g_ref, wr_ref, hf_ref, meta_ref, metat_ref, cnt_ref, carry_ref):
    first = jnp.logical_and(pl.program_id(0) == 0, pl.program_id(1) == 0)

    @pl.when(first)
    def _():
        carry_ref[...] = jnp.zeros_like(carry_ref)

    hf = _modnorm(xn, g_ref[...], m[4:5], m[3:4])
    hf_ref[0] = hf
    h0, h1 = _bf16_terms(hf, 2)
    w0, w1 = _bf16_terms(wr_ref[...], 2)
    logits = (jnp.dot(h0, w0, preferred_element_type=F32) + jnp.dot(h0, w1, preferred_element_type=F32)
              + jnp.dot(h1, w0, preferred_element_type=F32))
    lane = lax.broadcasted_iota(jnp.int32, logits.shape, 1).astype(F32)
    neg = -jnp.inf
    far = float(LANES)
    lg = jnp.where(lane < MOE_GROUPS, logits, neg)
    mg = jnp.max(lg, axis=1, keepdims=True)
    grp = jnp.min(jnp.where(lg == mg, lane, far), axis=1, keepdims=True)
    gw = 1.0 / jnp.sum(jnp.exp(lg - mg), axis=1, keepdims=True)
    el = lane - MOE_GROUPS
    in_grp = jnp.logical_and(el >= grp * MOE_PER_GROUP, el < (grp + 1.0) * MOE_PER_GROUP)
    le = jnp.where(in_grp, logits, neg)
    m0 = jnp.max(le, axis=1, keepdims=True)
    i0 = jnp.min(jnp.where(le == m0, lane, far), axis=1, keepdims=True)
    le1 = jnp.where(lane == i0, neg, le)
    m1 = jnp.max(le1, axis=1, keepdims=True)
    i1 = jnp.min(jnp.where(le1 == m1, lane, far), axis=1, keepdims=True)
    p1 = jnp.exp(m1 - m0)
    gate0 = gw / (1.0 + p1)
    gate1 = gw * p1 / (1.0 + p1)
    e0 = i0 - MOE_GROUPS
    e1 = i1 - MOE_GROUPS
    oh0 = (lane == e0).astype(F32)
    oh1 = (lane == e1).astype(F32)
    cnt = oh0 + oh1
    rr = lax.broadcasted_iota(jnp.int32, (TB, TB), 0)
    cc = lax.broadcasted_iota(jnp.int32, (TB, TB), 1)
    before = _bdot((cc < rr).astype(F32), cnt) + carry_ref[...]
    rank0 = jnp.sum(oh0 * before, axis=1, keepdims=True)
    rank1 = jnp.sum(oh1 * before, axis=1, keepdims=True)
    carry_ref[...] = carry_ref[...] + jnp.sum(cnt, axis=0, keepdims=True)
    cnt_ref[...] = carry_ref[...]
    meta = jnp.where(lane == 0, e0, 0.0)
    meta = jnp.where(lane == 1, e1, meta)
    meta = jnp.where(lane == 2, rank0, meta)
    meta = jnp.where(lane == 3, rank1, meta)
    meta = jnp.where(lane == 4, gate0, meta)
    meta = jnp.where(lane == 5, gate1, meta)
    meta_ref[0] = meta
    metat_ref[0, 0] = meta.T[0:SUBLANES]


def _route_specs(B, S, D):
    in_specs = [pl.BlockSpec((1, D), lambda b, j: (0, 0)), pl.BlockSpec((D, LANES), lambda b, j: (0, 0))]
    out_specs = [pl.BlockSpec((1, TB, D), lambda b, j: (b, j, 0)),
                 pl.BlockSpec((1, TB, LANES), lambda b, j: (b, j, 0)),
                 pl.BlockSpec((1, 1, SUBLANES, TB), lambda b, j: (b, j, 0, 0)),
                 pl.BlockSpec((1, LANES), lambda b, j: (0, 0))]
    out_shapes = [jax.ShapeDtypeStruct((B, S, D), F32), jax.ShapeDtypeStruct((B, S, LANES), F32),
                  jax.ShapeDtypeStruct((B, S // TB, SUBLANES, TB), F32), jax.ShapeDtypeStruct((1, LANES), F32)]
    return in_specs, out_specs, out_shapes, [pltpu.VMEM((1, LANES), F32)]


def _routing_tables(metat, counts):
    B, nj = metat.shape[:2]
    n_tok = B * nj * TB
    mi = jnp.transpose(metat[:, :, 0:4, :], (2, 0, 1, 3)).reshape(4, n_tok).astype(jnp.int32)
    cnt = counts[0, :MOE_EXPERTS].astype(jnp.int32)
    padded = (cnt + EXPERT_BLOCK - 1) // EXPERT_BLOCK * EXPERT_BLOCK
    pad_end = jnp.cumsum(padded)
    pad_start = pad_end - padded
    dest = jnp.concatenate([pad_start[mi[0]] + mi[2], pad_start[mi[1]] + mi[3]])
    n_blocks = -(-(n_tok * MOE_TOP_K) // EXPERT_BLOCK) + MOE_EXPERTS
    starts = jnp.arange(n_blocks, dtype=jnp.int32) * EXPERT_BLOCK
    block_e = jnp.minimum(jnp.sum((pad_end[None, :] <= starts[:, None]).astype(jnp.int32), axis=1), MOE_EXPERTS - 1)
    n_used = (pad_end[-1] // EXPERT_BLOCK).astype(jnp.int32).reshape(1)
    present = cnt > 0
    ids = jnp.arange(MOE_EXPERTS, dtype=jnp.int32)
    later = jnp.where(jnp.logical_and(present[None, :], ids[None, :] > ids[:, None]), ids[None, :], MOE_EXPERTS)
    nxt = jnp.min(later, axis=1)
    nxt = jnp.where(nxt == MOE_EXPERTS, -1, nxt)
    wslot = (jnp.cumsum(present.astype(jnp.int32)) - 1) % 2
    pads = jnp.concatenate([pad_start + cnt, pad_end, wslot, nxt]).astype(jnp.int32)
    return dest, block_e, n_used, pads, n_blocks


def _expert_kernel(be_ref, nu_ref, dest_ref, pads_ref, hf_ref, wg_ref, wu_ref, wd_ref, o_ref, xbuf, wg_bf, wu_bf, wd_bf,
                   wg_f32, wu_f32, wd_f32, rows_ref, sem, wsem, *, n_tok, layer):
    j = pl.program_id(0)
    n_used = nu_ref[0]
    EB = EXPERT_BLOCK
    E = MOE_EXPERTS

    def weight_copies(e, ws):
        return [pltpu.make_async_copy(src.at[layer, e], dst.at[ws], wsem.at[ws])
                for src, dst in ((wg_ref, wg_f32), (wu_ref, wu_f32), (wd_ref, wd_f32))]

    def row_copy(blk, slot, r):
        row = rows_ref[blk * EB + r]
        return pltpu.make_async_copy(hf_ref.at[pl.ds(row, 1), :], xbuf.at[slot, pl.ds(r, 1), :], sem.at[slot])

    def issue(blk, slot):
        def body(r, carry):
            row_copy(blk, slot, r).start()
            return carry
        lax.fori_loop(0, EB, body, 0, unroll=DMA_UNROLL)

    @pl.when(j == 0)
    def _():
        fill_mask = (1 << (n_tok.bit_length() - 1)) - 1

        def fill(i, carry):
            rows_ref[i] = jnp.bitwise_and(i, fill_mask)
            return carry
        for e in range(MOE_EXPERTS):
            lax.fori_loop(pads_ref[e], pads_ref[MOE_EXPERTS + e], fill, 0)

        def invert(t, carry):
            rows_ref[dest_ref[t]] = t
            rows_ref[dest_ref[n_tok + t]] = t
            return carry
        lax.fori_loop(0, n_tok, invert, 0, unroll=TABLE_UNROLL)
        issue(0, 0)

        @pl.when(n_used > 1)
        def _():
            issue(1, 1)

    def compute(prefetch_ahead):
        slot = lax.rem(j, GATHER_SLOTS)

        def body(r, carry):
            row_copy(j, slot, r).wait()
            return carry
        lax.fori_loop(0, EB, body, 0, unroll=DMA_UNROLL)

        e_cur = be_ref[j]

        @pl.when(jnp.logical_or(j == 0, e_cur != be_ref[jnp.maximum(j - 1, 0)]))
        def _():
            ws = pads_ref[2 * E + e_cur]
            e_next = pads_ref[3 * E + e_cur]

            @pl.when(j == 0)
            def _():
                for cp in weight_copies(e_cur, ws):
                    cp.start()
            for cp in weight_copies(e_cur, ws):
                cp.wait()
            wg_bf[...] = wg_f32[ws].astype(BF16)
            wu_bf[...] = wu_f32[ws].astype(BF16)
            wd_bf[...] = wd_f32[ws].astype(BF16)

            @pl.when(e_next >= 0)
            def _():
                for cp in weight_copies(e_next, 1 - ws):
                    cp.start()

        xb = xbuf[slot].astype(BF16)
        if prefetch_ahead:
            ahead = lax.rem(j + 2, GATHER_SLOTS)
            for r in range(EB):
                row_copy(j + 2, ahead, r).start(priority=GATHER_DMA_PRIORITY)
        hg = jnp.dot(xb, wg_bf[...], preferred_element_type=F32)
        hu = jnp.dot(xb, wu_bf[...], preferred_element_type=F32)
        hm = (hg * jax.nn.sigmoid(hg) * hu).astype(BF16)
        o_ref[...] = jnp.dot(hm, wd_bf[...], preferred_element_type=F32)

    @pl.when(j + 2 < n_used)
    def _():
        compute(True)

    @pl.when(jnp.logical_and(j < n_used, j + 2 >= n_used))
    def _():
        compute(False)

    @pl.when(j >= n_used)
    def _():
        o_ref[...] = jnp.zeros_like(o_ref)


def _experts(hf2d, dest, block_e, n_used, pads, n_blocks, layer, w_gate, w_up, w_down):
    N, D = hf2d.shape
    _, E, _, De = w_gate.shape
    EB = EXPERT_BLOCK
    grid_spec = pltpu.PrefetchScalarGridSpec(
        num_scalar_prefetch=4,
        grid=(n_blocks,),
        in_specs=[pl.BlockSpec(memory_space=pl.ANY)] * 4,
        out_specs=pl.BlockSpec((EB, D), lambda j, be, nu, ds, pd: (j, 0)),
        scratch_shapes=[pltpu.VMEM((GATHER_SLOTS, EB, D), F32), pltpu.VMEM((D, De), BF16), pltpu.VMEM((D, De), BF16),
                        pltpu.VMEM((De, D), BF16), pltpu.VMEM((2, D, De), F32), pltpu.VMEM((2, D, De), F32),
                        pltpu.VMEM((2, De, D), F32), pltpu.SMEM((n_blocks * EB,), jnp.int32),
                        pltpu.SemaphoreType.DMA((GATHER_SLOTS,)), pltpu.SemaphoreType.DMA((2,))],
    )
    return pl.pallas_call(
        functools.partial(_expert_kernel, n_tok=N, layer=layer),
        grid_spec=grid_spec,
        out_shape=jax.ShapeDtypeStruct((n_blocks * EB, D), F32),
        compiler_params=_cparams(("arbitrary",)),
        name="moe_experts",
    )(block_e, n_used, dest, pads, hf2d, w_gate, w_up, w_down)


def _combine_kernel(dest_ref, yb_ref, x_ref, meta_ref, mod_ref, *rest, nj, project):
    if project:
        mod1_ref, g1_ref, w_ref, cos_ref, s1_ref, s2_ref, o_ref, p_ref, ybuf, sem = rest
    else:
        fn_ref, o_ref, ybuf, sem = rest
    b = pl.program_id(0)
    j = pl.program_id(1)
    step = b * nj + j
    n_steps = pl.num_programs(0) * nj
    n_tok = n_steps * TB

    def row_copy(st, slot, r, k):
        dst_row = dest_ref[k * n_tok + st * TB + r]
        return pltpu.make_async_copy(yb_ref.at[pl.ds(dst_row, 1), :], ybuf.at[slot, k, pl.ds(r, 1), :], sem.at[slot])

    def issue(st, slot):
        def body(r, carry):
            row_copy(st, slot, r, 0).start()
            row_copy(st, slot, r, 1).start()
            return carry
        lax.fori_loop(0, TB, body, 0, unroll=DMA_UNROLL)

    @pl.when(step == 0)
    def _():
        issue(0, 0)
        issue(1, 1)

    slot = lax.rem(step, GATHER_SLOTS)

    def body(prefetch_ahead):
        def wbody(r, carry):
            row_copy(step, slot, r, 0).wait()
            row_copy(step, slot, r, 1).wait()
            return carry
        lax.fori_loop(0, TB, wbody, 0, unroll=DMA_UNROLL)
        meta = meta_ref[0]
        f = meta[:, 4:5] * ybuf[slot, 0] + meta[:, 5:6] * ybuf[slot, 1]
        m = mod_ref[0, 0]
        xn = x_ref[0] + m[5:6] * f
        ahead = lax.rem(step + 2, GATHER_SLOTS)

        def issue_rows(i, n):
            if prefetch_ahead:
                for r in range(i * TB // n, (i + 1) * TB // n):
                    row_copy(step + 2, ahead, r, 0).start()
                    row_copy(step + 2, ahead, r, 1).start()

        if project:
            o_ref[0] = xn
            m1 = mod1_ref[0, 0]
            h = _modnorm(xn, g1_ref[...], m1[1:2], m1[0:1]).astype(BF16)
            _project_block(h, w_ref, p_ref, (cos_ref, s1_ref, s2_ref), before_piece=issue_rows)
        else:
            issue_rows(0, 1)
            ms = jnp.mean(xn * xn, axis=-1, keepdims=True)
            o_ref[0] = xn * lax.rsqrt(ms + NORM_EPS) * fn_ref[...]

    @pl.when(step + 2 < n_steps)
    def _():
        body(True)

    @pl.when(step + 2 >= n_steps)
    def _():
        body(False)


def _combine(dest, yb, X, meta, mod, all_lat, final_gain=None, project=None):
    B, S, D = X.shape
    nj = S // TB

    def mspec(lat_only):
        if lat_only:
            return pl.BlockSpec((1, 1, 6, D), lambda b, j, ds: (b, 1, 0, 0))
        return pl.BlockSpec((1, 1, 6, D), lambda b, j, ds: (b, jnp.minimum(j, 1), 0, 0))

    row_spec = pl.BlockSpec((1, TB, D), lambda b, j, ds: (b, j, 0))
    in_specs = [pl.BlockSpec(memory_space=pl.ANY), row_spec,
                pl.BlockSpec((1, TB, LANES), lambda b, j, ds: (b, j, 0)), mspec(all_lat)]
    args = [dest, yb, X, meta, mod]
    if project is None:
        in_specs.append(pl.BlockSpec((1, D), lambda b, j, ds: (0, 0)))
        args.append(final_gain.reshape(1, D))
        out_specs = row_spec
        out_shape = jax.ShapeDtypeStruct((B, S, D), F32)
    else:
        mod1, gain1, w_bf16, (cos, s1, s2) = project
        N = w_bf16.shape[1]
        in_specs += [mspec(False), pl.BlockSpec((1, D), lambda b, j, ds: (0, 0)),
                     pl.BlockSpec((D, N), lambda b, j, ds: (0, 0))]
        in_specs += [pl.BlockSpec((TB, LANES), lambda b, j, ds: (j, 0))] * 3
        args += [mod1, gain1.reshape(1, D), w_bf16, cos, s1, s2]
        out_specs = [row_spec, pl.BlockSpec((1, TB, N), lambda b, j, ds: (b, j, 0))]
        out_shape = [jax.ShapeDtypeStruct((B, S, D), F32), jax.ShapeDtypeStruct((B, S, N), BF16)]
    grid_spec = pltpu.PrefetchScalarGridSpec(
        num_scalar_prefetch=1,
        grid=(B, nj),
        in_specs=in_specs,
        out_specs=out_specs,
        scratch_shapes=[pltpu.VMEM((GATHER_SLOTS, 2, TB, D), F32), pltpu.SemaphoreType.DMA((GATHER_SLOTS,))],
    )
    return pl.pallas_call(
        functools.partial(_combine_kernel, nj=nj, project=project is not None),
        grid_spec=grid_spec,
        out_shape=out_shape,
        compiler_params=_cparams(("arbitrary", "arbitrary")),
        name="moe_combine_final" if project is None else "moe_combine_proj",
    )(*args)


def _moe(routed, mod, layer, w_gate, w_up, w_down, all_lat, final_gain=None, project=None):
    X, hf, meta, metat, counts = routed
    B, S, D = X.shape
    dest, block_e, n_used, pads, n_blocks = _routing_tables(metat, counts)
    yb = _experts(hf.reshape(B * S, D), dest, block_e, n_used, pads, n_blocks, layer, w_gate, w_up, w_down)
    return _combine(dest, yb, X, meta, mod, all_lat, final_gain, project)


def _diff_attn_kernel(*refs, lam_init):
    q_refs = refs[:ATT_SUB]
    k_ref, v_ref, dl_ref, sg_ref, o_ref, vx_ref = refs[ATT_SUB:]

    @pl.when(pl.program_id(2) == 0)
    def _():
        vlane = lax.broadcasted_iota(jnp.int32, (vx_ref.shape[0], LANES), 1)
        vx_ref[:, 0:C_VDIM] = v_ref[0]
        vx_ref[:, C_VDIM:C_VDIM + LANES] = jnp.where(vlane == 0, 1.0, 0.0).astype(BF16)

    k = k_ref[0]
    dl = dl_ref[...]
    lam = (jnp.exp(jnp.sum(dl[0:1] * dl[1:2], axis=1, keepdims=True))
           - jnp.exp(jnp.sum(dl[2:3] * dl[3:4], axis=1, keepdims=True)) + lam_init)

    def scores(t):
        q = q_refs[t][0] * (C_HEAD ** -0.5)
        lane = lax.broadcasted_iota(jnp.int32, q.shape, 1)
        zero = jnp.zeros_like(q)
        qs = (jnp.where(lane < C_HEAD, q, zero), jnp.where(lane >= C_HEAD, q, zero))
        return [lax.dot_general(qq, k, (((1,), (1,)), ((), ())), preferred_element_type=F32) for qq in qs]

    def finish(t, s):
        e = [jnp.exp(si - jnp.max(si, axis=1, keepdims=True)).astype(BF16) for si in s]
        oz = [jnp.dot(ei, vx_ref[...], preferred_element_type=F32) for ei in e]
        on = [ozi[:, 0:C_VDIM] * (1.0 / ozi[:, C_VDIM:C_VDIM + 1]) for ozi in oz]
        o = on[0] - lam * on[1]
        ms = jnp.mean(o * o, axis=-1, keepdims=True)
        o_ref[0, t * ATT_TQ:(t + 1) * ATT_TQ, :] = (
            o * lax.rsqrt(ms + DIFF_EPS) * sg_ref[...] * (1.0 - lam_init)).astype(o_ref.dtype)

    s_next = scores(0)
    for t in range(ATT_SUB):
        s_cur = s_next
        if t + 1 < ATT_SUB:
            s_next = scores(t + 1)
        finish(t, s_cur)


def _diff_attn(P2, t_ctx, diff_lambda, diff_subln, lam_init):
    B, S, _ = P2.shape
    T = S - t_ctx
    qoff = t_ctx // ATT_TQ
    step = ATT_SUB * ATT_TQ

    def q_spec(t):
        return pl.BlockSpec((1, ATT_TQ, LANES), lambda b, h, q: (b, q * ATT_SUB + t + qoff, C_Q_OFF // LANES + h))

    return pl.pallas_call(
        functools.partial(_diff_attn_kernel, lam_init=lam_init),
        grid=(B, C_HEADS, T // step),
        in_specs=[q_spec(t) for t in range(ATT_SUB)] + [
                  pl.BlockSpec((1, S, LANES), lambda b, h, q: (b, 0, C_K_OFF // LANES + h)),
                  pl.BlockSpec((1, S, LANES), lambda b, h, q: (b, 0, C_V_OFF // LANES + h)),
                  pl.BlockSpec((4, C_HEAD), lambda b, h, q: (0, 0)),
                  pl.BlockSpec((1, C_VDIM), lambda b, h, q: (0, 0))],
        out_specs=pl.BlockSpec((1, step, LANES), lambda b, h, q: (b, q, h)),
        out_shape=jax.ShapeDtypeStruct((B, T, C_DIM), BF16),
        scratch_shapes=[pltpu.VMEM((S, C_VDIM + LANES), BF16)],
        compiler_params=_cparams(("arbitrary", "arbitrary", "arbitrary")),
        name="diff_attn",
    )(*([P2] * ATT_SUB), P2, P2, diff_lambda, diff_subln.reshape(1, -1))


def _retention_kernel(q_ref, k_ref, v0_ref, v1_ref, o_ref, r_ref):
    d = pl.program_id(1)
    c = pl.program_id(2)

    @pl.when(c == 0)
    def _():
        r_ref[...] = jnp.zeros_like(r_ref)

    C = RET_CHUNK
    ti = lax.broadcasted_iota(jnp.int32, (C, C), 0)
    si = lax.broadcasted_iota(jnp.int32, (C, C), 1)
    rel = ((ti - si) * (1 - 2 * d)).astype(F32)
    col = lax.broadcasted_iota(jnp.int32, (C, 1), 0)
    pos = jnp.where(d == 0, col, C - 1 - col).astype(F32)
    chains = [(bi, h) for bi in range(RET_NB) for h in range(D_HEADS)]
    r0s = [r_ref[bi * D_HEADS + h] for bi, h in chains]
    decs = []
    for h in range(D_HEADS):
        lg_f = math.log(1.0 - 2.0 ** (-5.0 - h))
        lg_b = math.log(1.0 - 2.0 ** (-5.0 - (D_HEADS - 1 - h)))
        lg = jnp.where(d == 0, jnp.full((1, 1), lg_f, F32), jnp.full((1, 1), lg_b, F32))
        decs.append((jnp.where(rel >= 0, jnp.exp(lg * jnp.maximum(rel, 0.0)), 0.0),
                     jnp.exp(lg * (pos + 1.0)), jnp.exp(lg * (C - 1.0 - pos)), jnp.exp(lg * C)))
    outs, r_new = [], []
    for n, (bi, h) in enumerate(chains):
        inner_dec, q_dec, k_dec, c_dec = decs[h]
        q = q_ref[bi, :, h * D_KDIM:(h + 1) * D_KDIM]
        k = k_ref[bi, :, h * D_KDIM:(h + 1) * D_KDIM]
        vref = v0_ref if h < 2 else v1_ref
        v = vref[bi, :, (h % 2) * D_VDIM:(h % 2 + 1) * D_VDIM]
        inner = lax.dot_general(q, k, (((1,), (1,)), ((), ())), preferred_element_type=F32) * inner_dec
        outs.append(_bdot(inner, v) + _bdot(q, r0s[n]) * q_dec)
        r_new.append(r0s[n] * c_dec + _bdot(k.astype(F32) * k_dec, v, ((0,), (0,))))
    for n in range(len(chains)):
        r_ref[n] = r_new[n]
    for bi in range(RET_NB):
        o_ref[0, bi] = jnp.concatenate(outs[bi * D_HEADS:(bi + 1) * D_HEADS], axis=1).astype(o_ref.dtype)


def _retention(P2, t_ctx):
    B, S, _ = P2.shape
    C = RET_CHUNK
    nc = S // C
    nctx = t_ctx // C

    def cidx(d, c):
        back = jnp.where(c < nctx, nctx - 1 - c, nc - 1 + nctx - c)
        return jnp.where(d == 0, c, back)

    w = D_HEADS * D_KDIM
    return pl.pallas_call(
        _retention_kernel,
        grid=(B // RET_NB, 2, nc),
        in_specs=[pl.BlockSpec((RET_NB, C, w), lambda b, d, c: (b, cidx(d, c), D_Q_OFF // w)),
                  pl.BlockSpec((RET_NB, C, w), lambda b, d, c: (b, cidx(d, c), D_K_OFF // w)),
                  pl.BlockSpec((RET_NB, C, w), lambda b, d, c: (b, cidx(d, c), D_V_OFF // w)),
                  pl.BlockSpec((RET_NB, C, w), lambda b, d, c: (b, cidx(d, c), D_V_OFF // w + 1))],
        out_specs=pl.BlockSpec((1, RET_NB, C, D_DIM), lambda b, d, c: (d, b, cidx(d, c), 0)),
        out_shape=jax.ShapeDtypeStruct((2, B, S, D_DIM), BF16),
        scratch_shapes=[pltpu.VMEM((RET_NB * D_HEADS, D_KDIM, D_VDIM), F32)],
        compiler_params=_cparams(("arbitrary", "arbitrary", "arbitrary")),
        name="retention",
    )(P2, P2, P2, P2)


def _odd_merge_kernel(c_ref, o2_ref, gate_ref, x_ref, mod_ref, rn_ref, wo_ref, gffn_ref, wr_ref, o_ref, *route_refs):
    o = o2_ref[0, 0].astype(F32) + o2_ref[1, 0].astype(F32)
    parts = []
    for h in range(D_HEADS):
        oh = o[:, h * D_VDIM:(h + 1) * D_VDIM]
        ms = jnp.mean(oh * oh, axis=-1, keepdims=True)
        parts.append(oh * lax.rsqrt(ms + NORM_EPS))
    gt = gate_ref[0].astype(F32)
    dl = jnp.concatenate(parts, axis=1) * rn_ref[...] * (gt * jax.nn.sigmoid(gt))
    out = (jnp.dot(c_ref[0], wo_ref[0:C_DIM, :], preferred_element_type=F32)
           + _bdot(dl, wo_ref[C_DIM:C_DIM + D_DIM, :]))
    m = mod_ref[0, 0]
    xn = x_ref[0] + m[2:3] * out
    o_ref[0] = xn
    _route_block(xn, m, gffn_ref, wr_ref, *route_refs)


def _odd_merge(c_lat, o2, P2, X, mod, ret_norm, wo_bf16, t_ctx, ffn_gain, w_router):
    B, S, D = X.shape
    T = S - t_ctx
    off = t_ctx // TB
    r_in, r_out, r_shapes, r_scratch = _route_specs(B, T, D)
    return pl.pallas_call(
        _odd_merge_kernel,
        grid=(B, T // TB),
        in_specs=[pl.BlockSpec((1, TB, C_DIM), lambda b, j: (b, j, 0)),
                  pl.BlockSpec((2, 1, TB, D_DIM), lambda b, j: (0, b, j + off, 0)),
                  pl.BlockSpec((1, TB, D_DIM), lambda b, j: (b, j + off, D_G_OFF // D_DIM)),
                  pl.BlockSpec((1, TB, D), lambda b, j: (b, j + off, 0)),
                  _mod_spec(D, all_lat=True),
                  pl.BlockSpec((1, D_DIM), lambda b, j: (0, 0)),
                  pl.BlockSpec((C_DIM + D_DIM, D), lambda b, j: (0, 0))] + r_in,
        out_specs=[pl.BlockSpec((1, TB, D), lambda b, j: (b, j, 0))] + r_out,
        out_shape=[jax.ShapeDtypeStruct((B, T, D), F32)] + r_shapes,
        scratch_shapes=r_scratch,
        compiler_params=_cparams(("arbitrary", "arbitrary")),
        name="odd_merge_route",
    )(c_lat, o2, P2, X, mod, ret_norm.reshape(1, -1), wo_bf16, ffn_gain.reshape(1, D), w_router)


def _rope_tables(t_ctx, T):
    half = C_HEAD // 4
    inv = ROPE_BASE ** (-jnp.arange(half, dtype=F32) / half)
    t = jnp.arange(T, dtype=jnp.int32)
    ar = (t // GRID_W).astype(F32)[:, None] * inv[None, :]
    ac = (t % GRID_W).astype(F32)[:, None] * inv[None, :]
    z = jnp.zeros_like(ar)
    cos = jnp.concatenate([jnp.cos(ar), jnp.cos(ar), jnp.cos(ac), jnp.cos(ac)], axis=1)
    s1 = jnp.concatenate([-jnp.sin(ar), z, -jnp.sin(ac), z], axis=1)
    s2 = jnp.concatenate([z, jnp.sin(ar), z, jnp.sin(ac)], axis=1)

    def full(tab, ctx_val):
        tab = jnp.concatenate([jnp.full((t_ctx, C_HEAD), ctx_val, F32), tab], axis=0)
        return jnp.tile(tab, (1, LANES // C_HEAD))

    return full(cos, 1.0), full(s1, 0.0), full(s2, 0.0)


def _head_indicator(n, width):
    i = jnp.arange(n) // width
    return (i[:, None] == i[None, :]).astype(BF16)


def kernel(x, c, ctx, c_ctx, ada_w, ada_b, norm_mix, norm_ffn, ev_w_in, ev_w_out, rwkv_shift, rwkv_w0, rwkv_w_up, rwkv_a0, rwkv_a_up, rwkv_g_up, rwkv_k_k, rwkv_k_a, rwkv_r_k, rwkv_ln_g, rwkv_ln_b, pool_w, pool_scale, od_w_in, od_w_out, diff_lambda, diff_subln, ret_norm, moe_router_group, moe_router_expert, moe_w_gate, moe_w_up, moe_w_down, final_norm):
    B, T, D = x.shape
    t_ctx = ctx.shape[1]
    assert ada_w.shape[0] == 2 and t_ctx == TB and T % TB == 0 and T % GRID_W == 0 and B < 16
    assert B % SCAN_NB == 0 and B % RET_NB == 0

    cpad = jnp.zeros((16, D), F32).at[:B].set(c).at[B].set(c_ctx)
    mods = _ada(cpad, ada_w, ada_b)

    def layer_mod(i):
        lat = mods[i, :B]
        cm = jnp.broadcast_to(mods[i, B][None], lat.shape)
        return jnp.stack([cm, lat], axis=1).reshape(B, 2, 6, D)

    def router_w(i):
        w = jnp.concatenate([moe_router_group[i], moe_router_expert[i]], axis=1)
        return jnp.pad(w, ((0, 0), (0, LANES - w.shape[1])))

    ehead = _head_indicator(A_DIM, A_HEAD)

    mod0 = layer_mod(0)
    P = _proj(ctx, x, mod0, norm_mix[0], ev_w_in[0].astype(BF16), BF16)
    zl = jnp.zeros((LORA_W, A_DIM), F32)
    lora_blocks = []
    for d in range(2):
        lora_blocks.append(jnp.concatenate([rwkv_w_up[0, d], zl], axis=0))
        lora_blocks.append(jnp.concatenate([zl, rwkv_a_up[0, d]], axis=0))
    lora_w = jnp.concatenate(lora_blocks, axis=1).astype(BF16)
    wa0 = jnp.concatenate([rwkv_w0[0, 0], rwkv_a0[0, 0], rwkv_w0[0, 1], rwkv_a0[0, 1]]).reshape(1, -1)
    lw, kt, bb, v, kk, r, bonus, g, yb = _even_prep(
        P, t_ctx, rwkv_shift[0], wa0, lora_w, rwkv_g_up[0].astype(BF16), rwkv_k_k[0], rwkv_k_a[0], rwkv_r_k[0],
        pool_w[0].astype(BF16), pool_scale[0], ehead)
    y2 = _rwkv_scan(lw, kt, bb, v, kk, r, t_ctx)
    routed = _even_merge(y2, bonus, g, yb, ctx, x, mod0, rwkv_ln_g[0], rwkv_ln_b[0], ev_w_out[0].astype(BF16), ehead,
                         norm_ffn[0], router_w(0))
    mod1 = layer_mod(1)
    X, P2 = _moe(routed, mod0, 0, moe_w_gate, moe_w_up, moe_w_down, all_lat=False,
                 project=(mod1, norm_mix[1], od_w_in[0].astype(BF16), _rope_tables(t_ctx, T)))

    lam_init = 0.8 - 0.6 * math.exp(-0.3 * 1)
    c_lat = _diff_attn(P2, t_ctx, diff_lambda[0], diff_subln[0], lam_init)
    o2 = _retention(P2, t_ctx)
    routed = _odd_merge(c_lat, o2, P2, X, mod1, ret_norm[0], od_w_out[0].astype(BF16), t_ctx, norm_ffn[1], router_w(1))
    return _moe(routed, mod1, 1, moe_w_gate, moe_w_up, moe_w_down, all_lat=True,
                final_gain=final_norm)
```

```python
import functools
import math

import jax
import jax.numpy as jnp
from jax import lax
from jax.experimental import pallas as pl
from jax.experimental.pallas import tpu as pltpu

F32 = jnp.float32
BF16 = jnp.bfloat16
HI = lax.Precision.HIGHEST

GRID_W = 64
NORM_EPS = 1e-6
ROPE_BASE = 10000.0
A_HEADS, A_HEAD = 8, 64
A_DIM = A_HEADS * A_HEAD
LORA_W, LORA_A, LORA_G = 64, 64, 128
W_DECAY_SCALE = 0.606531
RWKV_GN_EPS = 64e-5
POOL_WINDOWS = (2, 4, 8, 16)
POOL_GROUP = 128
B_DIM = POOL_GROUP * 4
C_HEADS, C_HEAD = 4, 64
C_VDIM = 2 * C_HEAD
C_DIM = C_HEADS * C_VDIM
DIFF_EPS = 1e-5
D_HEADS, D_KDIM, D_VDIM = 4, 64, 128
D_DIM = D_HEADS * D_VDIM
A_K_OFF, A_V_OFF = 0, A_DIM
W_LORA_OFF = 2 * A_DIM
A_LORA_OFF = W_LORA_OFF + LORA_W
A_R_OFF = A_LORA_OFF + LORA_A
G_LORA_OFF = A_R_OFF + A_DIM
POOL_OFF = G_LORA_OFF + LORA_G
EVEN_IN_COLS = POOL_OFF + B_DIM
C_K_OFF = 0
C_V_OFF = C_HEADS * 2 * C_HEAD
D_K_OFF = C_V_OFF + C_DIM
D_V_OFF = D_K_OFF + D_HEADS * D_KDIM
C_Q_OFF = D_V_OFF + D_DIM
D_Q_OFF = C_Q_OFF + C_HEADS * 2 * C_HEAD
D_G_OFF = D_Q_OFF + D_HEADS * D_KDIM
ODD_IN_COLS = D_G_OFF + D_DIM
MOE_GROUPS, MOE_PER_GROUP = 4, 8
MOE_EXPERTS = MOE_GROUPS * MOE_PER_GROUP
MOE_TOP_K = 2
EXPERT_BLOCK = 256

LANES = 128
SUBLANES = 8
HALO = 16
TB = 256
RWKV_CHUNK = 64
RET_CHUNK = 256
RET_NB = 2
SCAN_NB = 4
ATT_TQ = 256
ATT_SUB = 8
VMEM_LIMIT = 56 * 1024 * 1024
DMA_UNROLL = 8
GATHER_SLOTS = 3
TABLE_UNROLL = 32
GATHER_DMA_PRIORITY = 1


def _cparams(sem):
    return pltpu.CompilerParams(dimension_semantics=sem, vmem_limit_bytes=VMEM_LIMIT)


def _modnorm(x, gain, scale, shift, eps=NORM_EPS):
    ms = jnp.mean(x * x, axis=-1, keepdims=True)
    return x * lax.rsqrt(ms + eps) * gain * (1.0 + scale) + shift


def _bdot(a, b, dims=((1,), (0,))):
    return lax.dot_general(a.astype(BF16), b.astype(BF16), (dims, ((), ())), preferred_element_type=F32)


def _bf16_terms(x, terms):
    parts = []
    for _ in range(terms):
        p = x.astype(BF16)
        parts.append(p)
        x = x - p.astype(F32)
    return parts


def _dot_split_lhs(x, w_bf16, terms=2):
    return sum(jnp.dot(p, w_bf16, preferred_element_type=F32) for p in _bf16_terms(x, terms))


def _ada_kernel(c_ref, w_ref, b_ref, o_ref):
    c = c_ref[...]
    s = c * jax.nn.sigmoid(c)
    o_ref[0] = jnp.dot(s, w_ref[0], precision=HI, preferred_element_type=F32) + b_ref[0]


def _ada(cpad, ada_w, ada_b):
    L, D, N = ada_w.shape
    tn = 1536
    return pl.pallas_call(
        _ada_kernel,
        grid=(L, N // tn),
        in_specs=[pl.BlockSpec((16, D), lambda l, n: (0, 0)),
                  pl.BlockSpec((1, D, tn), lambda l, n: (l, 0, n)),
                  pl.BlockSpec((1, 1, tn), lambda l, n: (l, 0, n))],
        out_specs=pl.BlockSpec((1, 16, tn), lambda l, n: (l, 0, n)),
        out_shape=jax.ShapeDtypeStruct((L, 16, N), F32),
        compiler_params=_cparams(("arbitrary", "arbitrary")),
        name="ada_mod",
    )(cpad, ada_w, ada_b.reshape(L, 1, N))


def _mod_spec(D, all_lat=False):
    if all_lat:
        return pl.BlockSpec((1, 1, 6, D), lambda b, j: (b, 1, 0, 0))
    return pl.BlockSpec((1, 1, 6, D), lambda b, j: (b, jnp.minimum(j, 1), 0, 0))


def _rope_plan():
    ranges = [(C_K_OFF, C_V_OFF), (D_K_OFF, D_V_OFF), (C_Q_OFF, D_G_OFF)]
    rope_chunks = tuple(g for lo, hi in ranges for g in range(lo // LANES, hi // LANES))
    kscale_chunks = tuple(range(D_K_OFF // LANES, D_V_OFF // LANES))
    return rope_chunks, kscale_chunks


def _project_block(h, w_ref, o_ref, rope_refs=None, before_piece=None):
    n_out = w_ref.shape[1]
    rope_chunks, kscale_chunks = _rope_plan() if rope_refs is not None else ((), ())
    cw = 512
    for n0 in range(0, n_out, cw):
        if before_piece is not None:
            before_piece(n0 // cw, n_out // cw)
        p = jnp.dot(h, w_ref[:, n0:n0 + cw], preferred_element_type=F32)
        if rope_refs is None:
            o_ref[0, :, n0:n0 + cw] = p.astype(o_ref.dtype)
            continue
        cos_ref, s1_ref, s2_ref = rope_refs
        for q in range(cw // LANES):
            gi = n0 // LANES + q
            sub = p[:, q * LANES:(q + 1) * LANES]
            if gi in rope_chunks:
                sub = (sub * cos_ref[...] + pltpu.roll(sub, LANES - 16, 1) * s1_ref[...]
                       + pltpu.roll(sub, 16, 1) * s2_ref[...])
            if gi in kscale_chunks:
                sub = sub * (D_KDIM ** -0.5)
            o_ref[0, :, n0 + q * LANES:n0 + (q + 1) * LANES] = sub.astype(o_ref.dtype)


def _stream_specs(D):
    return [pl.BlockSpec((1, TB, D), lambda b, j: (b, 0, 0)),
            pl.BlockSpec((1, TB, D), lambda b, j: (b, jnp.maximum(j - 1, 0), 0))]


def _stream_block(ctx_ref, x_ref):
    return jnp.where(pl.program_id(1) == 0, ctx_ref[0], x_ref[0])


def _proj_kernel(ctx_ref, x_ref, mod_ref, g_ref, w_ref, o_ref):
    m = mod_ref[0, 0]
    h = _modnorm(_stream_block(ctx_ref, x_ref), g_ref[...], m[1:2], m[0:1]).astype(BF16)
    _project_block(h, w_ref, o_ref)


def _proj(ctx, x, mod, gain, w_bf16, out_dtype):
    B, T, D = x.shape
    S = ctx.shape[1] + T
    N = w_bf16.shape[1]
    return pl.pallas_call(
        _proj_kernel,
        grid=(B, S // TB),
        in_specs=_stream_specs(D) + [
                  _mod_spec(D),
                  pl.BlockSpec((1, D), lambda b, j: (0, 0)),
                  pl.BlockSpec((D, N), lambda b, j: (0, 0))],
        out_specs=pl.BlockSpec((1, TB, N), lambda b, j: (b, j, 0)),
        out_shape=jax.ShapeDtypeStruct((B, S, N), out_dtype),
        compiler_params=_cparams(("arbitrary", "arbitrary")),
        name="proj",
    )(ctx, x, mod, gain.reshape(1, D), w_bf16)


def _even_prep_kernel(p_ref, pp_ref, pn_ref, sh_ref, wa0_ref, lora_ref, gup_ref, kkg_ref, ka_ref, rk_ref,
                      pw_ref, ps_ref, eh_ref,
                      lw_ref, kt_ref, bb_ref, v_ref, kkn_ref, r_ref, bonus_ref, g_ref, yb_ref,
                      *, nblk, t_ctx, t_lat):
    j = pl.program_id(1)
    prev_ok = j >= 2
    next_ok = jnp.logical_and(j >= 1, j < nblk - 1)
    n_ext = TB + 2 * HALO

    def ext(c0, c1):
        pv = jnp.where(prev_ok, pp_ref[0, :, c0:c1].astype(F32), 0.0)
        nx = jnp.where(next_ok, pn_ref[0, :, c0:c1].astype(F32), 0.0)
        return jnp.concatenate([pv, p_ref[0, :, c0:c1].astype(F32), nx], axis=0)

    def shift3(c0, c1, k0):
        e = ext(c0, c1)
        kern = sh_ref[:, k0:k0 + (c1 - c0)]
        y = kern[0:1] * pltpu.roll(e, 1, 0) + kern[1:2] * e + kern[2:3] * pltpu.roll(e, n_ext - 1, 0)
        return y[HALO:HALO + TB]

    k = shift3(A_K_OFF, A_K_OFF + A_DIM, 0)
    v = shift3(A_V_OFF, A_V_OFF + A_DIM, A_DIM)
    r = shift3(A_R_OFF, A_R_OFF + A_DIM, 2 * A_DIM)

    c = p_ref[0, :, W_LORA_OFF:W_LORA_OFF + LANES].astype(F32)
    lane = lax.broadcasted_iota(jnp.int32, c.shape, 1)
    lin = jnp.where(lane < LORA_W, jnp.tanh(c), c)
    z = _bdot(lin, lora_ref[...]) + wa0_ref[...]

    eh = eh_ref[...]
    kk0 = k * kkg_ref[...]
    ss = _dot_split_lhs(kk0 * kk0, eh)
    kk = kk0 * lax.rsqrt(jnp.maximum(ss, 1e-12))
    kt_sum = None
    for d in range(2):
        zw = z[:, (2 * d) * A_DIM:(2 * d + 1) * A_DIM]
        za = z[:, (2 * d + 1) * A_DIM:(2 * d + 2) * A_DIM]
        a = jax.nn.sigmoid(za)
        kt = k * (1.0 + (a - 1.0) * ka_ref[...])
        lw_ref[d, 0] = -W_DECAY_SCALE * jax.nn.sigmoid(zw)
        kt_ref[d, 0] = kt.astype(kt_ref.dtype)
        bb_ref[d, 0] = (a * kk).astype(bb_ref.dtype)
        kt_sum = kt if kt_sum is None else kt_sum + kt
    v_ref[0] = v.astype(v_ref.dtype)
    kkn_ref[0] = kk.astype(kkn_ref.dtype)
    r_ref[0] = r.astype(r_ref.dtype)
    bonus_ref[0] = (_dot_split_lhs(r * kt_sum * rk_ref[...], eh) * v).astype(bonus_ref.dtype)
    gl = p_ref[0, :, G_LORA_OFF:G_LORA_OFF + LORA_G].astype(F32)
    g_ref[0] = _bdot(jax.nn.sigmoid(gl), gup_ref[...]).astype(g_ref.dtype)

    row = lax.broadcasted_iota(jnp.int32, (TB, POOL_GROUP), 0)
    tpos = row + jnp.maximum(j - 1, 0) * TB
    tseg = jnp.where(j == 0, t_ctx, t_lat)
    for gi, win in enumerate(POOL_WINDOWS):
        c0 = POOL_OFF + gi * POOL_GROUP
        e = ext(c0, c0 + POOL_GROUP)
        s = e + pltpu.roll(e, 1, 0)
        sh = 1
        while 2 * sh < win:
            s = pltpu.roll(s, sh, 0) + pltpu.roll(s, n_ext - sh, 0)
            sh *= 2
        s = s[HALO:HALO + TB]
        lo = jnp.clip(tpos - win // 2, 0, tseg)
        hi = jnp.clip(tpos - win // 2 + win, 0, tseg)
        mean = s / (hi - lo).astype(F32)
        dd = mean - p_ref[0, :, c0:c0 + POOL_GROUP].astype(F32)
        y = _bdot(dd, pw_ref[gi])
        yb_ref[0, :, gi * POOL_GROUP:(gi + 1) * POOL_GROUP] = (
            y * ps_ref[:, gi * POOL_GROUP:(gi + 1) * POOL_GROUP]).astype(yb_ref.dtype)


def _even_prep(P, t_ctx, shift_k, wa0, lora_w, g_up, k_k, k_a, r_k, pool_w, pool_scale, ehead):
    B, S, N = P.shape
    nblk = S // TB
    hb = TB // HALO
    const2 = lambda b, j: (0, 0)
    in_specs = [
        pl.BlockSpec((1, TB, N), lambda b, j: (b, j, 0)),
        pl.BlockSpec((1, HALO, N), lambda b, j: (b, jnp.maximum(j * hb - 1, 0), 0)),
        pl.BlockSpec((1, HALO, N), lambda b, j: (b, jnp.minimum((j + 1) * hb, S // HALO - 1), 0)),
        pl.BlockSpec((3, 3 * A_DIM), const2),
        pl.BlockSpec((1, 4 * A_DIM), const2),
        pl.BlockSpec((LANES, 4 * A_DIM), const2),
        pl.BlockSpec((LORA_G, A_DIM), const2),
        pl.BlockSpec((1, A_DIM), const2),
        pl.BlockSpec((1, A_DIM), const2),
        pl.BlockSpec((1, A_DIM), const2),
        pl.BlockSpec((4, POOL_GROUP, POOL_GROUP), lambda b, j: (0, 0, 0)),
        pl.BlockSpec((1, B_DIM), const2),
        pl.BlockSpec((A_DIM, A_DIM), const2),
    ]
    two_f32 = jax.ShapeDtypeStruct((2, B, S, A_DIM), F32)
    two = jax.ShapeDtypeStruct((2, B, S, A_DIM), BF16)
    one = jax.ShapeDtypeStruct((B, S, A_DIM), BF16)
    two_spec = pl.BlockSpec((2, 1, TB, A_DIM), lambda b, j: (0, b, j, 0))
    one_spec = pl.BlockSpec((1, TB, A_DIM), lambda b, j: (b, j, 0))
    return pl.pallas_call(
        functools.partial(_even_prep_kernel, nblk=nblk, t_ctx=t_ctx, t_lat=S - t_ctx),
        grid=(B, nblk),
        in_specs=in_specs,
        out_specs=[two_spec] * 3 + [one_spec] * 6,
        out_shape=[two_f32, two, two] + [one] * 6,
        compiler_params=_cparams(("arbitrary", "arbitrary")),
        name="even_prep",
    )(P, P, P, shift_k, wa0, lora_w, g_up, k_k.reshape(1, -1), k_a.reshape(1, -1), r_k.reshape(1, -1),
      pool_w, pool_scale.reshape(1, -1), ehead)


def _rwkv_scan_kernel(lw_ref, kt_ref, bb_ref, v_ref, kk_ref, r_ref, y_ref, s_ref):
    d = pl.program_id(1)
    c = pl.program_id(2)

    @pl.when(c == 0)
    def _():
        s_ref[...] = jnp.zeros_like(s_ref)

    C = RWKV_CHUNK
    tt = lax.broadcasted_iota(jnp.int32, (C, C), 0)
    ii = lax.broadcasted_iota(jnp.int32, (C, C), 1)
    rel = (tt - ii) * (1 - 2 * d)
    strict = rel > 0
    incl = rel >= 0
    t2 = lax.broadcasted_iota(jnp.int32, (C, 2 * C), 0)
    i2 = lax.broadcasted_iota(jnp.int32, (C, 2 * C), 1) & (C - 1)
    incl2 = (t2 - i2) * (1 - 2 * d) >= 0
    tri = jnp.where(incl, 1.0, 0.0).astype(BF16)
    n_dbl = int(math.log2(C))
    hsl = [slice(h * A_HEAD, (h + 1) * A_HEAD) for h in range(A_HEADS)]
    ar, bk, vh, bkp, ptot = [], [], [], [], []
    for bi in range(SCAN_NB):
        lw = lw_ref[0, bi]
        cum = sum(jnp.dot(tri, p, preferred_element_type=F32) for p in _bf16_terms(lw, 3))
        tot = jnp.sum(lw, axis=0, keepdims=True)
        p_inv = jnp.exp(-cum)
        p_rest = jnp.exp(tot - cum)
        p_tot = jnp.exp(tot)
        bb = bb_ref[0, bi]
        kt = kt_ref[0, bi]
        at_all = (-kk_ref[bi] * jnp.exp(cum - lw)).astype(BF16)
        rt_all = (r_ref[bi] * jnp.exp(cum)).astype(BF16)
        bt_all = (bb * p_inv).astype(BF16)
        kti_all = (kt * p_inv).astype(BF16)
        bp_all = (bb * p_rest).astype(BF16)
        kp_all = (kt * p_rest).astype(BF16)
        v_all = v_ref[bi].astype(BF16)
        for s in hsl:
            ar.append(jnp.concatenate([at_all[:, s], rt_all[:, s]], axis=0))
            bk.append(jnp.concatenate([bt_all[:, s], kti_all[:, s]], axis=0))
            bkp.append(jnp.concatenate([bp_all[:, s], kp_all[:, s]], axis=0))
            vh.append(v_all[:, s])
            ptot.append(p_tot[:, s])
    chains = range(SCAN_NB * A_HEADS)
    s0 = [s_ref[n] for n in chains]
    m = [_bdot(ar[n], bk[n], ((1,), (1,))) for n in chains]
    arh = [_bdot(ar[n], s0[n], ((1,), (1,))) for n in chains]
    lp = [jnp.where(strict, m[n][:C, :C], 0.0) for n in chains]
    a_ak = [jnp.where(strict, m[n][:C, C:], 0.0) for n in chains]
    a_r = [jnp.where(incl2, m[n][C:, :], 0.0) for n in chains]
    u = [arh[n][:C] + _bdot(a_ak[n], vh[n]) for n in chains]
    for q in range(n_dbl):
        u = [u[n] + _bdot(lp[n], u[n]) for n in chains]
        if q < n_dbl - 1:
            lp = [_bdot(lp[n], lp[n]) for n in chains]
    uv = [jnp.concatenate([u[n].astype(BF16), vh[n]], axis=0) for n in chains]
    outs = [arh[n][C:] + _bdot(a_r[n], uv[n]) for n in chains]
    s_new = [s0[n] * ptot[n] + _bdot(uv[n], bkp[n], ((0,), (0,))) for n in chains]
    for n in chains:
        s_ref[n] = s_new[n]
    for bi in range(SCAN_NB):
        y_ref[0, bi] = jnp.concatenate(outs[bi * A_HEADS:(bi + 1) * A_HEADS], axis=1).astype(y_ref.dtype)


def _rwkv_scan(lw, kt, bb, v, kk, r, t_ctx):
    _, B, S, _ = lw.shape
    C = RWKV_CHUNK
    nc = S // C
    nctx = t_ctx // C

    def cidx(d, c):
        back = jnp.where(c < nctx, nctx - 1 - c, nc - 1 + nctx - c)
        return jnp.where(d == 0, c, back)

    two_spec = pl.BlockSpec((1, SCAN_NB, C, A_DIM), lambda b, d, c: (d, b, cidx(d, c), 0))
    one_spec = pl.BlockSpec((SCAN_NB, C, A_DIM), lambda b, d, c: (b, cidx(d, c), 0))
    return pl.pallas_call(
        _rwkv_scan_kernel,
        grid=(B // SCAN_NB, 2, nc),
        in_specs=[two_spec] * 3 + [one_spec] * 3,
        out_specs=two_spec,
        out_shape=jax.ShapeDtypeStruct((2, B, S, A_DIM), BF16),
        scratch_shapes=[pltpu.VMEM((SCAN_NB * A_HEADS, A_HEAD, A_HEAD), F32)],
        compiler_params=_cparams(("arbitrary", "arbitrary", "arbitrary")),
        name="rwkv_scan",
    )(lw, kt, bb, v, kk, r)


def _even_merge_kernel(y2_ref, bonus_ref, g_ref, yb_ref, ctx_ref, x_ref, mod_ref, lng_ref, lnb_ref, wo_ref, eh_ref,
                       gffn_ref, wr_ref, o_ref, *route_refs):
    y = y2_ref[0, 0].astype(F32) + y2_ref[1, 0].astype(F32)
    eh = eh_ref[...]
    mu = _dot_split_lhs(y, eh) * (1.0 / A_HEAD)
    yc = y - mu
    var = _dot_split_lhs(yc * yc, eh) * (1.0 / A_HEAD)
    yn = yc * lax.rsqrt(var + RWKV_GN_EPS) * lng_ref[...] + lnb_ref[...]
    ya = (yn + bonus_ref[0]) * g_ref[0]
    out = _bdot(ya, wo_ref[0:A_DIM, :]) + _bdot(yb_ref[0], wo_ref[A_DIM:A_DIM + B_DIM, :])
    m = mod_ref[0, 0]
    xn = _stream_block(ctx_ref, x_ref) + m[2:3] * out
    o_ref[0] = xn
    _route_block(xn, m, gffn_ref, wr_ref, *route_refs)


def _even_merge(y2, bonus, g, yb, ctx, x, mod, ln_g, ln_b, wo_bf16, ehead, ffn_gain, w_router):
    B, T, D = x.shape
    S = ctx.shape[1] + T
    r_in, r_out, r_shapes, r_scratch = _route_specs(B, S, D)
    one_spec = pl.BlockSpec((1, TB, A_DIM), lambda b, j: (b, j, 0))
    const2 = lambda b, j: (0, 0)
    return pl.pallas_call(
        _even_merge_kernel,
        grid=(B, S // TB),
        in_specs=[pl.BlockSpec((2, 1, TB, A_DIM), lambda b, j: (0, b, j, 0)), one_spec, one_spec, one_spec]
        + _stream_specs(D) + [_mod_spec(D),
                  pl.BlockSpec((1, A_DIM), const2), pl.BlockSpec((1, A_DIM), const2),
                  pl.BlockSpec((A_DIM + B_DIM, D), const2), pl.BlockSpec((A_DIM, A_DIM), const2)] + r_in,
        out_specs=[pl.BlockSpec((1, TB, D), lambda b, j: (b, j, 0))] + r_out,
        out_shape=[jax.ShapeDtypeStruct((B, S, D), F32)] + r_shapes,
        scratch_shapes=r_scratch,
        compiler_params=_cparams(("arbitrary", "arbitrary")),
        name="even_merge_route",
    )(y2, bonus, g, yb, ctx, x, mod, ln_g.reshape(1, -1), ln_b.reshape(1, -1), wo_bf16, ehead,
      ffn_gain.reshape(1, D), w_router)


def _route_block(xn, m, g_ref, wr_ref, hf_ref, meta_ref, metat_ref, cnt_ref, carry_ref):
    first = jnp.logical_and(pl.program_id(0) == 0, pl.program_id(1) == 0)

    @pl.when(first)
    def _():
        carry_ref[...] = jnp.zeros_like(carry_ref)

    hf = _modnorm(xn, g_ref[...], m[4:5], m[3:4])
    hf_ref[0] = hf
    h0, h1 = _bf16_terms(hf, 2)
    w0, w1 = _bf16_terms(wr_ref[...], 2)
    logits = (jnp.dot(h0, w0, preferred_element_type=F32) + jnp.dot(h0, w1, preferred_element_type=F32)
              + jnp.dot(h1, w0, preferred_element_type=F32))
    lane = lax.broadcasted_iota(jnp.int32, logits.shape, 1).astype(F32)
    neg = -jnp.inf
    far = float(LANES)
    lg = jnp.where(lane < MOE_GROUPS, logits, neg)
    mg = jnp.max(lg, axis=1, keepdims=True)
    grp = jnp.min(jnp.where(lg == mg, lane, far), axis=1, keepdims=True)
    gw = 1.0 / jnp.sum(jnp.exp(lg - mg), axis=1, keepdims=True)
    el = lane - MOE_GROUPS
    in_grp = jnp.logical_and(el >= grp * MOE_PER_GROUP, el < (grp + 1.0) * MOE_PER_GROUP)
    le = jnp.where(in_grp, logits, neg)
    m0 = jnp.max(le, axis=1, keepdims=True)
    i0 = jnp.min(jnp.where(le == m0, lane, far), axis=1, keepdims=True)
    le1 = jnp.where(lane == i0, neg, le)
    m1 = jnp.max(le1, axis=1, keepdims=True)
    i1 = jnp.min(jnp.where(le1 == m1, lane, far), axis=1, keepdims=True)
    p1 = jnp.exp(m1 - m0)
    gate0 = gw / (1.0 + p1)
    gate1 = gw * p1 / (1.0 + p1)
    e0 = i0 - MOE_GROUPS
    e1 = i1 - MOE_GROUPS
    oh0 = (lane == e0).astype(F32)
    oh1 = (lane == e1).astype(F32)
    cnt = oh0 + oh1
    rr = lax.broadcasted_iota(jnp.int32, (TB, TB), 0)
    cc = lax.broadcasted_iota(jnp.int32, (TB, TB), 1)
    before = _bdot((cc < rr).astype(F32), cnt) + carry_ref[...]
    rank0 = jnp.sum(oh0 * before, axis=1, keepdims=True)
    rank1 = jnp.sum(oh1 * before, axis=1, keepdims=True)
    carry_ref[...] = carry_ref[...] + jnp.sum(cnt, axis=0, keepdims=True)
    cnt_ref[...] = carry_ref[...]
    meta = jnp.where(lane == 0, e0, 0.0)
    meta = jnp.where(lane == 1, e1, meta)
    meta = jnp.where(lane == 2, rank0, meta)
    meta = jnp.where(lane == 3, rank1, meta)
    meta = jnp.where(lane == 4, gate0, meta)
    meta = jnp.where(lane == 5, gate1, meta)
    meta_ref[0] = meta
    metat_ref[:, 0, 0, :] = meta.T[0:SUBLANES]


def _route_specs(B, S, D):
    in_specs = [pl.BlockSpec((1, D), lambda b, j: (0, 0)), pl.BlockSpec((D, LANES), lambda b, j: (0, 0))]
    out_specs = [pl.BlockSpec((1, TB, D), lambda b, j: (b, j, 0)),
                 pl.BlockSpec((1, TB, LANES), lambda b, j: (b, j, 0)),
                 pl.BlockSpec((SUBLANES, 1, 1, TB), lambda b, j: (0, b * (S // TB) + j, 0, 0)),
                 pl.BlockSpec((1, LANES), lambda b, j: (0, 0))]
    out_shapes = [jax.ShapeDtypeStruct((B, S, D), F32), jax.ShapeDtypeStruct((B, S, LANES), F32),
                  jax.ShapeDtypeStruct((SUBLANES, B * (S // TB), 1, TB), F32), jax.ShapeDtypeStruct((1, LANES), F32)]
    return in_specs, out_specs, out_shapes, [pltpu.VMEM((1, LANES), F32)]


def _routing_tables(metat, counts):
    n_tok = metat.shape[1] * TB
    mi = metat.reshape(SUBLANES, n_tok)[0:4].astype(jnp.int32)
    cnt = counts[0, :MOE_EXPERTS].astype(jnp.int32)
    padded = (cnt + EXPERT_BLOCK - 1) // EXPERT_BLOCK * EXPERT_BLOCK
    pad_end = jnp.cumsum(padded)
    pad_start = pad_end - padded
    dest = jnp.concatenate([pad_start[mi[0]] + mi[2], pad_start[mi[1]] + mi[3]])
    n_blocks = -(-(n_tok * MOE_TOP_K) // EXPERT_BLOCK) + MOE_EXPERTS
    starts = jnp.arange(n_blocks, dtype=jnp.int32) * EXPERT_BLOCK
    block_e = jnp.minimum(jnp.sum((pad_end[None, :] <= starts[:, None]).astype(jnp.int32), axis=1), MOE_EXPERTS - 1)
    n_used = (pad_end[-1] // EXPERT_BLOCK).astype(jnp.int32).reshape(1)
    present = cnt > 0
    ids = jnp.arange(MOE_EXPERTS, dtype=jnp.int32)
    later = jnp.where(jnp.logical_and(present[None, :], ids[None, :] > ids[:, None]), ids[None, :], MOE_EXPERTS)
    nxt = jnp.min(later, axis=1)
    nxt = jnp.where(nxt == MOE_EXPERTS, -1, nxt)
    wslot = (jnp.cumsum(present.astype(jnp.int32)) - 1) % 2
    pads = jnp.concatenate([pad_start + cnt, pad_end, wslot, nxt]).astype(jnp.int32)
    return dest, block_e, n_used, pads, n_blocks


def _expert_kernel(be_ref, nu_ref, dest_ref, pads_ref, hf_ref, wg_ref, wu_ref, wd_ref, o_ref, xbuf, wg_bf, wu_bf, wd_bf,
                   wg_f32, wu_f32, wd_f32, rows_ref, sem, wsem, *, n_tok, layer):
    j = pl.program_id(0)
    n_used = nu_ref[0]
    EB = EXPERT_BLOCK
    E = MOE_EXPERTS

    def weight_copies(e, ws):
        return [pltpu.make_async_copy(src.at[layer, e], dst.at[ws], wsem.at[ws])
                for src, dst in ((wg_ref, wg_f32), (wu_ref, wu_f32), (wd_ref, wd_f32))]

    def row_copy(blk, slot, r):
        row = rows_ref[blk * EB + r]
        return pltpu.make_async_copy(hf_ref.at[pl.ds(row, 1), :], xbuf.at[slot, pl.ds(r, 1), :], sem.at[slot])

    def issue(blk, slot):
        def body(r, carry):
            row_copy(blk, slot, r).start()
            return carry
        lax.fori_loop(0, EB, body, 0, unroll=DMA_UNROLL)

    @pl.when(j == 0)
    def _():
        fill_mask = (1 << (n_tok.bit_length() - 1)) - 1

        def fill(i, carry):
            rows_ref[i] = jnp.bitwise_and(i, fill_mask)
            return carry
        for e in range(MOE_EXPERTS):
            lax.fori_loop(pads_ref[e], pads_ref[MOE_EXPERTS + e], fill, 0)

        def invert(t, carry):
            rows_ref[dest_ref[t]] = t
            rows_ref[dest_ref[n_tok + t]] = t
            return carry
        lax.fori_loop(0, n_tok, invert, 0, unroll=TABLE_UNROLL)
        issue(0, 0)

        @pl.when(n_used > 1)
        def _():
            issue(1, 1)

    def compute(prefetch_ahead):
        slot = lax.rem(j, GATHER_SLOTS)

        def body(r, carry):
            row_copy(j, slot, r).wait()
            return carry
        lax.fori_loop(0, EB, body, 0, unroll=DMA_UNROLL)

        e_cur = be_ref[j]

        @pl.when(jnp.logical_or(j == 0, e_cur != be_ref[jnp.maximum(j - 1, 0)]))
        def _():
            ws = pads_ref[2 * E + e_cur]
            e_next = pads_ref[3 * E + e_cur]

            @pl.when(j == 0)
            def _():
                for cp in weight_copies(e_cur, ws):
                    cp.start()
            for cp in weight_copies(e_cur, ws):
                cp.wait()
            wg_bf[...] = wg_f32[ws].astype(BF16)
            wu_bf[...] = wu_f32[ws].astype(BF16)
            wd_bf[...] = wd_f32[ws].astype(BF16)

            @pl.when(e_next >= 0)
            def _():
                for cp in weight_copies(e_next, 1 - ws):
                    cp.start()

        xb = xbuf[slot].astype(BF16)
        if prefetch_ahead:
            ahead = lax.rem(j + 2, GATHER_SLOTS)
            for r in range(EB):
                row_copy(j + 2, ahead, r).start(priority=GATHER_DMA_PRIORITY)
        hg = jnp.dot(xb, wg_bf[...], preferred_element_type=F32)
        hu = jnp.dot(xb, wu_bf[...], preferred_element_type=F32)
        hm = (hg * jax.nn.sigmoid(hg) * hu).astype(BF16)
        o_ref[...] = jnp.dot(hm, wd_bf[...], preferred_element_type=F32)

    @pl.when(j + 2 < n_used)
    def _():
        compute(True)

    @pl.when(jnp.logical_and(j < n_used, j + 2 >= n_used))
    def _():
        compute(False)

    @pl.when(j >= n_used)
    def _():
        o_ref[...] = jnp.zeros_like(o_ref)


def _experts(hf2d, dest, block_e, n_used, pads, n_blocks, layer, w_gate, w_up, w_down):
    N, D = hf2d.shape
    _, E, _, De = w_gate.shape
    EB = EXPERT_BLOCK
    grid_spec = pltpu.PrefetchScalarGridSpec(
        num_scalar_prefetch=4,
        grid=(n_blocks,),
        in_specs=[pl.BlockSpec(memory_space=pl.ANY)] * 4,
        out_specs=pl.BlockSpec((EB, D), lambda j, be, nu, ds, pd: (j, 0)),
        scratch_shapes=[pltpu.VMEM((GATHER_SLOTS, EB, D), F32), pltpu.VMEM((D, De), BF16), pltpu.VMEM((D, De), BF16),
                        pltpu.VMEM((De, D), BF16), pltpu.VMEM((2, D, De), F32), pltpu.VMEM((2, D, De), F32),
                        pltpu.VMEM((2, De, D), F32), pltpu.SMEM((n_blocks * EB,), jnp.int32),
                        pltpu.SemaphoreType.DMA((GATHER_SLOTS,)), pltpu.SemaphoreType.DMA((2,))],
    )
    return pl.pallas_call(
        functools.partial(_expert_kernel, n_tok=N, layer=layer),
        grid_spec=grid_spec,
        out_shape=jax.ShapeDtypeStruct((n_blocks * EB, D), F32),
        compiler_params=_cparams(("arbitrary",)),
        name="moe_experts",
    )(block_e, n_used, dest, pads, hf2d, w_gate, w_up, w_down)


def _combine_kernel(dest_ref, yb_ref, x_ref, meta_ref, mod_ref, *rest, nj, project):
    if project:
        mod1_ref, g1_ref, w_ref, cos_ref, s1_ref, s2_ref, o_ref, p_ref, ybuf, sem = rest
    else:
        fn_ref, o_ref, ybuf, sem = rest
    b = pl.program_id(0)
    j = pl.program_id(1)
    step = b * nj + j
    n_steps = pl.num_programs(0) * nj
    n_tok = n_steps * TB

    def row_copy(st, slot, r, k):
        dst_row = dest_ref[k * n_tok + st * TB + r]
        return pltpu.make_async_copy(yb_ref.at[pl.ds(dst_row, 1), :], ybuf.at[slot, k, pl.ds(r, 1), :], sem.at[slot])

    def issue(st, slot):
        def body(r, carry):
            row_copy(st, slot, r, 0).start()
            row_copy(st, slot, r, 1).start()
            return carry
        lax.fori_loop(0, TB, body, 0, unroll=DMA_UNROLL)

    @pl.when(step == 0)
    def _():
        issue(0, 0)
        issue(1, 1)

    slot = lax.rem(step, GATHER_SLOTS)

    def body(prefetch_ahead):
        def wbody(r, carry):
            row_copy(step, slot, r, 0).wait()
            row_copy(step, slot, r, 1).wait()
            return carry
        lax.fori_loop(0, TB, wbody, 0, unroll=DMA_UNROLL)
        meta = meta_ref[0]
        f = meta[:, 4:5] * ybuf[slot, 0] + meta[:, 5:6] * ybuf[slot, 1]
        m = mod_ref[0, 0]
        xn = x_ref[0] + m[5:6] * f
        ahead = lax.rem(step + 2, GATHER_SLOTS)

        def issue_rows(i, n):
            if prefetch_ahead:
                for r in range(i * TB // n, (i + 1) * TB // n):
                    row_copy(step + 2, ahead, r, 0).start()
                    row_copy(step + 2, ahead, r, 1).start()

        if project:
            o_ref[0] = xn
            m1 = mod1_ref[0, 0]
            h = _modnorm(xn, g1_ref[...], m1[1:2], m1[0:1]).astype(BF16)
            _project_block(h, w_ref, p_ref, (cos_ref, s1_ref, s2_ref), before_piece=issue_rows)
        else:
            issue_rows(0, 1)
            ms = jnp.mean(xn * xn, axis=-1, keepdims=True)
            o_ref[0] = xn * lax.rsqrt(ms + NORM_EPS) * fn_ref[...]

    @pl.when(step + 2 < n_steps)
    def _():
        body(True)

    @pl.when(step + 2 >= n_steps)
    def _():
        body(False)


def _combine(dest, yb, X, meta, mod, all_lat, final_gain=None, project=None):
    B, S, D = X.shape
    nj = S // TB

    def mspec(lat_only):
        if lat_only:
            return pl.BlockSpec((1, 1, 6, D), lambda b, j, ds: (b, 1, 0, 0))
        return pl.BlockSpec((1, 1, 6, D), lambda b, j, ds: (b, jnp.minimum(j, 1), 0, 0))

    row_spec = pl.BlockSpec((1, TB, D), lambda b, j, ds: (b, j, 0))
    in_specs = [pl.BlockSpec(memory_space=pl.ANY), row_spec,
                pl.BlockSpec((1, TB, LANES), lambda b, j, ds: (b, j, 0)), mspec(all_lat)]
    args = [dest, yb, X, meta, mod]
    if project is None:
        in_specs.append(pl.BlockSpec((1, D), lambda b, j, ds: (0, 0)))
        args.append(final_gain.reshape(1, D))
        out_specs = row_spec
        out_shape = jax.ShapeDtypeStruct((B, S, D), F32)
    else:
        mod1, gain1, w_bf16, (cos, s1, s2) = project
        N = w_bf16.shape[1]
        in_specs += [mspec(False), pl.BlockSpec((1, D), lambda b, j, ds: (0, 0)),
                     pl.BlockSpec((D, N), lambda b, j, ds: (0, 0))]
        in_specs += [pl.BlockSpec((TB, LANES), lambda b, j, ds: (j, 0))] * 3
        args += [mod1, gain1.reshape(1, D), w_bf16, cos, s1, s2]
        out_specs = [row_spec, pl.BlockSpec((1, TB, N), lambda b, j, ds: (b, j, 0))]
        out_shape = [jax.ShapeDtypeStruct((B, S, D), F32), jax.ShapeDtypeStruct((B, S, N), BF16)]
    grid_spec = pltpu.PrefetchScalarGridSpec(
        num_scalar_prefetch=1,
        grid=(B, nj),
        in_specs=in_specs,
        out_specs=out_specs,
        scratch_shapes=[pltpu.VMEM((GATHER_SLOTS, 2, TB, D), F32), pltpu.SemaphoreType.DMA((GATHER_SLOTS,))],
    )
    return pl.pallas_call(
        functools.partial(_combine_kernel, nj=nj, project=project is not None),
        grid_spec=grid_spec,
        out_shape=out_shape,
        compiler_params=_cparams(("arbitrary", "arbitrary")),
        name="moe_combine_final" if project is None else "moe_combine_proj",
    )(*args)


def _moe(routed, mod, layer, w_gate, w_up, w_down, all_lat, final_gain=None, project=None):
    X, hf, meta, metat, counts = routed
    B, S, D = X.shape
    dest, block_e, n_used, pads, n_blocks = _routing_tables(metat, counts)
    yb = _experts(hf.reshape(B * S, D), dest, block_e, n_used, pads, n_blocks, layer, w_gate, w_up, w_down)
    return _combine(dest, yb, X, meta, mod, all_lat, final_gain, project)


def _diff_attn_kernel(*refs, lam_init):
    q_refs = refs[:ATT_SUB]
    k_ref, v_ref, dl_ref, sg_ref, o_ref, vx_ref = refs[ATT_SUB:]

    @pl.when(pl.program_id(2) == 0)
    def _():
        vlane = lax.broadcasted_iota(jnp.int32, (vx_ref.shape[0], LANES), 1)
        vx_ref[:, 0:C_VDIM] = v_ref[0]
        vx_ref[:, C_VDIM:C_VDIM + LANES] = jnp.where(vlane == 0, 1.0, 0.0).astype(BF16)

    k = k_ref[0]
    dl = dl_ref[...]
    lam = (jnp.exp(jnp.sum(dl[0:1] * dl[1:2], axis=1, keepdims=True))
           - jnp.exp(jnp.sum(dl[2:3] * dl[3:4], axis=1, keepdims=True)) + lam_init)

    def scores(t):
        q = q_refs[t][0] * (C_HEAD ** -0.5)
        lane = lax.broadcasted_iota(jnp.int32, q.shape, 1)
        zero = jnp.zeros_like(q)
        qs = (jnp.where(lane < C_HEAD, q, zero), jnp.where(lane >= C_HEAD, q, zero))
        return [lax.dot_general(qq, k, (((1,), (1,)), ((), ())), preferred_element_type=F32) for qq in qs]

    def finish(t, s):
        e = [jnp.exp(si - jnp.max(si, axis=1, keepdims=True)).astype(BF16) for si in s]
        oz = [jnp.dot(ei, vx_ref[...], preferred_element_type=F32) for ei in e]
        on = [ozi[:, 0:C_VDIM] * (1.0 / ozi[:, C_VDIM:C_VDIM + 1]) for ozi in oz]
        o = on[0] - lam * on[1]
        ms = jnp.mean(o * o, axis=-1, keepdims=True)
        o_ref[0, t * ATT_TQ:(t + 1) * ATT_TQ, :] = (
            o * lax.rsqrt(ms + DIFF_EPS) * sg_ref[...] * (1.0 - lam_init)).astype(o_ref.dtype)

    s_next = scores(0)
    for t in range(ATT_SUB):
        s_cur = s_next
        if t + 1 < ATT_SUB:
            s_next = scores(t + 1)
        finish(t, s_cur)


def _diff_attn(P2, t_ctx, diff_lambda, diff_subln, lam_init):
    B, S, _ = P2.shape
    T = S - t_ctx
    qoff = t_ctx // ATT_TQ
    step = ATT_SUB * ATT_TQ

    def q_spec(t):
        return pl.BlockSpec((1, ATT_TQ, LANES), lambda b, h, q: (b, q * ATT_SUB + t + qoff, C_Q_OFF // LANES + h))

    return pl.pallas_call(
        functools.partial(_diff_attn_kernel, lam_init=lam_init),
        grid=(B, C_HEADS, T // step),
        in_specs=[q_spec(t) for t in range(ATT_SUB)] + [
                  pl.BlockSpec((1, S, LANES), lambda b, h, q: (b, 0, C_K_OFF // LANES + h)),
                  pl.BlockSpec((1, S, LANES), lambda b, h, q: (b, 0, C_V_OFF // LANES + h)),
                  pl.BlockSpec((4, C_HEAD), lambda b, h, q: (0, 0)),
                  pl.BlockSpec((1, C_VDIM), lambda b, h, q: (0, 0))],
        out_specs=pl.BlockSpec((1, step, LANES), lambda b, h, q: (b, q, h)),
        out_shape=jax.ShapeDtypeStruct((B, T, C_DIM), BF16),
        scratch_shapes=[pltpu.VMEM((S, C_VDIM + LANES), BF16)],
        compiler_params=_cparams(("arbitrary", "arbitrary", "arbitrary")),
        name="diff_attn",
    )(*([P2] * ATT_SUB), P2, P2, diff_lambda, diff_subln.reshape(1, -1))


def _retention_kernel(q_ref, k_ref, v0_ref, v1_ref, o_ref, r_ref):
    d = pl.program_id(1)
    c = pl.program_id(2)

    @pl.when(c == 0)
    def _():
        r_ref[...] = jnp.zeros_like(r_ref)

    C = RET_CHUNK
    ti = lax.broadcasted_iota(jnp.int32, (C, C), 0)
    si = lax.broadcasted_iota(jnp.int32, (C, C), 1)
    rel = ((ti - si) * (1 - 2 * d)).astype(F32)
    col = lax.broadcasted_iota(jnp.int32, (C, 1), 0)
    pos = jnp.where(d == 0, col, C - 1 - col).astype(F32)
    chains = [(bi, h) for bi in range(RET_NB) for h in range(D_HEADS)]
    r0s = [r_ref[bi * D_HEADS + h] for bi, h in chains]
    decs = []
    for h in range(D_HEADS):
        lg_f = math.log(1.0 - 2.0 ** (-5.0 - h))
        lg_b = math.log(1.0 - 2.0 ** (-5.0 - (D_HEADS - 1 - h)))
        lg = jnp.where(d == 0, jnp.full((1, 1), lg_f, F32), jnp.full((1, 1), lg_b, F32))
        decs.append((jnp.where(rel >= 0, jnp.exp(lg * jnp.maximum(rel, 0.0)), 0.0),
                     jnp.exp(lg * (pos + 1.0)), jnp.exp(lg * (C - 1.0 - pos)), jnp.exp(lg * C)))
    outs, r_new = [], []
    for n, (bi, h) in enumerate(chains):
        inner_dec, q_dec, k_dec, c_dec = decs[h]
        q = q_ref[bi, :, h * D_KDIM:(h + 1) * D_KDIM]
        k = k_ref[bi, :, h * D_KDIM:(h + 1) * D_KDIM]
        vref = v0_ref if h < 2 else v1_ref
        v = vref[bi, :, (h % 2) * D_VDIM:(h % 2 + 1) * D_VDIM]
        inner = lax.dot_general(q, k, (((1,), (1,)), ((), ())), preferred_element_type=F32) * inner_dec
        outs.append(_bdot(inner, v) + _bdot(q, r0s[n]) * q_dec)
        r_new.append(r0s[n] * c_dec + _bdot(k.astype(F32) * k_dec, v, ((0,), (0,))))
    for n in range(len(chains)):
        r_ref[n] = r_new[n]
    for bi in range(RET_NB):
        o_ref[0, bi] = jnp.concatenate(outs[bi * D_HEADS:(bi + 1) * D_HEADS], axis=1).astype(o_ref.dtype)


def _retention(P2, t_ctx):
    B, S, _ = P2.shape
    C = RET_CHUNK
    nc = S // C
    nctx = t_ctx // C

    def cidx(d, c):
        back = jnp.where(c < nctx, nctx - 1 - c, nc - 1 + nctx - c)
        return jnp.where(d == 0, c, back)

    w = D_HEADS * D_KDIM
    return pl.pallas_call(
        _retention_kernel,
        grid=(B // RET_NB, 2, nc),
        in_specs=[pl.BlockSpec((RET_NB, C, w), lambda b, d, c: (b, cidx(d, c), D_Q_OFF // w)),
                  pl.BlockSpec((RET_NB, C, w), lambda b, d, c: (b, cidx(d, c), D_K_OFF // w)),
                  pl.BlockSpec((RET_NB, C, w), lambda b, d, c: (b, cidx(d, c), D_V_OFF // w)),
                  pl.BlockSpec((RET_NB, C, w), lambda b, d, c: (b, cidx(d, c), D_V_OFF // w + 1))],
        out_specs=pl.BlockSpec((1, RET_NB, C, D_DIM), lambda b, d, c: (d, b, cidx(d, c), 0)),
        out_shape=jax.ShapeDtypeStruct((2, B, S, D_DIM), BF16),
        scratch_shapes=[pltpu.VMEM((RET_NB * D_HEADS, D_KDIM, D_VDIM), F32)],
        compiler_params=_cparams(("arbitrary", "arbitrary", "arbitrary")),
        name="retention",
    )(P2, P2, P2, P2)


def _odd_merge_kernel(c_ref, o2_ref, gate_ref, x_ref, mod_ref, rn_ref, wo_ref, gffn_ref, wr_ref, o_ref, *route_refs):
    o = o2_ref[0, 0].astype(F32) + o2_ref[1, 0].astype(F32)
    parts = []
    for h in range(D_HEADS):
        oh = o[:, h * D_VDIM:(h + 1) * D_VDIM]
        ms = jnp.mean(oh * oh, axis=-1, keepdims=True)
        parts.append(oh * lax.rsqrt(ms + NORM_EPS))
    gt = gate_ref[0].astype(F32)
    dl = jnp.concatenate(parts, axis=1) * rn_ref[...] * (gt * jax.nn.sigmoid(gt))
    out = (jnp.dot(c_ref[0], wo_ref[0:C_DIM, :], preferred_element_type=F32)
           + _bdot(dl, wo_ref[C_DIM:C_DIM + D_DIM, :]))
    m = mod_ref[0, 0]
    xn = x_ref[0] + m[2:3] * out
    o_ref[0] = xn
    _route_block(xn, m, gffn_ref, wr_ref, *route_refs)


def _odd_merge(c_lat, o2, P2, X, mod, ret_norm, wo_bf16, t_ctx, ffn_gain, w_router):
    B, S, D = X.shape
    T = S - t_ctx
    off = t_ctx // TB
    r_in, r_out, r_shapes, r_scratch = _route_specs(B, T, D)
    return pl.pallas_call(
        _odd_merge_kernel,
        grid=(B, T // TB),
        in_specs=[pl.BlockSpec((1, TB, C_DIM), lambda b, j: (b, j, 0)),
                  pl.BlockSpec((2, 1, TB, D_DIM), lambda b, j: (0, b, j + off, 0)),
                  pl.BlockSpec((1, TB, D_DIM), lambda b, j: (b, j + off, D_G_OFF // D_DIM)),
                  pl.BlockSpec((1, TB, D), lambda b, j: (b, j + off, 0)),
                  _mod_spec(D, all_lat=True),
                  pl.BlockSpec((1, D_DIM), lambda b, j: (0, 0)),
                  pl.BlockSpec((C_DIM + D_DIM, D), lambda b, j: (0, 0))] + r_in,
        out_specs=[pl.BlockSpec((1, TB, D), lambda b, j: (b, j, 0))] + r_out,
        out_shape=[jax.ShapeDtypeStruct((B, T, D), F32)] + r_shapes,
        scratch_shapes=r_scratch,
        compiler_params=_cparams(("arbitrary", "arbitrary")),
        name="odd_merge_route",
    )(c_lat, o2, P2, X, mod, ret_norm.reshape(1, -1), wo_bf16, ffn_gain.reshape(1, D), w_router)


def _rope_tables(t_ctx, T):
    half = C_HEAD // 4
    inv = ROPE_BASE ** (-jnp.arange(half, dtype=F32) / half)
    t = jnp.arange(T, dtype=jnp.int32)
    ar = (t // GRID_W).astype(F32)[:, None] * inv[None, :]
    ac = (t % GRID_W).astype(F32)[:, None] * inv[None, :]
    z = jnp.zeros_like(ar)
    cos = jnp.concatenate([jnp.cos(ar), jnp.cos(ar), jnp.cos(ac), jnp.cos(ac)], axis=1)
    s1 = jnp.concatenate([-jnp.sin(ar), z, -jnp.sin(ac), z], axis=1)
    s2 = jnp.concatenate([z, jnp.sin(ar), z, jnp.sin(ac)], axis=1)

    def full(tab, ctx_val):
        tab = jnp.concatenate([jnp.full((t_ctx, C_HEAD), ctx_val, F32), tab], axis=0)
        return jnp.tile(tab, (1, LANES // C_HEAD))

    return full(cos, 1.0), full(s1, 0.0), full(s2, 0.0)


def _head_indicator(n, width):
    i = jnp.arange(n) // width
    return (i[:, None] == i[None, :]).astype(BF16)


def kernel(x, c, ctx, c_ctx, ada_w, ada_b, norm_mix, norm_ffn, ev_w_in, ev_w_out, rwkv_shift, rwkv_w0, rwkv_w_up, rwkv_a0, rwkv_a_up, rwkv_g_up, rwkv_k_k, rwkv_k_a, rwkv_r_k, rwkv_ln_g, rwkv_ln_b, pool_w, pool_scale, od_w_in, od_w_out, diff_lambda, diff_subln, ret_norm, moe_router_group, moe_router_expert, moe_w_gate, moe_w_up, moe_w_down, final_norm):
    B, T, D = x.shape
    t_ctx = ctx.shape[1]
    assert ada_w.shape[0] == 2 and t_ctx == TB and T % TB == 0 and T % GRID_W == 0 and B < 16
    assert B % SCAN_NB == 0 and B % RET_NB == 0

    cpad = jnp.zeros((16, D), F32).at[:B].set(c).at[B].set(c_ctx)
    mods = _ada(cpad, ada_w, ada_b)

    def layer_mod(i):
        lat = mods[i, :B]
        cm = jnp.broadcast_to(mods[i, B][None], lat.shape)
        return jnp.stack([cm, lat], axis=1).reshape(B, 2, 6, D)

    def router_w(i):
        w = jnp.concatenate([moe_router_group[i], moe_router_expert[i]], axis=1)
        return jnp.pad(w, ((0, 0), (0, LANES - w.shape[1])))

    ehead = _head_indicator(A_DIM, A_HEAD)

    mod0 = layer_mod(0)
    P = _proj(ctx, x, mod0, norm_mix[0], ev_w_in[0].astype(BF16), BF16)
    zl = jnp.zeros((LORA_W, A_DIM), F32)
    lora_blocks = []
    for d in range(2):
        lora_blocks.append(jnp.concatenate([rwkv_w_up[0, d], zl], axis=0))
        lora_blocks.append(jnp.concatenate([zl, rwkv_a_up[0, d]], axis=0))
    lora_w = jnp.concatenate(lora_blocks, axis=1).astype(BF16)
    wa0 = jnp.concatenate([rwkv_w0[0, 0], rwkv_a0[0, 0], rwkv_w0[0, 1], rwkv_a0[0, 1]]).reshape(1, -1)
    lw, kt, bb, v, kk, r, bonus, g, yb = _even_prep(
        P, t_ctx, rwkv_shift[0], wa0, lora_w, rwkv_g_up[0].astype(BF16), rwkv_k_k[0], rwkv_k_a[0], rwkv_r_k[0],
        pool_w[0].astype(BF16), pool_scale[0], ehead)
    y2 = _rwkv_scan(lw, kt, bb, v, kk, r, t_ctx)
    routed = _even_merge(y2, bonus, g, yb, ctx, x, mod0, rwkv_ln_g[0], rwkv_ln_b[0], ev_w_out[0].astype(BF16), ehead,
                         norm_ffn[0], router_w(0))
    mod1 = layer_mod(1)
    X, P2 = _moe(routed, mod0, 0, moe_w_gate, moe_w_up, moe_w_down, all_lat=False,
                 project=(mod1, norm_mix[1], od_w_in[0].astype(BF16), _rope_tables(t_ctx, T)))

    lam_init = 0.8 - 0.6 * math.exp(-0.3 * 1)
    c_lat = _diff_attn(P2, t_ctx, diff_lambda[0], diff_subln[0], lam_init)
    o2 = _retention(P2, t_ctx)
    routed = _odd_merge(c_lat, o2, P2, X, mod1, ret_norm[0], od_w_out[0].astype(BF16), t_ctx, norm_ffn[1], router_w(1))
    return _moe(routed, mod1, 1, moe_w_gate, moe_w_up, moe_w_down, all_lat=True,
                final_gain=final_norm)
```

```python
import functools
import math

import jax
import jax.numpy as jnp
from jax import lax
from jax.experimental import pallas as pl
from jax.experimental.pallas import tpu as pltpu

F32 = jnp.float32
BF16 = jnp.bfloat16
HI = lax.Precision.HIGHEST

GRID_W = 64
NORM_EPS = 1e-6
ROPE_BASE = 10000.0
A_HEADS, A_HEAD = 8, 64
A_DIM = A_HEADS * A_HEAD
LORA_W, LORA_A, LORA_G = 64, 64, 128
W_DECAY_SCALE = 0.606531
RWKV_GN_EPS = 64e-5
POOL_WINDOWS = (2, 4, 8, 16)
POOL_GROUP = 128
B_DIM = POOL_GROUP * 4
C_HEADS, C_HEAD = 4, 64
C_VDIM = 2 * C_HEAD
C_DIM = C_HEADS * C_VDIM
DIFF_EPS = 1e-5
D_HEADS, D_KDIM, D_VDIM = 4, 64, 128
D_DIM = D_HEADS * D_VDIM
A_K_OFF, A_V_OFF = 0, A_DIM
W_LORA_OFF = 2 * A_DIM
A_LORA_OFF = W_LORA_OFF + LORA_W
A_R_OFF = A_LORA_OFF + LORA_A
G_LORA_OFF = A_R_OFF + A_DIM
POOL_OFF = G_LORA_OFF + LORA_G
EVEN_IN_COLS = POOL_OFF + B_DIM
C_K_OFF = 0
C_V_OFF = C_HEADS * 2 * C_HEAD
D_K_OFF = C_V_OFF + C_DIM
D_V_OFF = D_K_OFF + D_HEADS * D_KDIM
C_Q_OFF = D_V_OFF + D_DIM
D_Q_OFF = C_Q_OFF + C_HEADS * 2 * C_HEAD
D_G_OFF = D_Q_OFF + D_HEADS * D_KDIM
ODD_IN_COLS = D_G_OFF + D_DIM
MOE_GROUPS, MOE_PER_GROUP = 4, 8
MOE_EXPERTS = MOE_GROUPS * MOE_PER_GROUP
MOE_TOP_K = 2
EXPERT_BLOCK = 256

LANES = 128
SUBLANES = 8
HALO = 16
TB = 256
RWKV_CHUNK = 64
RET_CHUNK = 256
RET_NB = 2
SCAN_NB = 4
ATT_TQ = 256
ATT_SUB = 8
VMEM_LIMIT = 56 * 1024 * 1024
DMA_UNROLL = 8
GATHER_SLOTS = 3
TABLE_UNROLL = 32
GATHER_DMA_PRIORITY = 1


def _cparams(sem):
    return pltpu.CompilerParams(dimension_semantics=sem, vmem_limit_bytes=VMEM_LIMIT)


def _modnorm(x, gain, scale, shift, eps=NORM_EPS):
    ms = jnp.mean(x * x, axis=-1, keepdims=True)
    return x * lax.rsqrt(ms + eps) * gain * (1.0 + scale) + shift


def _bdot(a, b, dims=((1,), (0,))):
    return lax.dot_general(a.astype(BF16), b.astype(BF16), (dims, ((), ())), preferred_element_type=F32)


def _bf16_terms(x, terms):
    parts = []
    for _ in range(terms):
        p = x.astype(BF16)
        parts.append(p)
        x = x - p.astype(F32)
    return parts


def _dot_split_lhs(x, w_bf16, terms=2):
    return sum(jnp.dot(p, w_bf16, preferred_element_type=F32) for p in _bf16_terms(x, terms))


def _ada_kernel(c_ref, w_ref, b_ref, o_ref):
    c = c_ref[...]
    s = c * jax.nn.sigmoid(c)
    o_ref[0] = jnp.dot(s, w_ref[0], precision=HI, preferred_element_type=F32) + b_ref[0]


def _ada(cpad, ada_w, ada_b):
    L, D, N = ada_w.shape
    tn = 1536
    return pl.pallas_call(
        _ada_kernel,
        grid=(L, N // tn),
        in_specs=[pl.BlockSpec((16, D), lambda l, n: (0, 0)),
                  pl.BlockSpec((1, D, tn), lambda l, n: (l, 0, n)),
                  pl.BlockSpec((1, 1, tn), lambda l, n: (l, 0, n))],
        out_specs=pl.BlockSpec((1, 16, tn), lambda l, n: (l, 0, n)),
        out_shape=jax.ShapeDtypeStruct((L, 16, N), F32),
        compiler_params=_cparams(("arbitrary", "arbitrary")),
        name="ada_mod",
    )(cpad, ada_w, ada_b.reshape(L, 1, N))


def _mod_spec(D, all_lat=False):
    if all_lat:
        return pl.BlockSpec((1, 1, 6, D), lambda b, j: (b, 1, 0, 0))
    return pl.BlockSpec((1, 1, 6, D), lambda b, j: (b, jnp.minimum(j, 1), 0, 0))


def _rope_plan():
    ranges = [(C_K_OFF, C_V_OFF), (D_K_OFF, D_V_OFF), (C_Q_OFF, D_G_OFF)]
    rope_chunks = tuple(g for lo, hi in ranges for g in range(lo // LANES, hi // LANES))
    kscale_chunks = tuple(range(D_K_OFF // LANES, D_V_OFF // LANES))
    return rope_chunks, kscale_chunks


def _project_block(h, w_ref, o_ref, rope_refs=None, before_piece=None):
    n_out = w_ref.shape[1]
    rope_chunks, kscale_chunks = _rope_plan() if rope_refs is not None else ((), ())
    cw = 512
    for n0 in range(0, n_out, cw):
        if before_piece is not None:
            before_piece(n0 // cw, n_out // cw)
        p = jnp.dot(h, w_ref[:, n0:n0 + cw], preferred_element_type=F32)
        if rope_refs is None:
            o_ref[0, :, n0:n0 + cw] = p.astype(o_ref.dtype)
            continue
        cos_ref, s1_ref, s2_ref = rope_refs
        for q in range(cw // LANES):
            gi = n0 // LANES + q
            sub = p[:, q * LANES:(q + 1) * LANES]
            if gi in rope_chunks:
                sub = (sub * cos_ref[...] + pltpu.roll(sub, LANES - 16, 1) * s1_ref[...]
                       + pltpu.roll(sub, 16, 1) * s2_ref[...])
            if gi in kscale_chunks:
                sub = sub * (D_KDIM ** -0.5)
            o_ref[0, :, n0 + q * LANES:n0 + (q + 1) * LANES] = sub.astype(o_ref.dtype)


def _stream_specs(D):
    return [pl.BlockSpec((1, TB, D), lambda b, j: (b, 0, 0)),
            pl.BlockSpec((1, TB, D), lambda b, j: (b, jnp.maximum(j - 1, 0), 0))]


def _stream_block(ctx_ref, x_ref):
    return jnp.where(pl.program_id(1) == 0, ctx_ref[0], x_ref[0])


def _proj_kernel(ctx_ref, x_ref, mod_ref, g_ref, w_ref, o_ref):
    m = mod_ref[0, 0]
    h = _modnorm(_stream_block(ctx_ref, x_ref), g_ref[...], m[1:2], m[0:1]).astype(BF16)
    _project_block(h, w_ref, o_ref)


def _proj(ctx, x, mod, gain, w_bf16, out_dtype):
    B, T, D = x.shape
    S = ctx.shape[1] + T
    N = w_bf16.shape[1]
    return pl.pallas_call(
        _proj_kernel,
        grid=(B, S // TB),
        in_specs=_stream_specs(D) + [
                  _mod_spec(D),
                  pl.BlockSpec((1, D), lambda b, j: (0, 0)),
                  pl.BlockSpec((D, N), lambda b, j: (0, 0))],
        out_specs=pl.BlockSpec((1, TB, N), lambda b, j: (b, j, 0)),
        out_shape=jax.ShapeDtypeStruct((B, S, N), out_dtype),
        compiler_params=_cparams(("arbitrary", "arbitrary")),
        name="proj",
    )(ctx, x, mod, gain.reshape(1, D), w_bf16)


def _even_prep_kernel(p_ref, pp_ref, pn_ref, sh_ref, wa0_ref, lora_ref, gup_ref, kkg_ref, ka_ref, rk_ref,
                      pw_ref, ps_ref, eh_ref,
                      lw_ref, kt_ref, bb_ref, v_ref, kkn_ref, r_ref, bonus_ref, g_ref, yb_ref,
                      *, nblk, t_ctx, t_lat):
    j = pl.program_id(1)
    prev_ok = j >= 2
    next_ok = jnp.logical_and(j >= 1, j < nblk - 1)
    n_ext = TB + 2 * HALO

    def ext(c0, c1):
        pv = jnp.where(prev_ok, pp_ref[0, :, c0:c1].astype(F32), 0.0)
        nx = jnp.where(next_ok, pn_ref[0, :, c0:c1].astype(F32), 0.0)
        return jnp.concatenate([pv, p_ref[0, :, c0:c1].astype(F32), nx], axis=0)

    def shift3(c0, c1, k0):
        e = ext(c0, c1)
        kern = sh_ref[:, k0:k0 + (c1 - c0)]
        y = kern[0:1] * pltpu.roll(e, 1, 0) + kern[1:2] * e + kern[2:3] * pltpu.roll(e, n_ext - 1, 0)
        return y[HALO:HALO + TB]

    k = shift3(A_K_OFF, A_K_OFF + A_DIM, 0)
    v = shift3(A_V_OFF, A_V_OFF + A_DIM, A_DIM)
    r = shift3(A_R_OFF, A_R_OFF + A_DIM, 2 * A_DIM)

    c = p_ref[0, :, W_LORA_OFF:W_LORA_OFF + LANES].astype(F32)
    lane = lax.broadcasted_iota(jnp.int32, c.shape, 1)
    lin = jnp.where(lane < LORA_W, jnp.tanh(c), c)
    z = _bdot(lin, lora_ref[...]) + wa0_ref[...]

    eh = eh_ref[...]
    kk0 = k * kkg_ref[...]
    ss = _dot_split_lhs(kk0 * kk0, eh)
    kk = kk0 * lax.rsqrt(jnp.maximum(ss, 1e-12))
    kt_sum = None
    for d in range(2):
        zw = z[:, (2 * d) * A_DIM:(2 * d + 1) * A_DIM]
        za = z[:, (2 * d + 1) * A_DIM:(2 * d + 2) * A_DIM]
        a = jax.nn.sigmoid(za)
        kt = k * (1.0 + (a - 1.0) * ka_ref[...])
        lw_ref[d, 0] = -W_DECAY_SCALE * jax.nn.sigmoid(zw)
        kt_ref[d, 0] = kt.astype(kt_ref.dtype)
        bb_ref[d, 0] = (a * kk).astype(bb_ref.dtype)
        kt_sum = kt if kt_sum is None else kt_sum + kt
    v_ref[0] = v.astype(v_ref.dtype)
    kkn_ref[0] = kk.astype(kkn_ref.dtype)
    r_ref[0] = r.astype(r_ref.dtype)
    bonus_ref[0] = (_dot_split_lhs(r * kt_sum * rk_ref[...], eh) * v).astype(bonus_ref.dtype)
    gl = p_ref[0, :, G_LORA_OFF:G_LORA_OFF + LORA_G].astype(F32)
    g_ref[0] = _bdot(jax.nn.sigmoid(gl), gup_ref[...]).astype(g_ref.dtype)

    row = lax.broadcasted_iota(jnp.int32, (TB, POOL_GROUP), 0)
    tpos = row + jnp.maximum(j - 1, 0) * TB
    tseg = jnp.where(j == 0, t_ctx, t_lat)
    for gi, win in enumerate(POOL_WINDOWS):
        c0 = POOL_OFF + gi * POOL_GROUP
        e = ext(c0, c0 + POOL_GROUP)
        s = e + pltpu.roll(e, 1, 0)
        sh = 1
        while 2 * sh < win:
            s = pltpu.roll(s, sh, 0) + pltpu.roll(s, n_ext - sh, 0)
            sh *= 2
        s = s[HALO:HALO + TB]
        lo = jnp.clip(tpos - win // 2, 0, tseg)
        hi = jnp.clip(tpos - win // 2 + win, 0, tseg)
        mean = s / (hi - lo).astype(F32)
        dd = mean - p_ref[0, :, c0:c0 + POOL_GROUP].astype(F32)
        y = _bdot(dd, pw_ref[gi])
        yb_ref[0, :, gi * POOL_GROUP:(gi + 1) * POOL_GROUP] = (
            y * ps_ref[:, gi * POOL_GROUP:(gi + 1) * POOL_GROUP]).astype(yb_ref.dtype)


def _even_prep(P, t_ctx, shift_k, wa0, lora_w, g_up, k_k, k_a, r_k, pool_w, pool_scale, ehead):
    B, S, N = P.shape
    nblk = S // TB
    hb = TB // HALO
    const2 = lambda b, j: (0, 0)
    in_specs = [
        pl.BlockSpec((1, TB, N), lambda b, j: (b, j, 0)),
        pl.BlockSpec((1, HALO, N), lambda b, j: (b, jnp.maximum(j * hb - 1, 0), 0)),
        pl.BlockSpec((1, HALO, N), lambda b, j: (b, jnp.minimum((j + 1) * hb, S // HALO - 1), 0)),
        pl.BlockSpec((3, 3 * A_DIM), const2),
        pl.BlockSpec((1, 4 * A_DIM), const2),
        pl.BlockSpec((LANES, 4 * A_DIM), const2),
        pl.BlockSpec((LORA_G, A_DIM), const2),
        pl.BlockSpec((1, A_DIM), const2),
        pl.BlockSpec((1, A_DIM), const2),
        pl.BlockSpec((1, A_DIM), const2),
        pl.BlockSpec((4, POOL_GROUP, POOL_GROUP), lambda b, j: (0, 0, 0)),
        pl.BlockSpec((1, B_DIM), const2),
        pl.BlockSpec((A_DIM, A_DIM), const2),
    ]
    two_f32 = jax.ShapeDtypeStruct((2, B, S, A_DIM), F32)
    two = jax.ShapeDtypeStruct((2, B, S, A_DIM), BF16)
    one = jax.ShapeDtypeStruct((B, S, A_DIM), BF16)
    two_spec = pl.BlockSpec((2, 1, TB, A_DIM), lambda b, j: (0, b, j, 0))
    one_spec = pl.BlockSpec((1, TB, A_DIM), lambda b, j: (b, j, 0))
    return pl.pallas_call(
        functools.partial(_even_prep_kernel, nblk=nblk, t_ctx=t_ctx, t_lat=S - t_ctx),
        grid=(B, nblk),
        in_specs=in_specs,
        out_specs=[two_spec] * 3 + [one_spec] * 6,
        out_shape=[two_f32, two, two] + [one] * 6,
        compiler_params=_cparams(("arbitrary", "arbitrary")),
        name="even_prep",
    )(P, P, P, shift_k, wa0, lora_w, g_up, k_k.reshape(1, -1), k_a.reshape(1, -1), r_k.reshape(1, -1),
      pool_w, pool_scale.reshape(1, -1), ehead)


def _rwkv_scan_kernel(lw_ref, kt_ref, bb_ref, v_ref, kk_ref, r_ref, y_ref, s_ref):
    d = pl.program_id(1)
    c = pl.program_id(2)

    @pl.when(c == 0)
    def _():
        s_ref[...] = jnp.zeros_like(s_ref)

    C = RWKV_CHUNK
    tt = lax.broadcasted_iota(jnp.int32, (C, C), 0)
    ii = lax.broadcasted_iota(jnp.int32, (C, C), 1)
    rel = (tt - ii) * (1 - 2 * d)
    strict = rel > 0
    incl = rel >= 0
    t2 = lax.broadcasted_iota(jnp.int32, (C, 2 * C), 0)
    i2 = lax.broadcasted_iota(jnp.int32, (C, 2 * C), 1) & (C - 1)
    incl2 = (t2 - i2) * (1 - 2 * d) >= 0
    tri = jnp.where(incl, 1.0, 0.0).astype(BF16)
    n_dbl = int(math.log2(C))
    hsl = [slice(h * A_HEAD, (h + 1) * A_HEAD) for h in range(A_HEADS)]
    ar, bk, vh, bkp, ptot = [], [], [], [], []
    for bi in range(SCAN_NB):
        lw = lw_ref[0, bi]
        cum = sum(jnp.dot(tri, p, preferred_element_type=F32) for p in _bf16_terms(lw, 3))
        tot = jnp.sum(lw, axis=0, keepdims=True)
        p_inv = jnp.exp(-cum)
        p_rest = jnp.exp(tot - cum)
        p_tot = jnp.exp(tot)
        bb = bb_ref[0, bi]
        kt = kt_ref[0, bi]
        at_all = (-kk_ref[bi] * jnp.exp(cum - lw)).astype(BF16)
        rt_all = (r_ref[bi] * jnp.exp(cum)).astype(BF16)
        bt_all = (bb * p_inv).astype(BF16)
        kti_all = (kt * p_inv).astype(BF16)
        bp_all = (bb * p_rest).astype(BF16)
        kp_all = (kt * p_rest).astype(BF16)
        v_all = v_ref[bi].astype(BF16)
        for s in hsl:
            ar.append(jnp.concatenate([at_all[:, s], rt_all[:, s]], axis=0))
            bk.append(jnp.concatenate([bt_all[:, s], kti_all[:, s]], axis=0))
            bkp.append(jnp.concatenate([bp_all[:, s], kp_all[:, s]], axis=0))
            vh.append(v_all[:, s])
            ptot.append(p_tot[:, s])
    chains = range(SCAN_NB * A_HEADS)
    s0 = [s_ref[n] for n in chains]
    m = [_bdot(ar[n], bk[n], ((1,), (1,))) for n in chains]
    arh = [_bdot(ar[n], s0[n], ((1,), (1,))) for n in chains]
    lp = [jnp.where(strict, m[n][:C, :C], 0.0) for n in chains]
    a_ak = [jnp.where(strict, m[n][:C, C:], 0.0) for n in chains]
    a_r = [jnp.where(incl2, m[n][C:, :], 0.0) for n in chains]
    u = [arh[n][:C] + _bdot(a_ak[n], vh[n]) for n in chains]
    for q in range(n_dbl):
        u = [u[n] + _bdot(lp[n], u[n]) for n in chains]
        if q < n_dbl - 1:
            lp = [_bdot(lp[n], lp[n]) for n in chains]
    uv = [jnp.concatenate([u[n].astype(BF16), vh[n]], axis=0) for n in chains]
    outs = [arh[n][C:] + _bdot(a_r[n], uv[n]) for n in chains]
    s_new = [s0[n] * ptot[n] + _bdot(uv[n], bkp[n], ((0,), (0,))) for n in chains]
    for n in chains:
        s_ref[n] = s_new[n]
    for bi in range(SCAN_NB):
        y_ref[0, bi] = jnp.concatenate(outs[bi * A_HEADS:(bi + 1) * A_HEADS], axis=1).astype(y_ref.dtype)


def _rwkv_scan(lw, kt, bb, v, kk, r, t_ctx):
    _, B, S, _ = lw.shape
    C = RWKV_CHUNK
    nc = S // C
    nctx = t_ctx // C

    def cidx(d, c):
        back = jnp.where(c < nctx, nctx - 1 - c, nc - 1 + nctx - c)
        return jnp.where(d == 0, c, back)

    two_spec = pl.BlockSpec((1, SCAN_NB, C, A_DIM), lambda b, d, c: (d, b, cidx(d, c), 0))
    one_spec = pl.BlockSpec((SCAN_NB, C, A_DIM), lambda b, d, c: (b, cidx(d, c), 0))
    return pl.pallas_call(
        _rwkv_scan_kernel,
        grid=(B // SCAN_NB, 2, nc),
        in_specs=[two_spec] * 3 + [one_spec] * 3,
        out_specs=two_spec,
        out_shape=jax.ShapeDtypeStruct((2, B, S, A_DIM), BF16),
        scratch_shapes=[pltpu.VMEM((SCAN_NB * A_HEADS, A_HEAD, A_HEAD), F32)],
        compiler_params=_cparams(("arbitrary", "arbitrary", "arbitrary")),
        name="rwkv_scan",
    )(lw, kt, bb, v, kk, r)


def _even_merge_kernel(y2_ref, bonus_ref, g_ref, yb_ref, ctx_ref, x_ref, mod_ref, lng_ref, lnb_ref, wo_ref, eh_ref,
                       gffn_ref, wr_ref, o_ref, *route_refs):
    y = y2_ref[0, 0].astype(F32) + y2_ref[1, 0].astype(F32)
    eh = eh_ref[...]
    mu = _dot_split_lhs(y, eh) * (1.0 / A_HEAD)
    yc = y - mu
    var = _dot_split_lhs(yc * yc, eh) * (1.0 / A_HEAD)
    yn = yc * lax.rsqrt(var + RWKV_GN_EPS) * lng_ref[...] + lnb_ref[...]
    ya = (yn + bonus_ref[0]) * g_ref[0]
    out = _bdot(ya, wo_ref[0:A_DIM, :]) + _bdot(yb_ref[0], wo_ref[A_DIM:A_DIM + B_DIM, :])
    m = mod_ref[0, 0]
    xn = _stream_block(ctx_ref, x_ref) + m[2:3] * out
    o_ref[0] = xn
    _route_block(xn, m, gffn_ref, wr_ref, *route_refs)


def _even_merge(y2, bonus, g, yb, ctx, x, mod, ln_g, ln_b, wo_bf16, ehead, ffn_gain, w_router):
    B, T, D = x.shape
    S = ctx.shape[1] + T
    r_in, r_out, r_shapes, r_scratch = _route_specs(B, S, D)
    one_spec = pl.BlockSpec((1, TB, A_DIM), lambda b, j: (b, j, 0))
    const2 = lambda b, j: (0, 0)
    return pl.pallas_call(
        _even_merge_kernel,
        grid=(B, S // TB),
        in_specs=[pl.BlockSpec((2, 1, TB, A_DIM), lambda b, j: (0, b, j, 0)), one_spec, one_spec, one_spec]
        + _stream_specs(D) + [_mod_spec(D),
                  pl.BlockSpec((1, A_DIM), const2), pl.BlockSpec((1, A_DIM), const2),
                  pl.BlockSpec((A_DIM + B_DIM, D), const2), pl.BlockSpec((A_DIM, A_DIM), const2)] + r_in,
        out_specs=[pl.BlockSpec((1, TB, D), lambda b, j: (b, j, 0))] + r_out,
        out_shape=[jax.ShapeDtypeStruct((B, S, D), F32)] + r_shapes,
        scratch_shapes=r_scratch,
        compiler_params=_cparams(("arbitrary", "arbitrary")),
        name="even_merge_route",
    )(y2, bonus, g, yb, ctx, x, mod, ln_g.reshape(1, -1), ln_b.reshape(1, -1), wo_bf16, ehead,
      ffn_gain.reshape(1, D), w_router)


def _route_block(xn, m, g_ref, wr_ref, hf_ref, meta_ref, metat_ref, cnt_ref, carry_ref):
    first = jnp.logical_and(pl.program_id(0) == 0, pl.program_id(1) == 0)

    @pl.when(first)
    def _():
        carry_ref[...] = jnp.zeros_like(carry_ref)

    hf = _modnorm(xn, g_ref[...], m[4:5], m[3:4])
    hf_ref[0] = hf
    h0, h1 = _bf16_terms(hf, 2)
    w0, w1 = _bf16_terms(wr_ref[...], 2)
    logits = (jnp.dot(h0, w0, preferred_element_type=F32) + jnp.dot(h0, w1, preferred_element_type=F32)
              + jnp.dot(h1, w0, preferred_element_type=F32))
    lane = lax.broadcasted_iota(jnp.int32, logits.shape, 1).astype(F32)
    neg = -jnp.inf
    far = float(LANES)
    lg = jnp.where(lane < MOE_GROUPS, logits, neg)
    mg = jnp.max(lg, axis=1, keepdims=True)
    grp = jnp.min(jnp.where(lg == mg, lane, far), axis=1, keepdims=True)
    gw = 1.0 / jnp.sum(jnp.exp(lg - mg), axis=1, keepdims=True)
    el = lane - MOE_GROUPS
    in_grp = jnp.logical_and(el >= grp * MOE_PER_GROUP, el < (grp + 1.0) * MOE_PER_GROUP)
    le = jnp.where(in_grp, logits, neg)
    m0 = jnp.max(le, axis=1, keepdims=True)
    i0 = jnp.min(jnp.where(le == m0, lane, far), axis=1, keepdims=True)
    le1 = jnp.where(lane == i0, neg, le)
    m1 = jnp.max(le1, axis=1, keepdims=True)
    i1 = jnp.min(jnp.where(le1 == m1, lane, far), axis=1, keepdims=True)
    p1 = jnp.exp(m1 - m0)
    gate0 = gw / (1.0 + p1)
    gate1 = gw * p1 / (1.0 + p1)
    e0 = i0 - MOE_GROUPS
    e1 = i1 - MOE_GROUPS
    oh0 = (lane == e0).astype(F32)
    oh1 = (lane == e1).astype(F32)
    cnt = oh0 + oh1
    rr = lax.broadcasted_iota(jnp.int32, (TB, TB), 0)
    cc = lax.broadcasted_iota(jnp.int32, (TB, TB), 1)
    before = _bdot((cc < rr).astype(F32), cnt) + carry_ref[...]
    rank0 = jnp.sum(oh0 * before, axis=1, keepdims=True)
    rank1 = jnp.sum(oh1 * before, axis=1, keepdims=True)
    carry_ref[...] = carry_ref[...] + jnp.sum(cnt, axis=0, keepdims=True)
    cnt_ref[...] = carry_ref[...]
    meta = jnp.where(lane == 0, e0, 0.0)
    meta = jnp.where(lane == 1, e1, meta)
    meta = jnp.where(lane == 2, rank0, meta)
    meta = jnp.where(lane == 3, rank1, meta)
    meta = jnp.where(lane == 4, gate0, meta)
    meta = jnp.where(lane == 5, gate1, meta)
    meta_ref[0] = meta
    metat_ref[0, 0] = meta.T[0:SUBLANES]


def _route_specs(B, S, D):
    in_specs = [pl.BlockSpec((1, D), lambda b, j: (0, 0)), pl.BlockSpec((D, LANES), lambda b, j: (0, 0))]
    out_specs = [pl.BlockSpec((1, TB, D), lambda b, j: (b, j, 0)),
                 pl.BlockSpec((1, TB, LANES), lambda b, j: (b, j, 0)),
                 pl.BlockSpec((1, 1, SUBLANES, TB), lambda b, j: (b, j, 0, 0)),
                 pl.BlockSpec((1, LANES), lambda b, j: (0, 0))]
    out_shapes = [jax.ShapeDtypeStruct((B, S, D), F32), jax.ShapeDtypeStruct((B, S, LANES), F32),
                  jax.ShapeDtypeStruct((B, S // TB, SUBLANES, TB), F32), jax.ShapeDtypeStruct((1, LANES), F32)]
    return in_specs, out_specs, out_shapes, [pltpu.VMEM((1, LANES), F32)]


def _routing_tables(metat, counts):
    B, nj = metat.shape[:2]
    n_tok = B * nj * TB
    mi = jnp.transpose(metat[:, :, 0:4, :], (2, 0, 1, 3)).reshape(4, n_tok).astype(jnp.int32)
    cnt = counts[0, :MOE_EXPERTS].astype(jnp.int32)
    padded = (cnt + EXPERT_BLOCK - 1) // EXPERT_BLOCK * EXPERT_BLOCK
    pad_end = jnp.cumsum(padded)
    pad_start = pad_end - padded
    e_all = mi[0:2].reshape(-1)
    onehot = (e_all[:, None] == jnp.arange(MOE_EXPERTS, dtype=jnp.int32)[None, :]).astype(F32)
    base = jnp.dot(onehot, pad_start.astype(F32), precision=HI).astype(jnp.int32)
    dest = base + mi[2:4].reshape(-1)
    n_blocks = -(-(n_tok * MOE_TOP_K) // EXPERT_BLOCK) + MOE_EXPERTS
    starts = jnp.arange(n_blocks, dtype=jnp.int32) * EXPERT_BLOCK
    block_e = jnp.minimum(jnp.sum((pad_end[None, :] <= starts[:, None]).astype(jnp.int32), axis=1), MOE_EXPERTS - 1)
    n_used = (pad_end[-1] // EXPERT_BLOCK).astype(jnp.int32).reshape(1)
    present = cnt > 0
    ids = jnp.arange(MOE_EXPERTS, dtype=jnp.int32)
    later = jnp.where(jnp.logical_and(present[None, :], ids[None, :] > ids[:, None]), ids[None, :], MOE_EXPERTS)
    nxt = jnp.min(later, axis=1)
    nxt = jnp.where(nxt == MOE_EXPERTS, -1, nxt)
    wslot = (jnp.cumsum(present.astype(jnp.int32)) - 1) % 2
    pads = jnp.concatenate([pad_start + cnt, pad_end, wslot, nxt]).astype(jnp.int32)
    return dest, block_e, n_used, pads, n_blocks


def _expert_kernel(be_ref, nu_ref, dest_ref, pads_ref, hf_ref, wg_ref, wu_ref, wd_ref, o_ref, xbuf, wg_bf, wu_bf, wd_bf,
                   wg_f32, wu_f32, wd_f32, rows_ref, sem, wsem, *, n_tok, layer):
    j = pl.program_id(0)
    n_used = nu_ref[0]
    EB = EXPERT_BLOCK
    E = MOE_EXPERTS

    def weight_copies(e, ws):
        return [pltpu.make_async_copy(src.at[layer, e], dst.at[ws], wsem.at[ws])
                for src, dst in ((wg_ref, wg_f32), (wu_ref, wu_f32), (wd_ref, wd_f32))]

    def row_copy(blk, slot, r):
        row = rows_ref[blk * EB + r]
        return pltpu.make_async_copy(hf_ref.at[pl.ds(row, 1), :], xbuf.at[slot, pl.ds(r, 1), :], sem.at[slot])

    def issue(blk, slot):
        def body(r, carry):
            row_copy(blk, slot, r).start()
            return carry
        lax.fori_loop(0, EB, body, 0, unroll=DMA_UNROLL)

    @pl.when(j == 0)
    def _():
        fill_mask = (1 << (n_tok.bit_length() - 1)) - 1

        def fill(i, carry):
            rows_ref[i] = jnp.bitwise_and(i, fill_mask)
            return carry
        for e in range(MOE_EXPERTS):
            lax.fori_loop(pads_ref[e], pads_ref[MOE_EXPERTS + e], fill, 0)

        def invert(t, carry):
            rows_ref[dest_ref[t]] = t
            rows_ref[dest_ref[n_tok + t]] = t
            return carry
        lax.fori_loop(0, n_tok, invert, 0, unroll=TABLE_UNROLL)
        issue(0, 0)

        @pl.when(n_used > 1)
        def _():
            issue(1, 1)

    def compute(prefetch_ahead):
        slot = lax.rem(j, GATHER_SLOTS)

        def body(r, carry):
            row_copy(j, slot, r).wait()
            return carry
        lax.fori_loop(0, EB, body, 0, unroll=DMA_UNROLL)

        e_cur = be_ref[j]

        @pl.when(jnp.logical_or(j == 0, e_cur != be_ref[jnp.maximum(j - 1, 0)]))
        def _():
            ws = pads_ref[2 * E + e_cur]
            e_next = pads_ref[3 * E + e_cur]

            @pl.when(j == 0)
            def _():
                for cp in weight_copies(e_cur, ws):
                    cp.start()
            for cp in weight_copies(e_cur, ws):
                cp.wait()
            wg_bf[...] = wg_f32[ws].astype(BF16)
            wu_bf[...] = wu_f32[ws].astype(BF16)
            wd_bf[...] = wd_f32[ws].astype(BF16)

            @pl.when(e_next >= 0)
            def _():
                for cp in weight_copies(e_next, 1 - ws):
                    cp.start()

        xb = xbuf[slot].astype(BF16)
        if prefetch_ahead:
            ahead = lax.rem(j + 2, GATHER_SLOTS)
            for r in range(EB):
                row_copy(j + 2, ahead, r).start(priority=GATHER_DMA_PRIORITY)
        hg = jnp.dot(xb, wg_bf[...], preferred_element_type=F32)
        hu = jnp.dot(xb, wu_bf[...], preferred_element_type=F32)
        hm = (hg * jax.nn.sigmoid(hg) * hu).astype(BF16)
        o_ref[...] = jnp.dot(hm, wd_bf[...], preferred_element_type=F32)

    @pl.when(j + 2 < n_used)
    def _():
        compute(True)

    @pl.when(jnp.logical_and(j < n_used, j + 2 >= n_used))
    def _():
        compute(False)

    @pl.when(j >= n_used)
    def _():
        o_ref[...] = jnp.zeros_like(o_ref)


def _experts(hf2d, dest, block_e, n_used, pads, n_blocks, layer, w_gate, w_up, w_down):
    N, D = hf2d.shape
    _, E, _, De = w_gate.shape
    EB = EXPERT_BLOCK
    grid_spec = pltpu.PrefetchScalarGridSpec(
        num_scalar_prefetch=4,
        grid=(n_blocks,),
        in_specs=[pl.BlockSpec(memory_space=pl.ANY)] * 4,
        out_specs=pl.BlockSpec((EB, D), lambda j, be, nu, ds, pd: (j, 0)),
        scratch_shapes=[pltpu.VMEM((GATHER_SLOTS, EB, D), F32), pltpu.VMEM((D, De), BF16), pltpu.VMEM((D, De), BF16),
                        pltpu.VMEM((De, D), BF16), pltpu.VMEM((2, D, De), F32), pltpu.VMEM((2, D, De), F32),
                        pltpu.VMEM((2, De, D), F32), pltpu.SMEM((n_blocks * EB,), jnp.int32),
                        pltpu.SemaphoreType.DMA((GATHER_SLOTS,)), pltpu.SemaphoreType.DMA((2,))],
    )
    return pl.pallas_call(
        functools.partial(_expert_kernel, n_tok=N, layer=layer),
        grid_spec=grid_spec,
        out_shape=jax.ShapeDtypeStruct((n_blocks * EB, D), F32),
        compiler_params=_cparams(("arbitrary",)),
        name="moe_experts",
    )(block_e, n_used, dest, pads, hf2d, w_gate, w_up, w_down)


def _combine_kernel(dest_ref, yb_ref, x_ref, meta_ref, mod_ref, *rest, nj, project):
    if project:
        mod1_ref, g1_ref, w_ref, cos_ref, s1_ref, s2_ref, o_ref, p_ref, ybuf, sem = rest
    else:
        fn_ref, o_ref, ybuf, sem = rest
    b = pl.program_id(0)
    j = pl.program_id(1)
    step = b * nj + j
    n_steps = pl.num_programs(0) * nj
    n_tok = n_steps * TB

    def row_copy(st, slot, r, k):
        dst_row = dest_ref[k * n_tok + st * TB + r]
        return pltpu.make_async_copy(yb_ref.at[pl.ds(dst_row, 1), :], ybuf.at[slot, k, pl.ds(r, 1), :], sem.at[slot])

    def issue(st, slot):
        def body(r, carry):
            row_copy(st, slot, r, 0).start()
            row_copy(st, slot, r, 1).start()
            return carry
        lax.fori_loop(0, TB, body, 0, unroll=DMA_UNROLL)

    @pl.when(step == 0)
    def _():
        issue(0, 0)
        issue(1, 1)

    slot = lax.rem(step, GATHER_SLOTS)

    def body(prefetch_ahead):
        def wbody(r, carry):
            row_copy(step, slot, r, 0).wait()
            row_copy(step, slot, r, 1).wait()
            return carry
        lax.fori_loop(0, TB, wbody, 0, unroll=DMA_UNROLL)
        meta = meta_ref[0]
        f = meta[:, 4:5] * ybuf[slot, 0] + meta[:, 5:6] * ybuf[slot, 1]
        m = mod_ref[0, 0]
        xn = x_ref[0] + m[5:6] * f
        ahead = lax.rem(step + 2, GATHER_SLOTS)

        def issue_rows(i, n):
            if prefetch_ahead:
                for r in range(i * TB // n, (i + 1) * TB // n):
                    row_copy(step + 2, ahead, r, 0).start()
                    row_copy(step + 2, ahead, r, 1).start()

        if project:
            o_ref[0] = xn
            m1 = mod1_ref[0, 0]
            h = _modnorm(xn, g1_ref[...], m1[1:2], m1[0:1]).astype(BF16)
            _project_block(h, w_ref, p_ref, (cos_ref, s1_ref, s2_ref), before_piece=issue_rows)
        else:
            issue_rows(0, 1)
            ms = jnp.mean(xn * xn, axis=-1, keepdims=True)
            o_ref[0] = xn * lax.rsqrt(ms + NORM_EPS) * fn_ref[...]

    @pl.when(step + 2 < n_steps)
    def _():
        body(True)

    @pl.when(step + 2 >= n_steps)
    def _():
        body(False)


def _combine(dest, yb, X, meta, mod, all_lat, final_gain=None, project=None):
    B, S, D = X.shape
    nj = S // TB

    def mspec(lat_only):
        if lat_only:
            return pl.BlockSpec((1, 1, 6, D), lambda b, j, ds: (b, 1, 0, 0))
        return pl.BlockSpec((1, 1, 6, D), lambda b, j, ds: (b, jnp.minimum(j, 1), 0, 0))

    row_spec = pl.BlockSpec((1, TB, D), lambda b, j, ds: (b, j, 0))
    in_specs = [pl.BlockSpec(memory_space=pl.ANY), row_spec,
                pl.BlockSpec((1, TB, LANES), lambda b, j, ds: (b, j, 0)), mspec(all_lat)]
    args = [dest, yb, X, meta, mod]
    if project is None:
        in_specs.append(pl.BlockSpec((1, D), lambda b, j, ds: (0, 0)))
        args.append(final_gain.reshape(1, D))
        out_specs = row_spec
        out_shape = jax.ShapeDtypeStruct((B, S, D), F32)
    else:
        mod1, gain1, w_bf16, (cos, s1, s2) = project
        N = w_bf16.shape[1]
        in_specs += [mspec(False), pl.BlockSpec((1, D), lambda b, j, ds: (0, 0)),
                     pl.BlockSpec((D, N), lambda b, j, ds: (0, 0))]
        in_specs += [pl.BlockSpec((TB, LANES), lambda b, j, ds: (j, 0))] * 3
        args += [mod1, gain1.reshape(1, D), w_bf16, cos, s1, s2]
        out_specs = [row_spec, pl.BlockSpec((1, TB, N), lambda b, j, ds: (b, j, 0))]
        out_shape = [jax.ShapeDtypeStruct((B, S, D), F32), jax.ShapeDtypeStruct((B, S, N), BF16)]
    grid_spec = pltpu.PrefetchScalarGridSpec(
        num_scalar_prefetch=1,
        grid=(B, nj),
        in_specs=in_specs,
        out_specs=out_specs,
        scratch_shapes=[pltpu.VMEM((GATHER_SLOTS, 2, TB, D), F32), pltpu.SemaphoreType.DMA((GATHER_SLOTS,))],
    )
    return pl.pallas_call(
        functools.partial(_combine_kernel, nj=nj, project=project is not None),
        grid_spec=grid_spec,
        out_shape=out_shape,
        compiler_params=_cparams(("arbitrary", "arbitrary")),
        name="moe_combine_final" if project is None else "moe_combine_proj",
    )(*args)


def _moe(routed, mod, layer, w_gate, w_up, w_down, all_lat, final_gain=None, project=None):
    X, hf, meta, metat, counts = routed
    B, S, D = X.shape
    dest, block_e, n_used, pads, n_blocks = _routing_tables(metat, counts)
    yb = _experts(hf.reshape(B * S, D), dest, block_e, n_used, pads, n_blocks, layer, w_gate, w_up, w_down)
    return _combine(dest, yb, X, meta, mod, all_lat, final_gain, project)


def _diff_attn_kernel(*refs, lam_init):
    q_refs = refs[:ATT_SUB]
    k_ref, v_ref, dl_ref, sg_ref, o_ref, vx_ref = refs[ATT_SUB:]

    @pl.when(pl.program_id(2) == 0)
    def _():
        vlane = lax.broadcasted_iota(jnp.int32, (vx_ref.shape[0], LANES), 1)
        vx_ref[:, 0:C_VDIM] = v_ref[0]
        vx_ref[:, C_VDIM:C_VDIM + LANES] = jnp.where(vlane == 0, 1.0, 0.0).astype(BF16)

    k = k_ref[0]
    dl = dl_ref[...]
    lam = (jnp.exp(jnp.sum(dl[0:1] * dl[1:2], axis=1, keepdims=True))
           - jnp.exp(jnp.sum(dl[2:3] * dl[3:4], axis=1, keepdims=True)) + lam_init)

    def scores(t):
        q = q_refs[t][0] * (C_HEAD ** -0.5)
        lane = lax.broadcasted_iota(jnp.int32, q.shape, 1)
        zero = jnp.zeros_like(q)
        qs = (jnp.where(lane < C_HEAD, q, zero), jnp.where(lane >= C_HEAD, q, zero))
        return [lax.dot_general(qq, k, (((1,), (1,)), ((), ())), preferred_element_type=F32) for qq in qs]

    def finish(t, s):
        e = [jnp.exp(si - jnp.max(si, axis=1, keepdims=True)).astype(BF16) for si in s]
        oz = [jnp.dot(ei, vx_ref[...], preferred_element_type=F32) for ei in e]
        on = [ozi[:, 0:C_VDIM] * (1.0 / ozi[:, C_VDIM:C_VDIM + 1]) for ozi in oz]
        o = on[0] - lam * on[1]
        ms = jnp.mean(o * o, axis=-1, keepdims=True)
        o_ref[0, t * ATT_TQ:(t + 1) * ATT_TQ, :] = (
            o * lax.rsqrt(ms + DIFF_EPS) * sg_ref[...] * (1.0 - lam_init)).astype(o_ref.dtype)

    s_next = scores(0)
    for t in range(ATT_SUB):
        s_cur = s_next
        if t + 1 < ATT_SUB:
            s_next = scores(t + 1)
        finish(t, s_cur)


def _diff_attn(P2, t_ctx, diff_lambda, diff_subln, lam_init):
    B, S, _ = P2.shape
    T = S - t_ctx
    qoff = t_ctx // ATT_TQ
    step = ATT_SUB * ATT_TQ

    def q_spec(t):
        return pl.BlockSpec((1, ATT_TQ, LANES), lambda b, h, q: (b, q * ATT_SUB + t + qoff, C_Q_OFF // LANES + h))

    return pl.pallas_call(
        functools.partial(_diff_attn_kernel, lam_init=lam_init),
        grid=(B, C_HEADS, T // step),
        in_specs=[q_spec(t) for t in range(ATT_SUB)] + [
                  pl.BlockSpec((1, S, LANES), lambda b, h, q: (b, 0, C_K_OFF // LANES + h)),
                  pl.BlockSpec((1, S, LANES), lambda b, h, q: (b, 0, C_V_OFF // LANES + h)),
                  pl.BlockSpec((4, C_HEAD), lambda b, h, q: (0, 0)),
                  pl.BlockSpec((1, C_VDIM), lambda b, h, q: (0, 0))],
        out_specs=pl.BlockSpec((1, step, LANES), lambda b, h, q: (b, q, h)),
        out_shape=jax.ShapeDtypeStruct((B, T, C_DIM), BF16),
        scratch_shapes=[pltpu.VMEM((S, C_VDIM + LANES), BF16)],
        compiler_params=_cparams(("arbitrary", "arbitrary", "arbitrary")),
        name="diff_attn",
    )(*([P2] * ATT_SUB), P2, P2, diff_lambda, diff_subln.reshape(1, -1))


def _retention_kernel(q_ref, k_ref, v0_ref, v1_ref, o_ref, r_ref):
    d = pl.program_id(1)
    c = pl.program_id(2)

    @pl.when(c == 0)
    def _():
        r_ref[...] = jnp.zeros_like(r_ref)

    C = RET_CHUNK
    ti = lax.broadcasted_iota(jnp.int32, (C, C), 0)
    si = lax.broadcasted_iota(jnp.int32, (C, C), 1)
    rel = ((ti - si) * (1 - 2 * d)).astype(F32)
    col = lax.broadcasted_iota(jnp.int32, (C, 1), 0)
    pos = jnp.where(d == 0, col, C - 1 - col).astype(F32)
    chains = [(bi, h) for bi in range(RET_NB) for h in range(D_HEADS)]
    r0s = [r_ref[bi * D_HEADS + h] for bi, h in chains]
    decs = []
    for h in range(D_HEADS):
        lg_f = math.log(1.0 - 2.0 ** (-5.0 - h))
        lg_b = math.log(1.0 - 2.0 ** (-5.0 - (D_HEADS - 1 - h)))
        lg = jnp.where(d == 0, jnp.full((1, 1), lg_f, F32), jnp.full((1, 1), lg_b, F32))
        decs.append((jnp.where(rel >= 0, jnp.exp(lg * jnp.maximum(rel, 0.0)), 0.0),
                     jnp.exp(lg * (pos + 1.0)), jnp.exp(lg * (C - 1.0 - pos)), jnp.exp(lg * C)))
    outs, r_new = [], []
    for n, (bi, h) in enumerate(chains):
        inner_dec, q_dec, k_dec, c_dec = decs[h]
        q = q_ref[bi, :, h * D_KDIM:(h + 1) * D_KDIM]
        k = k_ref[bi, :, h * D_KDIM:(h + 1) * D_KDIM]
        vref = v0_ref if h < 2 else v1_ref
        v = vref[bi, :, (h % 2) * D_VDIM:(h % 2 + 1) * D_VDIM]
        inner = lax.dot_general(q, k, (((1,), (1,)), ((), ())), preferred_element_type=F32) * inner_dec
        outs.append(_bdot(inner, v) + _bdot(q, r0s[n]) * q_dec)
        r_new.append(r0s[n] * c_dec + _bdot(k.astype(F32) * k_dec, v, ((0,), (0,))))
    for n in range(len(chains)):
        r_ref[n] = r_new[n]
    for bi in range(RET_NB):
        o_ref[0, bi] = jnp.concatenate(outs[bi * D_HEADS:(bi + 1) * D_HEADS], axis=1).astype(o_ref.dtype)


def _retention(P2, t_ctx):
    B, S, _ = P2.shape
    C = RET_CHUNK
    nc = S // C
    nctx = t_ctx // C

    def cidx(d, c):
        back = jnp.where(c < nctx, nctx - 1 - c, nc - 1 + nctx - c)
        return jnp.where(d == 0, c, back)

    w = D_HEADS * D_KDIM
    return pl.pallas_call(
        _retention_kernel,
        grid=(B // RET_NB, 2, nc),
        in_specs=[pl.BlockSpec((RET_NB, C, w), lambda b, d, c: (b, cidx(d, c), D_Q_OFF // w)),
                  pl.BlockSpec((RET_NB, C, w), lambda b, d, c: (b, cidx(d, c), D_K_OFF // w)),
                  pl.BlockSpec((RET_NB, C, w), lambda b, d, c: (b, cidx(d, c), D_V_OFF // w)),
                  pl.BlockSpec((RET_NB, C, w), lambda b, d, c: (b, cidx(d, c), D_V_OFF // w + 1))],
        out_specs=pl.BlockSpec((1, RET_NB, C, D_DIM), lambda b, d, c: (d, b, cidx(d, c), 0)),
        out_shape=jax.ShapeDtypeStruct((2, B, S, D_DIM), BF16),
        scratch_shapes=[pltpu.VMEM((RET_NB * D_HEADS, D_KDIM, D_VDIM), F32)],
        compiler_params=_cparams(("arbitrary", "arbitrary", "arbitrary")),
        name="retention",
    )(P2, P2, P2, P2)


def _odd_merge_kernel(c_ref, o2_ref, gate_ref, x_ref, mod_ref, rn_ref, wo_ref, gffn_ref, wr_ref, o_ref, *route_refs):
    o = o2_ref[0, 0].astype(F32) + o2_ref[1, 0].astype(F32)
    parts = []
    for h in range(D_HEADS):
        oh = o[:, h * D_VDIM:(h + 1) * D_VDIM]
        ms = jnp.mean(oh * oh, axis=-1, keepdims=True)
        parts.append(oh * lax.rsqrt(ms + NORM_EPS))
    gt = gate_ref[0].astype(F32)
    dl = jnp.concatenate(parts, axis=1) * rn_ref[...] * (gt * jax.nn.sigmoid(gt))
    out = (jnp.dot(c_ref[0], wo_ref[0:C_DIM, :], preferred_element_type=F32)
           + _bdot(dl, wo_ref[C_DIM:C_DIM + D_DIM, :]))
    m = mod_ref[0, 0]
    xn = x_ref[0] + m[2:3] * out
    o_ref[0] = xn
    _route_block(xn, m, gffn_ref, wr_ref, *route_refs)


def _odd_merge(c_lat, o2, P2, X, mod, ret_norm, wo_bf16, t_ctx, ffn_gain, w_router):
    B, S, D = X.shape
    T = S - t_ctx
    off = t_ctx // TB
    r_in, r_out, r_shapes, r_scratch = _route_specs(B, T, D)
    return pl.pallas_call(
        _odd_merge_kernel,
        grid=(B, T // TB),
        in_specs=[pl.BlockSpec((1, TB, C_DIM), lambda b, j: (b, j, 0)),
                  pl.BlockSpec((2, 1, TB, D_DIM), lambda b, j: (0, b, j + off, 0)),
                  pl.BlockSpec((1, TB, D_DIM), lambda b, j: (b, j + off, D_G_OFF // D_DIM)),
                  pl.BlockSpec((1, TB, D), lambda b, j: (b, j + off, 0)),
                  _mod_spec(D, all_lat=True),
                  pl.BlockSpec((1, D_DIM), lambda b, j: (0, 0)),
                  pl.BlockSpec((C_DIM + D_DIM, D), lambda b, j: (0, 0))] + r_in,
        out_specs=[pl.BlockSpec((1, TB, D), lambda b, j: (b, j, 0))] + r_out,
        out_shape=[jax.ShapeDtypeStruct((B, T, D), F32)] + r_shapes,
        scratch_shapes=r_scratch,
        compiler_params=_cparams(("arbitrary", "arbitrary")),
        name="odd_merge_route",
    )(c_lat, o2, P2, X, mod, ret_norm.reshape(1, -1), wo_bf16, ffn_gain.reshape(1, D), w_router)


def _rope_tables(t_ctx, T):
    half = C_HEAD // 4
    inv = ROPE_BASE ** (-jnp.arange(half, dtype=F32) / half)
    t = jnp.arange(T, dtype=jnp.int32)
    ar = (t // GRID_W).astype(F32)[:, None] * inv[None, :]
    ac = (t % GRID_W).astype(F32)[:, None] * inv[None, :]
    z = jnp.zeros_like(ar)
    cos = jnp.concatenate([jnp.cos(ar), jnp.cos(ar), jnp.cos(ac), jnp.cos(ac)], axis=1)
    s1 = jnp.concatenate([-jnp.sin(ar), z, -jnp.sin(ac), z], axis=1)
    s2 = jnp.concatenate([z, jnp.sin(ar), z, jnp.sin(ac)], axis=1)

    def full(tab, ctx_val):
        tab = jnp.concatenate([jnp.full((t_ctx, C_HEAD), ctx_val, F32), tab], axis=0)
        return jnp.tile(tab, (1, LANES // C_HEAD))

    return full(cos, 1.0), full(s1, 0.0), full(s2, 0.0)


def _head_indicator(n, width):
    i = jnp.arange(n) // width
    return (i[:, None] == i[None, :]).astype(BF16)


def kernel(x, c, ctx, c_ctx, ada_w, ada_b, norm_mix, norm_ffn, ev_w_in, ev_w_out, rwkv_shift, rwkv_w0, rwkv_w_up, rwkv_a0, rwkv_a_up, rwkv_g_up, rwkv_k_k, rwkv_k_a, rwkv_r_k, rwkv_ln_g, rwkv_ln_b, pool_w, pool_scale, od_w_in, od_w_out, diff_lambda, diff_subln, ret_norm, moe_router_group, moe_router_expert, moe_w_gate, moe_w_up, moe_w_down, final_norm):
    B, T, D = x.shape
    t_ctx = ctx.shape[1]
    assert ada_w.shape[0] == 2 and t_ctx == TB and T % TB == 0 and T % GRID_W == 0 and B < 16
    assert B % SCAN_NB == 0 and B % RET_NB == 0

    cpad = jnp.zeros((16, D), F32).at[:B].set(c).at[B].set(c_ctx)
    mods = _ada(cpad, ada_w, ada_b)

    def layer_mod(i):
        lat = mods[i, :B]
        cm = jnp.broadcast_to(mods[i, B][None], lat.shape)
        return jnp.stack([cm, lat], axis=1).reshape(B, 2, 6, D)

    def router_w(i):
        w = jnp.concatenate([moe_router_group[i], moe_router_expert[i]], axis=1)
        return jnp.pad(w, ((0, 0), (0, LANES - w.shape[1])))

    ehead = _head_indicator(A_DIM, A_HEAD)

    mod0 = layer_mod(0)
    P = _proj(ctx, x, mod0, norm_mix[0], ev_w_in[0].astype(BF16), BF16)
    zl = jnp.zeros((LORA_W, A_DIM), F32)
    lora_blocks = []
    for d in range(2):
        lora_blocks.append(jnp.concatenate([rwkv_w_up[0, d], zl], axis=0))
        lora_blocks.append(jnp.concatenate([zl, rwkv_a_up[0, d]], axis=0))
    lora_w = jnp.concatenate(lora_blocks, axis=1).astype(BF16)
    wa0 = jnp.concatenate([rwkv_w0[0, 0], rwkv_a0[0, 0], rwkv_w0[0, 1], rwkv_a0[0, 1]]).reshape(1, -1)
    lw, kt, bb, v, kk, r, bonus, g, yb = _even_prep(
        P, t_ctx, rwkv_shift[0], wa0, lora_w, rwkv_g_up[0].astype(BF16), rwkv_k_k[0], rwkv_k_a[0], rwkv_r_k[0],
        pool_w[0].astype(BF16), pool_scale[0], ehead)
    y2 = _rwkv_scan(lw, kt, bb, v, kk, r, t_ctx)
    routed = _even_merge(y2, bonus, g, yb, ctx, x, mod0, rwkv_ln_g[0], rwkv_ln_b[0], ev_w_out[0].astype(BF16), ehead,
                         norm_ffn[0], router_w(0))
    mod1 = layer_mod(1)
    X, P2 = _moe(routed, mod0, 0, moe_w_gate, moe_w_up, moe_w_down, all_lat=False,
                 project=(mod1, norm_mix[1], od_w_in[0].astype(BF16), _rope_tables(t_ctx, T)))

    lam_init = 0.8 - 0.6 * math.exp(-0.3 * 1)
    c_lat = _diff_attn(P2, t_ctx, diff_lambda[0], diff_subln[0], lam_init)
    o2 = _retention(P2, t_ctx)
    routed = _odd_merge(c_lat, o2, P2, X, mod1, ret_norm[0], od_w_out[0].astype(BF16), t_ctx, norm_ffn[1], router_w(1))
    return _moe(routed, mod1, 1, moe_w_gate, moe_w_up, moe_w_down, all_lat=True,
                final_gain=final_norm)
```
